```python
import math
import jax, jax.numpy as jnp
from jax import lax
import numpy as np

D_MODEL = 1024
BATCH = 4
SEQ = 4096
DEPTH = 2

D_MIX = D_MODEL
D_ATTN = D_MIX // 2
D_CONV = D_MIX - D_ATTN
N_HEADS = 4
QK_NOPE = 128
QK_ROPE = 64
V_HEAD = D_ATTN // N_HEADS
Q_LORA = 256
KV_LORA = 128
CONV_WIDTH = 3
D_IN = Q_LORA + KV_LORA + QK_ROPE + 3 * D_CONV
ROPE_BASE = 10000.0
Q_BLOCK = 128
D_FF = 2816
N_EXPERTS = 8
TOP_K = 2
D_FF_EXPERT = 3584
N_DENSE = (DEPTH + 1) // 2
N_MOE = DEPTH // 2
RMS_EPS = 1e-6
LN_EPS = 1e-5
DEEPNORM_ALPHA = (2 * DEPTH) ** 0.25
DEEPNORM_BETA = (8 * DEPTH) ** -0.25
ADA_INIT_SCALE = 0.1

kernel_name = "hybrid_mla_shortconv_moe_deepnorm_adaln"


def rms_norm(x, g):
    xf = x.astype(jnp.float32)
    y = xf * lax.rsqrt(jnp.mean(xf * xf, axis=-1, keepdims=True) + RMS_EPS)
    return (y * g.astype(jnp.float32)).astype(x.dtype)


def layer_norm(x, g, b):
    xf = x.astype(jnp.float32)
    mu = jnp.mean(xf, axis=-1, keepdims=True)
    var = jnp.mean(jnp.square(xf - mu), axis=-1, keepdims=True)
    y = (xf - mu) * lax.rsqrt(var + LN_EPS) * g.astype(jnp.float32) + b.astype(jnp.float32)
    return y.astype(x.dtype)


def rope_angles(positions):
    inv_freq = ROPE_BASE ** (-jnp.arange(0, QK_ROPE, 2, dtype=jnp.float32) / QK_ROPE)
    ang = positions.astype(jnp.float32)[..., None] * inv_freq
    return jnp.cos(ang), jnp.sin(ang)


def apply_rope(x, cos, sin):
    half = x.shape[-1] // 2
    x1 = x[..., :half].astype(jnp.float32)
    x2 = x[..., half:].astype(jnp.float32)
    y = jnp.concatenate([x1 * cos - x2 * sin, x2 * cos + x1 * sin], axis=-1)
    return y.astype(x.dtype)


def causal_mla_attention(q_nope, q_rope, k_nope, k_rope, v):
    b, s, h, _ = q_nope.shape
    nb = s // Q_BLOCK
    scale = (QK_NOPE + QK_ROPE) ** -0.5
    qn = q_nope.reshape(b, nb, Q_BLOCK, h, QK_NOPE).transpose(1, 0, 2, 3, 4)
    qr = q_rope.reshape(b, nb, Q_BLOCK, h, QK_ROPE).transpose(1, 0, 2, 3, 4)
    key_pos = jnp.arange(s)

    def one_block(args):
        qn_b, qr_b, i = args
        sc = (jnp.einsum('bqhd,bkhd->bhqk', qn_b, k_nope)
              + jnp.einsum('bqhr,bkr->bhqk', qr_b, k_rope)).astype(jnp.float32) * scale
        q_pos = i * Q_BLOCK + jnp.arange(Q_BLOCK)
        mask = key_pos[None, :] <= q_pos[:, None]
        sc = jnp.where(mask[None, None], sc, jnp.finfo(jnp.float32).min)
        p = jax.nn.softmax(sc, axis=-1).astype(v.dtype)
        return jnp.einsum('bhqk,bkhd->bqhd', p, v)

    out = lax.map(one_block, (qn, qr, jnp.arange(nb)))
    return out.transpose(1, 0, 2, 3, 4).reshape(b, s, h * V_HEAD)


def causal_short_conv(u, w):
    s = u.shape[1]
    up = jnp.pad(u, ((0, 0), (CONV_WIDTH - 1, 0), (0, 0)))
    return sum(w[j] * up[:, j:j + s] for j in range(CONV_WIDTH))


def hybrid_mixer(h, cos, sin, w_in, q_norm_g, kv_norm_g, w_uq, w_ukv, conv_w, mix_norm_g, w_o):
    b, s, _ = h.shape
    z = h @ w_in
    o1 = Q_LORA
    o2 = o1 + KV_LORA
    o3 = o2 + QK_ROPE
    o4 = o3 + D_CONV
    o5 = o4 + D_CONV
    c_q, c_kv, k_rope = z[..., :o1], z[..., o1:o2], z[..., o2:o3]
    gate_b, gate_c, u = z[..., o3:o4], z[..., o4:o5], z[..., o5:]
    q = (rms_norm(c_q, q_norm_g) @ w_uq).reshape(b, s, N_HEADS, QK_NOPE + QK_ROPE)
    kv = (rms_norm(c_kv, kv_norm_g) @ w_ukv).reshape(b, s, N_HEADS, QK_NOPE + V_HEAD)
    q_nope, q_rope = q[..., :QK_NOPE], q[..., QK_NOPE:]
    k_nope, v = kv[..., :QK_NOPE], kv[..., QK_NOPE:]
    q_rope = apply_rope(q_rope, cos[:, :, None, :], sin[:, :, None, :])
    k_rope = apply_rope(k_rope, cos, sin)
    attn = causal_mla_attention(q_nope, q_rope, k_nope, k_rope, v)
    conv = gate_b * causal_short_conv(gate_c * u, conv_w)
    y = jnp.concatenate([rms_norm(attn, mix_norm_g[:D_ATTN]),
                         rms_norm(conv, mix_norm_g[D_ATTN:])], axis=-1)
    return y @ w_o


def swiglu(h, wg, wu, wd):
    return (jax.nn.silu(h @ wg) * (h @ wu)) @ wd


def top2_moe(h, router_w, router_b, wg, wu, wd):
    b, s, d = h.shape
    t = h.reshape(b * s, d)
    logits = (t @ router_w).astype(jnp.float32) + router_b.astype(jnp.float32)
    top_v, top_i = lax.top_k(logits, TOP_K)
    gates = jax.nn.softmax(top_v, axis=-1)
    combine = jnp.sum(jax.nn.one_hot(top_i, N_EXPERTS, dtype=jnp.float32) * gates[..., None], axis=1)
    combine = combine.astype(h.dtype)
    out = jnp.zeros_like(t)
    for e in range(N_EXPERTS):
        out = out + combine[:, e:e + 1] * swiglu(t, wg[e], wu[e], wd[e])
    return out.reshape(b, s, d)


def setup_inputs(seed: int = 0) -> dict:
    key = jax.random.key(seed)
    ks = iter(jax.random.split(key, 32))
    nrm = lambda shape, std: jax.random.normal(next(ks), shape, jnp.float32) * std
    gain = lambda shape: 1.0 + nrm(shape, 0.01)
    L = DEPTH
    x = nrm((BATCH, SEQ, D_MODEL), 1.0)
    c = nrm((BATCH, D_MODEL), 1.0)
    positions = (jnp.arange(SEQ, dtype=jnp.int32)[None, :]
                 + jax.random.randint(next(ks), (BATCH, 1), 0, 1024, dtype=jnp.int32))
    b_ada = nrm((L, 6, D_MODEL), 0.01)
    b_ada = b_ada.at[:, 2].add(1.0).at[:, 5].add(1.0).reshape(L, 6 * D_MODEL)
    return {
        "x": x,
        "c": c,
        "positions": positions,
        "w_in": nrm((L, D_MODEL, D_IN), D_MODEL ** -0.5),
        "q_norm_g": gain((L, Q_LORA)),
        "kv_norm_g": gain((L, KV_LORA)),
        "w_uq": nrm((L, Q_LORA, N_HEADS * (QK_NOPE + QK_ROPE)), Q_LORA ** -0.5),
        "w_ukv": nrm((L, KV_LORA, N_HEADS * (QK_NOPE + V_HEAD)), KV_LORA ** -0.5),
        "conv_w": nrm((L, CONV_WIDTH, D_CONV), CONV_WIDTH ** -0.5),
        "mix_norm_g": gain((L, D_MIX)),
        "w_o": nrm((L, D_MIX, D_MODEL), D_MIX ** -0.5 * DEEPNORM_BETA),
        "w_ada": nrm((L, D_MODEL, 6 * D_MODEL), D_MODEL ** -0.5 * ADA_INIT_SCALE),
        "b_ada": b_ada,
        "ln1_g": gain((L, D_MODEL)),
        "ln1_b": nrm((L, D_MODEL), 0.01),
        "ln2_g": gain((L, D_MODEL)),
        "ln2_b": nrm((L, D_MODEL), 0.01),
        "ffn_w_gate": nrm((N_DENSE, D_MODEL, D_FF), D_MODEL ** -0.5),
        "ffn_w_up": nrm((N_DENSE, D_MODEL, D_FF), D_MODEL ** -0.5),
        "ffn_w_down": nrm((N_DENSE, D_FF, D_MODEL), D_FF ** -0.5 * DEEPNORM_BETA),
        "moe_router_w": nrm((N_MOE, D_MODEL, N_EXPERTS), D_MODEL ** -0.5),
        "moe_router_b": nrm((N_MOE, N_EXPERTS), 0.01),
        "moe_w_gate": nrm((N_MOE, N_EXPERTS, D_MODEL, D_FF_EXPERT), D_MODEL ** -0.5),
        "moe_w_up": nrm((N_MOE, N_EXPERTS, D_MODEL, D_FF_EXPERT), D_MODEL ** -0.5),
        "moe_w_down": nrm((N_MOE, N_EXPERTS, D_FF_EXPERT, D_MODEL), D_FF_EXPERT ** -0.5 * DEEPNORM_BETA),
    }


def reference(x, c, positions, w_in, q_norm_g, kv_norm_g, w_uq, w_ukv, conv_w, mix_norm_g, w_o,
              w_ada, b_ada, ln1_g, ln1_b, ln2_g, ln2_b, ffn_w_gate, ffn_w_up, ffn_w_down,
              moe_router_w, moe_router_b, moe_w_gate, moe_w_up, moe_w_down):
    cos, sin = rope_angles(positions)
    c_act = jax.nn.silu(c)
    for l in range(DEPTH):
        mod = (c_act @ w_ada[l] + b_ada[l])[:, None, :]
        shift1, scale1, gate1, shift2, scale2, gate2 = jnp.split(mod, 6, axis=-1)
        h = x * (1 + scale1) + shift1
        y = hybrid_mixer(h, cos, sin, w_in[l], q_norm_g[l], kv_norm_g[l], w_uq[l], w_ukv[l],
                         conv_w[l], mix_norm_g[l], w_o[l])
        x = layer_norm(DEEPNORM_ALPHA * x + gate1 * y, ln1_g[l], ln1_b[l])
        h = x * (1 + scale2) + shift2
        if l % 2 == 0:
            i = l // 2
            f = swiglu(h, ffn_w_gate[i], ffn_w_up[i], ffn_w_down[i])
        else:
            i = l // 2
            f = top2_moe(h, moe_router_w[i], moe_router_b[i], moe_w_gate[i], moe_w_up[i], moe_w_down[i])
        x = layer_norm(DEEPNORM_ALPHA * x + gate2 * f, ln2_g[l], ln2_b[l])
    return x
```

```python
import functools

import jax
import jax.numpy as jnp
from jax import lax
from jax.experimental import pallas as pl
from jax.experimental.pallas import tpu as pltpu

F32 = jnp.float32
BF16 = jnp.bfloat16

D_MODEL = 1024
N_HEADS = 4
QK_NOPE = 128
QK_ROPE = 64
V_HEAD = 128
Q_LORA = 256
KV_LORA = 128
D_ATTN = N_HEADS * V_HEAD
D_CONV = D_MODEL - D_ATTN
CONV_WIDTH = 3
ROPE_BASE = 10000.0
D_FF = 2816
N_EXPERTS = 8
D_FF_EXPERT = 3584
RMS_EPS = 1e-6
LN_EPS = 1e-5
DEPTH = 2
DEEPNORM_ALPHA = (2 * DEPTH) ** 0.25
SM_SCALE = (QK_NOPE + QK_ROPE) ** -0.5

LANES = 128
QK_PAD = 2 * LANES
D_IN_PAD = 2048
ROW_ALIGN = 16
SLOT_CHUNK = 256
VMEM_LIMIT = 56 * 1024 * 1024
NEG_BIG = -1e30


def _cparams(n_axes):
    return pltpu.CompilerParams(dimension_semantics=("arbitrary",) * n_axes, vmem_limit_bytes=VMEM_LIMIT)


def _dot(a, b):
    return jnp.dot(a, b, preferred_element_type=F32)


def _rms(x, g):
    return x * lax.rsqrt(jnp.mean(x * x, axis=-1, keepdims=True) + RMS_EPS) * g


def _layer_norm(x, g, b):
    mu = jnp.mean(x, axis=-1, keepdims=True)
    xc = x - mu
    var = jnp.mean(xc * xc, axis=-1, keepdims=True)
    return xc * lax.rsqrt(var + LN_EPS) * g + b


def _silu(x):
    return x * jax.nn.sigmoid(x)


def _ada_kernel(c_ref, w_ref, b_ref, o_ref):
    ca = _silu(c_ref[...]).astype(BF16)
    o_ref[0, 0] = _dot(ca, w_ref[0].astype(BF16)) + b_ref[0]


def _ada_mod(c_pad, w_ada, b_ada):
    n_layers = w_ada.shape[0]
    bp = c_pad.shape[0]
    return pl.pallas_call(
        _ada_kernel,
        grid=(n_layers, 6),
        in_specs=[
            pl.BlockSpec((bp, D_MODEL), lambda l, j: (0, 0)),
            pl.BlockSpec((1, D_MODEL, D_MODEL), lambda l, j: (l, 0, j)),
            pl.BlockSpec((1, 1, D_MODEL), lambda l, j: (l, 0, j)),
        ],
        out_specs=pl.BlockSpec((1, 1, bp, D_MODEL), lambda l, j: (l, j, 0, 0)),
        out_shape=jax.ShapeDtypeStruct((n_layers, 6, bp, D_MODEL), F32),
        compiler_params=_cparams(2),
        name="ada_mod",
    )(c_pad, w_ada, b_ada.reshape(n_layers, 1, 6 * D_MODEL))


def _rope_kernel(pos_ref, f_ref, o_ref):
    ang = pos_ref[...].astype(F32) * f_ref[0:1, :]
    c = jnp.cos(ang)
    s = jnp.sin(ang)
    o_ref[:, 0:LANES] = c * f_ref[1:2, :]
    o_ref[:, LANES:2 * LANES] = s * f_ref[2:3, :]
    o_ref[:, 2 * LANES:3 * LANES] = s * f_ref[3:4, :]


def _rope_tables(positions):
    t = positions.size
    tr = min(1024, t)
    half = QK_ROPE // 2
    inv_freq = ROPE_BASE ** (-jnp.arange(0, QK_ROPE, 2, dtype=F32) / QK_ROPE)
    zeros = jnp.zeros((half,), F32)
    ones = jnp.ones((half,), F32)
    consts = jnp.stack([
        jnp.concatenate([inv_freq, inv_freq, zeros, zeros]),
        jnp.concatenate([ones, ones, zeros, zeros]),
        jnp.concatenate([-ones, zeros, zeros, zeros]),
        jnp.concatenate([zeros, ones, zeros, zeros]),
    ])
    consts = jnp.pad(consts, ((0, 4), (0, 0)))
    return pl.pallas_call(
        _rope_kernel,
        grid=(t // tr,),
        in_specs=[pl.BlockSpec((tr, 1), lambda i: (i, 0)), pl.BlockSpec((8, LANES), lambda i: (0, 0))],
        out_specs=pl.BlockSpec((tr, 3 * LANES), lambda i: (i, 0)),
        out_shape=jax.ShapeDtypeStruct((t, 3 * LANES), F32),
        compiler_params=_cparams(1),
        name="rope_tables",
    )(positions.reshape(t, 1), consts)


def _inproj_kernel(x_ref, mod_ref, win_ref, wuq_ref, wukv_ref, qg_ref, kvg_ref, cw_ref, cg_ref, rope_ref,
                   q_ref, k_ref, v_ref, cv_ref, cu_ext, *, ts):
    j = pl.program_id(1)
    x = x_ref[0]
    shift = mod_ref[0, 0:1, :]
    scale = mod_ref[0, 1:2, :]
    h = (x * (1.0 + scale) + shift).astype(BF16)
    z = _dot(h, win_ref[...])
    c_q = z[:, 0:Q_LORA]
    c_kv = z[:, Q_LORA:Q_LORA + KV_LORA]
    k_rope = z[:, 384:512]
    gate_b = z[:, 512:1024]
    gate_c = z[:, 1024:1536]
    u = z[:, 1536:2048]

    q = _dot(_rms(c_q, qg_ref[...]).astype(BF16), wuq_ref[...])
    kv = _dot(_rms(c_kv, kvg_ref[...]).astype(BF16), wukv_ref[...])

    cos_d = rope_ref[:, 0:LANES]
    sin_a = rope_ref[:, LANES:2 * LANES]
    sin_b = rope_ref[:, 2 * LANES:3 * LANES]

    def rope(t):
        return t * cos_d + pltpu.roll(t, 96, 1) * sin_a + pltpu.roll(t, 32, 1) * sin_b

    k_rot = rope(k_rope).astype(BF16)
    for hh in range(N_HEADS):
        qo = hh * QK_PAD
        q_ref[0, hh, :, 0:LANES] = (q[:, qo:qo + LANES] * SM_SCALE).astype(BF16)
        q_ref[0, hh, :, LANES:QK_PAD] = (rope(q[:, qo + LANES:qo + QK_PAD]) * SM_SCALE).astype(BF16)
        ko = hh * (QK_NOPE + V_HEAD)
        k_ref[0, hh, :, 0:LANES] = kv[:, ko:ko + QK_NOPE].astype(BF16)
        k_ref[0, hh, :, LANES:QK_PAD] = k_rot
        v_ref[0, hh] = kv[:, ko + QK_NOPE:ko + QK_NOPE + V_HEAD].astype(BF16)

    @pl.when(j == 0)
    def _():
        cu_ext[0:8, :] = jnp.zeros((8, D_CONV), F32)

    @pl.when(j > 0)
    def _():
        cu_ext[0:8, :] = cu_ext[ts:ts + 8, :]

    cu_ext[8:ts + 8, :] = gate_c * u
    conv = (cw_ref[2:3, :] * cu_ext[8:ts + 8, :]
            + cw_ref[1:2, :] * cu_ext[pl.ds(7, ts), :]
            + cw_ref[0:1, :] * cu_ext[pl.ds(6, ts), :])
    conv = gate_b * conv
    cv_ref[0] = _rms(conv, cg_ref[...]).astype(BF16)


def _inproj(x, mod_l, w_in_p, w_uq_p, w_ukv, qg, kvg, conv_w, conv_g, rope_tab, ts):
    b, s, _ = x.shape
    nsb = s // ts
    const = lambda shape: pl.BlockSpec(shape, lambda bi, j: (0,) * len(shape))
    return pl.pallas_call(
        functools.partial(_inproj_kernel, ts=ts),
        grid=(b, nsb),
        in_specs=[
            pl.BlockSpec((1, ts, D_MODEL), lambda bi, j: (bi, j, 0)),
            pl.BlockSpec((1, 8, D_MODEL), lambda bi, j: (bi, 0, 0)),
            const((D_MODEL, D_IN_PAD)),
            const((Q_LORA, N_HEADS * QK_PAD)),
            const((KV_LORA, N_HEADS * (QK_NOPE + V_HEAD))),
            const((1, Q_LORA)),
            const((1, KV_LORA)),
            const((8, D_CONV)),
            const((1, D_CONV)),
            pl.BlockSpec((ts, 3 * LANES), lambda bi, j: (bi * nsb + j, 0)),
        ],
        out_specs=[
            pl.BlockSpec((1, N_HEADS, ts, QK_PAD), lambda bi, j: (bi, 0, j, 0)),
            pl.BlockSpec((1, N_HEADS, ts, QK_PAD), lambda bi, j: (bi, 0, j, 0)),
            pl.BlockSpec((1, N_HEADS, ts, V_HEAD), lambda bi, j: (bi, 0, j, 0)),
            pl.BlockSpec((1, ts, D_CONV), lambda bi, j: (bi, j, 0)),
        ],
        out_shape=[
            jax.ShapeDtypeStruct((b, N_HEADS, s, QK_PAD), BF16),
            jax.ShapeDtypeStruct((b, N_HEADS, s, QK_PAD), BF16),
            jax.ShapeDtypeStruct((b, N_HEADS, s, V_HEAD), BF16),
            jax.ShapeDtypeStruct((b, s, D_CONV), BF16),
        ],
        scratch_shapes=[pltpu.VMEM((ts + 8, D_CONV), F32)],
        compiler_params=_cparams(2),
        name="inproj",
    )(x, mod_l, w_in_p, w_uq_p, w_ukv, qg, kvg, conv_w, conv_g, rope_tab)


def _attn_kernel(q_ref, k_ref, v_ref, o_ref, m_sc, l_sc, acc_sc, *, tq):
    i = pl.program_id(2)
    q = q_ref[0, 0]
    m_sc[...] = jnp.full(m_sc.shape, NEG_BIG, F32)
    l_sc[...] = jnp.zeros(l_sc.shape, F32)
    acc_sc[...] = jnp.zeros(acc_sc.shape, F32)

    def step(jk, masked):
        start = pl.multiple_of(jk * tq, tq)
        kj = k_ref[0, 0, pl.ds(start, tq), :]
        vj = v_ref[0, 0, pl.ds(start, tq), :]
        s = lax.dot_general(q, kj, (((1,), (1,)), ((), ())), preferred_element_type=F32)
        if masked:
            row = lax.broadcasted_iota(jnp.int32, (tq, tq), 0)
            col = lax.broadcasted_iota(jnp.int32, (tq, tq), 1)
            s = jnp.where(col <= row, s, NEG_BIG)
        m_prev = m_sc[...]
        m_new = jnp.maximum(m_prev, jnp.max(s, axis=-1, keepdims=True))
        alpha = jnp.exp(m_prev - m_new)
        p = jnp.exp(s - m_new)
        l_sc[...] = alpha * l_sc[...] + jnp.sum(p, axis=-1, keepdims=True)
        acc_sc[...] = alpha * acc_sc[...] + _dot(p.astype(BF16), vj)
        m_sc[...] = m_new

    def body(jk, carry):
        step(jk, False)
        return carry

    lax.fori_loop(0, i, body, 0)
    step(i, True)
    o_ref[0] = (acc_sc[...] / l_sc[...]).astype(BF16)


def _attention(q, k, v, tq):
    b, nh, s, _ = q.shape
    return pl.pallas_call(
        functools.partial(_attn_kernel, tq=tq),
        grid=(b, nh, s // tq),
        in_specs=[
            pl.BlockSpec((1, 1, tq, QK_PAD), lambda bi, hh, i: (bi, hh, i, 0)),
            pl.BlockSpec((1, 1, s, QK_PAD), lambda bi, hh, i: (bi, hh, 0, 0)),
            pl.BlockSpec((1, 1, s, V_HEAD), lambda bi, hh, i: (bi, hh, 0, 0)),
        ],
        out_specs=pl.BlockSpec((1, tq, V_HEAD), lambda bi, hh, i: (bi, i, hh)),
        out_shape=jax.ShapeDtypeStruct((b, s, D_ATTN), BF16),
        scratch_shapes=[pltpu.VMEM((tq, 1), F32), pltpu.VMEM((tq, 1), F32), pltpu.VMEM((tq, V_HEAD), F32)],
        compiler_params=_cparams(3),
        name="attention",
    )(q, k, v)


def _outproj_core(attn_ref, cv_ref, x_ref, mod_ref, wo_ref, ga_ref, lng_ref, lnb_ref):
    a = attn_ref[0].astype(F32)
    an = _rms(a, ga_ref[...]).astype(BF16)
    y = _dot(an, wo_ref[0:D_ATTN, :]) + _dot(cv_ref[0], wo_ref[D_ATTN:D_MODEL, :])
    gate1 = mod_ref[0, 2:3, :]
    x1 = _layer_norm(DEEPNORM_ALPHA * x_ref[0] + gate1 * y, lng_ref[...], lnb_ref[...])
    h2 = x1 * (1.0 + mod_ref[0, 4:5, :]) + mod_ref[0, 3:4, :]
    return x1, h2


def _outproj_kernel(attn_ref, cv_ref, x_ref, mod_ref, wo_ref, ga_ref, lng_ref, lnb_ref, x1_ref, h2_ref):
    x1, h2 = _outproj_core(attn_ref, cv_ref, x_ref, mod_ref, wo_ref, ga_ref, lng_ref, lnb_ref)
    x1_ref[0] = x1
    h2_ref[0] = h2.astype(BF16)


def _outproj_router_kernel(attn_ref, cv_ref, x_ref, mod_ref, wo_ref, ga_ref, lng_ref, lnb_ref, rw_ref, rb_ref,
                           x1_ref, h2_ref, rel_ref, gate_ref, relt_ref, base_ref, cnt_ref, run_sc, *, ts):
    first = jnp.logical_and(pl.program_id(0) == 0, pl.program_id(1) == 0)

    @pl.when(first)
    def _():
        run_sc[...] = jnp.zeros(run_sc.shape, F32)

    x1, h2 = _outproj_core(attn_ref, cv_ref, x_ref, mod_ref, wo_ref, ga_ref, lng_ref, lnb_ref)
    x1_ref[0] = x1
    h2_ref[0] = h2.astype(BF16)

    lane = lax.broadcasted_iota(jnp.int32, (ts, LANES), 1)
    logits = jnp.dot(h2, rw_ref[...], preferred_element_type=F32, precision=lax.Precision.HIGHEST) + rb_ref[...]
    logits = jnp.where(lane < N_EXPERTS, logits, NEG_BIG)
    v1 = jnp.max(logits, axis=-1, keepdims=True)
    i1 = jnp.min(jnp.where(logits == v1, lane, LANES), axis=-1, keepdims=True)
    rest = jnp.where(lane == i1, NEG_BIG, logits)
    v2 = jnp.max(rest, axis=-1, keepdims=True)
    i2 = jnp.min(jnp.where(rest == v2, lane, LANES), axis=-1, keepdims=True)
    e21 = jnp.exp(v2 - v1)
    g1 = 1.0 / (1.0 + e21)
    g2 = e21 / (1.0 + e21)
    is1 = lane == i1
    is2 = lane == i2
    sel = jnp.where(jnp.logical_or(is1, is2), 1.0, 0.0)
    gate_ref[...] = jnp.where(is1, g1, 0.0) + jnp.where(is2, g2, 0.0)

    r_i = lax.broadcasted_iota(jnp.int32, (ts, ts), 0)
    c_i = lax.broadcasted_iota(jnp.int32, (ts, ts), 1)
    tri = jnp.where(c_i < r_i, 1.0, 0.0).astype(BF16)
    rank = _dot(tri, sel.astype(BF16))
    rel = jnp.where(sel > 0.0, rank, -1.0)
    rel_ref[...] = rel
    relt_ref[0] = jnp.transpose(rel)[0:N_EXPERTS, :]

    n_tile = jnp.sum(sel, axis=0, keepdims=True)
    n_pad = jnp.floor((n_tile + (ROW_ALIGN - 1)) * (1.0 / ROW_ALIGN)) * ROW_ALIGN
    base_ref[0] = run_sc[...].astype(jnp.int32)
    cnt_ref[0] = n_tile.astype(jnp.int32)
    run_sc[...] = run_sc[...] + n_pad


def _outproj(attn, convn, x, mod_l, w_o, g_attn, ln_g, ln_b, ts, router=None):
    b, s, _ = x.shape
    nsb = s // ts
    const = lambda shape: pl.BlockSpec(shape, lambda bi, j: (0,) * len(shape))
    tile3 = lambda w: pl.BlockSpec((1, ts, w), lambda bi, j: (bi, j, 0))
    in_specs = [
        tile3(D_ATTN), tile3(D_CONV), tile3(D_MODEL),
        pl.BlockSpec((1, 8, D_MODEL), lambda bi, j: (bi, 0, 0)),
        const((D_MODEL, D_MODEL)), const((1, D_ATTN)), const((1, D_MODEL)), const((1, D_MODEL)),
    ]
    out_specs = [tile3(D_MODEL), tile3(D_MODEL)]
    out_shape = [jax.ShapeDtypeStruct((b, s, D_MODEL), F32), jax.ShapeDtypeStruct((b, s, D_MODEL), BF16)]
    args = [attn, convn, x, mod_l, w_o, g_attn, ln_g, ln_b]
    if router is None:
        return pl.pallas_call(
            _outproj_kernel, grid=(b, nsb), in_specs=in_specs, out_specs=out_specs, out_shape=out_shape,
            compiler_params=_cparams(2), name="outproj",
        )(*args)
    rw, rb = router
    nt = b * nsb
    flat = lambda w: pl.BlockSpec((ts, w), lambda bi, j: (bi * nsb + j, 0))
    per_tile = lambda r, w: pl.BlockSpec((1, r, w), lambda bi, j: (bi * nsb + j, 0, 0))
    in_specs += [const((D_MODEL, LANES)), const((1, LANES))]
    out_specs += [flat(LANES), flat(LANES), per_tile(N_EXPERTS, ts), per_tile(1, LANES), per_tile(1, LANES)]
    out_shape += [
        jax.ShapeDtypeStruct((nt * ts, LANES), F32), jax.ShapeDtypeStruct((nt * ts, LANES), F32),
        jax.ShapeDtypeStruct((nt, N_EXPERTS, ts), F32),
        jax.ShapeDtypeStruct((nt, 1, LANES), jnp.int32), jax.ShapeDtypeStruct((nt, 1, LANES), jnp.int32),
    ]
    return pl.pallas_call(
        functools.partial(_outproj_router_kernel, ts=ts), grid=(b, nsb), in_specs=in_specs, out_specs=out_specs,
        out_shape=out_shape, scratch_shapes=[pltpu.VMEM((1, LANES), F32)],
        compiler_params=_cparams(2), name="outproj_router",
    )(*args, rw, rb)


def _ffn_chunks(width, step):
    return [(c, min(step, width - c)) for c in range(0, width, step)]


def _ffn_kernel(h_ref, x1_ref, mod_ref, wg_ref, wu_ref, wd_ref, lng_ref, lnb_ref, o_ref, acc_sc):
    h = h_ref[...]
    for idx, (c, w) in enumerate(_ffn_chunks(D_FF, 512)):
        g = _dot(h, wg_ref[:, c:c + w])
        u = _dot(h, wu_ref[:, c:c + w])
        part = _dot((_silu(g) * u).astype(BF16), wd_ref[c:c + w, :])
        if idx == 0:
            acc_sc[...] = part
        else:
            acc_sc[...] += part
    gate2 = mod_ref[0, 5:6, :]
    o_ref[...] = _layer_norm(DEEPNORM_ALPHA * x1_ref[...] + gate2 * acc_sc[...], lng_ref[...], lnb_ref[...])


def _dense_ffn(h2, x1, mod_l, wg, wu, wd, ln_g, ln_b, tm, tiles_per_batch):
    t = h2.shape[0]
    const = lambda shape: pl.BlockSpec(shape, lambda i: (0,) * len(shape), pipeline_mode=pl.Buffered(1))
    return pl.pallas_call(
        _ffn_kernel,
        grid=(t // tm,),
        in_specs=[
            pl.BlockSpec((tm, D_MODEL), lambda i: (i, 0)),
            pl.BlockSpec((tm, D_MODEL), lambda i: (i, 0)),
            pl.BlockSpec((1, 8, D_MODEL), lambda i: (i // tiles_per_batch, 0, 0)),
            const((D_MODEL, D_FF)), const((D_MODEL, D_FF)), const((D_FF, D_MODEL)),
            const((1, D_MODEL)), const((1, D_MODEL)),
        ],
        out_specs=pl.BlockSpec((tm, D_MODEL), lambda i: (i, 0)),
        out_shape=jax.ShapeDtypeStruct((t, D_MODEL), F32),
        scratch_shapes=[pltpu.VMEM((tm, D_MODEL), F32)],
        compiler_params=_cparams(1),
        name="dense_ffn",
    )(h2, x1, mod_l, wg, wu, wd, ln_g, ln_b)


def _group_copy(src, dst, sem):
    return pltpu.make_async_copy(src, dst, sem)


def _dispatch_kernel(base_ref, cnt_ref, h_ref, relt_ref, xs_ref, stage, zbuf, sems, zsem, *, ts, zero_rows):
    i = pl.program_id(0)
    n_chunks = ts // SLOT_CHUNK
    h = h_ref[...]
    slot = lax.broadcasted_iota(jnp.int32, (SLOT_CHUNK, ts), 0).astype(F32)

    def chunk_copy(e, c):
        a = pl.multiple_of(base_ref[i * N_EXPERTS + e], ROW_ALIGN)
        dst = xs_ref.at[e, pl.ds(a + c * SLOT_CHUNK, SLOT_CHUNK), :]
        return _group_copy(stage.at[e, c], dst, sems.at[e, c])

    for e in range(N_EXPERTS):
        n = cnt_ref[i * N_EXPERTS + e]
        for c in range(n_chunks):
            @pl.when(n > c * SLOT_CHUNK)
            def _(e=e, c=c):
                onehot = jnp.where(relt_ref[0, e:e + 1, :] == slot + float(c * SLOT_CHUNK), 1.0, 0.0).astype(BF16)
                stage[e, c] = _dot(onehot, h).astype(BF16)
                chunk_copy(e, c).start()

    for e in range(N_EXPERTS):
        n = cnt_ref[i * N_EXPERTS + e]
        for c in range(n_chunks):
            @pl.when(n > c * SLOT_CHUNK)
            def _(e=e, c=c):
                chunk_copy(e, c).wait()

    @pl.when(i == pl.num_programs(0) - 1)
    def _():
        zbuf[...] = jnp.zeros(zbuf.shape, BF16)
        for e in range(N_EXPERTS):
            n = cnt_ref[i * N_EXPERTS + e]
            end = base_ref[i * N_EXPERTS + e] + ((n + (ROW_ALIGN - 1)) // ROW_ALIGN) * ROW_ALIGN
            end = pl.multiple_of(end, ROW_ALIGN)
            copies = [
                _group_copy(zbuf, xs_ref.at[e, pl.ds(end + r * SLOT_CHUNK, SLOT_CHUNK), :], zsem.at[e])
                for r in range(zero_rows // SLOT_CHUNK)
            ]
            for cp in copies:
                cp.start()
            for cp in copies:
                cp.wait()


def _dispatch(base, cnt, h2, relt, ts, cap, zero_rows):
    t = h2.shape[0]
    nt = t // ts
    n_chunks = ts // SLOT_CHUNK
    grid_spec = pltpu.PrefetchScalarGridSpec(
        num_scalar_prefetch=2,
        grid=(nt,),
        in_specs=[
            pl.BlockSpec((ts, D_MODEL), lambda i, b_, c_: (i, 0)),
            pl.BlockSpec((1, N_EXPERTS, ts), lambda i, b_, c_: (i, 0, 0)),
        ],
        out_specs=pl.BlockSpec(memory_space=pl.ANY),
        scratch_shapes=[
            pltpu.VMEM((N_EXPERTS, n_chunks, SLOT_CHUNK, D_MODEL), BF16),
            pltpu.VMEM((SLOT_CHUNK, D_MODEL), BF16),
            pltpu.SemaphoreType.DMA((N_EXPERTS, n_chunks)),
            pltpu.SemaphoreType.DMA((N_EXPERTS,)),
        ],
    )
    return pl.pallas_call(
        functools.partial(_dispatch_kernel, ts=ts, zero_rows=zero_rows),
        grid_spec=grid_spec,
        out_shape=jax.ShapeDtypeStruct((N_EXPERTS, cap, D_MODEL), BF16),
        compiler_params=_cparams(1),
        name="moe_dispatch",
    )(base, cnt, h2, relt)


def _expert_kernel(te_ref, tr_ref, nv_ref, x_ref, wg_ref, wu_ref, wd_ref, o_ref, acc_sc):
    g_i = pl.program_id(0)
    k = pl.program_id(1)

    @pl.when(g_i < nv_ref[0])
    def _():
        x = x_ref[0]
        g = _dot(x, wg_ref[0])
        u = _dot(x, wu_ref[0])
        part = _dot((_silu(g) * u).astype(BF16), wd_ref[0])

        @pl.when(k == 0)
        def _():
            acc_sc[...] = part

        @pl.when(k > 0)
        def _():
            acc_sc[...] += part

        @pl.when(k == pl.num_programs(1) - 1)
        def _():
            o_ref[0] = acc_sc[...].astype(BF16)


def _expert_ffn(tile_e, tile_r, n_valid, xs, wg, wu, wd, tm, tf, n_steps):
    nk = D_FF_EXPERT // tf
    cap = xs.shape[1]

    def k_eff(g, k, nv):
        return jnp.where(g < nv[0], k, nk - 1)

    grid_spec = pltpu.PrefetchScalarGridSpec(
        num_scalar_prefetch=3,
        grid=(n_steps, nk),
        in_specs=[
            pl.BlockSpec((1, tm, D_MODEL), lambda g, k, te, tr, nv: (te[g], tr[g], 0)),
            pl.BlockSpec((1, D_MODEL, tf), lambda g, k, te, tr, nv: (te[g], 0, k_eff(g, k, nv))),
            pl.BlockSpec((1, D_MODEL, tf), lambda g, k, te, tr, nv: (te[g], 0, k_eff(g, k, nv))),
            pl.BlockSpec((1, tf, D_MODEL), lambda g, k, te, tr, nv: (te[g], k_eff(g, k, nv), 0)),
        ],
        out_specs=pl.BlockSpec((1, tm, D_MODEL), lambda g, k, te, tr, nv: (te[g], tr[g], 0)),
        scratch_shapes=[pltpu.VMEM((tm, D_MODEL), F32)],
    )
    return pl.pallas_call(
        _expert_kernel,
        grid_spec=grid_spec,
        out_shape=jax.ShapeDtypeStruct((N_EXPERTS, cap, D_MODEL), BF16),
        compiler_params=_cparams(2),
        name="moe_experts",
    )(tile_e, tile_r, n_valid, xs, wg, wu, wd)


def _combine_kernel(base_ref, cnt_ref, ys_ref, rel_ref, gate_ref, x1_ref, mod_ref, lng_ref, lnb_ref, o_ref,
                    slab, acc_sc, sems, *, ts):
    i = pl.program_id(0)
    n_chunks = ts // SLOT_CHUNK
    slot = lax.broadcasted_iota(jnp.int32, (ts, SLOT_CHUNK), 1).astype(F32)
    acc_sc[...] = jnp.zeros(acc_sc.shape, F32)

    def chunk_copy(e, c):
        a = pl.multiple_of(base_ref[i * N_EXPERTS + e], ROW_ALIGN)
        src = ys_ref.at[e, pl.ds(a + c * SLOT_CHUNK, SLOT_CHUNK), :]
        return _group_copy(src, slab.at[e, c], sems.at[e, c])

    for e in range(N_EXPERTS):
        n = cnt_ref[i * N_EXPERTS + e]
        for c in range(n_chunks):
            @pl.when(n > c * SLOT_CHUNK)
            def _(e=e, c=c):
                chunk_copy(e, c).start()

    for e in range(N_EXPERTS):
        n = cnt_ref[i * N_EXPERTS + e]
        for c in range(n_chunks):
            @pl.when(n > c * SLOT_CHUNK)
            def _(e=e, c=c):
                chunk_copy(e, c).wait()
                onehot = jnp.where(rel_ref[:, e:e + 1] == slot + float(c * SLOT_CHUNK), 1.0, 0.0).astype(BF16)
                acc_sc[...] += _dot(onehot, slab[e, c]) * gate_ref[:, e:e + 1]

    gate2 = mod_ref[0, 5:6, :]
    o_ref[...] = _layer_norm(DEEPNORM_ALPHA * x1_ref[...] + gate2 * acc_sc[...], lng_ref[...], lnb_ref[...])


def _combine(base, cnt, ys, rel, gates, x1, mod_l, ln_g, ln_b, ts, tiles_per_batch):
    t = x1.shape[0]
    n_chunks = ts // SLOT_CHUNK
    grid_spec = pltpu.PrefetchScalarGridSpec(
        num_scalar_prefetch=2,
        grid=(t // ts,),
        in_specs=[
            pl.BlockSpec(memory_space=pl.ANY),
            pl.BlockSpec((ts, LANES), lambda i, b_, c_: (i, 0)),
            pl.BlockSpec((ts, LANES), lambda i, b_, c_: (i, 0)),
            pl.BlockSpec((ts, D_MODEL), lambda i, b_, c_: (i, 0)),
            pl.BlockSpec((1, 8, D_MODEL), lambda i, b_, c_: (i // tiles_per_batch, 0, 0)),
            pl.BlockSpec((1, D_MODEL), lambda i, b_, c_: (0, 0)),
            pl.BlockSpec((1, D_MODEL), lambda i, b_, c_: (0, 0)),
        ],
        out_specs=pl.BlockSpec((ts, D_MODEL), lambda i, b_, c_: (i, 0)),
        scratch_shapes=[
            pltpu.VMEM((N_EXPERTS, n_chunks, SLOT_CHUNK, D_MODEL), BF16),
            pltpu.VMEM((ts, D_MODEL), F32),
            pltpu.SemaphoreType.DMA((N_EXPERTS, n_chunks)),
        ],
    )
    return pl.pallas_call(
        functools.partial(_combine_kernel, ts=ts),
        grid_spec=grid_spec,
        out_shape=jax.ShapeDtypeStruct((t, D_MODEL), F32),
        compiler_params=_cparams(1),
        name="moe_combine",
    )(base, cnt, ys, rel, gates, x1, mod_l, ln_g, ln_b)


def _expert_tile_plan(base, cnt, nt, tm, n_steps):
    last_base = base.reshape(nt, N_EXPERTS)[-1]
    last_cnt = cnt.reshape(nt, N_EXPERTS)[-1]
    total = last_base + ((last_cnt + (ROW_ALIGN - 1)) // ROW_ALIGN) * ROW_ALIGN
    tiles = (total + SLOT_CHUNK + tm - 1) // tm
    ends = jnp.cumsum(tiles)
    n_valid = ends[-1]
    g = jnp.minimum(jnp.arange(n_steps, dtype=jnp.int32), n_valid - 1)
    tile_e = jnp.sum((g[:, None] >= ends[None, :]).astype(jnp.int32), axis=1)
    tile_r = g - (ends - tiles)[tile_e]
    return tile_e.astype(jnp.int32), tile_r.astype(jnp.int32), n_valid.reshape(1).astype(jnp.int32)


def _top2_moe(h2, x1, mod_l, routing, wg, wu, wd, ln_g, ln_b, ts, tiles_per_batch):
    rel, gates, relt, base3, cnt3 = routing
    t = h2.shape[0]
    nt = t // ts
    tm, tf = 1024, 512
    base = base3[:, 0, :N_EXPERTS].reshape(-1)
    cnt = cnt3[:, 0, :N_EXPERTS].reshape(-1)
    zero_rows = tm + SLOT_CHUNK
    rows_max = t + ROW_ALIGN * nt
    cap = -(-(rows_max + zero_rows) // tm) * tm
    rows_all = 2 * t + ROW_ALIGN * N_EXPERTS * nt
    n_steps = -(-(rows_all + N_EXPERTS * SLOT_CHUNK) // tm) + N_EXPERTS
    xs = _dispatch(base, cnt, h2, relt, ts, cap, zero_rows)
    tile_e, tile_r, n_valid = _expert_tile_plan(base, cnt, nt, tm, n_steps)
    ys = _expert_ffn(tile_e, tile_r, n_valid, xs, wg, wu, wd, tm, tf, n_steps)
    return _combine(base, cnt, ys, rel, gates, x1, mod_l, ln_g, ln_b, ts, tiles_per_batch)


def kernel(x, c, positions, w_in, q_norm_g, kv_norm_g, w_uq, w_ukv, conv_w, mix_norm_g, w_o, w_ada, b_ada,
           ln1_g, ln1_b, ln2_g, ln2_b, ffn_w_gate, ffn_w_up, ffn_w_down, moe_router_w, moe_router_b,
           moe_w_gate, moe_w_up, moe_w_down):
    b, s, _ = x.shape
    n_layers = w_in.shape[0]
    ts = min(512, s)
    t = b * s
    tiles_per_batch = s // ts

    bp = -(-b // 16) * 16
    c_pad = jnp.pad(c, ((0, bp - b), (0, 0)))
    mod = _ada_mod(c_pad, w_ada, b_ada)
    mod = jnp.pad(jnp.transpose(mod[:, :, :b, :], (0, 2, 1, 3)), ((0, 0), (0, 0), (0, 2), (0, 0)))
    rope_tab = _rope_tables(positions)

    row = lambda v: v.reshape(1, -1)
    for l in range(n_layers):
        o3 = Q_LORA + KV_LORA + QK_ROPE
        w_in_p = jnp.concatenate(
            [w_in[l][:, :o3], jnp.zeros((D_MODEL, D_IN_PAD - w_in.shape[2]), F32), w_in[l][:, o3:]], axis=1
        ).astype(BF16)
        w_uq_p = jnp.pad(
            w_uq[l].reshape(Q_LORA, N_HEADS, QK_NOPE + QK_ROPE), ((0, 0), (0, 0), (0, QK_PAD - QK_NOPE - QK_ROPE))
        ).reshape(Q_LORA, N_HEADS * QK_PAD).astype(BF16)
        conv_w_p = jnp.pad(conv_w[l], ((0, 8 - CONV_WIDTH), (0, 0)))
        q, k, v, convn = _inproj(x, mod[l], w_in_p, w_uq_p, w_ukv[l].astype(BF16), row(q_norm_g[l]),
                                 row(kv_norm_g[l]), conv_w_p, row(mix_norm_g[l][D_ATTN:]), rope_tab, ts)
        attn = _attention(q, k, v, ts)
        common = (attn, convn, x, mod[l], w_o[l].astype(BF16), row(mix_norm_g[l][:D_ATTN]), row(ln1_g[l]),
                  row(ln1_b[l]), ts)
        if l % 2 == 0:
            i = l // 2
            x1, h2 = _outproj(*common)
            x = _dense_ffn(h2.reshape(t, D_MODEL), x1.reshape(t, D_MODEL), mod[l], ffn_w_gate[i].astype(BF16),
                           ffn_w_up[i].astype(BF16), ffn_w_down[i].astype(BF16), row(ln2_g[l]), row(ln2_b[l]),
                           ts, tiles_per_batch)
        else:
            i = l // 2
            rw = jnp.pad(moe_router_w[i], ((0, 0), (0, LANES - N_EXPERTS)))
            rb = jnp.pad(moe_router_b[i], (0, LANES - N_EXPERTS)).reshape(1, LANES)
            x1, h2, *routing = _outproj(*common, router=(rw, rb))
            x = _top2_moe(h2.reshape(t, D_MODEL), x1.reshape(t, D_MODEL), mod[l], routing,
                          moe_w_gate[i].astype(BF16), moe_w_up[i].astype(BF16), moe_w_down[i].astype(BF16),
                          row(ln2_g[l]), row(ln2_b[l]), ts, tiles_per_batch)
        x = x.reshape(b, s, D_MODEL)
    return x
```

```python
import functools

import jax
import jax.numpy as jnp
from jax import lax
from jax.experimental import pallas as pl
from jax.experimental.pallas import tpu as pltpu

F32 = jnp.float32
BF16 = jnp.bfloat16

D_MODEL = 1024
N_HEADS = 4
QK_NOPE = 128
QK_ROPE = 64
V_HEAD = 128
Q_LORA = 256
KV_LORA = 128
D_ATTN = N_HEADS * V_HEAD
D_CONV = D_MODEL - D_ATTN
CONV_WIDTH = 3
ROPE_BASE = 10000.0
D_FF = 2816
N_EXPERTS = 8
D_FF_EXPERT = 3584
RMS_EPS = 1e-6
LN_EPS = 1e-5
DEPTH = 2
DEEPNORM_ALPHA = (2 * DEPTH) ** 0.25
SM_SCALE = (QK_NOPE + QK_ROPE) ** -0.5
LOG2E = 1.4426950408889634
Q_SCALE = SM_SCALE * LOG2E

LANES = 128
QK_PAD = 2 * LANES
D_IN_PAD = 2048
ROW_ALIGN = 16
SLOT_CHUNK = 256
VMEM_LIMIT = 56 * 1024 * 1024
NEG_BIG = -1e30


def _cparams(n_axes):
    return pltpu.CompilerParams(dimension_semantics=("arbitrary",) * n_axes, vmem_limit_bytes=VMEM_LIMIT)


def _dot(a, b):
    return jnp.dot(a, b, preferred_element_type=F32)


def _rms(x, g):
    return x * lax.rsqrt(jnp.mean(x * x, axis=-1, keepdims=True) + RMS_EPS) * g


def _layer_norm(x, g, b):
    mu = jnp.mean(x, axis=-1, keepdims=True)
    xc = x - mu
    var = jnp.mean(xc * xc, axis=-1, keepdims=True)
    return xc * lax.rsqrt(var + LN_EPS) * g + b


def _silu(x):
    return x * jax.nn.sigmoid(x)


def _ada_kernel(c_ref, w_ref, b_ref, o_ref):
    ca = _silu(c_ref[...]).astype(BF16)
    o_ref[0, 0] = _dot(ca, w_ref[0].astype(BF16)) + b_ref[0]


def _ada_mod(c_pad, w_ada, b_ada):
    n_layers = w_ada.shape[0]
    bp = c_pad.shape[0]
    return pl.pallas_call(
        _ada_kernel,
        grid=(n_layers, 6),
        in_specs=[
            pl.BlockSpec((bp, D_MODEL), lambda l, j: (0, 0)),
            pl.BlockSpec((1, D_MODEL, D_MODEL), lambda l, j: (l, 0, j)),
            pl.BlockSpec((1, 1, D_MODEL), lambda l, j: (l, 0, j)),
        ],
        out_specs=pl.BlockSpec((1, 1, bp, D_MODEL), lambda l, j: (l, j, 0, 0)),
        out_shape=jax.ShapeDtypeStruct((n_layers, 6, bp, D_MODEL), F32),
        compiler_params=_cparams(2),
        name="ada_mod",
    )(c_pad, w_ada, b_ada.reshape(n_layers, 1, 6 * D_MODEL))


def _rope_kernel(pos_ref, f_ref, o_ref):
    ang = pos_ref[...].astype(F32) * f_ref[0:1, :]
    c = jnp.cos(ang)
    s = jnp.sin(ang)
    o_ref[:, 0:LANES] = c * f_ref[1:2, :]
    o_ref[:, LANES:2 * LANES] = s * f_ref[2:3, :]
    o_ref[:, 2 * LANES:3 * LANES] = s * f_ref[3:4, :]


def _rope_tables(positions):
    t = positions.size
    tr = min(1024, t)
    half = QK_ROPE // 2
    inv_freq = ROPE_BASE ** (-jnp.arange(0, QK_ROPE, 2, dtype=F32) / QK_ROPE)
    zeros = jnp.zeros((half,), F32)
    ones = jnp.ones((half,), F32)
    consts = jnp.stack([
        jnp.concatenate([inv_freq, inv_freq, zeros, zeros]),
        jnp.concatenate([ones, ones, zeros, zeros]),
        jnp.concatenate([-ones, zeros, zeros, zeros]),
        jnp.concatenate([zeros, ones, zeros, zeros]),
    ])
    consts = jnp.pad(consts, ((0, 4), (0, 0)))
    return pl.pallas_call(
        _rope_kernel,
        grid=(t // tr,),
        in_specs=[pl.BlockSpec((tr, 1), lambda i: (i, 0)), pl.BlockSpec((8, LANES), lambda i: (0, 0))],
        out_specs=pl.BlockSpec((tr, 3 * LANES), lambda i: (i, 0)),
        out_shape=jax.ShapeDtypeStruct((t, 3 * LANES), F32),
        compiler_params=_cparams(1),
        name="rope_tables",
    )(positions.reshape(t, 1), consts)


def _inproj_kernel(x_ref, mod_ref, win_ref, wuq_ref, wukv_ref, qg_ref, kvg_ref, cw_ref, cg_ref, rope_ref,
                   q_ref, k_ref, vt_ref, cv_ref, cu_ext, *, ts):
    j = pl.program_id(1)
    x = x_ref[0]
    shift = mod_ref[0, 0:1, :]
    scale = mod_ref[0, 1:2, :]
    h = (x * (1.0 + scale) + shift).astype(BF16)
    z = _dot(h, win_ref[...])
    c_q = z[:, 0:Q_LORA]
    c_kv = z[:, Q_LORA:Q_LORA + KV_LORA]
    k_rope = z[:, 384:512]
    gate_b = z[:, 512:1024]
    gate_c = z[:, 1024:1536]
    u = z[:, 1536:2048]

    q = _dot(_rms(c_q, qg_ref[...]).astype(BF16), wuq_ref[...])
    kv = _dot(_rms(c_kv, kvg_ref[...]).astype(BF16), wukv_ref[...])

    cos_d = rope_ref[:, 0:LANES]
    sin_a = rope_ref[:, LANES:2 * LANES]
    sin_b = rope_ref[:, 2 * LANES:3 * LANES]

    def rope(t):
        return t * cos_d + pltpu.roll(t, 96, 1) * sin_a + pltpu.roll(t, 32, 1) * sin_b

    k_rot = rope(k_rope).astype(BF16)
    for hh in range(N_HEADS):
        qo = hh * QK_PAD
        q_ref[0, hh, :, 0:LANES] = (q[:, qo:qo + LANES] * Q_SCALE).astype(BF16)
        q_ref[0, hh, :, LANES:QK_PAD] = (rope(q[:, qo + LANES:qo + QK_PAD]) * Q_SCALE).astype(BF16)
        ko = hh * (QK_NOPE + V_HEAD)
        k_ref[0, hh, :, 0:LANES] = kv[:, ko:ko + QK_NOPE].astype(BF16)
        k_ref[0, hh, :, LANES:QK_PAD] = k_rot
        vt_ref[0, hh, 0] = jnp.transpose(kv[:, ko + QK_NOPE:ko + QK_NOPE + V_HEAD]).astype(BF16)

    @pl.when(j == 0)
    def _():
        cu_ext[0:8, :] = jnp.zeros((8, D_CONV), F32)

    @pl.when(j > 0)
    def _():
        cu_ext[0:8, :] = cu_ext[ts:ts + 8, :]

    cu_ext[8:ts + 8, :] = gate_c * u
    conv = (cw_ref[2:3, :] * cu_ext[8:ts + 8, :]
            + cw_ref[1:2, :] * cu_ext[pl.ds(7, ts), :]
            + cw_ref[0:1, :] * cu_ext[pl.ds(6, ts), :])
    conv = gate_b * conv
    cv_ref[0] = _rms(conv, cg_ref[...]).astype(BF16)


def _inproj(x, mod_l, w_in_p, w_uq_p, w_ukv, qg, kvg, conv_w, conv_g, rope_tab, ts):
    b, s, _ = x.shape
    nsb = s // ts
    const = lambda shape: pl.BlockSpec(shape, lambda bi, j: (0,) * len(shape))
    return pl.pallas_call(
        functools.partial(_inproj_kernel, ts=ts),
        grid=(b, nsb),
        in_specs=[
            pl.BlockSpec((1, ts, D_MODEL), lambda bi, j: (bi, j, 0)),
            pl.BlockSpec((1, 8, D_MODEL), lambda bi, j: (bi, 0, 0)),
            const((D_MODEL, D_IN_PAD)),
            const((Q_LORA, N_HEADS * QK_PAD)),
            const((KV_LORA, N_HEADS * (QK_NOPE + V_HEAD))),
            const((1, Q_LORA)),
            const((1, KV_LORA)),
            const((8, D_CONV)),
            const((1, D_CONV)),
            pl.BlockSpec((ts, 3 * LANES), lambda bi, j: (bi * nsb + j, 0)),
        ],
        out_specs=[
            pl.BlockSpec((1, N_HEADS, ts, QK_PAD), lambda bi, j: (bi, 0, j, 0)),
            pl.BlockSpec((1, N_HEADS, ts, QK_PAD), lambda bi, j: (bi, 0, j, 0)),
            pl.BlockSpec((1, N_HEADS, 1, V_HEAD, ts), lambda bi, j: (bi, 0, j, 0, 0)),
            pl.BlockSpec((1, ts, D_CONV), lambda bi, j: (bi, j, 0)),
        ],
        out_shape=[
            jax.ShapeDtypeStruct((b, N_HEADS, s, QK_PAD), BF16),
            jax.ShapeDtypeStruct((b, N_HEADS, s, QK_PAD), BF16),
            jax.ShapeDtypeStruct((b, N_HEADS, nsb, V_HEAD, ts), BF16),
            jax.ShapeDtypeStruct((b, s, D_CONV), BF16),
        ],
        scratch_shapes=[pltpu.VMEM((ts + 8, D_CONV), F32)],
        compiler_params=_cparams(2),
        name="inproj",
    )(x, mod_l, w_in_p, w_uq_p, w_ukv, qg, kvg, conv_w, conv_g, rope_tab)


Q_CHUNK = 256


K_ROWS = 64


def _attn_kernel(q_ref, k_ref, vt_ref, o_ref, *scratch, tq):
    i = pl.program_id(1)
    n_chunks = tq // Q_CHUNK
    chains = [(hh, c) + tuple(scratch[5 * (hh * n_chunks + c):5 * (hh * n_chunks + c) + 5])
              for hh in range(N_HEADS) for c in range(n_chunks)]
    for _, _, m_sc, l_sc, acc_sc, _, _ in chains:
        m_sc[...] = jnp.full(m_sc.shape, NEG_BIG, F32)
        l_sc[...] = jnp.zeros(l_sc.shape, F32)
        acc_sc[...] = jnp.zeros(acc_sc.shape, F32)

    def step(jk, diagonal):
        start = pl.multiple_of(jk * tq, tq)
        n_keys = lambda c: (c + 1) * Q_CHUNK if diagonal else tq
        for hh, c, _, _, _, s_sc, _ in chains:
            nk = n_keys(c)
            kj = k_ref[0, hh, pl.ds(start, nk), :]
            qc = q_ref[0, hh, c * Q_CHUNK:(c + 1) * Q_CHUNK, :]
            s_sc[0:nk, :] = lax.dot_general(kj, qc, (((1,), (1,)), ((), ())), preferred_element_type=F32)
        for hh, c, m_sc, l_sc, acc_sc, s_sc, p_sc in chains:
            nk = n_keys(c)
            if diagonal:
                d0 = nk - Q_CHUNK
                key = lax.broadcasted_iota(jnp.int32, (Q_CHUNK, Q_CHUNK), 0)
                qry = lax.broadcasted_iota(jnp.int32, (Q_CHUNK, Q_CHUNK), 1)
                s_sc[d0:nk, :] = jnp.where(key <= qry, s_sc[d0:nk, :], NEG_BIG)
            blk_max = s_sc[0:K_ROWS, :]
            for r in range(K_ROWS, nk, K_ROWS):
                blk_max = jnp.maximum(blk_max, s_sc[r:r + K_ROWS, :])
            m_prev = m_sc[...]
            m_new = jnp.maximum(m_prev, jnp.max(blk_max, axis=0, keepdims=True))
            alpha = jnp.exp2(m_prev - m_new)
            p_sum = jnp.zeros((K_ROWS, Q_CHUNK), F32)
            for r in range(0, nk, K_ROWS):
                p = jnp.exp2(s_sc[r:r + K_ROWS, :] - m_new)
                p_sum = p_sum + p
                p_sc[r:r + K_ROWS, :] = p.astype(BF16)
            l_sc[...] = alpha * l_sc[...] + jnp.sum(p_sum, axis=0, keepdims=True)
            acc_sc[...] = alpha * acc_sc[...] + _dot(vt_ref[0, hh, jk, :, 0:nk], p_sc[0:nk, :])
            m_sc[...] = m_new

    def body(jk, carry):
        step(jk, False)
        return carry

    lax.fori_loop(0, i, body, 0)
    step(i, True)
    for hh, c, _, l_sc, acc_sc, _, _ in chains:
        o_ref[0, c * Q_CHUNK:(c + 1) * Q_CHUNK, hh * V_HEAD:(hh + 1) * V_HEAD] = (
            jnp.transpose(acc_sc[...] / l_sc[...]).astype(BF16))


def _attention(q, k, vt, tq):
    b, nh, s, _ = q.shape
    return pl.pallas_call(
        functools.partial(_attn_kernel, tq=tq),
        grid=(b, s // tq),
        in_specs=[
            pl.BlockSpec((1, nh, tq, QK_PAD), lambda bi, i: (bi, 0, i, 0)),
            pl.BlockSpec((1, nh, s, QK_PAD), lambda bi, i: (bi, 0, 0, 0)),
            pl.BlockSpec((1, nh, s // tq, V_HEAD, tq), lambda bi, i: (bi, 0, 0, 0, 0)),
        ],
        out_specs=pl.BlockSpec((1, tq, D_ATTN), lambda bi, i: (bi, i, 0)),
        out_shape=jax.ShapeDtypeStruct((b, s, D_ATTN), BF16),
        scratch_shapes=[
            pltpu.VMEM((1, Q_CHUNK), F32), pltpu.VMEM((1, Q_CHUNK), F32), pltpu.VMEM((V_HEAD, Q_CHUNK), F32),
            pltpu.VMEM((tq, Q_CHUNK), F32), pltpu.VMEM((tq, Q_CHUNK), BF16),
        ] * (nh * (tq // Q_CHUNK)),
        compiler_params=_cparams(2),
        name="attention",
    )(q, k, vt)


def _outproj_core(attn_ref, cv_ref, x_ref, mod_ref, wo_ref, ga_ref, lng_ref, lnb_ref):
    a = attn_ref[0].astype(F32)
    an = _rms(a, ga_ref[...]).astype(BF16)
    y = _dot(an, wo_ref[0:D_ATTN, :]) + _dot(cv_ref[0], wo_ref[D_ATTN:D_MODEL, :])
    gate1 = mod_ref[0, 2:3, :]
    x1 = _layer_norm(DEEPNORM_ALPHA * x_ref[0] + gate1 * y, lng_ref[...], lnb_ref[...])
    h2 = x1 * (1.0 + mod_ref[0, 4:5, :]) + mod_ref[0, 3:4, :]
    return x1, h2


def _outproj_kernel(attn_ref, cv_ref, x_ref, mod_ref, wo_ref, ga_ref, lng_ref, lnb_ref, x1_ref, h2_ref):
    x1, h2 = _outproj_core(attn_ref, cv_ref, x_ref, mod_ref, wo_ref, ga_ref, lng_ref, lnb_ref)
    x1_ref[0] = x1
    h2_ref[0] = h2.astype(BF16)


def _outproj_router_kernel(attn_ref, cv_ref, x_ref, mod_ref, wo_ref, ga_ref, lng_ref, lnb_ref, rw_ref, rb_ref,
                           x1_ref, h2_ref, rel_ref, gate_ref, relt_ref, base_ref, cnt_ref, run_sc, *, ts):
    first = jnp.logical_and(pl.program_id(0) == 0, pl.program_id(1) == 0)

    @pl.when(first)
    def _():
        run_sc[...] = jnp.zeros(run_sc.shape, F32)

    x1, h2 = _outproj_core(attn_ref, cv_ref, x_ref, mod_ref, wo_ref, ga_ref, lng_ref, lnb_ref)
    x1_ref[0] = x1
    h2_ref[0] = h2.astype(BF16)

    lane = lax.broadcasted_iota(jnp.int32, (ts, LANES), 1)
    logits = jnp.dot(h2, rw_ref[...], preferred_element_type=F32, precision=lax.Precision.HIGHEST) + rb_ref[...]
    logits = jnp.where(lane < N_EXPERTS, logits, NEG_BIG)
    v1 = jnp.max(logits, axis=-1, keepdims=True)
    i1 = jnp.min(jnp.where(logits == v1, lane, LANES), axis=-1, keepdims=True)
    rest = jnp.where(lane == i1, NEG_BIG, logits)
    v2 = jnp.max(rest, axis=-1, keepdims=True)
    i2 = jnp.min(jnp.where(rest == v2, lane, LANES), axis=-1, keepdims=True)
    e21 = jnp.exp(v2 - v1)
    g1 = 1.0 / (1.0 + e21)
    g2 = e21 / (1.0 + e21)
    is1 = lane == i1
    is2 = lane == i2
    sel = jnp.where(jnp.logical_or(is1, is2), 1.0, 0.0)
    gate_ref[...] = jnp.where(is1, g1, 0.0) + jnp.where(is2, g2, 0.0)

    r_i = lax.broadcasted_iota(jnp.int32, (ts, ts), 0)
    c_i = lax.broadcasted_iota(jnp.int32, (ts, ts), 1)
    tri = jnp.where(c_i < r_i, 1.0, 0.0).astype(BF16)
    rank = _dot(tri, sel.astype(BF16))
    rel = jnp.where(sel > 0.0, rank, -1.0)
    rel_ref[...] = rel
    relt_ref[0] = jnp.transpose(rel)[0:N_EXPERTS, :]

    n_tile = jnp.sum(sel, axis=0, keepdims=True)
    n_pad = jnp.floor((n_tile + (ROW_ALIGN - 1)) * (1.0 / ROW_ALIGN)) * ROW_ALIGN
    base_ref[0] = run_sc[...].astype(jnp.int32)
    cnt_ref[0] = n_tile.astype(jnp.int32)
    run_sc[...] = run_sc[...] + n_pad


def _outproj(attn, convn, x, mod_l, w_o, g_attn, ln_g, ln_b, ts, router=None):
    b, s, _ = x.shape
    nsb = s // ts
    const = lambda shape: pl.BlockSpec(shape, lambda bi, j: (0,) * len(shape))
    tile3 = lambda w: pl.BlockSpec((1, ts, w), lambda bi, j: (bi, j, 0))
    in_specs = [
        tile3(D_ATTN), tile3(D_CONV), tile3(D_MODEL),
        pl.BlockSpec((1, 8, D_MODEL), lambda bi, j: (bi, 0, 0)),
        const((D_MODEL, D_MODEL)), const((1, D_ATTN)), const((1, D_MODEL)), const((1, D_MODEL)),
    ]
    out_specs = [tile3(D_MODEL), tile3(D_MODEL)]
    out_shape = [jax.ShapeDtypeStruct((b, s, D_MODEL), F32), jax.ShapeDtypeStruct((b, s, D_MODEL), BF16)]
    args = [attn, convn, x, mod_l, w_o, g_attn, ln_g, ln_b]
    if router is None:
        return pl.pallas_call(
            _outproj_kernel, grid=(b, nsb), in_specs=in_specs, out_specs=out_specs, out_shape=out_shape,
            compiler_params=_cparams(2), name="outproj",
        )(*args)
    rw, rb = router
    nt = b * nsb
    flat = lambda w: pl.BlockSpec((ts, w), lambda bi, j: (bi * nsb + j, 0))
    per_tile = lambda r, w: pl.BlockSpec((1, r, w), lambda bi, j: (bi * nsb + j, 0, 0))
    in_specs += [const((D_MODEL, LANES)), const((1, LANES))]
    out_specs += [flat(LANES), flat(LANES), per_tile(N_EXPERTS, ts), per_tile(1, LANES), per_tile(1, LANES)]
    out_shape += [
        jax.ShapeDtypeStruct((nt * ts, LANES), F32), jax.ShapeDtypeStruct((nt * ts, LANES), F32),
        jax.ShapeDtypeStruct((nt, N_EXPERTS, ts), F32),
        jax.ShapeDtypeStruct((nt, 1, LANES), jnp.int32), jax.ShapeDtypeStruct((nt, 1, LANES), jnp.int32),
    ]
    return pl.pallas_call(
        functools.partial(_outproj_router_kernel, ts=ts), grid=(b, nsb), in_specs=in_specs, out_specs=out_specs,
        out_shape=out_shape, scratch_shapes=[pltpu.VMEM((1, LANES), F32)],
        compiler_params=_cparams(2), name="outproj_router",
    )(*args, rw, rb)


def _ffn_chunks(width, step):
    return [(c, min(step, width - c)) for c in range(0, width, step)]


def _ffn_kernel(h_ref, x1_ref, mod_ref, wg_ref, wu_ref, wd_ref, lng_ref, lnb_ref, o_ref, acc_sc):
    h = h_ref[...]
    for idx, (c, w) in enumerate(_ffn_chunks(D_FF, 512)):
        g = _dot(h, wg_ref[:, c:c + w])
        u = _dot(h, wu_ref[:, c:c + w])
        part = _dot((_silu(g) * u).astype(BF16), wd_ref[c:c + w, :])
        if idx == 0:
            acc_sc[...] = part
        else:
            acc_sc[...] += part
    gate2 = mod_ref[0, 5:6, :]
    o_ref[...] = _layer_norm(DEEPNORM_ALPHA * x1_ref[...] + gate2 * acc_sc[...], lng_ref[...], lnb_ref[...])


def _dense_ffn(h2, x1, mod_l, wg, wu, wd, ln_g, ln_b, tm, tiles_per_batch):
    t = h2.shape[0]
    const = lambda shape: pl.BlockSpec(shape, lambda i: (0,) * len(shape), pipeline_mode=pl.Buffered(1))
    return pl.pallas_call(
        _ffn_kernel,
        grid=(t // tm,),
        in_specs=[
            pl.BlockSpec((tm, D_MODEL), lambda i: (i, 0)),
            pl.BlockSpec((tm, D_MODEL), lambda i: (i, 0)),
            pl.BlockSpec((1, 8, D_MODEL), lambda i: (i // tiles_per_batch, 0, 0)),
            const((D_MODEL, D_FF)), const((D_MODEL, D_FF)), const((D_FF, D_MODEL)),
            const((1, D_MODEL)), const((1, D_MODEL)),
        ],
        out_specs=pl.BlockSpec((tm, D_MODEL), lambda i: (i, 0)),
        out_shape=jax.ShapeDtypeStruct((t, D_MODEL), F32),
        scratch_shapes=[pltpu.VMEM((tm, D_MODEL), F32)],
        compiler_params=_cparams(1),
        name="dense_ffn",
    )(h2, x1, mod_l, wg, wu, wd, ln_g, ln_b)


def _group_copy(src, dst, sem):
    return pltpu.make_async_copy(src, dst, sem)


def _dispatch_kernel(base_ref, cnt_ref, h_ref, relt_ref, xs_ref, stage, zbuf, sems, zsem, *, ts, zero_rows):
    i = pl.program_id(0)
    n_chunks = ts // SLOT_CHUNK
    h = h_ref[...]
    slot = lax.broadcasted_iota(jnp.int32, (SLOT_CHUNK, ts), 0).astype(F32)

    def chunk_copy(e, c):
        a = pl.multiple_of(base_ref[i * N_EXPERTS + e], ROW_ALIGN)
        dst = xs_ref.at[e, pl.ds(a + c * SLOT_CHUNK, SLOT_CHUNK), :]
        return _group_copy(stage.at[e, c], dst, sems.at[e, c])

    for e in range(N_EXPERTS):
        n = cnt_ref[i * N_EXPERTS + e]
        for c in range(n_chunks):
            @pl.when(n > c * SLOT_CHUNK)
            def _(e=e, c=c):
                onehot = jnp.where(relt_ref[0, e:e + 1, :] == slot + float(c * SLOT_CHUNK), 1.0, 0.0).astype(BF16)
                stage[e, c] = _dot(onehot, h).astype(BF16)
                chunk_copy(e, c).start()

    for e in range(N_EXPERTS):
        n = cnt_ref[i * N_EXPERTS + e]
        for c in range(n_chunks):
            @pl.when(n > c * SLOT_CHUNK)
            def _(e=e, c=c):
                chunk_copy(e, c).wait()

    @pl.when(i == pl.num_programs(0) - 1)
    def _():
        zbuf[...] = jnp.zeros(zbuf.shape, BF16)
        for e in range(N_EXPERTS):
            n = cnt_ref[i * N_EXPERTS + e]
            end = base_ref[i * N_EXPERTS + e] + ((n + (ROW_ALIGN - 1)) // ROW_ALIGN) * ROW_ALIGN
            end = pl.multiple_of(end, ROW_ALIGN)
            copies = [
                _group_copy(zbuf, xs_ref.at[e, pl.ds(end + r * SLOT_CHUNK, SLOT_CHUNK), :], zsem.at[e])
                for r in range(zero_rows // SLOT_CHUNK)
            ]
            for cp in copies:
                cp.start()
            for cp in copies:
                cp.wait()


def _dispatch(base, cnt, h2, relt, ts, cap, zero_rows):
    t = h2.shape[0]
    nt = t // ts
    n_chunks = ts // SLOT_CHUNK
    grid_spec = pltpu.PrefetchScalarGridSpec(
        num_scalar_prefetch=2,
        grid=(nt,),
        in_specs=[
            pl.BlockSpec((ts, D_MODEL), lambda i, b_, c_: (i, 0)),
            pl.BlockSpec((1, N_EXPERTS, ts), lambda i, b_, c_: (i, 0, 0)),
        ],
        out_specs=pl.BlockSpec(memory_space=pl.ANY),
        scratch_shapes=[
            pltpu.VMEM((N_EXPERTS, n_chunks, SLOT_CHUNK, D_MODEL), BF16),
            pltpu.VMEM((SLOT_CHUNK, D_MODEL), BF16),
            pltpu.SemaphoreType.DMA((N_EXPERTS, n_chunks)),
            pltpu.SemaphoreType.DMA((N_EXPERTS,)),
        ],
    )
    return pl.pallas_call(
        functools.partial(_dispatch_kernel, ts=ts, zero_rows=zero_rows),
        grid_spec=grid_spec,
        out_shape=jax.ShapeDtypeStruct((N_EXPERTS, cap, D_MODEL), BF16),
        compiler_params=_cparams(1),
        name="moe_dispatch",
    )(base, cnt, h2, relt)


def _expert_kernel(te_ref, tr_ref, nv_ref, x_ref, wg_ref, wu_ref, wd_ref, o_ref, acc_sc):
    g_i = pl.program_id(0)
    k = pl.program_id(1)

    @pl.when(g_i < nv_ref[0])
    def _():
        x = x_ref[0]
        g = _dot(x, wg_ref[0].astype(BF16))
        u = _dot(x, wu_ref[0].astype(BF16))
        part = _dot((_silu(g) * u).astype(BF16), wd_ref[0].astype(BF16))

        @pl.when(k == 0)
        def _():
            acc_sc[...] = part

        @pl.when(k > 0)
        def _():
            acc_sc[...] += part

        @pl.when(k == pl.num_programs(1) - 1)
        def _():
            o_ref[0] = acc_sc[...].astype(BF16)


def _expert_ffn(tile_e, tile_r, n_valid, xs, wg, wu, wd, tm, tf, n_steps):
    nk = D_FF_EXPERT // tf
    cap = xs.shape[1]

    def k_eff(g, k, nv):
        return jnp.where(g < nv[0], k, nk - 1)

    grid_spec = pltpu.PrefetchScalarGridSpec(
        num_scalar_prefetch=3,
        grid=(n_steps, nk),
        in_specs=[
            pl.BlockSpec((1, tm, D_MODEL), lambda g, k, te, tr, nv: (te[g], tr[g], 0)),
            pl.BlockSpec((1, D_MODEL, tf), lambda g, k, te, tr, nv: (te[g], 0, k_eff(g, k, nv))),
            pl.BlockSpec((1, D_MODEL, tf), lambda g, k, te, tr, nv: (te[g], 0, k_eff(g, k, nv))),
            pl.BlockSpec((1, tf, D_MODEL), lambda g, k, te, tr, nv: (te[g], k_eff(g, k, nv), 0)),
        ],
        out_specs=pl.BlockSpec((1, tm, D_MODEL), lambda g, k, te, tr, nv: (te[g], tr[g], 0)),
        scratch_shapes=[pltpu.VMEM((tm, D_MODEL), F32)],
    )
    return pl.pallas_call(
        _expert_kernel,
        grid_spec=grid_spec,
        out_shape=jax.ShapeDtypeStruct((N_EXPERTS, cap, D_MODEL), BF16),
        compiler_params=_cparams(2),
        name="moe_experts",
    )(tile_e, tile_r, n_valid, xs, wg, wu, wd)


def _combine_kernel(base_ref, cnt_ref, ys_ref, rel_ref, gate_ref, x1_ref, mod_ref, lng_ref, lnb_ref, o_ref,
                    slab, acc_sc, sems, *, ts):
    i = pl.program_id(0)
    n_chunks = ts // SLOT_CHUNK
    slot = lax.broadcasted_iota(jnp.int32, (ts, SLOT_CHUNK), 1).astype(F32)
    acc_sc[...] = jnp.zeros(acc_sc.shape, F32)

    def chunk_copy(e, c):
        a = pl.multiple_of(base_ref[i * N_EXPERTS + e], ROW_ALIGN)
        src = ys_ref.at[e, pl.ds(a + c * SLOT_CHUNK, SLOT_CHUNK), :]
        return _group_copy(src, slab.at[e, c], sems.at[e, c])

    for e in range(N_EXPERTS):
        n = cnt_ref[i * N_EXPERTS + e]
        for c in range(n_chunks):
            @pl.when(n > c * SLOT_CHUNK)
            def _(e=e, c=c):
                chunk_copy(e, c).start()

    for e in range(N_EXPERTS):
        n = cnt_ref[i * N_EXPERTS + e]
        for c in range(n_chunks):
            @pl.when(n > c * SLOT_CHUNK)
            def _(e=e, c=c):
                chunk_copy(e, c).wait()
                onehot = jnp.where(rel_ref[:, e:e + 1] == slot + float(c * SLOT_CHUNK), 1.0, 0.0).astype(BF16)
                acc_sc[...] += _dot(onehot, slab[e, c]) * gate_ref[:, e:e + 1]

    gate2 = mod_ref[0, 5:6, :]
    o_ref[...] = _layer_norm(DEEPNORM_ALPHA * x1_ref[...] + gate2 * acc_sc[...], lng_ref[...], lnb_ref[...])


def _combine(base, cnt, ys, rel, gates, x1, mod_l, ln_g, ln_b, ts, tiles_per_batch):
    t = x1.shape[0]
    n_chunks = ts // SLOT_CHUNK
    grid_spec = pltpu.PrefetchScalarGridSpec(
        num_scalar_prefetch=2,
        grid=(t // ts,),
        in_specs=[
            pl.BlockSpec(memory_space=pl.ANY),
            pl.BlockSpec((ts, LANES), lambda i, b_, c_: (i, 0)),
            pl.BlockSpec((ts, LANES), lambda i, b_, c_: (i, 0)),
            pl.BlockSpec((ts, D_MODEL), lambda i, b_, c_: (i, 0)),
            pl.BlockSpec((1, 8, D_MODEL), lambda i, b_, c_: (i // tiles_per_batch, 0, 0)),
            pl.BlockSpec((1, D_MODEL), lambda i, b_, c_: (0, 0)),
            pl.BlockSpec((1, D_MODEL), lambda i, b_, c_: (0, 0)),
        ],
        out_specs=pl.BlockSpec((ts, D_MODEL), lambda i, b_, c_: (i, 0)),
        scratch_shapes=[
            pltpu.VMEM((N_EXPERTS, n_chunks, SLOT_CHUNK, D_MODEL), BF16),
            pltpu.VMEM((ts, D_MODEL), F32),
            pltpu.SemaphoreType.DMA((N_EXPERTS, n_chunks)),
        ],
    )
    return pl.pallas_call(
        functools.partial(_combine_kernel, ts=ts),
        grid_spec=grid_spec,
        out_shape=jax.ShapeDtypeStruct((t, D_MODEL), F32),
        compiler_params=_cparams(1),
        name="moe_combine",
    )(base, cnt, ys, rel, gates, x1, mod_l, ln_g, ln_b)


def _expert_tile_plan(base, cnt, nt, tm, n_steps):
    last_base = base.reshape(nt, N_EXPERTS)[-1]
    last_cnt = cnt.reshape(nt, N_EXPERTS)[-1]
    total = last_base + ((last_cnt + (ROW_ALIGN - 1)) // ROW_ALIGN) * ROW_ALIGN
    tiles = (total + SLOT_CHUNK + tm - 1) // tm
    ends = jnp.cumsum(tiles)
    n_valid = ends[-1]
    g = jnp.minimum(jnp.arange(n_steps, dtype=jnp.int32), n_valid - 1)
    tile_e = jnp.sum((g[:, None] >= ends[None, :]).astype(jnp.int32), axis=1)
    tile_r = g - (ends - tiles)[tile_e]
    return tile_e.astype(jnp.int32), tile_r.astype(jnp.int32), n_valid.reshape(1).astype(jnp.int32)


def _top2_moe(h2, x1, mod_l, routing, wg, wu, wd, ln_g, ln_b, ts, tiles_per_batch):
    rel, gates, relt, base3, cnt3 = routing
    t = h2.shape[0]
    nt = t // ts
    tm, tf = 1024, 512
    base = base3[:, 0, :N_EXPERTS].reshape(-1)
    cnt = cnt3[:, 0, :N_EXPERTS].reshape(-1)
    zero_rows = tm + SLOT_CHUNK
    rows_max = t + ROW_ALIGN * nt
    cap = -(-(rows_max + zero_rows) // tm) * tm
    rows_all = 2 * t + ROW_ALIGN * N_EXPERTS * nt
    n_steps = -(-(rows_all + N_EXPERTS * SLOT_CHUNK) // tm) + N_EXPERTS
    xs = _dispatch(base, cnt, h2, relt, ts, cap, zero_rows)
    tile_e, tile_r, n_valid = _expert_tile_plan(base, cnt, nt, tm, n_steps)
    ys = _expert_ffn(tile_e, tile_r, n_valid, xs, wg, wu, wd, tm, tf, n_steps)
    return _combine(base, cnt, ys, rel, gates, x1, mod_l, ln_g, ln_b, ts, tiles_per_batch)


def kernel(x, c, positions, w_in, q_norm_g, kv_norm_g, w_uq, w_ukv, conv_w, mix_norm_g, w_o, w_ada, b_ada,
           ln1_g, ln1_b, ln2_g, ln2_b, ffn_w_gate, ffn_w_up, ffn_w_down, moe_router_w, moe_router_b,
           moe_w_gate, moe_w_up, moe_w_down):
    b, s, _ = x.shape
    n_layers = w_in.shape[0]
    ts = min(512, s)
    t = b * s
    tiles_per_batch = s // ts

    bp = -(-b // 16) * 16
    c_pad = jnp.pad(c, ((0, bp - b), (0, 0)))
    mod = _ada_mod(c_pad, w_ada, b_ada)
    mod = jnp.pad(jnp.transpose(mod[:, :, :b, :], (0, 2, 1, 3)), ((0, 0), (0, 0), (0, 2), (0, 0)))
    rope_tab = _rope_tables(positions)

    row = lambda v: v.reshape(1, -1)
    for l in range(n_layers):
        o3 = Q_LORA + KV_LORA + QK_ROPE
        w_in_p = jnp.concatenate(
            [w_in[l][:, :o3], jnp.zeros((D_MODEL, D_IN_PAD - w_in.shape[2]), F32), w_in[l][:, o3:]], axis=1
        ).astype(BF16)
        w_uq_p = jnp.pad(
            w_uq[l].reshape(Q_LORA, N_HEADS, QK_NOPE + QK_ROPE), ((0, 0), (0, 0), (0, QK_PAD - QK_NOPE - QK_ROPE))
        ).reshape(Q_LORA, N_HEADS * QK_PAD).astype(BF16)
        conv_w_p = jnp.pad(conv_w[l], ((0, 8 - CONV_WIDTH), (0, 0)))
        q, k, v, convn = _inproj(x, mod[l], w_in_p, w_uq_p, w_ukv[l].astype(BF16), row(q_norm_g[l]),
                                 row(kv_norm_g[l]), conv_w_p, row(mix_norm_g[l][D_ATTN:]), rope_tab, ts)
        attn = _attention(q, k, v, ts)
        common = (attn, convn, x, mod[l], w_o[l].astype(BF16), row(mix_norm_g[l][:D_ATTN]), row(ln1_g[l]),
                  row(ln1_b[l]), ts)
        if l % 2 == 0:
            i = l // 2
            x1, h2 = _outproj(*common)
            x = _dense_ffn(h2.reshape(t, D_MODEL), x1.reshape(t, D_MODEL), mod[l], ffn_w_gate[i].astype(BF16),
                           ffn_w_up[i].astype(BF16), ffn_w_down[i].astype(BF16), row(ln2_g[l]), row(ln2_b[l]),
                           ts, tiles_per_batch)
        else:
            i = l // 2
            rw = jnp.pad(moe_router_w[i], ((0, 0), (0, LANES - N_EXPERTS)))
            rb = jnp.pad(moe_router_b[i], (0, LANES - N_EXPERTS)).reshape(1, LANES)
            x1, h2, *routing = _outproj(*common, router=(rw, rb))
            x = _top2_moe(h2.reshape(t, D_MODEL), x1.reshape(t, D_MODEL), mod[l], routing,
                          moe_w_gate[i], moe_w_up[i], moe_w_down[i], row(ln2_g[l]), row(ln2_b[l]), ts, tiles_per_batch)
        x = x.reshape(b, s, D_MODEL)
    return x
```

```python
import functools

import jax
import jax.numpy as jnp
from jax import lax
from jax.experimental import pallas as pl
from jax.experimental.pallas import tpu as pltpu

F32 = jnp.float32
BF16 = jnp.bfloat16

D_MODEL = 1024
N_HEADS = 4
QK_NOPE = 128
QK_ROPE = 64
V_HEAD = 128
Q_LORA = 256
KV_LORA = 128
D_ATTN = N_HEADS * V_HEAD
D_CONV = D_MODEL - D_ATTN
CONV_WIDTH = 3
ROPE_BASE = 10000.0
D_FF = 2816
N_EXPERTS = 8
D_FF_EXPERT = 3584
RMS_EPS = 1e-6
LN_EPS = 1e-5
DEPTH = 2
DEEPNORM_ALPHA = (2 * DEPTH) ** 0.25
SM_SCALE = (QK_NOPE + QK_ROPE) ** -0.5
LOG2E = 1.4426950408889634
Q_SCALE = SM_SCALE * LOG2E

LANES = 128
QK_PAD = 2 * LANES
D_IN_PAD = 2048
ROW_ALIGN = 16
SLOT_CHUNK = 256
EXPERT_ROWS = 1152
VMEM_LIMIT = 56 * 1024 * 1024
NEG_BIG = -1e30


def _cparams(n_axes):
    return pltpu.CompilerParams(dimension_semantics=("arbitrary",) * n_axes, vmem_limit_bytes=VMEM_LIMIT)


def _dot(a, b):
    return jnp.dot(a, b, preferred_element_type=F32)


def _rms(x, g):
    return x * lax.rsqrt(jnp.mean(x * x, axis=-1, keepdims=True) + RMS_EPS) * g


def _layer_norm(x, g, b):
    mu = jnp.mean(x, axis=-1, keepdims=True)
    xc = x - mu
    var = jnp.mean(xc * xc, axis=-1, keepdims=True)
    return xc * lax.rsqrt(var + LN_EPS) * g + b


def _silu(x):
    return x * jax.nn.sigmoid(x)


def _ada_kernel(c_ref, w_ref, b_ref, o_ref):
    ca = _silu(c_ref[...]).astype(BF16)
    o_ref[0, 0] = _dot(ca, w_ref[0].astype(BF16)) + b_ref[0]


def _ada_mod(c_pad, w_ada, b_ada):
    n_layers = w_ada.shape[0]
    bp = c_pad.shape[0]
    return pl.pallas_call(
        _ada_kernel,
        grid=(n_layers, 6),
        in_specs=[
            pl.BlockSpec((bp, D_MODEL), lambda l, j: (0, 0)),
            pl.BlockSpec((1, D_MODEL, D_MODEL), lambda l, j: (l, 0, j)),
            pl.BlockSpec((1, 1, D_MODEL), lambda l, j: (l, 0, j)),
        ],
        out_specs=pl.BlockSpec((1, 1, bp, D_MODEL), lambda l, j: (l, j, 0, 0)),
        out_shape=jax.ShapeDtypeStruct((n_layers, 6, bp, D_MODEL), F32),
        compiler_params=_cparams(2),
        name="ada_mod",
    )(c_pad, w_ada, b_ada.reshape(n_layers, 1, 6 * D_MODEL))


def _rope_kernel(pos_ref, f_ref, o_ref):
    ang = pos_ref[...].astype(F32) * f_ref[0:1, :]
    c = jnp.cos(ang)
    s = jnp.sin(ang)
    o_ref[:, 0:LANES] = c * f_ref[1:2, :]
    o_ref[:, LANES:2 * LANES] = s * f_ref[2:3, :]
    o_ref[:, 2 * LANES:3 * LANES] = s * f_ref[3:4, :]


def _rope_tables(positions):
    t = positions.size
    tr = min(1024, t)
    half = QK_ROPE // 2
    inv_freq = ROPE_BASE ** (-jnp.arange(0, QK_ROPE, 2, dtype=F32) / QK_ROPE)
    zeros = jnp.zeros((half,), F32)
    ones = jnp.ones((half,), F32)
    consts = jnp.stack([
        jnp.concatenate([inv_freq, inv_freq, zeros, zeros]),
        jnp.concatenate([ones, ones, zeros, zeros]),
        jnp.concatenate([-ones, zeros, zeros, zeros]),
        jnp.concatenate([zeros, ones, zeros, zeros]),
    ])
    consts = jnp.pad(consts, ((0, 4), (0, 0)))
    return pl.pallas_call(
        _rope_kernel,
        grid=(t // tr,),
        in_specs=[pl.BlockSpec((tr, 1), lambda i: (i, 0)), pl.BlockSpec((8, LANES), lambda i: (0, 0))],
        out_specs=pl.BlockSpec((tr, 3 * LANES), lambda i: (i, 0)),
        out_shape=jax.ShapeDtypeStruct((t, 3 * LANES), F32),
        compiler_params=_cparams(1),
        name="rope_tables",
    )(positions.reshape(t, 1), consts)


def _inproj_kernel(x_ref, mod_ref, win_ref, wuq_ref, wukv_ref, qg_ref, kvg_ref, cw_ref, cg_ref, rope_ref,
                   q_ref, k_ref, vt_ref, cv_ref, cu_ext, *, ts):
    j = pl.program_id(1)
    x = x_ref[0]
    shift = mod_ref[0, 0:1, :]
    scale = mod_ref[0, 1:2, :]
    h = (x * (1.0 + scale) + shift).astype(BF16)
    z = _dot(h, win_ref[...])
    c_q = z[:, 0:Q_LORA]
    c_kv = z[:, Q_LORA:Q_LORA + KV_LORA]
    k_rope = z[:, 384:512]
    gate_b = z[:, 512:1024]
    gate_c = z[:, 1024:1536]
    u = z[:, 1536:2048]

    q = _dot(_rms(c_q, qg_ref[...]).astype(BF16), wuq_ref[...])
    kv = _dot(_rms(c_kv, kvg_ref[...]).astype(BF16), wukv_ref[...])

    cos_d = rope_ref[:, 0:LANES]
    sin_a = rope_ref[:, LANES:2 * LANES]
    sin_b = rope_ref[:, 2 * LANES:3 * LANES]

    def rope(t):
        return t * cos_d + pltpu.roll(t, 96, 1) * sin_a + pltpu.roll(t, 32, 1) * sin_b

    k_rot = rope(k_rope).astype(BF16)
    for hh in range(N_HEADS):
        qo = hh * QK_PAD
        q_ref[0, hh, :, 0:LANES] = (q[:, qo:qo + LANES] * Q_SCALE).astype(BF16)
        q_ref[0, hh, :, LANES:QK_PAD] = (rope(q[:, qo + LANES:qo + QK_PAD]) * Q_SCALE).astype(BF16)
        ko = hh * (QK_NOPE + V_HEAD)
        k_ref[0, hh, :, 0:LANES] = kv[:, ko:ko + QK_NOPE].astype(BF16)
        k_ref[0, hh, :, LANES:QK_PAD] = k_rot
        vt_ref[0, hh, 0] = jnp.transpose(kv[:, ko + QK_NOPE:ko + QK_NOPE + V_HEAD]).astype(BF16)

    @pl.when(j == 0)
    def _():
        cu_ext[0:8, :] = jnp.zeros((8, D_CONV), F32)

    @pl.when(j > 0)
    def _():
        cu_ext[0:8, :] = cu_ext[ts:ts + 8, :]

    cu_ext[8:ts + 8, :] = gate_c * u
    conv = (cw_ref[2:3, :] * cu_ext[8:ts + 8, :]
            + cw_ref[1:2, :] * cu_ext[pl.ds(7, ts), :]
            + cw_ref[0:1, :] * cu_ext[pl.ds(6, ts), :])
    conv = gate_b * conv
    cv_ref[0] = _rms(conv, cg_ref[...]).astype(BF16)


def _inproj(x, mod_l, w_in_p, w_uq_p, w_ukv, qg, kvg, conv_w, conv_g, rope_tab, ts):
    b, s, _ = x.shape
    nsb = s // ts
    const = lambda shape: pl.BlockSpec(shape, lambda bi, j: (0,) * len(shape))
    return pl.pallas_call(
        functools.partial(_inproj_kernel, ts=ts),
        grid=(b, nsb),
        in_specs=[
            pl.BlockSpec((1, ts, D_MODEL), lambda bi, j: (bi, j, 0)),
            pl.BlockSpec((1, 8, D_MODEL), lambda bi, j: (bi, 0, 0)),
            const((D_MODEL, D_IN_PAD)),
            const((Q_LORA, N_HEADS * QK_PAD)),
            const((KV_LORA, N_HEADS * (QK_NOPE + V_HEAD))),
            const((1, Q_LORA)),
            const((1, KV_LORA)),
            const((8, D_CONV)),
            const((1, D_CONV)),
            pl.BlockSpec((ts, 3 * LANES), lambda bi, j: (bi * nsb + j, 0)),
        ],
        out_specs=[
            pl.BlockSpec((1, N_HEADS, ts, QK_PAD), lambda bi, j: (bi, 0, j, 0)),
            pl.BlockSpec((1, N_HEADS, ts, QK_PAD), lambda bi, j: (bi, 0, j, 0)),
            pl.BlockSpec((1, N_HEADS, 1, V_HEAD, ts), lambda bi, j: (bi, 0, j, 0, 0)),
            pl.BlockSpec((1, ts, D_CONV), lambda bi, j: (bi, j, 0)),
        ],
        out_shape=[
            jax.ShapeDtypeStruct((b, N_HEADS, s, QK_PAD), BF16),
            jax.ShapeDtypeStruct((b, N_HEADS, s, QK_PAD), BF16),
            jax.ShapeDtypeStruct((b, N_HEADS, nsb, V_HEAD, ts), BF16),
            jax.ShapeDtypeStruct((b, s, D_CONV), BF16),
        ],
        scratch_shapes=[pltpu.VMEM((ts + 8, D_CONV), F32)],
        compiler_params=_cparams(2),
        name="inproj",
    )(x, mod_l, w_in_p, w_uq_p, w_ukv, qg, kvg, conv_w, conv_g, rope_tab)


Q_CHUNK = 256


K_ROWS = 64


def _attn_kernel(q_ref, k_ref, vt_ref, o_ref, *scratch, tq):
    i = pl.program_id(1)
    n_chunks = tq // Q_CHUNK
    chains = [(hh, c) + tuple(scratch[5 * (hh * n_chunks + c):5 * (hh * n_chunks + c) + 5])
              for hh in range(N_HEADS) for c in range(n_chunks)]
    for _, _, m_sc, l_sc, acc_sc, _, _ in chains:
        m_sc[...] = jnp.full(m_sc.shape, NEG_BIG, F32)
        l_sc[...] = jnp.zeros(l_sc.shape, F32)
        acc_sc[...] = jnp.zeros(acc_sc.shape, F32)

    def step(jk, diagonal):
        start = pl.multiple_of(jk * tq, tq)
        n_keys = lambda c: (c + 1) * Q_CHUNK if diagonal else tq
        for hh, c, _, _, _, s_sc, _ in chains:
            nk = n_keys(c)
            kj = k_ref[0, hh, pl.ds(start, nk), :]
            qc = q_ref[0, hh, c * Q_CHUNK:(c + 1) * Q_CHUNK, :]
            s_sc[0:nk, :] = lax.dot_general(kj, qc, (((1,), (1,)), ((), ())), preferred_element_type=F32)
        for hh, c, m_sc, l_sc, acc_sc, s_sc, p_sc in chains:
            nk = n_keys(c)
            if diagonal:
                d0 = nk - Q_CHUNK
                key = lax.broadcasted_iota(jnp.int32, (Q_CHUNK, Q_CHUNK), 0)
                qry = lax.broadcasted_iota(jnp.int32, (Q_CHUNK, Q_CHUNK), 1)
                s_sc[d0:nk, :] = jnp.where(key <= qry, s_sc[d0:nk, :], NEG_BIG)
            blk_max = s_sc[0:K_ROWS, :]
            for r in range(K_ROWS, nk, K_ROWS):
                blk_max = jnp.maximum(blk_max, s_sc[r:r + K_ROWS, :])
            m_prev = m_sc[...]
            m_new = jnp.maximum(m_prev, jnp.max(blk_max, axis=0, keepdims=True))
            alpha = jnp.exp2(m_prev - m_new)
            p_sum = jnp.zeros((K_ROWS, Q_CHUNK), F32)
            for r in range(0, nk, K_ROWS):
                p = jnp.exp2(s_sc[r:r + K_ROWS, :] - m_new)
                p_sum = p_sum + p
                p_sc[r:r + K_ROWS, :] = p.astype(BF16)
            l_sc[...] = alpha * l_sc[...] + jnp.sum(p_sum, axis=0, keepdims=True)
            acc_sc[...] = alpha * acc_sc[...] + _dot(vt_ref[0, hh, jk, :, 0:nk], p_sc[0:nk, :])
            m_sc[...] = m_new

    def body(jk, carry):
        step(jk, False)
        return carry

    lax.fori_loop(0, i, body, 0)
    step(i, True)
    for hh, c, _, l_sc, acc_sc, _, _ in chains:
        o_ref[0, c * Q_CHUNK:(c + 1) * Q_CHUNK, hh * V_HEAD:(hh + 1) * V_HEAD] = (
            jnp.transpose(acc_sc[...] / l_sc[...]).astype(BF16))


def _attention(q, k, vt, tq):
    b, nh, s, _ = q.shape
    return pl.pallas_call(
        functools.partial(_attn_kernel, tq=tq),
        grid=(b, s // tq),
        in_specs=[
            pl.BlockSpec((1, nh, tq, QK_PAD), lambda bi, i: (bi, 0, i, 0)),
            pl.BlockSpec((1, nh, s, QK_PAD), lambda bi, i: (bi, 0, 0, 0)),
            pl.BlockSpec((1, nh, s // tq, V_HEAD, tq), lambda bi, i: (bi, 0, 0, 0, 0)),
        ],
        out_specs=pl.BlockSpec((1, tq, D_ATTN), lambda bi, i: (bi, i, 0)),
        out_shape=jax.ShapeDtypeStruct((b, s, D_ATTN), BF16),
        scratch_shapes=[
            pltpu.VMEM((1, Q_CHUNK), F32), pltpu.VMEM((1, Q_CHUNK), F32), pltpu.VMEM((V_HEAD, Q_CHUNK), F32),
            pltpu.VMEM((tq, Q_CHUNK), F32), pltpu.VMEM((tq, Q_CHUNK), BF16),
        ] * (nh * (tq // Q_CHUNK)),
        compiler_params=_cparams(2),
        name="attention",
    )(q, k, vt)


def _outproj_core(attn_ref, cv_ref, x_ref, mod_ref, wo_ref, ga_ref, lng_ref, lnb_ref):
    a = attn_ref[0].astype(F32)
    an = _rms(a, ga_ref[...]).astype(BF16)
    y = _dot(an, wo_ref[0:D_ATTN, :]) + _dot(cv_ref[0], wo_ref[D_ATTN:D_MODEL, :])
    gate1 = mod_ref[0, 2:3, :]
    x1 = _layer_norm(DEEPNORM_ALPHA * x_ref[0] + gate1 * y, lng_ref[...], lnb_ref[...])
    h2 = x1 * (1.0 + mod_ref[0, 4:5, :]) + mod_ref[0, 3:4, :]
    return x1, h2


def _outproj_kernel(attn_ref, cv_ref, x_ref, mod_ref, wo_ref, ga_ref, lng_ref, lnb_ref, x1_ref, h2_ref):
    x1, h2 = _outproj_core(attn_ref, cv_ref, x_ref, mod_ref, wo_ref, ga_ref, lng_ref, lnb_ref)
    x1_ref[0] = x1
    h2_ref[0] = h2.astype(BF16)


def _outproj_router_kernel(attn_ref, cv_ref, x_ref, mod_ref, wo_ref, ga_ref, lng_ref, lnb_ref, rw_ref, rb_ref,
                           x1_ref, h2_ref, rel_ref, gate_ref, relt_ref, base_ref, cnt_ref, run_sc, *, ts):
    first = jnp.logical_and(pl.program_id(0) == 0, pl.program_id(1) == 0)

    @pl.when(first)
    def _():
        run_sc[...] = jnp.zeros(run_sc.shape, F32)

    x1, h2 = _outproj_core(attn_ref, cv_ref, x_ref, mod_ref, wo_ref, ga_ref, lng_ref, lnb_ref)
    x1_ref[0] = x1
    h2_ref[0] = h2.astype(BF16)

    lane = lax.broadcasted_iota(jnp.int32, (ts, LANES), 1)
    logits = jnp.dot(h2, rw_ref[...], preferred_element_type=F32, precision=lax.Precision.HIGHEST) + rb_ref[...]
    logits = jnp.where(lane < N_EXPERTS, logits, NEG_BIG)
    v1 = jnp.max(logits, axis=-1, keepdims=True)
    i1 = jnp.min(jnp.where(logits == v1, lane, LANES), axis=-1, keepdims=True)
    rest = jnp.where(lane == i1, NEG_BIG, logits)
    v2 = jnp.max(rest, axis=-1, keepdims=True)
    i2 = jnp.min(jnp.where(rest == v2, lane, LANES), axis=-1, keepdims=True)
    e21 = jnp.exp(v2 - v1)
    g1 = 1.0 / (1.0 + e21)
    g2 = e21 / (1.0 + e21)
    is1 = lane == i1
    is2 = lane == i2
    sel = jnp.where(jnp.logical_or(is1, is2), 1.0, 0.0)
    gate_ref[...] = jnp.where(is1, g1, 0.0) + jnp.where(is2, g2, 0.0)

    r_i = lax.broadcasted_iota(jnp.int32, (ts, ts), 0)
    c_i = lax.broadcasted_iota(jnp.int32, (ts, ts), 1)
    tri = jnp.where(c_i < r_i, 1.0, 0.0).astype(BF16)
    rank = _dot(tri, sel.astype(BF16))
    rel = jnp.where(sel > 0.0, rank, -1.0)
    rel_ref[...] = rel
    relt_ref[0] = jnp.transpose(rel)[0:N_EXPERTS, :]

    n_tile = jnp.sum(sel, axis=0, keepdims=True)
    n_pad = jnp.floor((n_tile + (ROW_ALIGN - 1)) * (1.0 / ROW_ALIGN)) * ROW_ALIGN
    base_ref[0] = run_sc[...].astype(jnp.int32)
    cnt_ref[0] = n_tile.astype(jnp.int32)
    run_sc[...] = run_sc[...] + n_pad


def _outproj(attn, convn, x, mod_l, w_o, g_attn, ln_g, ln_b, ts, router=None):
    b, s, _ = x.shape
    nsb = s // ts
    const = lambda shape: pl.BlockSpec(shape, lambda bi, j: (0,) * len(shape))
    tile3 = lambda w: pl.BlockSpec((1, ts, w), lambda bi, j: (bi, j, 0))
    in_specs = [
        tile3(D_ATTN), tile3(D_CONV), tile3(D_MODEL),
        pl.BlockSpec((1, 8, D_MODEL), lambda bi, j: (bi, 0, 0)),
        const((D_MODEL, D_MODEL)), const((1, D_ATTN)), const((1, D_MODEL)), const((1, D_MODEL)),
    ]
    out_specs = [tile3(D_MODEL), tile3(D_MODEL)]
    out_shape = [jax.ShapeDtypeStruct((b, s, D_MODEL), F32), jax.ShapeDtypeStruct((b, s, D_MODEL), BF16)]
    args = [attn, convn, x, mod_l, w_o, g_attn, ln_g, ln_b]
    if router is None:
        return pl.pallas_call(
            _outproj_kernel, grid=(b, nsb), in_specs=in_specs, out_specs=out_specs, out_shape=out_shape,
            compiler_params=_cparams(2), name="outproj",
        )(*args)
    rw, rb = router
    nt = b * nsb
    flat = lambda w: pl.BlockSpec((ts, w), lambda bi, j: (bi * nsb + j, 0))
    per_tile = lambda r, w: pl.BlockSpec((1, r, w), lambda bi, j: (bi * nsb + j, 0, 0))
    in_specs += [const((D_MODEL, LANES)), const((1, LANES))]
    out_specs += [flat(LANES), flat(LANES), per_tile(N_EXPERTS, ts), per_tile(1, LANES), per_tile(1, LANES)]
    out_shape += [
        jax.ShapeDtypeStruct((nt * ts, LANES), F32), jax.ShapeDtypeStruct((nt * ts, LANES), F32),
        jax.ShapeDtypeStruct((nt, N_EXPERTS, ts), F32),
        jax.ShapeDtypeStruct((nt, 1, LANES), jnp.int32), jax.ShapeDtypeStruct((nt, 1, LANES), jnp.int32),
    ]
    return pl.pallas_call(
        functools.partial(_outproj_router_kernel, ts=ts), grid=(b, nsb), in_specs=in_specs, out_specs=out_specs,
        out_shape=out_shape, scratch_shapes=[pltpu.VMEM((1, LANES), F32)],
        compiler_params=_cparams(2), name="outproj_router",
    )(*args, rw, rb)


FF_CHUNK = 256


def _swiglu(x, wg, wu, wd, width):
    out = None
    pending = None
    for c in range(0, width, FF_CHUNK):
        w = min(FF_CHUNK, width - c)
        g = _dot(x, wg(c, w))
        u = _dot(x, wu(c, w))
        if pending is not None:
            d = _dot(pending[0], wd(*pending[1]))
            out = d if out is None else out + d
        pending = ((_silu(g) * u).astype(BF16), (c, w))
    d = _dot(pending[0], wd(*pending[1]))
    return d if out is None else out + d


def _ffn_kernel(h_ref, x1_ref, mod_ref, wg_ref, wu_ref, wd_ref, lng_ref, lnb_ref, o_ref):
    f = _swiglu(h_ref[...], lambda c, w: wg_ref[:, c:c + w], lambda c, w: wu_ref[:, c:c + w],
                lambda c, w: wd_ref[c:c + w, :], D_FF)
    gate2 = mod_ref[0, 5:6, :]
    o_ref[...] = _layer_norm(DEEPNORM_ALPHA * x1_ref[...] + gate2 * f, lng_ref[...], lnb_ref[...])


def _dense_ffn(h2, x1, mod_l, wg, wu, wd, ln_g, ln_b, tm, tiles_per_batch):
    t = h2.shape[0]
    const = lambda shape: pl.BlockSpec(shape, lambda i: (0,) * len(shape), pipeline_mode=pl.Buffered(1))
    return pl.pallas_call(
        _ffn_kernel,
        grid=(t // tm,),
        in_specs=[
            pl.BlockSpec((tm, D_MODEL), lambda i: (i, 0)),
            pl.BlockSpec((tm, D_MODEL), lambda i: (i, 0)),
            pl.BlockSpec((1, 8, D_MODEL), lambda i: (i // tiles_per_batch, 0, 0)),
            const((D_MODEL, D_FF)), const((D_MODEL, D_FF)), const((D_FF, D_MODEL)),
            const((1, D_MODEL)), const((1, D_MODEL)),
        ],
        out_specs=pl.BlockSpec((tm, D_MODEL), lambda i: (i, 0)),
        out_shape=jax.ShapeDtypeStruct((t, D_MODEL), F32),
        compiler_params=_cparams(1),
        name="dense_ffn",
    )(h2, x1, mod_l, wg, wu, wd, ln_g, ln_b)


def _group_copy(src, dst, sem):
    return pltpu.make_async_copy(src, dst, sem)


def _dispatch_kernel(base_ref, cnt_ref, h_ref, relt_ref, xs_ref, stage, extra, zbuf, sems, xsem, zsem, *,
                     ts, zero_rows):
    i = pl.program_id(0)
    last = pl.num_programs(0) - 1
    par = lax.rem(i, 2)
    n_chunks = ts // SLOT_CHUNK
    h = h_ref[...]
    slot = lax.broadcasted_iota(jnp.int32, (SLOT_CHUNK, ts), 0).astype(F32)

    def onehot(e, c):
        return jnp.where(relt_ref[0, e:e + 1, :] == slot + float(c * SLOT_CHUNK), 1.0, 0.0).astype(BF16)

    def first_copy(step, e):
        a = pl.multiple_of(base_ref[step * N_EXPERTS + e], ROW_ALIGN)
        p = lax.rem(step, 2)
        return _group_copy(stage.at[p, e], xs_ref.at[e, pl.ds(a, SLOT_CHUNK), :], sems.at[p, e])

    for e in range(N_EXPERTS):
        stage[par, e] = _dot(onehot(e, 0), h).astype(BF16)

    @pl.when(i > 0)
    def _():
        for e in range(N_EXPERTS):
            first_copy(i - 1, e).wait()

    for e in range(N_EXPERTS):
        first_copy(i, e).start()

    for e in range(N_EXPERTS):
        n = cnt_ref[i * N_EXPERTS + e]
        for c in range(1, n_chunks):
            @pl.when(n > c * SLOT_CHUNK)
            def _(e=e, c=c):
                a = pl.multiple_of(base_ref[i * N_EXPERTS + e], ROW_ALIGN)
                extra[...] = _dot(onehot(e, c), h).astype(BF16)
                cp = _group_copy(extra, xs_ref.at[e, pl.ds(a + c * SLOT_CHUNK, SLOT_CHUNK), :], xsem)
                cp.start()
                cp.wait()

    @pl.when(i == last)
    def _():
        for e in range(N_EXPERTS):
            first_copy(i, e).wait()
        zbuf[...] = jnp.zeros(zbuf.shape, BF16)
        for e in range(N_EXPERTS):
            n = cnt_ref[i * N_EXPERTS + e]
            end = base_ref[i * N_EXPERTS + e] + ((n + (ROW_ALIGN - 1)) // ROW_ALIGN) * ROW_ALIGN
            end = pl.multiple_of(end, ROW_ALIGN)
            copies = [
                _group_copy(zbuf, xs_ref.at[e, pl.ds(end + r * SLOT_CHUNK, SLOT_CHUNK), :], zsem.at[e])
                for r in range(zero_rows // SLOT_CHUNK)
            ]
            for cp in copies:
                cp.start()
            for cp in copies:
                cp.wait()


def _dispatch(base, cnt, h2, relt, ts, cap, zero_rows):
    t = h2.shape[0]
    nt = t // ts
    n_chunks = ts // SLOT_CHUNK
    grid_spec = pltpu.PrefetchScalarGridSpec(
        num_scalar_prefetch=2,
        grid=(nt,),
        in_specs=[
            pl.BlockSpec((ts, D_MODEL), lambda i, b_, c_: (i, 0)),
            pl.BlockSpec((1, N_EXPERTS, ts), lambda i, b_, c_: (i, 0, 0)),
        ],
        out_specs=pl.BlockSpec(memory_space=pl.ANY),
        scratch_shapes=[
            pltpu.VMEM((2, N_EXPERTS, SLOT_CHUNK, D_MODEL), BF16),
            pltpu.VMEM((SLOT_CHUNK, D_MODEL), BF16),
            pltpu.VMEM((SLOT_CHUNK, D_MODEL), BF16),
            pltpu.SemaphoreType.DMA((2, N_EXPERTS)),
            pltpu.SemaphoreType.DMA(()),
            pltpu.SemaphoreType.DMA((N_EXPERTS,)),
        ],
    )
    return pl.pallas_call(
        functools.partial(_dispatch_kernel, ts=ts, zero_rows=zero_rows),
        grid_spec=grid_spec,
        out_shape=jax.ShapeDtypeStruct((N_EXPERTS, cap, D_MODEL), BF16),
        compiler_params=_cparams(1),
        name="moe_dispatch",
    )(base, cnt, h2, relt)


def _expert_kernel(te_ref, tr_ref, nv_ref, x_ref, wg_ref, wu_ref, wd_ref, o_ref, acc_sc):
    g_i = pl.program_id(0)
    k = pl.program_id(1)

    valid = g_i < nv_ref[0]

    @pl.when(jnp.logical_and(valid, k == 0))
    def _():
        acc_sc[...] = jnp.zeros(acc_sc.shape, F32)

    @pl.when(valid)
    def _():
        part = _swiglu(x_ref[0], lambda c, w: wg_ref[0, :, c:c + w].astype(BF16),
                       lambda c, w: wu_ref[0, :, c:c + w].astype(BF16),
                       lambda c, w: wd_ref[0, c:c + w, :].astype(BF16), wg_ref.shape[2])
        total = acc_sc[...] + part
        acc_sc[...] = total
        o_ref[0] = total.astype(BF16)


def _expert_ffn(tile_e, tile_r, n_valid, xs, wg, wu, wd, tm, tf, n_steps):
    nk = D_FF_EXPERT // tf
    cap = xs.shape[1]

    def k_eff(g, k, nv):
        return jnp.where(g < nv[0], k, nk - 1)

    grid_spec = pltpu.PrefetchScalarGridSpec(
        num_scalar_prefetch=3,
        grid=(n_steps, nk),
        in_specs=[
            pl.BlockSpec((1, tm, D_MODEL), lambda g, k, te, tr, nv: (te[g], tr[g], 0)),
            pl.BlockSpec((1, D_MODEL, tf), lambda g, k, te, tr, nv: (te[g], 0, k_eff(g, k, nv))),
            pl.BlockSpec((1, D_MODEL, tf), lambda g, k, te, tr, nv: (te[g], 0, k_eff(g, k, nv))),
            pl.BlockSpec((1, tf, D_MODEL), lambda g, k, te, tr, nv: (te[g], k_eff(g, k, nv), 0)),
        ],
        out_specs=pl.BlockSpec((1, tm, D_MODEL), lambda g, k, te, tr, nv: (te[g], tr[g], 0)),
        scratch_shapes=[pltpu.VMEM((tm, D_MODEL), F32)],
    )
    return pl.pallas_call(
        _expert_kernel,
        grid_spec=grid_spec,
        out_shape=jax.ShapeDtypeStruct((N_EXPERTS, cap, D_MODEL), BF16),
        compiler_params=_cparams(2),
        name="moe_experts",
    )(tile_e, tile_r, n_valid, xs, wg, wu, wd)


def _combine_kernel(base_ref, cnt_ref, lim_ref, ys_ref, rel_ref, gate_ref, x1_ref, mod_ref, lng_ref, lnb_ref, o_ref,
                    slab, xslab, acc_sc, sems, xsem, *, ts):
    i = pl.program_id(0)
    last = pl.num_programs(0) - 1
    par = lax.rem(i, 2)
    n_chunks = ts // SLOT_CHUNK
    slot = lax.broadcasted_iota(jnp.int32, (ts, SLOT_CHUNK), 1).astype(F32)

    def slab_start(step, e, c):
        a = base_ref[step * N_EXPERTS + e] + c * SLOT_CHUNK
        start = jnp.maximum(jnp.minimum(a, lim_ref[e] - SLOT_CHUNK), 0)
        return pl.multiple_of(start, ROW_ALIGN), a - start

    def first_copy(step, e):
        start, _ = slab_start(step, e, 0)
        p = lax.rem(step, 2)
        return _group_copy(ys_ref.at[e, pl.ds(start, SLOT_CHUNK), :], slab.at[p, e], sems.at[p, e])

    def contribution(e, c, rows):
        _, shift = slab_start(i, e, c)
        offset = (shift - c * SLOT_CHUNK).astype(F32)
        onehot = jnp.where(rel_ref[:, e:e + 1] + offset == slot, 1.0, 0.0).astype(BF16)
        return _dot(onehot, rows) * gate_ref[:, e:e + 1]

    @pl.when(i == 0)
    def _():
        for e in range(N_EXPERTS):
            first_copy(i, e).start()

    @pl.when(i < last)
    def _():
        for e in range(N_EXPERTS):
            first_copy(i + 1, e).start()

    for e in range(N_EXPERTS):
        first_copy(i, e).wait()
    f = contribution(0, 0, slab[par, 0])
    for e in range(1, N_EXPERTS):
        f = f + contribution(e, 0, slab[par, e])
    acc_sc[...] = f

    for e in range(N_EXPERTS):
        n = cnt_ref[i * N_EXPERTS + e]
        for c in range(1, n_chunks):
            @pl.when(n > c * SLOT_CHUNK)
            def _(e=e, c=c):
                start, _ = slab_start(i, e, c)
                cp = _group_copy(ys_ref.at[e, pl.ds(start, SLOT_CHUNK), :], xslab, xsem)
                cp.start()
                cp.wait()
                acc_sc[...] += contribution(e, c, xslab[...])

    gate2 = mod_ref[0, 5:6, :]
    o_ref[...] = _layer_norm(DEEPNORM_ALPHA * x1_ref[...] + gate2 * acc_sc[...], lng_ref[...], lnb_ref[...])


def _combine(base, cnt, lim, ys, rel, gates, x1, mod_l, ln_g, ln_b, ts, tiles_per_batch):
    t = x1.shape[0]
    grid_spec = pltpu.PrefetchScalarGridSpec(
        num_scalar_prefetch=3,
        grid=(t // ts,),
        in_specs=[
            pl.BlockSpec(memory_space=pl.ANY),
            pl.BlockSpec((ts, LANES), lambda i, *_: (i, 0)),
            pl.BlockSpec((ts, LANES), lambda i, *_: (i, 0)),
            pl.BlockSpec((ts, D_MODEL), lambda i, *_: (i, 0)),
            pl.BlockSpec((1, 8, D_MODEL), lambda i, *_: (i // tiles_per_batch, 0, 0)),
            pl.BlockSpec((1, D_MODEL), lambda i, *_: (0, 0)),
            pl.BlockSpec((1, D_MODEL), lambda i, *_: (0, 0)),
        ],
        out_specs=pl.BlockSpec((ts, D_MODEL), lambda i, *_: (i, 0)),
        scratch_shapes=[
            pltpu.VMEM((2, N_EXPERTS, SLOT_CHUNK, D_MODEL), BF16),
            pltpu.VMEM((SLOT_CHUNK, D_MODEL), BF16),
            pltpu.VMEM((ts, D_MODEL), F32),
            pltpu.SemaphoreType.DMA((2, N_EXPERTS)),
            pltpu.SemaphoreType.DMA(()),
        ],
    )
    return pl.pallas_call(
        functools.partial(_combine_kernel, ts=ts),
        grid_spec=grid_spec,
        out_shape=jax.ShapeDtypeStruct((t, D_MODEL), F32),
        compiler_params=_cparams(1),
        name="moe_combine",
    )(base, cnt, lim, ys, rel, gates, x1, mod_l, ln_g, ln_b)


def _expert_tile_plan(base, cnt, nt, tm, n_steps):
    last_base = base.reshape(nt, N_EXPERTS)[-1]
    last_cnt = cnt.reshape(nt, N_EXPERTS)[-1]
    total = last_base + ((last_cnt + (ROW_ALIGN - 1)) // ROW_ALIGN) * ROW_ALIGN
    tiles = jnp.maximum((total + tm - 1) // tm, 1)
    ends = jnp.cumsum(tiles)
    n_valid = ends[-1]
    g = jnp.minimum(jnp.arange(n_steps, dtype=jnp.int32), n_valid - 1)
    tile_e = jnp.sum((g[:, None] >= ends[None, :]).astype(jnp.int32), axis=1)
    tile_r = g - (ends - tiles)[tile_e]
    lim = (tiles * tm).astype(jnp.int32)
    return tile_e.astype(jnp.int32), tile_r.astype(jnp.int32), n_valid.reshape(1).astype(jnp.int32), lim


def _top2_moe(h2, x1, mod_l, routing, wg, wu, wd, ln_g, ln_b, ts, tiles_per_batch):
    rel, gates, relt, base3, cnt3 = routing
    t = h2.shape[0]
    nt = t // ts
    tm, tf = EXPERT_ROWS, 512
    base = base3[:, 0, :N_EXPERTS].reshape(-1)
    cnt = cnt3[:, 0, :N_EXPERTS].reshape(-1)
    zero_rows = -(-tm // SLOT_CHUNK) * SLOT_CHUNK
    rows_max = t + ROW_ALIGN * nt
    cap = -(-(rows_max + SLOT_CHUNK + zero_rows) // tm) * tm
    rows_all = 2 * t + ROW_ALIGN * N_EXPERTS * nt
    n_steps = -(-rows_all // tm) + N_EXPERTS
    xs = _dispatch(base, cnt, h2, relt, ts, cap, zero_rows)
    tile_e, tile_r, n_valid, lim = _expert_tile_plan(base, cnt, nt, tm, n_steps)
    ys = _expert_ffn(tile_e, tile_r, n_valid, xs, wg, wu, wd, tm, tf, n_steps)
    return _combine(base, cnt, lim, ys, rel, gates, x1, mod_l, ln_g, ln_b, ts, tiles_per_batch)


def kernel(x, c, positions, w_in, q_norm_g, kv_norm_g, w_uq, w_ukv, conv_w, mix_norm_g, w_o, w_ada, b_ada,
           ln1_g, ln1_b, ln2_g, ln2_b, ffn_w_gate, ffn_w_up, ffn_w_down, moe_router_w, moe_router_b,
           moe_w_gate, moe_w_up, moe_w_down):
    b, s, _ = x.shape
    n_layers = w_in.shape[0]
    ts = min(512, s)
    t = b * s
    tiles_per_batch = s // ts

    bp = -(-b // 16) * 16
    c_pad = jnp.pad(c, ((0, bp - b), (0, 0)))
    mod = _ada_mod(c_pad, w_ada, b_ada)
    mod = jnp.pad(jnp.transpose(mod[:, :, :b, :], (0, 2, 1, 3)), ((0, 0), (0, 0), (0, 2), (0, 0)))
    rope_tab = _rope_tables(positions)

    row = lambda v: v.reshape(1, -1)
    for l in range(n_layers):
        o3 = Q_LORA + KV_LORA + QK_ROPE
        w_in_p = jnp.concatenate(
            [w_in[l][:, :o3], jnp.zeros((D_MODEL, D_IN_PAD - w_in.shape[2]), F32), w_in[l][:, o3:]], axis=1
        ).astype(BF16)
        w_uq_p = jnp.pad(
            w_uq[l].reshape(Q_LORA, N_HEADS, QK_NOPE + QK_ROPE), ((0, 0), (0, 0), (0, QK_PAD - QK_NOPE - QK_ROPE))
        ).reshape(Q_LORA, N_HEADS * QK_PAD).astype(BF16)
        conv_w_p = jnp.pad(conv_w[l], ((0, 8 - CONV_WIDTH), (0, 0)))
        q, k, v, convn = _inproj(x, mod[l], w_in_p, w_uq_p, w_ukv[l].astype(BF16), row(q_norm_g[l]),
                                 row(kv_norm_g[l]), conv_w_p, row(mix_norm_g[l][D_ATTN:]), rope_tab, ts)
        attn = _attention(q, k, v, ts)
        common = (attn, convn, x, mod[l], w_o[l].astype(BF16), row(mix_norm_g[l][:D_ATTN]), row(ln1_g[l]),
                  row(ln1_b[l]), ts)
        if l % 2 == 0:
            i = l // 2
            x1, h2 = _outproj(*common)
            x = _dense_ffn(h2.reshape(t, D_MODEL), x1.reshape(t, D_MODEL), mod[l], ffn_w_gate[i].astype(BF16),
                           ffn_w_up[i].astype(BF16), ffn_w_down[i].astype(BF16), row(ln2_g[l]), row(ln2_b[l]),
                           ts, tiles_per_batch)
        else:
            i = l // 2
            rw = jnp.pad(moe_router_w[i], ((0, 0), (0, LANES - N_EXPERTS)))
            rb = jnp.pad(moe_router_b[i], (0, LANES - N_EXPERTS)).reshape(1, LANES)
            x1, h2, *routing = _outproj(*common, router=(rw, rb))
            x = _top2_moe(h2.reshape(t, D_MODEL), x1.reshape(t, D_MODEL), mod[l], routing,
                          moe_w_gate[i], moe_w_up[i], moe_w_down[i], row(ln2_g[l]), row(ln2_b[l]), ts, tiles_per_batch)
        x = x.reshape(b, s, D_MODEL)
    return x
```

```python
import functools

import jax
import jax.numpy as jnp
from jax import lax
from jax.experimental import pallas as pl
from jax.experimental.pallas import tpu as pltpu

F32 = jnp.float32
BF16 = jnp.bfloat16

D_MODEL = 1024
N_HEADS = 4
QK_NOPE = 128
QK_ROPE = 64
V_HEAD = 128
Q_LORA = 256
KV_LORA = 128
D_ATTN = N_HEADS * V_HEAD
D_CONV = D_MODEL - D_ATTN
CONV_WIDTH = 3
ROPE_BASE = 10000.0
D_FF = 2816
N_EXPERTS = 8
D_FF_EXPERT = 3584
RMS_EPS = 1e-6
LN_EPS = 1e-5
DEPTH = 2
DEEPNORM_ALPHA = (2 * DEPTH) ** 0.25
SM_SCALE = (QK_NOPE + QK_ROPE) ** -0.5
LOG2E = 1.4426950408889634
Q_SCALE = SM_SCALE * LOG2E

LANES = 128
QK_PAD = 2 * LANES
D_IN_PAD = 2048
ROW_ALIGN = 16
SLOT_CHUNK = 256
EXPERT_ROWS = 1152
VMEM_LIMIT = 56 * 1024 * 1024
NEG_BIG = -1e30


def _cparams(n_axes):
    return pltpu.CompilerParams(dimension_semantics=("arbitrary",) * n_axes, vmem_limit_bytes=VMEM_LIMIT)


def _dot(a, b):
    return jnp.dot(a, b, preferred_element_type=F32)


def _rms(x, g):
    return x * lax.rsqrt(jnp.mean(x * x, axis=-1, keepdims=True) + RMS_EPS) * g


def _layer_norm(x, g, b):
    mu = jnp.mean(x, axis=-1, keepdims=True)
    xc = x - mu
    var = jnp.mean(xc * xc, axis=-1, keepdims=True)
    return xc * lax.rsqrt(var + LN_EPS) * g + b


def _silu(x):
    return x * jax.nn.sigmoid(x)


def _ada_kernel(c_ref, w_ref, b_ref, o_ref):
    ca = _silu(c_ref[...]).astype(BF16)
    o_ref[0, 0] = _dot(ca, w_ref[0].astype(BF16)) + b_ref[0]


def _ada_mod(c_pad, w_ada, b_ada):
    n_layers = w_ada.shape[0]
    bp = c_pad.shape[0]
    return pl.pallas_call(
        _ada_kernel,
        grid=(n_layers, 6),
        in_specs=[
            pl.BlockSpec((bp, D_MODEL), lambda l, j: (0, 0)),
            pl.BlockSpec((1, D_MODEL, D_MODEL), lambda l, j: (l, 0, j)),
            pl.BlockSpec((1, 1, D_MODEL), lambda l, j: (l, 0, j)),
        ],
        out_specs=pl.BlockSpec((1, 1, bp, D_MODEL), lambda l, j: (l, j, 0, 0)),
        out_shape=jax.ShapeDtypeStruct((n_layers, 6, bp, D_MODEL), F32),
        compiler_params=_cparams(2),
        name="ada_mod",
    )(c_pad, w_ada, b_ada.reshape(n_layers, 1, 6 * D_MODEL))


def _rope_kernel(pos_ref, f_ref, o_ref):
    ang = pos_ref[...].astype(F32) * f_ref[0:1, :]
    c = jnp.cos(ang)
    s = jnp.sin(ang)
    o_ref[:, 0:LANES] = c * f_ref[1:2, :]
    o_ref[:, LANES:2 * LANES] = s * f_ref[2:3, :]
    o_ref[:, 2 * LANES:3 * LANES] = s * f_ref[3:4, :]


def _rope_tables(positions):
    t = positions.size
    tr = min(1024, t)
    half = QK_ROPE // 2
    inv_freq = ROPE_BASE ** (-jnp.arange(0, QK_ROPE, 2, dtype=F32) / QK_ROPE)
    zeros = jnp.zeros((half,), F32)
    ones = jnp.ones((half,), F32)
    consts = jnp.stack([
        jnp.concatenate([inv_freq, inv_freq, zeros, zeros]),
        jnp.concatenate([ones, ones, zeros, zeros]),
        jnp.concatenate([-ones, zeros, zeros, zeros]),
        jnp.concatenate([zeros, ones, zeros, zeros]),
    ])
    consts = jnp.pad(consts, ((0, 4), (0, 0)))
    return pl.pallas_call(
        _rope_kernel,
        grid=(t // tr,),
        in_specs=[pl.BlockSpec((tr, 1), lambda i: (i, 0)), pl.BlockSpec((8, LANES), lambda i: (0, 0))],
        out_specs=pl.BlockSpec((tr, 3 * LANES), lambda i: (i, 0)),
        out_shape=jax.ShapeDtypeStruct((t, 3 * LANES), F32),
        compiler_params=_cparams(1),
        name="rope_tables",
    )(positions.reshape(t, 1), consts)


def _inproj_kernel(x_ref, mod_ref, win_ref, wuq_ref, wukv_ref, qg_ref, kvg_ref, cw_ref, cg_ref, rope_ref,
                   qt_ref, k_ref, vt_ref, cv_ref, cu_ext, *, ts):
    j = pl.program_id(1)
    x = x_ref[0]
    shift = mod_ref[0, 0:1, :]
    scale = mod_ref[0, 1:2, :]
    h = (x * (1.0 + scale) + shift).astype(BF16)
    z_lat = _dot(h, win_ref[:, 0:512])
    c_q = z_lat[:, 0:Q_LORA]
    c_kv = z_lat[:, Q_LORA:Q_LORA + KV_LORA]
    k_rope = z_lat[:, 384:512]
    z_gate = _dot(h, win_ref[:, 512:1536])
    gate_b = z_gate[:, 0:D_CONV]
    gate_c = z_gate[:, D_CONV:2 * D_CONV]
    q = _dot(_rms(c_q, qg_ref[...]).astype(BF16), wuq_ref[...])
    kv = _dot(_rms(c_kv, kvg_ref[...]).astype(BF16), wukv_ref[...])
    u = _dot(h, win_ref[:, 1536:2048])

    cos_d = rope_ref[:, 0:LANES]
    sin_a = rope_ref[:, LANES:2 * LANES]
    sin_b = rope_ref[:, 2 * LANES:3 * LANES]

    def rope(t):
        return t * cos_d + pltpu.roll(t, 96, 1) * sin_a + pltpu.roll(t, 32, 1) * sin_b

    k_rot = rope(k_rope).astype(BF16)
    for hh in range(N_HEADS):
        qo = hh * QK_PAD
        qt_ref[0, hh, 0:LANES, :] = jnp.transpose(q[:, qo:qo + LANES] * Q_SCALE).astype(BF16)
        qt_ref[0, hh, LANES:QK_PAD, :] = jnp.transpose(rope(q[:, qo + LANES:qo + QK_PAD]) * Q_SCALE).astype(BF16)
        ko = hh * (QK_NOPE + V_HEAD)
        k_ref[0, hh, :, 0:LANES] = kv[:, ko:ko + QK_NOPE].astype(BF16)
        k_ref[0, hh, :, LANES:QK_PAD] = k_rot
        vt_ref[0, hh, 0] = jnp.transpose(kv[:, ko + QK_NOPE:ko + QK_NOPE + V_HEAD]).astype(BF16)

    @pl.when(j == 0)
    def _():
        cu_ext[0:8, :] = jnp.zeros((8, D_CONV), F32)

    @pl.when(j > 0)
    def _():
        cu_ext[0:8, :] = cu_ext[ts:ts + 8, :]

    cu_ext[8:ts + 8, :] = gate_c * u
    conv = (cw_ref[2:3, :] * cu_ext[8:ts + 8, :]
            + cw_ref[1:2, :] * cu_ext[pl.ds(7, ts), :]
            + cw_ref[0:1, :] * cu_ext[pl.ds(6, ts), :])
    conv = gate_b * conv
    cv_ref[0] = _rms(conv, cg_ref[...]).astype(BF16)


def _inproj(x, mod_l, w_in_p, w_uq_p, w_ukv, qg, kvg, conv_w, conv_g, rope_tab, ts):
    b, s, _ = x.shape
    nsb = s // ts
    const = lambda shape: pl.BlockSpec(shape, lambda bi, j: (0,) * len(shape))
    return pl.pallas_call(
        functools.partial(_inproj_kernel, ts=ts),
        grid=(b, nsb),
        in_specs=[
            pl.BlockSpec((1, ts, D_MODEL), lambda bi, j: (bi, j, 0)),
            pl.BlockSpec((1, 8, D_MODEL), lambda bi, j: (bi, 0, 0)),
            const((D_MODEL, D_IN_PAD)),
            const((Q_LORA, N_HEADS * QK_PAD)),
            const((KV_LORA, N_HEADS * (QK_NOPE + V_HEAD))),
            const((1, Q_LORA)),
            const((1, KV_LORA)),
            const((8, D_CONV)),
            const((1, D_CONV)),
            pl.BlockSpec((ts, 3 * LANES), lambda bi, j: (bi * nsb + j, 0)),
        ],
        out_specs=[
            pl.BlockSpec((1, N_HEADS, QK_PAD, ts), lambda bi, j: (bi, 0, 0, j)),
            pl.BlockSpec((1, N_HEADS, ts, QK_PAD), lambda bi, j: (bi, 0, j, 0)),
            pl.BlockSpec((1, N_HEADS, 1, V_HEAD, ts), lambda bi, j: (bi, 0, j, 0, 0)),
            pl.BlockSpec((1, ts, D_CONV), lambda bi, j: (bi, j, 0)),
        ],
        out_shape=[
            jax.ShapeDtypeStruct((b, N_HEADS, QK_PAD, s), BF16),
            jax.ShapeDtypeStruct((b, N_HEADS, s, QK_PAD), BF16),
            jax.ShapeDtypeStruct((b, N_HEADS, nsb, V_HEAD, ts), BF16),
            jax.ShapeDtypeStruct((b, s, D_CONV), BF16),
        ],
        scratch_shapes=[pltpu.VMEM((ts + 8, D_CONV), F32)],
        compiler_params=_cparams(2),
        name="inproj",
    )(x, mod_l, w_in_p, w_uq_p, w_ukv, qg, kvg, conv_w, conv_g, rope_tab)


Q_CHUNK = 256


K_ROWS = 16


def _attn_kernel(qt_ref, k_ref, vt_ref, o_ref, *scratch, tq):
    i = pl.program_id(1)
    n_chunks = tq // Q_CHUNK
    chains = [(hh, c) + tuple(scratch[4 * (hh * n_chunks + c):4 * (hh * n_chunks + c) + 4])
              for hh in range(N_HEADS) for c in range(n_chunks)]
    for _, _, m_sc, l_sc, acc_sc, _ in chains:
        m_sc[...] = jnp.full(m_sc.shape, NEG_BIG, F32)
        l_sc[...] = jnp.zeros(l_sc.shape, F32)
        acc_sc[...] = jnp.zeros(acc_sc.shape, F32)

    def step(jk, diagonal):
        start = pl.multiple_of(jk * tq, tq)
        n_keys = lambda c: (c + 1) * Q_CHUNK if diagonal else tq
        for hh, c, _, _, _, s_sc in chains:
            nk = n_keys(c)
            kj = k_ref[0, hh, pl.ds(start, nk), :]
            s_sc[0:nk, :] = _dot(kj, qt_ref[0, hh, :, c * Q_CHUNK:(c + 1) * Q_CHUNK])
        for hh, c, m_sc, l_sc, acc_sc, s_sc in chains:
            nk = n_keys(c)
            if diagonal:
                d0 = nk - Q_CHUNK
                key = lax.broadcasted_iota(jnp.int32, (Q_CHUNK, Q_CHUNK), 0)
                qry = lax.broadcasted_iota(jnp.int32, (Q_CHUNK, Q_CHUNK), 1)
                s_sc[d0:nk, :] = jnp.where(key <= qry, s_sc[d0:nk, :], NEG_BIG)
            blk_max = s_sc[0:K_ROWS, :]
            for r in range(K_ROWS, nk, K_ROWS):
                blk_max = jnp.maximum(blk_max, s_sc[r:r + K_ROWS, :])
            m_prev = m_sc[...]
            m_new = jnp.maximum(m_prev, jnp.max(blk_max, axis=0, keepdims=True))
            alpha = jnp.exp2(m_prev - m_new)
            p_sum = jnp.zeros((K_ROWS, Q_CHUNK), F32)
            acc = alpha * acc_sc[...]
            for kt in range(0, nk, Q_CHUNK):
                parts = []
                for r in range(kt, kt + Q_CHUNK, K_ROWS):
                    p = jnp.exp2(s_sc[r:r + K_ROWS, :] - m_new)
                    p_sum = p_sum + p
                    parts.append(p.astype(BF16))
                acc = acc + _dot(vt_ref[0, hh, jk, :, kt:kt + Q_CHUNK], jnp.concatenate(parts, axis=0))
            l_sc[...] = alpha * l_sc[...] + jnp.sum(p_sum, axis=0, keepdims=True)
            acc_sc[...] = acc
            m_sc[...] = m_new

    def body(jk, carry):
        step(jk, False)
        return carry

    lax.fori_loop(0, i, body, 0)
    step(i, True)
    for hh, c, _, l_sc, acc_sc, _ in chains:
        o_ref[0, c * Q_CHUNK:(c + 1) * Q_CHUNK, hh * V_HEAD:(hh + 1) * V_HEAD] = (
            jnp.transpose(acc_sc[...] / l_sc[...]).astype(BF16))


def _attention(qt, k, vt, tq):
    b, nh, s, _ = k.shape
    return pl.pallas_call(
        functools.partial(_attn_kernel, tq=tq),
        grid=(b, s // tq),
        in_specs=[
            pl.BlockSpec((1, nh, QK_PAD, tq), lambda bi, i: (bi, 0, 0, i)),
            pl.BlockSpec((1, nh, s, QK_PAD), lambda bi, i: (bi, 0, 0, 0)),
            pl.BlockSpec((1, nh, s // tq, V_HEAD, tq), lambda bi, i: (bi, 0, 0, 0, 0)),
        ],
        out_specs=pl.BlockSpec((1, tq, D_ATTN), lambda bi, i: (bi, i, 0)),
        out_shape=jax.ShapeDtypeStruct((b, s, D_ATTN), BF16),
        scratch_shapes=[
            pltpu.VMEM((1, Q_CHUNK), F32), pltpu.VMEM((1, Q_CHUNK), F32), pltpu.VMEM((V_HEAD, Q_CHUNK), F32),
            pltpu.VMEM((tq, Q_CHUNK), F32),
        ] * (nh * (tq // Q_CHUNK)),
        compiler_params=_cparams(2),
        name="attention",
    )(qt, k, vt)


def _outproj_core(attn_ref, cv_ref, x_ref, mod_ref, wo_ref, ga_ref, lng_ref, lnb_ref):
    a = attn_ref[0].astype(F32)
    an = _rms(a, ga_ref[...]).astype(BF16)
    y = _dot(an, wo_ref[0:D_ATTN, :]) + _dot(cv_ref[0], wo_ref[D_ATTN:D_MODEL, :])
    gate1 = mod_ref[0, 2:3, :]
    x1 = _layer_norm(DEEPNORM_ALPHA * x_ref[0] + gate1 * y, lng_ref[...], lnb_ref[...])
    h2 = x1 * (1.0 + mod_ref[0, 4:5, :]) + mod_ref[0, 3:4, :]
    return x1, h2


def _outproj_kernel(attn_ref, cv_ref, x_ref, mod_ref, wo_ref, ga_ref, lng_ref, lnb_ref, x1_ref, h2_ref):
    x1, h2 = _outproj_core(attn_ref, cv_ref, x_ref, mod_ref, wo_ref, ga_ref, lng_ref, lnb_ref)
    x1_ref[0] = x1
    h2_ref[0] = h2.astype(BF16)


def _outproj_router_kernel(attn_ref, cv_ref, x_ref, mod_ref, wo_ref, ga_ref, lng_ref, lnb_ref, rw_ref, rb_ref,
                           x1_ref, h2_ref, rel_ref, gate_ref, relt_ref, base_ref, cnt_ref, run_sc, *, ts):
    first = jnp.logical_and(pl.program_id(0) == 0, pl.program_id(1) == 0)

    @pl.when(first)
    def _():
        run_sc[...] = jnp.zeros(run_sc.shape, F32)

    x1, h2 = _outproj_core(attn_ref, cv_ref, x_ref, mod_ref, wo_ref, ga_ref, lng_ref, lnb_ref)
    x1_ref[0] = x1
    h2_hi = h2.astype(BF16)
    h2_ref[0] = h2_hi

    lane = lax.broadcasted_iota(jnp.int32, (ts, LANES), 1)
    h2_lo = (h2 - h2_hi.astype(F32)).astype(BF16)
    prod = _dot(h2_hi, rw_ref[...]) + _dot(h2_lo, rw_ref[...])
    logits = prod + pltpu.roll(prod, LANES - N_EXPERTS, 1) + rb_ref[...]
    logits = jnp.where(lane < N_EXPERTS, logits, NEG_BIG)
    v1 = jnp.max(logits, axis=-1, keepdims=True)
    i1 = jnp.min(jnp.where(logits == v1, lane, LANES), axis=-1, keepdims=True)
    rest = jnp.where(lane == i1, NEG_BIG, logits)
    v2 = jnp.max(rest, axis=-1, keepdims=True)
    i2 = jnp.min(jnp.where(rest == v2, lane, LANES), axis=-1, keepdims=True)
    e21 = jnp.exp(v2 - v1)
    g1 = 1.0 / (1.0 + e21)
    g2 = e21 / (1.0 + e21)
    is1 = lane == i1
    is2 = lane == i2
    sel = jnp.where(jnp.logical_or(is1, is2), 1.0, 0.0)
    gate_ref[...] = jnp.where(is1, g1, 0.0) + jnp.where(is2, g2, 0.0)

    r_i = lax.broadcasted_iota(jnp.int32, (ts, ts), 0)
    c_i = lax.broadcasted_iota(jnp.int32, (ts, ts), 1)
    tri = jnp.where(c_i < r_i, 1.0, 0.0).astype(BF16)
    rank = _dot(tri, sel.astype(BF16))
    rel = jnp.where(sel > 0.0, rank, -1.0)
    rel_ref[...] = rel
    relt_ref[0] = jnp.transpose(rel)[0:N_EXPERTS, :]

    n_tile = jnp.sum(sel, axis=0, keepdims=True)
    n_pad = jnp.floor((n_tile + (ROW_ALIGN - 1)) * (1.0 / ROW_ALIGN)) * ROW_ALIGN
    base_ref[0] = run_sc[...].astype(jnp.int32)
    cnt_ref[0] = n_tile.astype(jnp.int32)
    run_sc[...] = run_sc[...] + n_pad


def _outproj(attn, convn, x, mod_l, w_o, g_attn, ln_g, ln_b, ts, router=None):
    b, s, _ = x.shape
    nsb = s // ts
    const = lambda shape: pl.BlockSpec(shape, lambda bi, j: (0,) * len(shape))
    tile3 = lambda w: pl.BlockSpec((1, ts, w), lambda bi, j: (bi, j, 0))
    in_specs = [
        tile3(D_ATTN), tile3(D_CONV), tile3(D_MODEL),
        pl.BlockSpec((1, 8, D_MODEL), lambda bi, j: (bi, 0, 0)),
        const((D_MODEL, D_MODEL)), const((1, D_ATTN)), const((1, D_MODEL)), const((1, D_MODEL)),
    ]
    out_specs = [tile3(D_MODEL), tile3(D_MODEL)]
    out_shape = [jax.ShapeDtypeStruct((b, s, D_MODEL), F32), jax.ShapeDtypeStruct((b, s, D_MODEL), BF16)]
    args = [attn, convn, x, mod_l, w_o, g_attn, ln_g, ln_b]
    if router is None:
        return pl.pallas_call(
            _outproj_kernel, grid=(b, nsb), in_specs=in_specs, out_specs=out_specs, out_shape=out_shape,
            compiler_params=_cparams(2), name="outproj",
        )(*args)
    rw, rb = router
    nt = b * nsb
    flat = lambda w: pl.BlockSpec((ts, w), lambda bi, j: (bi * nsb + j, 0))
    per_tile = lambda r, w: pl.BlockSpec((1, r, w), lambda bi, j: (bi * nsb + j, 0, 0))
    in_specs += [const((D_MODEL, LANES)), const((1, LANES))]
    out_specs += [flat(LANES), flat(LANES), per_tile(N_EXPERTS, ts), per_tile(1, LANES), per_tile(1, LANES)]
    out_shape += [
        jax.ShapeDtypeStruct((nt * ts, LANES), F32), jax.ShapeDtypeStruct((nt * ts, LANES), F32),
        jax.ShapeDtypeStruct((nt, N_EXPERTS, ts), F32),
        jax.ShapeDtypeStruct((nt, 1, LANES), jnp.int32), jax.ShapeDtypeStruct((nt, 1, LANES), jnp.int32),
    ]
    return pl.pallas_call(
        functools.partial(_outproj_router_kernel, ts=ts), grid=(b, nsb), in_specs=in_specs, out_specs=out_specs,
        out_shape=out_shape, scratch_shapes=[pltpu.VMEM((1, LANES), F32)],
        compiler_params=_cparams(2), name="outproj_router",
    )(*args, rw, rb)


FF_CHUNK = 256


def _swiglu(x, wg, wu, wd, width):
    out = None
    pending = None
    for c in range(0, width, FF_CHUNK):
        w = min(FF_CHUNK, width - c)
        g = _dot(x, wg(c, w))
        u = _dot(x, wu(c, w))
        if pending is not None:
            d = _dot(pending[0], wd(*pending[1]))
            out = d if out is None else out + d
        pending = ((_silu(g) * u).astype(BF16), (c, w))
    d = _dot(pending[0], wd(*pending[1]))
    return d if out is None else out + d


def _ffn_kernel(h_ref, x1_ref, mod_ref, wg_ref, wu_ref, wd_ref, lng_ref, lnb_ref, o_ref):
    f = _swiglu(h_ref[...], lambda c, w: wg_ref[:, c:c + w], lambda c, w: wu_ref[:, c:c + w],
                lambda c, w: wd_ref[c:c + w, :], D_FF)
    gate2 = mod_ref[0, 5:6, :]
    o_ref[...] = _layer_norm(DEEPNORM_ALPHA * x1_ref[...] + gate2 * f, lng_ref[...], lnb_ref[...])


def _dense_ffn(h2, x1, mod_l, wg, wu, wd, ln_g, ln_b, tm, tiles_per_batch):
    t = h2.shape[0]
    const = lambda shape: pl.BlockSpec(shape, lambda i: (0,) * len(shape), pipeline_mode=pl.Buffered(1))
    return pl.pallas_call(
        _ffn_kernel,
        grid=(t // tm,),
        in_specs=[
            pl.BlockSpec((tm, D_MODEL), lambda i: (i, 0)),
            pl.BlockSpec((tm, D_MODEL), lambda i: (i, 0)),
            pl.BlockSpec((1, 8, D_MODEL), lambda i: (i // tiles_per_batch, 0, 0)),
            const((D_MODEL, D_FF)), const((D_MODEL, D_FF)), const((D_FF, D_MODEL)),
            const((1, D_MODEL)), const((1, D_MODEL)),
        ],
        out_specs=pl.BlockSpec((tm, D_MODEL), lambda i: (i, 0)),
        out_shape=jax.ShapeDtypeStruct((t, D_MODEL), F32),
        compiler_params=_cparams(1),
        name="dense_ffn",
    )(h2, x1, mod_l, wg, wu, wd, ln_g, ln_b)


def _group_copy(src, dst, sem):
    return pltpu.make_async_copy(src, dst, sem)


def _dispatch_kernel(base_ref, cnt_ref, h_ref, relt_ref, xs_ref, stage, extra, zbuf, sems, xsem, zsem, *,
                     ts, zero_rows):
    i = pl.program_id(0)
    last = pl.num_programs(0) - 1
    par = lax.rem(i, 2)
    n_chunks = ts // SLOT_CHUNK
    h = h_ref[...]
    slot = lax.broadcasted_iota(jnp.int32, (SLOT_CHUNK, ts), 0).astype(F32)

    def onehot(e, c):
        return jnp.where(relt_ref[0, e:e + 1, :] == slot + float(c * SLOT_CHUNK), 1.0, 0.0).astype(BF16)

    def first_copy(step, e):
        a = pl.multiple_of(base_ref[step * N_EXPERTS + e], ROW_ALIGN)
        p = lax.rem(step, 2)
        return _group_copy(stage.at[p, e], xs_ref.at[e, pl.ds(a, SLOT_CHUNK), :], sems.at[p, e])

    for e in range(N_EXPERTS):
        stage[par, e] = _dot(onehot(e, 0), h).astype(BF16)

    @pl.when(i > 0)
    def _():
        for e in range(N_EXPERTS):
            first_copy(i - 1, e).wait()

    for e in range(N_EXPERTS):
        first_copy(i, e).start()

    for e in range(N_EXPERTS):
        n = cnt_ref[i * N_EXPERTS + e]
        for c in range(1, n_chunks):
            @pl.when(n > c * SLOT_CHUNK)
            def _(e=e, c=c):
                a = pl.multiple_of(base_ref[i * N_EXPERTS + e], ROW_ALIGN)
                extra[...] = _dot(onehot(e, c), h).astype(BF16)
                cp = _group_copy(extra, xs_ref.at[e, pl.ds(a + c * SLOT_CHUNK, SLOT_CHUNK), :], xsem)
                cp.start()
                cp.wait()

    @pl.when(i == last)
    def _():
        for e in range(N_EXPERTS):
            first_copy(i, e).wait()
        zbuf[...] = jnp.zeros(zbuf.shape, BF16)
        for e in range(N_EXPERTS):
            n = cnt_ref[i * N_EXPERTS + e]
            end = base_ref[i * N_EXPERTS + e] + ((n + (ROW_ALIGN - 1)) // ROW_ALIGN) * ROW_ALIGN
            end = pl.multiple_of(end, ROW_ALIGN)
            copies = [
                _group_copy(zbuf, xs_ref.at[e, pl.ds(end + r * SLOT_CHUNK, SLOT_CHUNK), :], zsem.at[e])
                for r in range(zero_rows // SLOT_CHUNK)
            ]
            for cp in copies:
                cp.start()
            for cp in copies:
                cp.wait()


def _dispatch(base, cnt, h2, relt, ts, cap, zero_rows):
    t = h2.shape[0]
    nt = t // ts
    n_chunks = ts // SLOT_CHUNK
    grid_spec = pltpu.PrefetchScalarGridSpec(
        num_scalar_prefetch=2,
        grid=(nt,),
        in_specs=[
            pl.BlockSpec((ts, D_MODEL), lambda i, b_, c_: (i, 0)),
            pl.BlockSpec((1, N_EXPERTS, ts), lambda i, b_, c_: (i, 0, 0)),
        ],
        out_specs=pl.BlockSpec(memory_space=pl.ANY),
        scratch_shapes=[
            pltpu.VMEM((2, N_EXPERTS, SLOT_CHUNK, D_MODEL), BF16),
            pltpu.VMEM((SLOT_CHUNK, D_MODEL), BF16),
            pltpu.VMEM((SLOT_CHUNK, D_MODEL), BF16),
            pltpu.SemaphoreType.DMA((2, N_EXPERTS)),
            pltpu.SemaphoreType.DMA(()),
            pltpu.SemaphoreType.DMA((N_EXPERTS,)),
        ],
    )
    return pl.pallas_call(
        functools.partial(_dispatch_kernel, ts=ts, zero_rows=zero_rows),
        grid_spec=grid_spec,
        out_shape=jax.ShapeDtypeStruct((N_EXPERTS, cap, D_MODEL), BF16),
        compiler_params=_cparams(1),
        name="moe_dispatch",
    )(base, cnt, h2, relt)


def _expert_kernel(te_ref, tr_ref, nv_ref, x_ref, wg_ref, wu_ref, wd_ref, o_ref, acc_sc):
    g_i = pl.program_id(0)
    k = pl.program_id(1)

    valid = g_i < nv_ref[0]

    @pl.when(jnp.logical_and(valid, k == 0))
    def _():
        acc_sc[...] = jnp.zeros(acc_sc.shape, F32)

    @pl.when(valid)
    def _():
        part = _swiglu(x_ref[0], lambda c, w: wg_ref[0, :, c:c + w].astype(BF16),
                       lambda c, w: wu_ref[0, :, c:c + w].astype(BF16),
                       lambda c, w: wd_ref[0, c:c + w, :].astype(BF16), wg_ref.shape[2])
        total = acc_sc[...] + part
        acc_sc[...] = total
        o_ref[0] = total.astype(BF16)


def _expert_ffn(tile_e, tile_r, n_valid, xs, wg, wu, wd, tm, tf, n_steps):
    nk = D_FF_EXPERT // tf
    cap = xs.shape[1]

    def k_eff(g, k, nv):
        return jnp.where(g < nv[0], k, nk - 1)

    grid_spec = pltpu.PrefetchScalarGridSpec(
        num_scalar_prefetch=3,
        grid=(n_steps, nk),
        in_specs=[
            pl.BlockSpec((1, tm, D_MODEL), lambda g, k, te, tr, nv: (te[g], tr[g], 0)),
            pl.BlockSpec((1, D_MODEL, tf), lambda g, k, te, tr, nv: (te[g], 0, k_eff(g, k, nv))),
            pl.BlockSpec((1, D_MODEL, tf), lambda g, k, te, tr, nv: (te[g], 0, k_eff(g, k, nv))),
            pl.BlockSpec((1, tf, D_MODEL), lambda g, k, te, tr, nv: (te[g], k_eff(g, k, nv), 0)),
        ],
        out_specs=pl.BlockSpec((1, tm, D_MODEL), lambda g, k, te, tr, nv: (te[g], tr[g], 0)),
        scratch_shapes=[pltpu.VMEM((tm, D_MODEL), F32)],
    )
    return pl.pallas_call(
        _expert_kernel,
        grid_spec=grid_spec,
        out_shape=jax.ShapeDtypeStruct((N_EXPERTS, cap, D_MODEL), BF16),
        compiler_params=_cparams(2),
        name="moe_experts",
    )(tile_e, tile_r, n_valid, xs, wg, wu, wd)


def _combine_kernel(base_ref, cnt_ref, lim_ref, ys_ref, rel_ref, gate_ref, x1_ref, mod_ref, lng_ref, lnb_ref, o_ref,
                    slab, xslab, acc_sc, sems, xsem, *, ts):
    i = pl.program_id(0)
    last = pl.num_programs(0) - 1
    par = lax.rem(i, 2)
    n_chunks = ts // SLOT_CHUNK
    slot = lax.broadcasted_iota(jnp.int32, (ts, SLOT_CHUNK), 1).astype(F32)

    def slab_start(step, e, c):
        a = base_ref[step * N_EXPERTS + e] + c * SLOT_CHUNK
        start = jnp.maximum(jnp.minimum(a, lim_ref[e] - SLOT_CHUNK), 0)
        return pl.multiple_of(start, ROW_ALIGN), a - start

    def first_copy(step, e):
        start, _ = slab_start(step, e, 0)
        p = lax.rem(step, 2)
        return _group_copy(ys_ref.at[e, pl.ds(start, SLOT_CHUNK), :], slab.at[p, e], sems.at[p, e])

    def contribution(e, c, rows):
        _, shift = slab_start(i, e, c)
        offset = (shift - c * SLOT_CHUNK).astype(F32)
        onehot = jnp.where(rel_ref[:, e:e + 1] + offset == slot, 1.0, 0.0).astype(BF16)
        return _dot(onehot, rows) * gate_ref[:, e:e + 1]

    @pl.when(i == 0)
    def _():
        for e in range(N_EXPERTS):
            first_copy(i, e).start()

    @pl.when(i < last)
    def _():
        for e in range(N_EXPERTS):
            first_copy(i + 1, e).start()

    for e in range(N_EXPERTS):
        first_copy(i, e).wait()
    f = contribution(0, 0, slab[par, 0])
    for e in range(1, N_EXPERTS):
        f = f + contribution(e, 0, slab[par, e])
    acc_sc[...] = f

    for e in range(N_EXPERTS):
        n = cnt_ref[i * N_EXPERTS + e]
        for c in range(1, n_chunks):
            @pl.when(n > c * SLOT_CHUNK)
            def _(e=e, c=c):
                start, _ = slab_start(i, e, c)
                cp = _group_copy(ys_ref.at[e, pl.ds(start, SLOT_CHUNK), :], xslab, xsem)
                cp.start()
                cp.wait()
                acc_sc[...] += contribution(e, c, xslab[...])

    gate2 = mod_ref[0, 5:6, :]
    o_ref[...] = _layer_norm(DEEPNORM_ALPHA * x1_ref[...] + gate2 * acc_sc[...], lng_ref[...], lnb_ref[...])


def _combine(base, cnt, lim, ys, rel, gates, x1, mod_l, ln_g, ln_b, ts, tiles_per_batch):
    t = x1.shape[0]
    grid_spec = pltpu.PrefetchScalarGridSpec(
        num_scalar_prefetch=3,
        grid=(t // ts,),
        in_specs=[
            pl.BlockSpec(memory_space=pl.ANY),
            pl.BlockSpec((ts, LANES), lambda i, *_: (i, 0)),
            pl.BlockSpec((ts, LANES), lambda i, *_: (i, 0)),
            pl.BlockSpec((ts, D_MODEL), lambda i, *_: (i, 0)),
            pl.BlockSpec((1, 8, D_MODEL), lambda i, *_: (i // tiles_per_batch, 0, 0)),
            pl.BlockSpec((1, D_MODEL), lambda i, *_: (0, 0)),
            pl.BlockSpec((1, D_MODEL), lambda i, *_: (0, 0)),
        ],
        out_specs=pl.BlockSpec((ts, D_MODEL), lambda i, *_: (i, 0)),
        scratch_shapes=[
            pltpu.VMEM((2, N_EXPERTS, SLOT_CHUNK, D_MODEL), BF16),
            pltpu.VMEM((SLOT_CHUNK, D_MODEL), BF16),
            pltpu.VMEM((ts, D_MODEL), F32),
            pltpu.SemaphoreType.DMA((2, N_EXPERTS)),
            pltpu.SemaphoreType.DMA(()),
        ],
    )
    return pl.pallas_call(
        functools.partial(_combine_kernel, ts=ts),
        grid_spec=grid_spec,
        out_shape=jax.ShapeDtypeStruct((t, D_MODEL), F32),
        compiler_params=_cparams(1),
        name="moe_combine",
    )(base, cnt, lim, ys, rel, gates, x1, mod_l, ln_g, ln_b)


def _expert_tile_plan(base, cnt, nt, tm, n_steps):
    last_base = base.reshape(nt, N_EXPERTS)[-1]
    last_cnt = cnt.reshape(nt, N_EXPERTS)[-1]
    total = last_base + ((last_cnt + (ROW_ALIGN - 1)) // ROW_ALIGN) * ROW_ALIGN
    tiles = jnp.maximum((total + tm - 1) // tm, 1)
    ends = jnp.cumsum(tiles)
    n_valid = ends[-1]
    g = jnp.minimum(jnp.arange(n_steps, dtype=jnp.int32), n_valid - 1)
    tile_e = jnp.sum((g[:, None] >= ends[None, :]).astype(jnp.int32), axis=1)
    tile_r = g - (ends - tiles)[tile_e]
    lim = (tiles * tm).astype(jnp.int32)
    return tile_e.astype(jnp.int32), tile_r.astype(jnp.int32), n_valid.reshape(1).astype(jnp.int32), lim


def _top2_moe(h2, x1, mod_l, routing, wg, wu, wd, ln_g, ln_b, ts, tiles_per_batch):
    rel, gates, relt, base3, cnt3 = routing
    t = h2.shape[0]
    nt = t // ts
    tm, tf = EXPERT_ROWS, 512
    base = base3[:, 0, :N_EXPERTS].reshape(-1)
    cnt = cnt3[:, 0, :N_EXPERTS].reshape(-1)
    zero_rows = -(-tm // SLOT_CHUNK) * SLOT_CHUNK
    rows_max = t + ROW_ALIGN * nt
    cap = -(-(rows_max + SLOT_CHUNK + zero_rows) // tm) * tm
    rows_all = 2 * t + ROW_ALIGN * N_EXPERTS * nt
    n_steps = -(-rows_all // tm) + N_EXPERTS
    xs = _dispatch(base, cnt, h2, relt, ts, cap, zero_rows)
    tile_e, tile_r, n_valid, lim = _expert_tile_plan(base, cnt, nt, tm, n_steps)
    ys = _expert_ffn(tile_e, tile_r, n_valid, xs, wg, wu, wd, tm, tf, n_steps)
    return _combine(base, cnt, lim, ys, rel, gates, x1, mod_l, ln_g, ln_b, ts, tiles_per_batch)


def kernel(x, c, positions, w_in, q_norm_g, kv_norm_g, w_uq, w_ukv, conv_w, mix_norm_g, w_o, w_ada, b_ada,
           ln1_g, ln1_b, ln2_g, ln2_b, ffn_w_gate, ffn_w_up, ffn_w_down, moe_router_w, moe_router_b,
           moe_w_gate, moe_w_up, moe_w_down):
    b, s, _ = x.shape
    n_layers = w_in.shape[0]
    ts = min(512, s)
    t = b * s
    tiles_per_batch = s // ts

    bp = -(-b // 16) * 16
    c_pad = jnp.pad(c, ((0, bp - b), (0, 0)))
    mod = _ada_mod(c_pad, w_ada, b_ada)
    mod = jnp.pad(jnp.transpose(mod[:, :, :b, :], (0, 2, 1, 3)), ((0, 0), (0, 0), (0, 2), (0, 0)))
    rope_tab = _rope_tables(positions)

    row = lambda v: v.reshape(1, -1)
    for l in range(n_layers):
        o3 = Q_LORA + KV_LORA + QK_ROPE
        w_in_p = jnp.concatenate(
            [w_in[l][:, :o3], jnp.zeros((D_MODEL, D_IN_PAD - w_in.shape[2]), F32), w_in[l][:, o3:]], axis=1
        ).astype(BF16)
        w_uq_p = jnp.pad(
            w_uq[l].reshape(Q_LORA, N_HEADS, QK_NOPE + QK_ROPE), ((0, 0), (0, 0), (0, QK_PAD - QK_NOPE - QK_ROPE))
        ).reshape(Q_LORA, N_HEADS * QK_PAD).astype(BF16)
        conv_w_p = jnp.pad(conv_w[l], ((0, 8 - CONV_WIDTH), (0, 0)))
        q, k, v, convn = _inproj(x, mod[l], w_in_p, w_uq_p, w_ukv[l].astype(BF16), row(q_norm_g[l]),
                                 row(kv_norm_g[l]), conv_w_p, row(mix_norm_g[l][D_ATTN:]), rope_tab, ts)
        attn = _attention(q, k, v, ts)
        common = (attn, convn, x, mod[l], w_o[l].astype(BF16), row(mix_norm_g[l][:D_ATTN]), row(ln1_g[l]),
                  row(ln1_b[l]), ts)
        if l % 2 == 0:
            i = l // 2
            x1, h2 = _outproj(*common)
            x = _dense_ffn(h2.reshape(t, D_MODEL), x1.reshape(t, D_MODEL), mod[l], ffn_w_gate[i].astype(BF16),
                           ffn_w_up[i].astype(BF16), ffn_w_down[i].astype(BF16), row(ln2_g[l]), row(ln2_b[l]),
                           ts, tiles_per_batch)
        else:
            i = l // 2
            rw_hi = moe_router_w[i].astype(BF16)
            rw_lo = (moe_router_w[i] - rw_hi.astype(F32)).astype(BF16)
            rw = jnp.pad(jnp.concatenate([rw_hi, rw_lo], axis=1), ((0, 0), (0, LANES - 2 * N_EXPERTS)))
            rb = jnp.pad(moe_router_b[i], (0, LANES - N_EXPERTS)).reshape(1, LANES)
            x1, h2, *routing = _outproj(*common, router=(rw, rb))
            x = _top2_moe(h2.reshape(t, D_MODEL), x1.reshape(t, D_MODEL), mod[l], routing,
                          moe_w_gate[i], moe_w_up[i], moe_w_down[i], row(ln2_g[l]), row(ln2_b[l]), ts, tiles_per_batch)
        x = x.reshape(b, s, D_MODEL)
    return x
```

```python
import functools

import jax
import jax.numpy as jnp
from jax import lax
from jax.experimental import pallas as pl
from jax.experimental.pallas import tpu as pltpu

F32 = jnp.float32
BF16 = jnp.bfloat16

D_MODEL = 1024
N_HEADS = 4
QK_NOPE = 128
QK_ROPE = 64
V_HEAD = 128
Q_LORA = 256
KV_LORA = 128
D_ATTN = N_HEADS * V_HEAD
D_CONV = D_MODEL - D_ATTN
CONV_WIDTH = 3
ROPE_BASE = 10000.0
D_FF = 2816
N_EXPERTS = 8
D_FF_EXPERT = 3584
RMS_EPS = 1e-6
LN_EPS = 1e-5
DEPTH = 2
DEEPNORM_ALPHA = (2 * DEPTH) ** 0.25
SM_SCALE = (QK_NOPE + QK_ROPE) ** -0.5
LOG2E = 1.4426950408889634
Q_SCALE = SM_SCALE * LOG2E

LANES = 128
QK_PAD = 2 * LANES
D_IN_PAD = 2048
ROW_ALIGN = 16
SLOT_CHUNK = 256
EXPERT_ROWS = 1152
EXPERT_ROW_SIZES = (384, 768, 1152)
VMEM_LIMIT = 56 * 1024 * 1024
NEG_BIG = -1e30


def _cparams(n_axes):
    return pltpu.CompilerParams(dimension_semantics=("arbitrary",) * n_axes, vmem_limit_bytes=VMEM_LIMIT)


def _dot(a, b):
    return jnp.dot(a, b, preferred_element_type=F32)


def _rms(x, g):
    return x * lax.rsqrt(jnp.mean(x * x, axis=-1, keepdims=True) + RMS_EPS) * g


def _layer_norm(x, g, b):
    mu = jnp.mean(x, axis=-1, keepdims=True)
    xc = x - mu
    var = jnp.mean(xc * xc, axis=-1, keepdims=True)
    return xc * lax.rsqrt(var + LN_EPS) * g + b


def _silu(x):
    return x * jax.nn.sigmoid(x)


def _ada_kernel(c_ref, w_ref, b_ref, o_ref):
    ca = _silu(c_ref[...]).astype(BF16)
    o_ref[0, 0] = _dot(ca, w_ref[0].astype(BF16)) + b_ref[0]


def _ada_mod(c_pad, w_ada, b_ada):
    n_layers = w_ada.shape[0]
    bp = c_pad.shape[0]
    return pl.pallas_call(
        _ada_kernel,
        grid=(n_layers, 6),
        in_specs=[
            pl.BlockSpec((bp, D_MODEL), lambda l, j: (0, 0)),
            pl.BlockSpec((1, D_MODEL, D_MODEL), lambda l, j: (l, 0, j)),
            pl.BlockSpec((1, 1, D_MODEL), lambda l, j: (l, 0, j)),
        ],
        out_specs=pl.BlockSpec((1, 1, bp, D_MODEL), lambda l, j: (l, j, 0, 0)),
        out_shape=jax.ShapeDtypeStruct((n_layers, 6, bp, D_MODEL), F32),
        compiler_params=_cparams(2),
        name="ada_mod",
    )(c_pad, w_ada, b_ada.reshape(n_layers, 1, 6 * D_MODEL))


def _rope_kernel(pos_ref, f_ref, o_ref):
    ang = pos_ref[...].astype(F32) * f_ref[0:1, :]
    c = jnp.cos(ang)
    s = jnp.sin(ang)
    o_ref[:, 0:LANES] = c * f_ref[1:2, :]
    o_ref[:, LANES:2 * LANES] = s * f_ref[2:3, :]
    o_ref[:, 2 * LANES:3 * LANES] = s * f_ref[3:4, :]


def _rope_tables(positions):
    t = positions.size
    tr = min(1024, t)
    half = QK_ROPE // 2
    inv_freq = ROPE_BASE ** (-jnp.arange(0, QK_ROPE, 2, dtype=F32) / QK_ROPE)
    zeros = jnp.zeros((half,), F32)
    ones = jnp.ones((half,), F32)
    consts = jnp.stack([
        jnp.concatenate([inv_freq, inv_freq, zeros, zeros]),
        jnp.concatenate([ones, ones, zeros, zeros]),
        jnp.concatenate([-ones, zeros, zeros, zeros]),
        jnp.concatenate([zeros, ones, zeros, zeros]),
    ])
    consts = jnp.pad(consts, ((0, 4), (0, 0)))
    return pl.pallas_call(
        _rope_kernel,
        grid=(t // tr,),
        in_specs=[pl.BlockSpec((tr, 1), lambda i: (i, 0)), pl.BlockSpec((8, LANES), lambda i: (0, 0))],
        out_specs=pl.BlockSpec((tr, 3 * LANES), lambda i: (i, 0)),
        out_shape=jax.ShapeDtypeStruct((t, 3 * LANES), F32),
        compiler_params=_cparams(1),
        name="rope_tables",
    )(positions.reshape(t, 1), consts)


def _inproj_kernel(x_ref, mod_ref, win_ref, wuq_ref, wukv_ref, qg_ref, kvg_ref, cw_ref, cg_ref, rope_ref,
                   qt_ref, k_ref, vt_ref, cv_ref, cu_ext, *, ts):
    j = pl.program_id(1)
    x = x_ref[0]
    shift = mod_ref[0, 0:1, :]
    scale = mod_ref[0, 1:2, :]
    h = (x * (1.0 + scale) + shift).astype(BF16)
    z_lat = _dot(h, win_ref[:, 0:512])
    c_q = z_lat[:, 0:Q_LORA]
    c_kv = z_lat[:, Q_LORA:Q_LORA + KV_LORA]
    k_rope = z_lat[:, 384:512]
    z_gate = _dot(h, win_ref[:, 512:1536])
    gate_b = z_gate[:, 0:D_CONV]
    gate_c = z_gate[:, D_CONV:2 * D_CONV]
    q = _dot(_rms(c_q, qg_ref[...]).astype(BF16), wuq_ref[...])
    kv = _dot(_rms(c_kv, kvg_ref[...]).astype(BF16), wukv_ref[...])
    u = _dot(h, win_ref[:, 1536:2048])

    cos_d = rope_ref[:, 0:LANES]
    sin_a = rope_ref[:, LANES:2 * LANES]
    sin_b = rope_ref[:, 2 * LANES:3 * LANES]

    def rope(t):
        return t * cos_d + pltpu.roll(t, 96, 1) * sin_a + pltpu.roll(t, 32, 1) * sin_b

    k_rot = rope(k_rope).astype(BF16)
    for hh in range(N_HEADS):
        qo = hh * QK_PAD
        qt_ref[0, hh, 0:LANES, :] = jnp.transpose(q[:, qo:qo + LANES] * Q_SCALE).astype(BF16)
        qt_ref[0, hh, LANES:QK_PAD, :] = jnp.transpose(rope(q[:, qo + LANES:qo + QK_PAD]) * Q_SCALE).astype(BF16)
        ko = hh * (QK_NOPE + V_HEAD)
        k_ref[0, hh, :, 0:LANES] = kv[:, ko:ko + QK_NOPE].astype(BF16)
        k_ref[0, hh, :, LANES:QK_PAD] = k_rot
        vt_ref[0, hh, 0] = jnp.transpose(kv[:, ko + QK_NOPE:ko + QK_NOPE + V_HEAD]).astype(BF16)

    @pl.when(j == 0)
    def _():
        cu_ext[0:8, :] = jnp.zeros((8, D_CONV), F32)

    @pl.when(j > 0)
    def _():
        cu_ext[0:8, :] = cu_ext[ts:ts + 8, :]

    cu_ext[8:ts + 8, :] = gate_c * u
    conv = (cw_ref[2:3, :] * cu_ext[8:ts + 8, :]
            + cw_ref[1:2, :] * cu_ext[pl.ds(7, ts), :]
            + cw_ref[0:1, :] * cu_ext[pl.ds(6, ts), :])
    conv = gate_b * conv
    cv_ref[0] = _rms(conv, cg_ref[...]).astype(BF16)


def _inproj(l, x, mod, w_in_p, w_uq_p, w_ukv, qg, kvg, conv_w, conv_g, rope_tab, ts):
    b, s, _ = x.shape
    nsb = s // ts
    const = lambda shape: pl.BlockSpec((None,) + shape, lambda bi, j: (l,) + (0,) * len(shape))
    return pl.pallas_call(
        functools.partial(_inproj_kernel, ts=ts),
        grid=(b, nsb),
        in_specs=[
            pl.BlockSpec((1, ts, D_MODEL), lambda bi, j: (bi, j, 0)),
            pl.BlockSpec((None, 1, 8, D_MODEL), lambda bi, j: (l, bi, 0, 0)),
            const((D_MODEL, D_IN_PAD)),
            const((Q_LORA, N_HEADS * QK_PAD)),
            const((KV_LORA, N_HEADS * (QK_NOPE + V_HEAD))),
            const((1, Q_LORA)),
            const((1, KV_LORA)),
            const((8, D_CONV)),
            const((1, D_CONV)),
            pl.BlockSpec((ts, 3 * LANES), lambda bi, j: (bi * nsb + j, 0)),
        ],
        out_specs=[
            pl.BlockSpec((1, N_HEADS, QK_PAD, ts), lambda bi, j: (bi, 0, 0, j)),
            pl.BlockSpec((1, N_HEADS, ts, QK_PAD), lambda bi, j: (bi, 0, j, 0)),
            pl.BlockSpec((1, N_HEADS, 1, V_HEAD, ts), lambda bi, j: (bi, 0, j, 0, 0)),
            pl.BlockSpec((1, ts, D_CONV), lambda bi, j: (bi, j, 0)),
        ],
        out_shape=[
            jax.ShapeDtypeStruct((b, N_HEADS, QK_PAD, s), BF16),
            jax.ShapeDtypeStruct((b, N_HEADS, s, QK_PAD), BF16),
            jax.ShapeDtypeStruct((b, N_HEADS, nsb, V_HEAD, ts), BF16),
            jax.ShapeDtypeStruct((b, s, D_CONV), BF16),
        ],
        scratch_shapes=[pltpu.VMEM((ts + 8, D_CONV), F32)],
        compiler_params=_cparams(2),
        name="inproj",
    )(x, mod, w_in_p, w_uq_p, w_ukv, qg, kvg, conv_w, conv_g, rope_tab)


Q_CHUNK = 256


K_ROWS = 16


def _attn_kernel(qt_ref, k_ref, vt_ref, o_ref, *scratch, tq):
    i = pl.program_id(1)
    n_chunks = tq // Q_CHUNK
    chains = [(hh, c) + tuple(scratch[4 * (hh * n_chunks + c):4 * (hh * n_chunks + c) + 4])
              for hh in range(N_HEADS) for c in range(n_chunks)]
    for _, _, m_sc, l_sc, acc_sc, _ in chains:
        m_sc[...] = jnp.full(m_sc.shape, NEG_BIG, F32)
        l_sc[...] = jnp.zeros(l_sc.shape, F32)
        acc_sc[...] = jnp.zeros(acc_sc.shape, F32)

    def step(jk, diagonal):
        start = pl.multiple_of(jk * tq, tq)
        n_keys = lambda c: (c + 1) * Q_CHUNK if diagonal else tq
        for hh, c, _, _, _, s_sc in chains:
            nk = n_keys(c)
            kj = k_ref[0, hh, pl.ds(start, nk), :]
            s_sc[0:nk, :] = _dot(kj, qt_ref[0, hh, :, c * Q_CHUNK:(c + 1) * Q_CHUNK])
        for hh, c, m_sc, l_sc, acc_sc, s_sc in chains:
            nk = n_keys(c)
            if diagonal:
                d0 = nk - Q_CHUNK
                key = lax.broadcasted_iota(jnp.int32, (Q_CHUNK, Q_CHUNK), 0)
                qry = lax.broadcasted_iota(jnp.int32, (Q_CHUNK, Q_CHUNK), 1)
                s_sc[d0:nk, :] = jnp.where(key <= qry, s_sc[d0:nk, :], NEG_BIG)
            blk_max = s_sc[0:K_ROWS, :]
            for r in range(K_ROWS, nk, K_ROWS):
                blk_max = jnp.maximum(blk_max, s_sc[r:r + K_ROWS, :])
            m_prev = m_sc[...]
            m_new = jnp.maximum(m_prev, jnp.max(blk_max, axis=0, keepdims=True))
            alpha = jnp.exp2(m_prev - m_new)
            p_sum = jnp.zeros((K_ROWS, Q_CHUNK), F32)
            acc = alpha * acc_sc[...]
            for kt in range(0, nk, Q_CHUNK):
                parts = []
                for r in range(kt, kt + Q_CHUNK, K_ROWS):
                    p = jnp.exp2(s_sc[r:r + K_ROWS, :] - m_new)
                    p_sum = p_sum + p
                    parts.append(p.astype(BF16))
                acc = acc + _dot(vt_ref[0, hh, jk, :, kt:kt + Q_CHUNK], jnp.concatenate(parts, axis=0))
            l_sc[...] = alpha * l_sc[...] + jnp.sum(p_sum, axis=0, keepdims=True)
            acc_sc[...] = acc
            m_sc[...] = m_new

    def body(jk, carry):
        step(jk, False)
        return carry

    lax.fori_loop(0, i, body, 0)
    step(i, True)
    for hh, c, _, l_sc, acc_sc, _ in chains:
        o_ref[0, c * Q_CHUNK:(c + 1) * Q_CHUNK, hh * V_HEAD:(hh + 1) * V_HEAD] = (
            jnp.transpose(acc_sc[...] / l_sc[...]).astype(BF16))


def _attention(qt, k, vt, tq):
    b, nh, s, _ = k.shape
    return pl.pallas_call(
        functools.partial(_attn_kernel, tq=tq),
        grid=(b, s // tq),
        in_specs=[
            pl.BlockSpec((1, nh, QK_PAD, tq), lambda bi, i: (bi, 0, 0, i)),
            pl.BlockSpec((1, nh, s, QK_PAD), lambda bi, i: (bi, 0, 0, 0)),
            pl.BlockSpec((1, nh, s // tq, V_HEAD, tq), lambda bi, i: (bi, 0, 0, 0, 0)),
        ],
        out_specs=pl.BlockSpec((1, tq, D_ATTN), lambda bi, i: (bi, i, 0)),
        out_shape=jax.ShapeDtypeStruct((b, s, D_ATTN), BF16),
        scratch_shapes=[
            pltpu.VMEM((1, Q_CHUNK), F32), pltpu.VMEM((1, Q_CHUNK), F32), pltpu.VMEM((V_HEAD, Q_CHUNK), F32),
            pltpu.VMEM((tq, Q_CHUNK), F32),
        ] * (nh * (tq // Q_CHUNK)),
        compiler_params=_cparams(2),
        name="attention",
    )(qt, k, vt)


def _outproj_core(attn_ref, cv_ref, x_ref, mod_ref, wo_ref, ga_ref, lng_ref, lnb_ref):
    a = attn_ref[0].astype(F32)
    an = _rms(a, ga_ref[...]).astype(BF16)
    y = _dot(an, wo_ref[0:D_ATTN, :]) + _dot(cv_ref[0], wo_ref[D_ATTN:D_MODEL, :])
    gate1 = mod_ref[0, 2:3, :]
    x1 = _layer_norm(DEEPNORM_ALPHA * x_ref[0] + gate1 * y, lng_ref[...], lnb_ref[...])
    h2 = x1 * (1.0 + mod_ref[0, 4:5, :]) + mod_ref[0, 3:4, :]
    return x1, h2


def _outproj_kernel(attn_ref, cv_ref, x_ref, mod_ref, wo_ref, ga_ref, lng_ref, lnb_ref, x1_ref, h2_ref):
    x1, h2 = _outproj_core(attn_ref, cv_ref, x_ref, mod_ref, wo_ref, ga_ref, lng_ref, lnb_ref)
    x1_ref[0] = x1
    h2_ref[0] = h2.astype(BF16)


def _outproj_router_kernel(attn_ref, cv_ref, x_ref, mod_ref, wo_ref, ga_ref, lng_ref, lnb_ref, rw_ref, rb_ref,
                           x1_ref, h2_ref, rel_ref, gate_ref, relt_ref, base_ref, cnt_ref, run_sc, *, ts):
    first = jnp.logical_and(pl.program_id(0) == 0, pl.program_id(1) == 0)

    @pl.when(first)
    def _():
        run_sc[...] = jnp.zeros(run_sc.shape, F32)

    x1, h2 = _outproj_core(attn_ref, cv_ref, x_ref, mod_ref, wo_ref, ga_ref, lng_ref, lnb_ref)
    x1_ref[0] = x1
    h2_hi = h2.astype(BF16)
    h2_ref[0] = h2_hi

    lane = lax.broadcasted_iota(jnp.int32, (ts, LANES), 1)
    h2_lo = (h2 - h2_hi.astype(F32)).astype(BF16)
    prod = _dot(h2_hi, rw_ref[...]) + _dot(h2_lo, rw_ref[...])
    logits = prod + pltpu.roll(prod, LANES - N_EXPERTS, 1) + rb_ref[...]
    logits = jnp.where(lane < N_EXPERTS, logits, NEG_BIG)
    v1 = jnp.max(logits, axis=-1, keepdims=True)
    i1 = jnp.min(jnp.where(logits == v1, lane, LANES), axis=-1, keepdims=True)
    rest = jnp.where(lane == i1, NEG_BIG, logits)
    v2 = jnp.max(rest, axis=-1, keepdims=True)
    i2 = jnp.min(jnp.where(rest == v2, lane, LANES), axis=-1, keepdims=True)
    e21 = jnp.exp(v2 - v1)
    g1 = 1.0 / (1.0 + e21)
    g2 = e21 / (1.0 + e21)
    is1 = lane == i1
    is2 = lane == i2
    sel = jnp.where(jnp.logical_or(is1, is2), 1.0, 0.0)
    gate_ref[...] = jnp.where(is1, g1, 0.0) + jnp.where(is2, g2, 0.0)

    r_i = lax.broadcasted_iota(jnp.int32, (ts, ts), 0)
    c_i = lax.broadcasted_iota(jnp.int32, (ts, ts), 1)
    tri = jnp.where(c_i < r_i, 1.0, 0.0).astype(BF16)
    rank = _dot(tri, sel.astype(BF16))
    rel = jnp.where(sel > 0.0, rank, -1.0)
    rel_ref[...] = rel
    relt_ref[0] = jnp.transpose(rel)[0:N_EXPERTS, :]

    n_tile = jnp.sum(sel, axis=0, keepdims=True)
    n_pad = jnp.floor((n_tile + (ROW_ALIGN - 1)) * (1.0 / ROW_ALIGN)) * ROW_ALIGN
    base_ref[0] = run_sc[...].astype(jnp.int32)
    cnt_ref[0] = n_tile.astype(jnp.int32)
    run_sc[...] = run_sc[...] + n_pad


def _outproj(l, attn, convn, x, mod, w_o, g_attn, ln_g, ln_b, ts, router=None):
    b, s, _ = x.shape
    nsb = s // ts
    const = lambda shape, idx=l: pl.BlockSpec((None,) + shape, lambda bi, j: (idx,) + (0,) * len(shape))
    tile3 = lambda w: pl.BlockSpec((1, ts, w), lambda bi, j: (bi, j, 0))
    in_specs = [
        tile3(D_ATTN), tile3(D_CONV), tile3(D_MODEL),
        pl.BlockSpec((None, 1, 8, D_MODEL), lambda bi, j: (l, bi, 0, 0)),
        const((D_MODEL, D_MODEL)), const((1, D_ATTN)), const((1, D_MODEL)), const((1, D_MODEL)),
    ]
    out_specs = [tile3(D_MODEL), tile3(D_MODEL)]
    out_shape = [jax.ShapeDtypeStruct((b, s, D_MODEL), F32), jax.ShapeDtypeStruct((b, s, D_MODEL), BF16)]
    args = [attn, convn, x, mod, w_o, g_attn, ln_g, ln_b]
    if router is None:
        return pl.pallas_call(
            _outproj_kernel, grid=(b, nsb), in_specs=in_specs, out_specs=out_specs, out_shape=out_shape,
            compiler_params=_cparams(2), name="outproj",
        )(*args)
    moe_i, rw, rb = router
    nt = b * nsb
    flat = lambda w: pl.BlockSpec((ts, w), lambda bi, j: (bi * nsb + j, 0))
    per_tile = lambda r, w: pl.BlockSpec((1, r, w), lambda bi, j: (bi * nsb + j, 0, 0))
    in_specs += [const((D_MODEL, LANES), moe_i), const((1, LANES), moe_i)]
    out_specs += [flat(LANES), flat(LANES), per_tile(N_EXPERTS, ts), per_tile(1, LANES), per_tile(1, LANES)]
    out_shape += [
        jax.ShapeDtypeStruct((nt * ts, LANES), F32), jax.ShapeDtypeStruct((nt * ts, LANES), F32),
        jax.ShapeDtypeStruct((nt, N_EXPERTS, ts), F32),
        jax.ShapeDtypeStruct((nt, 1, LANES), jnp.int32), jax.ShapeDtypeStruct((nt, 1, LANES), jnp.int32),
    ]
    return pl.pallas_call(
        functools.partial(_outproj_router_kernel, ts=ts), grid=(b, nsb), in_specs=in_specs, out_specs=out_specs,
        out_shape=out_shape, scratch_shapes=[pltpu.VMEM((1, LANES), F32)],
        compiler_params=_cparams(2), name="outproj_router",
    )(*args, rw, rb)


FF_CHUNK = 256


def _swiglu(x, wg, wu, wd, width):
    out = None
    pending = None
    for c in range(0, width, FF_CHUNK):
        w = min(FF_CHUNK, width - c)
        g = _dot(x, wg(c, w))
        u = _dot(x, wu(c, w))
        if pending is not None:
            d = _dot(pending[0], wd(*pending[1]))
            out = d if out is None else out + d
        pending = ((_silu(g) * u).astype(BF16), (c, w))
    d = _dot(pending[0], wd(*pending[1]))
    return d if out is None else out + d


def _ffn_kernel(h_ref, x1_ref, mod_ref, wg_ref, wu_ref, wd_ref, lng_ref, lnb_ref, o_ref):
    f = _swiglu(h_ref[...], lambda c, w: wg_ref[:, c:c + w], lambda c, w: wu_ref[:, c:c + w],
                lambda c, w: wd_ref[c:c + w, :], D_FF)
    gate2 = mod_ref[0, 5:6, :]
    o_ref[...] = _layer_norm(DEEPNORM_ALPHA * x1_ref[...] + gate2 * f, lng_ref[...], lnb_ref[...])


def _dense_ffn(l, dense_i, h2, x1, mod, wg, wu, wd, ln_g, ln_b, tm, tiles_per_batch):
    t = h2.shape[0]
    const = lambda shape, idx: pl.BlockSpec((None,) + shape, lambda i: (idx,) + (0,) * len(shape),
                                            pipeline_mode=pl.Buffered(1))
    return pl.pallas_call(
        _ffn_kernel,
        grid=(t // tm,),
        in_specs=[
            pl.BlockSpec((tm, D_MODEL), lambda i: (i, 0)),
            pl.BlockSpec((tm, D_MODEL), lambda i: (i, 0)),
            pl.BlockSpec((None, 1, 8, D_MODEL), lambda i: (l, i // tiles_per_batch, 0, 0)),
            const((D_MODEL, D_FF), dense_i), const((D_MODEL, D_FF), dense_i), const((D_FF, D_MODEL), dense_i),
            const((1, D_MODEL), l), const((1, D_MODEL), l),
        ],
        out_specs=pl.BlockSpec((tm, D_MODEL), lambda i: (i, 0)),
        out_shape=jax.ShapeDtypeStruct((t, D_MODEL), F32),
        compiler_params=_cparams(1),
        name="dense_ffn",
    )(h2, x1, mod, wg, wu, wd, ln_g, ln_b)


def _group_copy(src, dst, sem):
    return pltpu.make_async_copy(src, dst, sem)


def _dispatch_kernel(base_ref, cnt_ref, h_ref, relt_ref, xs_ref, stage, extra, zbuf, sems, xsem, zsem, *,
                     ts, zero_rows):
    i = pl.program_id(0)
    last = pl.num_programs(0) - 1
    par = lax.rem(i, 2)
    n_chunks = ts // SLOT_CHUNK
    h = h_ref[...]
    slot = lax.broadcasted_iota(jnp.int32, (SLOT_CHUNK, ts), 0).astype(F32)

    def onehot(e, c):
        return jnp.where(relt_ref[0, e:e + 1, :] == slot + float(c * SLOT_CHUNK), 1.0, 0.0).astype(BF16)

    def first_copy(step, e):
        a = pl.multiple_of(base_ref[step * N_EXPERTS + e], ROW_ALIGN)
        p = lax.rem(step, 2)
        return _group_copy(stage.at[p, e], xs_ref.at[e, pl.ds(a, SLOT_CHUNK), :], sems.at[p, e])

    for e in range(N_EXPERTS):
        stage[par, e] = _dot(onehot(e, 0), h).astype(BF16)

    @pl.when(i > 0)
    def _():
        for e in range(N_EXPERTS):
            first_copy(i - 1, e).wait()

    for e in range(N_EXPERTS):
        first_copy(i, e).start()

    for e in range(N_EXPERTS):
        n = cnt_ref[i * N_EXPERTS + e]
        for c in range(1, n_chunks):
            @pl.when(n > c * SLOT_CHUNK)
            def _(e=e, c=c):
                a = pl.multiple_of(base_ref[i * N_EXPERTS + e], ROW_ALIGN)
                extra[...] = _dot(onehot(e, c), h).astype(BF16)
                cp = _group_copy(extra, xs_ref.at[e, pl.ds(a + c * SLOT_CHUNK, SLOT_CHUNK), :], xsem)
                cp.start()
                cp.wait()

    @pl.when(i == last)
    def _():
        for e in range(N_EXPERTS):
            first_copy(i, e).wait()
        zbuf[...] = jnp.zeros(zbuf.shape, BF16)
        for e in range(N_EXPERTS):
            n = cnt_ref[i * N_EXPERTS + e]
            end = base_ref[i * N_EXPERTS + e] + ((n + (ROW_ALIGN - 1)) // ROW_ALIGN) * ROW_ALIGN
            end = pl.multiple_of(end, ROW_ALIGN)
            copies = [
                _group_copy(zbuf, xs_ref.at[e, pl.ds(end + r * SLOT_CHUNK, SLOT_CHUNK), :], zsem.at[e])
                for r in range(zero_rows // SLOT_CHUNK)
            ]
            for cp in copies:
                cp.start()
            for cp in copies:
                cp.wait()


def _dispatch(base, cnt, h2, relt, ts, cap, zero_rows):
    t = h2.shape[0]
    nt = t // ts
    n_chunks = ts // SLOT_CHUNK
    grid_spec = pltpu.PrefetchScalarGridSpec(
        num_scalar_prefetch=2,
        grid=(nt,),
        in_specs=[
            pl.BlockSpec((ts, D_MODEL), lambda i, b_, c_: (i, 0)),
            pl.BlockSpec((1, N_EXPERTS, ts), lambda i, b_, c_: (i, 0, 0)),
        ],
        out_specs=pl.BlockSpec(memory_space=pl.ANY),
        scratch_shapes=[
            pltpu.VMEM((2, N_EXPERTS, SLOT_CHUNK, D_MODEL), BF16),
            pltpu.VMEM((SLOT_CHUNK, D_MODEL), BF16),
            pltpu.VMEM((SLOT_CHUNK, D_MODEL), BF16),
            pltpu.SemaphoreType.DMA((2, N_EXPERTS)),
            pltpu.SemaphoreType.DMA(()),
            pltpu.SemaphoreType.DMA((N_EXPERTS,)),
        ],
    )
    return pl.pallas_call(
        functools.partial(_dispatch_kernel, ts=ts, zero_rows=zero_rows),
        grid_spec=grid_spec,
        out_shape=jax.ShapeDtypeStruct((N_EXPERTS, cap, D_MODEL), BF16),
        compiler_params=_cparams(1),
        name="moe_dispatch",
    )(base, cnt, h2, relt)


def _expert_kernel(te_ref, tr_ref, rows_ref, nv_ref, x_ref, wg_ref, wu_ref, wd_ref, o_ref, acc_sc):
    g_i = pl.program_id(0)
    k = pl.program_id(1)

    valid = g_i < nv_ref[0]

    @pl.when(jnp.logical_and(valid, k == 0))
    def _():
        acc_sc[...] = jnp.zeros(acc_sc.shape, F32)

    for m in EXPERT_ROW_SIZES:
        @pl.when(jnp.logical_and(valid, rows_ref[g_i] == m))
        def _(m=m):
            part = _swiglu(x_ref[0, 0:m, :], lambda c, w: wg_ref[0, :, c:c + w].astype(BF16),
                           lambda c, w: wu_ref[0, :, c:c + w].astype(BF16),
                           lambda c, w: wd_ref[0, c:c + w, :].astype(BF16), wg_ref.shape[2])
            total = acc_sc[0:m, :] + part
            acc_sc[0:m, :] = total
            o_ref[0, 0:m, :] = total.astype(BF16)


def _expert_ffn(moe_i, tile_e, tile_r, tile_rows, n_valid, xs, wg, wu, wd, tf, n_steps):
    tm = EXPERT_ROWS
    nk = D_FF_EXPERT // tf
    cap = xs.shape[1]

    def k_eff(g, k, nv):
        return jnp.where(g < nv[0], k, nk - 1)

    grid_spec = pltpu.PrefetchScalarGridSpec(
        num_scalar_prefetch=4,
        grid=(n_steps, nk),
        in_specs=[
            pl.BlockSpec((1, tm, D_MODEL), lambda g, k, te, tr, rw, nv: (te[g], tr[g], 0)),
            pl.BlockSpec((None, 1, D_MODEL, tf), lambda g, k, te, tr, rw, nv: (moe_i, te[g], 0, k_eff(g, k, nv))),
            pl.BlockSpec((None, 1, D_MODEL, tf), lambda g, k, te, tr, rw, nv: (moe_i, te[g], 0, k_eff(g, k, nv))),
            pl.BlockSpec((None, 1, tf, D_MODEL), lambda g, k, te, tr, rw, nv: (moe_i, te[g], k_eff(g, k, nv), 0)),
        ],
        out_specs=pl.BlockSpec((1, tm, D_MODEL), lambda g, k, te, tr, rw, nv: (te[g], tr[g], 0)),
        scratch_shapes=[pltpu.VMEM((tm, D_MODEL), F32)],
    )
    return pl.pallas_call(
        _expert_kernel,
        grid_spec=grid_spec,
        out_shape=jax.ShapeDtypeStruct((N_EXPERTS, cap, D_MODEL), BF16),
        compiler_params=_cparams(2),
        name="moe_experts",
    )(tile_e, tile_r, tile_rows, n_valid, xs, wg, wu, wd)


def _combine_kernel(base_ref, cnt_ref, lim_ref, ys_ref, rel_ref, gate_ref, x1_ref, mod_ref, lng_ref, lnb_ref, o_ref,
                    slab, xslab, acc_sc, sems, xsem, *, ts):
    i = pl.program_id(0)
    last = pl.num_programs(0) - 1
    par = lax.rem(i, 2)
    n_chunks = ts // SLOT_CHUNK
    slot = lax.broadcasted_iota(jnp.int32, (ts, SLOT_CHUNK), 1).astype(F32)

    def slab_start(step, e, c):
        a = base_ref[step * N_EXPERTS + e] + c * SLOT_CHUNK
        start = jnp.maximum(jnp.minimum(a, lim_ref[e] - SLOT_CHUNK), 0)
        return pl.multiple_of(start, ROW_ALIGN), a - start

    def first_copy(step, e):
        start, _ = slab_start(step, e, 0)
        p = lax.rem(step, 2)
        return _group_copy(ys_ref.at[e, pl.ds(start, SLOT_CHUNK), :], slab.at[p, e], sems.at[p, e])

    def contribution(e, c, rows):
        _, shift = slab_start(i, e, c)
        offset = (shift - c * SLOT_CHUNK).astype(F32)
        onehot = jnp.where(rel_ref[:, e:e + 1] + offset == slot, 1.0, 0.0).astype(BF16)
        return _dot(onehot, rows) * gate_ref[:, e:e + 1]

    @pl.when(i == 0)
    def _():
        for e in range(N_EXPERTS):
            first_copy(i, e).start()

    @pl.when(i < last)
    def _():
        for e in range(N_EXPERTS):
            first_copy(i + 1, e).start()

    for e in range(N_EXPERTS):
        first_copy(i, e).wait()
    f = contribution(0, 0, slab[par, 0])
    for e in range(1, N_EXPERTS):
        f = f + contribution(e, 0, slab[par, e])
    acc_sc[...] = f

    for e in range(N_EXPERTS):
        n = cnt_ref[i * N_EXPERTS + e]
        for c in range(1, n_chunks):
            @pl.when(n > c * SLOT_CHUNK)
            def _(e=e, c=c):
                start, _ = slab_start(i, e, c)
                cp = _group_copy(ys_ref.at[e, pl.ds(start, SLOT_CHUNK), :], xslab, xsem)
                cp.start()
                cp.wait()
                acc_sc[...] += contribution(e, c, xslab[...])

    gate2 = mod_ref[0, 5:6, :]
    o_ref[...] = _layer_norm(DEEPNORM_ALPHA * x1_ref[...] + gate2 * acc_sc[...], lng_ref[...], lnb_ref[...])


def _combine(l, base, cnt, lim, ys, rel, gates, x1, mod, ln_g, ln_b, ts, tiles_per_batch):
    t = x1.shape[0]
    grid_spec = pltpu.PrefetchScalarGridSpec(
        num_scalar_prefetch=3,
        grid=(t // ts,),
        in_specs=[
            pl.BlockSpec(memory_space=pl.ANY),
            pl.BlockSpec((ts, LANES), lambda i, *_: (i, 0)),
            pl.BlockSpec((ts, LANES), lambda i, *_: (i, 0)),
            pl.BlockSpec((ts, D_MODEL), lambda i, *_: (i, 0)),
            pl.BlockSpec((None, 1, 8, D_MODEL), lambda i, *_: (l, i // tiles_per_batch, 0, 0)),
            pl.BlockSpec((None, 1, D_MODEL), lambda i, *_: (l, 0, 0)),
            pl.BlockSpec((None, 1, D_MODEL), lambda i, *_: (l, 0, 0)),
        ],
        out_specs=pl.BlockSpec((ts, D_MODEL), lambda i, *_: (i, 0)),
        scratch_shapes=[
            pltpu.VMEM((2, N_EXPERTS, SLOT_CHUNK, D_MODEL), BF16),
            pltpu.VMEM((SLOT_CHUNK, D_MODEL), BF16),
            pltpu.VMEM((ts, D_MODEL), F32),
            pltpu.SemaphoreType.DMA((2, N_EXPERTS)),
            pltpu.SemaphoreType.DMA(()),
        ],
    )
    return pl.pallas_call(
        functools.partial(_combine_kernel, ts=ts),
        grid_spec=grid_spec,
        out_shape=jax.ShapeDtypeStruct((t, D_MODEL), F32),
        compiler_params=_cparams(1),
        name="moe_combine",
    )(base, cnt, lim, ys, rel, gates, x1, mod, ln_g, ln_b)


def _expert_tile_plan(base, cnt, nt, n_steps):
    tm = EXPERT_ROWS
    i32 = lambda v: v.astype(jnp.int32)
    last_base = base.reshape(nt, N_EXPERTS)[-1]
    last_cnt = cnt.reshape(nt, N_EXPERTS)[-1]
    total = last_base + ((last_cnt + (ROW_ALIGN - 1)) // ROW_ALIGN) * ROW_ALIGN
    tiles = jnp.maximum((total + tm - 1) // tm, 1)
    rest = total - (tiles - 1) * tm
    sizes = jnp.asarray(EXPERT_ROW_SIZES, jnp.int32)
    last_rows = sizes[jnp.sum((rest[:, None] > sizes[None, :]).astype(jnp.int32), axis=1)]
    ends = jnp.cumsum(tiles)
    n_valid = ends[-1]
    g = jnp.minimum(jnp.arange(n_steps, dtype=jnp.int32), n_valid - 1)
    tile_e = jnp.sum((g[:, None] >= ends[None, :]).astype(jnp.int32), axis=1)
    tile_r = g - (ends - tiles)[tile_e]
    tile_rows = jnp.where(tile_r == tiles[tile_e] - 1, last_rows[tile_e], tm)
    lim = (tiles - 1) * tm + last_rows
    return i32(tile_e), i32(tile_r), i32(tile_rows), i32(n_valid.reshape(1)), i32(lim)


def _top2_moe(l, moe_i, h2, x1, mod, routing, wg, wu, wd, ln_g, ln_b, ts, tiles_per_batch):
    rel, gates, relt, base3, cnt3 = routing
    t = h2.shape[0]
    nt = t // ts
    tm, tf = EXPERT_ROWS, 512
    base = base3[:, 0, :N_EXPERTS].reshape(-1)
    cnt = cnt3[:, 0, :N_EXPERTS].reshape(-1)
    zero_rows = -(-tm // SLOT_CHUNK) * SLOT_CHUNK
    rows_max = t + ROW_ALIGN * nt
    cap = -(-(rows_max + SLOT_CHUNK + zero_rows) // tm) * tm
    rows_all = 2 * t + ROW_ALIGN * N_EXPERTS * nt
    n_steps = -(-rows_all // tm) + N_EXPERTS
    xs = _dispatch(base, cnt, h2, relt, ts, cap, zero_rows)
    tile_e, tile_r, tile_rows, n_valid, lim = _expert_tile_plan(base, cnt, nt, n_steps)
    ys = _expert_ffn(moe_i, tile_e, tile_r, tile_rows, n_valid, xs, wg, wu, wd, tf, n_steps)
    return _combine(l, base, cnt, lim, ys, rel, gates, x1, mod, ln_g, ln_b, ts, tiles_per_batch)


def kernel(x, c, positions, w_in, q_norm_g, kv_norm_g, w_uq, w_ukv, conv_w, mix_norm_g, w_o, w_ada, b_ada,
           ln1_g, ln1_b, ln2_g, ln2_b, ffn_w_gate, ffn_w_up, ffn_w_down, moe_router_w, moe_router_b,
           moe_w_gate, moe_w_up, moe_w_down):
    b, s, _ = x.shape
    n_layers = w_in.shape[0]
    ts = min(512, s)
    t = b * s
    tiles_per_batch = s // ts

    bp = -(-b // 16) * 16
    c_pad = jnp.pad(c, ((0, bp - b), (0, 0)))
    mod = _ada_mod(c_pad, w_ada, b_ada)
    mod = jnp.pad(jnp.transpose(mod[:, :, :b, :], (0, 2, 1, 3)), ((0, 0), (0, 0), (0, 2), (0, 0)))
    rope_tab = _rope_tables(positions)

    rows = lambda v: v.reshape(v.shape[0], 1, -1)
    o3 = Q_LORA + KV_LORA + QK_ROPE
    w_in_p = jnp.concatenate(
        [w_in[:, :, :o3], jnp.zeros((n_layers, D_MODEL, D_IN_PAD - w_in.shape[2]), F32), w_in[:, :, o3:]], axis=2
    ).astype(BF16)
    w_uq_p = jnp.pad(
        w_uq.reshape(n_layers, Q_LORA, N_HEADS, QK_NOPE + QK_ROPE),
        ((0, 0), (0, 0), (0, 0), (0, QK_PAD - QK_NOPE - QK_ROPE))
    ).reshape(n_layers, Q_LORA, N_HEADS * QK_PAD).astype(BF16)
    w_ukv_b = w_ukv.astype(BF16)
    w_o_b = w_o.astype(BF16)
    conv_w_p = jnp.pad(conv_w, ((0, 0), (0, 8 - CONV_WIDTH), (0, 0)))
    g_attn, g_conv = rows(mix_norm_g[:, :D_ATTN]), rows(mix_norm_g[:, D_ATTN:])
    ffn_wg, ffn_wu, ffn_wd = ffn_w_gate.astype(BF16), ffn_w_up.astype(BF16), ffn_w_down.astype(BF16)
    rw_hi = moe_router_w.astype(BF16)
    rw_lo = (moe_router_w - rw_hi.astype(F32)).astype(BF16)
    rw = jnp.pad(jnp.concatenate([rw_hi, rw_lo], axis=2), ((0, 0), (0, 0), (0, LANES - 2 * N_EXPERTS)))
    rb = rows(jnp.pad(moe_router_b, ((0, 0), (0, LANES - N_EXPERTS))))

    for l in range(n_layers):
        qt, k, vt, convn = _inproj(l, x, mod, w_in_p, w_uq_p, w_ukv_b, rows(q_norm_g), rows(kv_norm_g), conv_w_p,
                                   g_conv, rope_tab, ts)
        attn = _attention(qt, k, vt, ts)
        common = (l, attn, convn, x, mod, w_o_b, g_attn, rows(ln1_g), rows(ln1_b), ts)
        i = l // 2
        if l % 2 == 0:
            x1, h2 = _outproj(*common)
            x = _dense_ffn(l, i, h2.reshape(t, D_MODEL), x1.reshape(t, D_MODEL), mod, ffn_wg, ffn_wu, ffn_wd,
                           rows(ln2_g), rows(ln2_b), ts, tiles_per_batch)
        else:
            x1, h2, *routing = _outproj(*common, router=(i, rw, rb))
            x = _top2_moe(l, i, h2.reshape(t, D_MODEL), x1.reshape(t, D_MODEL), mod, routing,
                          moe_w_gate, moe_w_up, moe_w_down, rows(ln2_g), rows(ln2_b), ts, tiles_per_batch)
        x = x.reshape(b, s, D_MODEL)
    return x
```

```python
import functools

import jax
import jax.numpy as jnp
from jax import lax
from jax.experimental import pallas as pl
from jax.experimental.pallas import tpu as pltpu

F32 = jnp.float32
BF16 = jnp.bfloat16

D_MODEL = 1024
N_HEADS = 4
QK_NOPE = 128
QK_ROPE = 64
V_HEAD = 128
Q_LORA = 256
KV_LORA = 128
D_ATTN = N_HEADS * V_HEAD
D_CONV = D_MODEL - D_ATTN
CONV_WIDTH = 3
ROPE_BASE = 10000.0
D_FF = 2816
N_EXPERTS = 8
D_FF_EXPERT = 3584
RMS_EPS = 1e-6
LN_EPS = 1e-5
DEPTH = 2
DEEPNORM_ALPHA = (2 * DEPTH) ** 0.25
SM_SCALE = (QK_NOPE + QK_ROPE) ** -0.5
LOG2E = 1.4426950408889634
Q_SCALE = SM_SCALE * LOG2E

LANES = 128
QK_PAD = 2 * LANES
D_IN_PAD = 2048
ROW_ALIGN = 16
SLOT_CHUNK = 256
EXPERT_ROWS = 1152
EXPERT_ROW_SIZES = (384, 768, 1152)
VMEM_LIMIT = 56 * 1024 * 1024
NEG_BIG = -1e30


def _cparams(n_axes):
    return pltpu.CompilerParams(dimension_semantics=("arbitrary",) * n_axes, vmem_limit_bytes=VMEM_LIMIT)


def _dot(a, b):
    return jnp.dot(a, b, preferred_element_type=F32)


def _rms(x, g):
    return x * lax.rsqrt(jnp.mean(x * x, axis=-1, keepdims=True) + RMS_EPS) * g


def _layer_norm(x, g, b):
    mu = jnp.mean(x, axis=-1, keepdims=True)
    xc = x - mu
    var = jnp.mean(xc * xc, axis=-1, keepdims=True)
    return xc * lax.rsqrt(var + LN_EPS) * g + b


def _silu(x):
    return x * jax.nn.sigmoid(x)


def _ada_kernel(c_ref, w_ref, b_ref, o_ref):
    ca = _silu(c_ref[...]).astype(BF16)
    o_ref[0, 0] = _dot(ca, w_ref[0].astype(BF16)) + b_ref[0]


def _ada_mod(c_pad, w_ada, b_ada):
    n_layers = w_ada.shape[0]
    bp = c_pad.shape[0]
    return pl.pallas_call(
        _ada_kernel,
        grid=(n_layers, 6),
        in_specs=[
            pl.BlockSpec((bp, D_MODEL), lambda l, j: (0, 0)),
            pl.BlockSpec((1, D_MODEL, D_MODEL), lambda l, j: (l, 0, j)),
            pl.BlockSpec((1, 1, D_MODEL), lambda l, j: (l, 0, j)),
        ],
        out_specs=pl.BlockSpec((1, 1, bp, D_MODEL), lambda l, j: (l, j, 0, 0)),
        out_shape=jax.ShapeDtypeStruct((n_layers, 6, bp, D_MODEL), F32),
        compiler_params=_cparams(2),
        name="ada_mod",
    )(c_pad, w_ada, b_ada.reshape(n_layers, 1, 6 * D_MODEL))


def _rope_kernel(pos_ref, f_ref, o_ref):
    ang = pos_ref[...].astype(F32) * f_ref[0:1, :]
    c = jnp.cos(ang)
    s = jnp.sin(ang)
    o_ref[:, 0:LANES] = c * f_ref[1:2, :]
    o_ref[:, LANES:2 * LANES] = s * f_ref[2:3, :]
    o_ref[:, 2 * LANES:3 * LANES] = s * f_ref[3:4, :]


def _rope_tables(positions):
    t = positions.size
    tr = min(1024, t)
    half = QK_ROPE // 2
    inv_freq = ROPE_BASE ** (-jnp.arange(0, QK_ROPE, 2, dtype=F32) / QK_ROPE)
    zeros = jnp.zeros((half,), F32)
    ones = jnp.ones((half,), F32)
    consts = jnp.stack([
        jnp.concatenate([inv_freq, inv_freq, zeros, zeros]),
        jnp.concatenate([ones, ones, zeros, zeros]),
        jnp.concatenate([-ones, zeros, zeros, zeros]),
        jnp.concatenate([zeros, ones, zeros, zeros]),
    ])
    consts = jnp.pad(consts, ((0, 4), (0, 0)))
    return pl.pallas_call(
        _rope_kernel,
        grid=(t // tr,),
        in_specs=[pl.BlockSpec((tr, 1), lambda i: (i, 0)), pl.BlockSpec((8, LANES), lambda i: (0, 0))],
        out_specs=pl.BlockSpec((tr, 3 * LANES), lambda i: (i, 0)),
        out_shape=jax.ShapeDtypeStruct((t, 3 * LANES), F32),
        compiler_params=_cparams(1),
        name="rope_tables",
    )(positions.reshape(t, 1), consts)


def _inproj_kernel(x_ref, mod_ref, win_ref, wuq_ref, wukv_ref, qg_ref, kvg_ref, cw_ref, cg_ref, rope_ref,
                   qt_ref, k_ref, vt_ref, cv_ref, cu_ext, *, ts):
    j = pl.program_id(1)
    x = x_ref[0]
    shift = mod_ref[0, 0:1, :]
    scale = mod_ref[0, 1:2, :]
    h = (x * (1.0 + scale) + shift).astype(BF16)
    z_lat = _dot(h, win_ref[:, 0:512])
    c_q = z_lat[:, 0:Q_LORA]
    c_kv = z_lat[:, Q_LORA:Q_LORA + KV_LORA]
    k_rope = z_lat[:, 384:512]
    z_gate = _dot(h, win_ref[:, 512:1536])
    gate_b = z_gate[:, 0:D_CONV]
    gate_c = z_gate[:, D_CONV:2 * D_CONV]
    q = _dot(_rms(c_q, qg_ref[...]).astype(BF16), wuq_ref[...])
    kv = _dot(_rms(c_kv, kvg_ref[...]).astype(BF16), wukv_ref[...])
    u = _dot(h, win_ref[:, 1536:2048])

    cos_d = rope_ref[:, 0:LANES]
    sin_a = rope_ref[:, LANES:2 * LANES]
    sin_b = rope_ref[:, 2 * LANES:3 * LANES]

    def rope(t):
        return t * cos_d + pltpu.roll(t, 96, 1) * sin_a + pltpu.roll(t, 32, 1) * sin_b

    k_rot = rope(k_rope).astype(BF16)
    for hh in range(N_HEADS):
        qo = hh * QK_PAD
        qt_ref[0, hh, 0:LANES, :] = jnp.transpose(q[:, qo:qo + LANES] * Q_SCALE).astype(BF16)
        qt_ref[0, hh, LANES:QK_PAD, :] = jnp.transpose(rope(q[:, qo + LANES:qo + QK_PAD]) * Q_SCALE).astype(BF16)
        ko = hh * (QK_NOPE + V_HEAD)
        k_ref[0, hh, :, 0:LANES] = kv[:, ko:ko + QK_NOPE].astype(BF16)
        k_ref[0, hh, :, LANES:QK_PAD] = k_rot
        vt_ref[0, hh, 0] = jnp.transpose(kv[:, ko + QK_NOPE:ko + QK_NOPE + V_HEAD]).astype(BF16)

    @pl.when(j == 0)
    def _():
        cu_ext[0:8, :] = jnp.zeros((8, D_CONV), F32)

    @pl.when(j > 0)
    def _():
        cu_ext[0:8, :] = cu_ext[ts:ts + 8, :]

    cu_ext[8:ts + 8, :] = gate_c * u
    conv = (cw_ref[2:3, :] * cu_ext[8:ts + 8, :]
            + cw_ref[1:2, :] * cu_ext[pl.ds(7, ts), :]
            + cw_ref[0:1, :] * cu_ext[pl.ds(6, ts), :])
    conv = gate_b * conv
    cv_ref[0] = _rms(conv, cg_ref[...]).astype(BF16)


def _inproj(l, x, mod, w_in_p, w_uq_p, w_ukv, qg, kvg, conv_w, conv_g, rope_tab, ts):
    b, s, _ = x.shape
    nsb = s // ts
    const = lambda shape: pl.BlockSpec((None,) + shape, lambda bi, j: (l,) + (0,) * len(shape))
    return pl.pallas_call(
        functools.partial(_inproj_kernel, ts=ts),
        grid=(b, nsb),
        in_specs=[
            pl.BlockSpec((1, ts, D_MODEL), lambda bi, j: (bi, j, 0)),
            pl.BlockSpec((None, 1, 8, D_MODEL), lambda bi, j: (l, bi, 0, 0)),
            const((D_MODEL, D_IN_PAD)),
            const((Q_LORA, N_HEADS * QK_PAD)),
            const((KV_LORA, N_HEADS * (QK_NOPE + V_HEAD))),
            const((1, Q_LORA)),
            const((1, KV_LORA)),
            const((8, D_CONV)),
            const((1, D_CONV)),
            pl.BlockSpec((ts, 3 * LANES), lambda bi, j: (bi * nsb + j, 0)),
        ],
        out_specs=[
            pl.BlockSpec((1, N_HEADS, QK_PAD, ts), lambda bi, j: (bi, 0, 0, j)),
            pl.BlockSpec((1, N_HEADS, ts, QK_PAD), lambda bi, j: (bi, 0, j, 0)),
            pl.BlockSpec((1, N_HEADS, 1, V_HEAD, ts), lambda bi, j: (bi, 0, j, 0, 0)),
            pl.BlockSpec((1, ts, D_CONV), lambda bi, j: (bi, j, 0)),
        ],
        out_shape=[
            jax.ShapeDtypeStruct((b, N_HEADS, QK_PAD, s), BF16),
            jax.ShapeDtypeStruct((b, N_HEADS, s, QK_PAD), BF16),
            jax.ShapeDtypeStruct((b, N_HEADS, nsb, V_HEAD, ts), BF16),
            jax.ShapeDtypeStruct((b, s, D_CONV), BF16),
        ],
        scratch_shapes=[pltpu.VMEM((ts + 8, D_CONV), F32)],
        compiler_params=_cparams(2),
        name="inproj",
    )(x, mod, w_in_p, w_uq_p, w_ukv, qg, kvg, conv_w, conv_g, rope_tab)


Q_CHUNK = 256


K_ROWS = 16


def _attn_kernel(qt_ref, k_ref, vt_ref, o_ref, *scratch, tq):
    i = pl.program_id(1)
    n_chunks = tq // Q_CHUNK
    per_chain = 5
    chains = [(hh, c) + tuple(scratch[per_chain * (hh * n_chunks + c):per_chain * (hh * n_chunks + c + 1)])
              for hh in range(N_HEADS) for c in range(n_chunks)]
    for _, _, m_sc, l_sc, acc_sc, _, _ in chains:
        m_sc[...] = jnp.full(m_sc.shape, NEG_BIG, F32)
        l_sc[...] = jnp.zeros(l_sc.shape, F32)
        acc_sc[...] = jnp.zeros(acc_sc.shape, F32)

    def scores(chain, jk, buf):
        hh, c = chain[0], chain[1]
        kj = k_ref[0, hh, pl.ds(pl.multiple_of(jk * tq, tq), tq), :]
        chain[5 + buf][...] = _dot(kj, qt_ref[0, hh, :, c * Q_CHUNK:(c + 1) * Q_CHUNK])

    def softmax_values(chain, jk, buf, diagonal):
        hh, c, m_sc, l_sc, acc_sc = chain[:5]
        s_sc = chain[5 + buf]
        nk = (c + 1) * Q_CHUNK if diagonal else tq
        if diagonal:
            d0 = nk - Q_CHUNK
            key = lax.broadcasted_iota(jnp.int32, (Q_CHUNK, Q_CHUNK), 0)
            qry = lax.broadcasted_iota(jnp.int32, (Q_CHUNK, Q_CHUNK), 1)
            s_sc[d0:nk, :] = jnp.where(key <= qry, s_sc[d0:nk, :], NEG_BIG)
        blk_max = s_sc[0:K_ROWS, :]
        for r in range(K_ROWS, nk, K_ROWS):
            blk_max = jnp.maximum(blk_max, s_sc[r:r + K_ROWS, :])
        m_prev = m_sc[...]
        m_new = jnp.maximum(m_prev, jnp.max(blk_max, axis=0, keepdims=True))
        alpha = jnp.exp2(m_prev - m_new)
        p_sum = jnp.zeros((K_ROWS, Q_CHUNK), F32)
        acc = alpha * acc_sc[...]
        for kt in range(0, nk, Q_CHUNK):
            parts = []
            for r in range(kt, kt + Q_CHUNK, K_ROWS):
                p = jnp.exp2(s_sc[r:r + K_ROWS, :] - m_new)
                p_sum = p_sum + p
                parts.append(p.astype(BF16))
            acc = acc + _dot(vt_ref[0, hh, jk, :, kt:kt + Q_CHUNK], jnp.concatenate(parts, axis=0))
        l_sc[...] = alpha * l_sc[...] + jnp.sum(p_sum, axis=0, keepdims=True)
        acc_sc[...] = acc
        m_sc[...] = m_new

    def stage(cur, cur_buf, diagonal, nxt=None):
        for chain in chains:
            if nxt is not None:
                scores(chain, nxt, 1 - cur_buf)
            softmax_values(chain, cur, cur_buf, diagonal)

    for chain in chains:
        scores(chain, 0, 0)

    def pair(jj, carry):
        stage(2 * jj, 0, False, nxt=2 * jj + 1)
        stage(2 * jj + 1, 1, False, nxt=2 * jj + 2)
        return carry

    lax.fori_loop(0, i // 2, pair, 0)

    @pl.when(lax.rem(i, 2) == 0)
    def _():
        stage(i, 0, True)

    @pl.when(lax.rem(i, 2) == 1)
    def _():
        stage(i - 1, 0, False, nxt=i)
        stage(i, 1, True)

    for hh, c, _, l_sc, acc_sc, _, _ in chains:
        o_ref[0, c * Q_CHUNK:(c + 1) * Q_CHUNK, hh * V_HEAD:(hh + 1) * V_HEAD] = (
            jnp.transpose(acc_sc[...] / l_sc[...]).astype(BF16))


def _attention(qt, k, vt, tq):
    b, nh, s, _ = k.shape
    return pl.pallas_call(
        functools.partial(_attn_kernel, tq=tq),
        grid=(b, s // tq),
        in_specs=[
            pl.BlockSpec((1, nh, QK_PAD, tq), lambda bi, i: (bi, 0, 0, i)),
            pl.BlockSpec((1, nh, s, QK_PAD), lambda bi, i: (bi, 0, 0, 0)),
            pl.BlockSpec((1, nh, s // tq, V_HEAD, tq), lambda bi, i: (bi, 0, 0, 0, 0)),
        ],
        out_specs=pl.BlockSpec((1, tq, D_ATTN), lambda bi, i: (bi, i, 0)),
        out_shape=jax.ShapeDtypeStruct((b, s, D_ATTN), BF16),
        scratch_shapes=[
            pltpu.VMEM((1, Q_CHUNK), F32), pltpu.VMEM((1, Q_CHUNK), F32), pltpu.VMEM((V_HEAD, Q_CHUNK), F32),
            pltpu.VMEM((tq, Q_CHUNK), F32), pltpu.VMEM((tq, Q_CHUNK), F32),
        ] * (nh * (tq // Q_CHUNK)),
        compiler_params=_cparams(2),
        name="attention",
    )(qt, k, vt)


def _outproj_core(attn_ref, cv_ref, x_ref, mod_ref, wo_ref, ga_ref, lng_ref, lnb_ref):
    a = attn_ref[0].astype(F32)
    an = _rms(a, ga_ref[...]).astype(BF16)
    y = _dot(an, wo_ref[0:D_ATTN, :]) + _dot(cv_ref[0], wo_ref[D_ATTN:D_MODEL, :])
    gate1 = mod_ref[0, 2:3, :]
    x1 = _layer_norm(DEEPNORM_ALPHA * x_ref[0] + gate1 * y, lng_ref[...], lnb_ref[...])
    h2 = x1 * (1.0 + mod_ref[0, 4:5, :]) + mod_ref[0, 3:4, :]
    return x1, h2


def _outproj_kernel(attn_ref, cv_ref, x_ref, mod_ref, wo_ref, ga_ref, lng_ref, lnb_ref, x1_ref, h2_ref):
    x1, h2 = _outproj_core(attn_ref, cv_ref, x_ref, mod_ref, wo_ref, ga_ref, lng_ref, lnb_ref)
    x1_ref[0] = x1
    h2_ref[0] = h2.astype(BF16)


def _outproj_router_kernel(attn_ref, cv_ref, x_ref, mod_ref, wo_ref, ga_ref, lng_ref, lnb_ref, rw_ref, rb_ref,
                           x1_ref, h2_ref, rel_ref, gate_ref, relt_ref, base_ref, cnt_ref, run_sc, *, ts):
    first = jnp.logical_and(pl.program_id(0) == 0, pl.program_id(1) == 0)

    @pl.when(first)
    def _():
        run_sc[...] = jnp.zeros(run_sc.shape, F32)

    x1, h2 = _outproj_core(attn_ref, cv_ref, x_ref, mod_ref, wo_ref, ga_ref, lng_ref, lnb_ref)
    x1_ref[0] = x1
    h2_hi = h2.astype(BF16)
    h2_ref[0] = h2_hi

    lane = lax.broadcasted_iota(jnp.int32, (ts, LANES), 1)
    h2_lo = (h2 - h2_hi.astype(F32)).astype(BF16)
    prod = _dot(h2_hi, rw_ref[...]) + _dot(h2_lo, rw_ref[...])
    logits = prod + pltpu.roll(prod, LANES - N_EXPERTS, 1) + rb_ref[...]
    logits = jnp.where(lane < N_EXPERTS, logits, NEG_BIG)
    v1 = jnp.max(logits, axis=-1, keepdims=True)
    i1 = jnp.min(jnp.where(logits == v1, lane, LANES), axis=-1, keepdims=True)
    rest = jnp.where(lane == i1, NEG_BIG, logits)
    v2 = jnp.max(rest, axis=-1, keepdims=True)
    i2 = jnp.min(jnp.where(rest == v2, lane, LANES), axis=-1, keepdims=True)
    e21 = jnp.exp(v2 - v1)
    g1 = 1.0 / (1.0 + e21)
    g2 = e21 / (1.0 + e21)
    is1 = lane == i1
    is2 = lane == i2
    sel = jnp.where(jnp.logical_or(is1, is2), 1.0, 0.0)
    gate_ref[...] = jnp.where(is1, g1, 0.0) + jnp.where(is2, g2, 0.0)

    r_i = lax.broadcasted_iota(jnp.int32, (ts, ts), 0)
    c_i = lax.broadcasted_iota(jnp.int32, (ts, ts), 1)
    tri = jnp.where(c_i < r_i, 1.0, 0.0).astype(BF16)
    rank = _dot(tri, sel.astype(BF16))
    rel = jnp.where(sel > 0.0, rank, -1.0)
    rel_ref[...] = rel
    relt_ref[0] = jnp.transpose(rel)[0:N_EXPERTS, :]

    n_tile = jnp.sum(sel, axis=0, keepdims=True)
    n_pad = jnp.floor((n_tile + (ROW_ALIGN - 1)) * (1.0 / ROW_ALIGN)) * ROW_ALIGN
    base_ref[0] = run_sc[...].astype(jnp.int32)
    cnt_ref[0] = n_tile.astype(jnp.int32)
    run_sc[...] = run_sc[...] + n_pad


def _outproj(l, attn, convn, x, mod, w_o, g_attn, ln_g, ln_b, ts, router=None):
    b, s, _ = x.shape
    nsb = s // ts
    const = lambda shape, idx=l: pl.BlockSpec((None,) + shape, lambda bi, j: (idx,) + (0,) * len(shape))
    tile3 = lambda w: pl.BlockSpec((1, ts, w), lambda bi, j: (bi, j, 0))
    in_specs = [
        tile3(D_ATTN), tile3(D_CONV), tile3(D_MODEL),
        pl.BlockSpec((None, 1, 8, D_MODEL), lambda bi, j: (l, bi, 0, 0)),
        const((D_MODEL, D_MODEL)), const((1, D_ATTN)), const((1, D_MODEL)), const((1, D_MODEL)),
    ]
    out_specs = [tile3(D_MODEL), tile3(D_MODEL)]
    out_shape = [jax.ShapeDtypeStruct((b, s, D_MODEL), F32), jax.ShapeDtypeStruct((b, s, D_MODEL), BF16)]
    args = [attn, convn, x, mod, w_o, g_attn, ln_g, ln_b]
    if router is None:
        return pl.pallas_call(
            _outproj_kernel, grid=(b, nsb), in_specs=in_specs, out_specs=out_specs, out_shape=out_shape,
            compiler_params=_cparams(2), name="outproj",
        )(*args)
    moe_i, rw, rb = router
    nt = b * nsb
    flat = lambda w: pl.BlockSpec((ts, w), lambda bi, j: (bi * nsb + j, 0))
    per_tile = lambda r, w: pl.BlockSpec((1, r, w), lambda bi, j: (bi * nsb + j, 0, 0))
    in_specs += [const((D_MODEL, LANES), moe_i), const((1, LANES), moe_i)]
    out_specs += [flat(LANES), flat(LANES), per_tile(N_EXPERTS, ts), per_tile(1, LANES), per_tile(1, LANES)]
    out_shape += [
        jax.ShapeDtypeStruct((nt * ts, LANES), F32), jax.ShapeDtypeStruct((nt * ts, LANES), F32),
        jax.ShapeDtypeStruct((nt, N_EXPERTS, ts), F32),
        jax.ShapeDtypeStruct((nt, 1, LANES), jnp.int32), jax.ShapeDtypeStruct((nt, 1, LANES), jnp.int32),
    ]
    return pl.pallas_call(
        functools.partial(_outproj_router_kernel, ts=ts), grid=(b, nsb), in_specs=in_specs, out_specs=out_specs,
        out_shape=out_shape, scratch_shapes=[pltpu.VMEM((1, LANES), F32)],
        compiler_params=_cparams(2), name="outproj_router",
    )(*args, rw, rb)


FF_CHUNK = 256


def _swiglu(x, wg, wu, wd, width):
    out = None
    pending = None
    for c in range(0, width, FF_CHUNK):
        w = min(FF_CHUNK, width - c)
        g = _dot(x, wg(c, w))
        u = _dot(x, wu(c, w))
        if pending is not None:
            d = _dot(pending[0], wd(*pending[1]))
            out = d if out is None else out + d
        pending = ((_silu(g) * u).astype(BF16), (c, w))
    d = _dot(pending[0], wd(*pending[1]))
    return d if out is None else out + d


def _ffn_kernel(h_ref, x1_ref, mod_ref, wg_ref, wu_ref, wd_ref, lng_ref, lnb_ref, o_ref):
    f = _swiglu(h_ref[...], lambda c, w: wg_ref[:, c:c + w], lambda c, w: wu_ref[:, c:c + w],
                lambda c, w: wd_ref[c:c + w, :], D_FF)
    gate2 = mod_ref[0, 5:6, :]
    o_ref[...] = _layer_norm(DEEPNORM_ALPHA * x1_ref[...] + gate2 * f, lng_ref[...], lnb_ref[...])


def _dense_ffn(l, dense_i, h2, x1, mod, wg, wu, wd, ln_g, ln_b, tm, tiles_per_batch):
    t = h2.shape[0]
    const = lambda shape, idx: pl.BlockSpec((None,) + shape, lambda i: (idx,) + (0,) * len(shape),
                                            pipeline_mode=pl.Buffered(1))
    return pl.pallas_call(
        _ffn_kernel,
        grid=(t // tm,),
        in_specs=[
            pl.BlockSpec((tm, D_MODEL), lambda i: (i, 0)),
            pl.BlockSpec((tm, D_MODEL), lambda i: (i, 0)),
            pl.BlockSpec((None, 1, 8, D_MODEL), lambda i: (l, i // tiles_per_batch, 0, 0)),
            const((D_MODEL, D_FF), dense_i), const((D_MODEL, D_FF), dense_i), const((D_FF, D_MODEL), dense_i),
            const((1, D_MODEL), l), const((1, D_MODEL), l),
        ],
        out_specs=pl.BlockSpec((tm, D_MODEL), lambda i: (i, 0)),
        out_shape=jax.ShapeDtypeStruct((t, D_MODEL), F32),
        compiler_params=_cparams(1),
        name="dense_ffn",
    )(h2, x1, mod, wg, wu, wd, ln_g, ln_b)


def _group_copy(src, dst, sem):
    return pltpu.make_async_copy(src, dst, sem)


def _dispatch_kernel(base_ref, cnt_ref, h_ref, relt_ref, xs_ref, stage, extra, zbuf, sems, xsem, zsem, *,
                     ts, zero_rows):
    i = pl.program_id(0)
    last = pl.num_programs(0) - 1
    par = lax.rem(i, 2)
    n_chunks = ts // SLOT_CHUNK
    h = h_ref[...]
    slot = lax.broadcasted_iota(jnp.int32, (SLOT_CHUNK, ts), 0).astype(F32)

    def onehot(e, c):
        return jnp.where(relt_ref[0, e:e + 1, :] == slot + float(c * SLOT_CHUNK), 1.0, 0.0).astype(BF16)

    def first_copy(step, e):
        a = pl.multiple_of(base_ref[step * N_EXPERTS + e], ROW_ALIGN)
        p = lax.rem(step, 2)
        return _group_copy(stage.at[p, e], xs_ref.at[e, pl.ds(a, SLOT_CHUNK), :], sems.at[p, e])

    for e in range(N_EXPERTS):
        stage[par, e] = _dot(onehot(e, 0), h).astype(BF16)

    @pl.when(i > 0)
    def _():
        for e in range(N_EXPERTS):
            first_copy(i - 1, e).wait()

    for e in range(N_EXPERTS):
        first_copy(i, e).start()

    for e in range(N_EXPERTS):
        n = cnt_ref[i * N_EXPERTS + e]
        for c in range(1, n_chunks):
            @pl.when(n > c * SLOT_CHUNK)
            def _(e=e, c=c):
                a = pl.multiple_of(base_ref[i * N_EXPERTS + e], ROW_ALIGN)
                extra[...] = _dot(onehot(e, c), h).astype(BF16)
                cp = _group_copy(extra, xs_ref.at[e, pl.ds(a + c * SLOT_CHUNK, SLOT_CHUNK), :], xsem)
                cp.start()
                cp.wait()

    @pl.when(i == last)
    def _():
        for e in range(N_EXPERTS):
            first_copy(i, e).wait()
        zbuf[...] = jnp.zeros(zbuf.shape, BF16)
        for e in range(N_EXPERTS):
            n = cnt_ref[i * N_EXPERTS + e]
            end = base_ref[i * N_EXPERTS + e] + ((n + (ROW_ALIGN - 1)) // ROW_ALIGN) * ROW_ALIGN
            end = pl.multiple_of(end, ROW_ALIGN)
            copies = [
                _group_copy(zbuf, xs_ref.at[e, pl.ds(end + r * SLOT_CHUNK, SLOT_CHUNK), :], zsem.at[e])
                for r in range(zero_rows // SLOT_CHUNK)
            ]
            for cp in copies:
                cp.start()
            for cp in copies:
                cp.wait()


def _dispatch(base, cnt, h2, relt, ts, cap, zero_rows):
    t = h2.shape[0]
    nt = t // ts
    n_chunks = ts // SLOT_CHUNK
    grid_spec = pltpu.PrefetchScalarGridSpec(
        num_scalar_prefetch=2,
        grid=(nt,),
        in_specs=[
            pl.BlockSpec((ts, D_MODEL), lambda i, b_, c_: (i, 0)),
            pl.BlockSpec((1, N_EXPERTS, ts), lambda i, b_, c_: (i, 0, 0)),
        ],
        out_specs=pl.BlockSpec(memory_space=pl.ANY),
        scratch_shapes=[
            pltpu.VMEM((2, N_EXPERTS, SLOT_CHUNK, D_MODEL), BF16),
            pltpu.VMEM((SLOT_CHUNK, D_MODEL), BF16),
            pltpu.VMEM((SLOT_CHUNK, D_MODEL), BF16),
            pltpu.SemaphoreType.DMA((2, N_EXPERTS)),
            pltpu.SemaphoreType.DMA(()),
            pltpu.SemaphoreType.DMA((N_EXPERTS,)),
        ],
    )
    return pl.pallas_call(
        functools.partial(_dispatch_kernel, ts=ts, zero_rows=zero_rows),
        grid_spec=grid_spec,
        out_shape=jax.ShapeDtypeStruct((N_EXPERTS, cap, D_MODEL), BF16),
        compiler_params=_cparams(1),
        name="moe_dispatch",
    )(base, cnt, h2, relt)


def _expert_kernel(te_ref, tr_ref, rows_ref, nv_ref, x_ref, wg_ref, wu_ref, wd_ref, o_ref, acc_sc):
    g_i = pl.program_id(0)
    k = pl.program_id(1)

    valid = g_i < nv_ref[0]

    @pl.when(jnp.logical_and(valid, k == 0))
    def _():
        acc_sc[...] = jnp.zeros(acc_sc.shape, F32)

    for m in EXPERT_ROW_SIZES:
        @pl.when(jnp.logical_and(valid, rows_ref[g_i] == m))
        def _(m=m):
            part = _swiglu(x_ref[0, 0:m, :], lambda c, w: wg_ref[0, :, c:c + w].astype(BF16),
                           lambda c, w: wu_ref[0, :, c:c + w].astype(BF16),
                           lambda c, w: wd_ref[0, c:c + w, :].astype(BF16), wg_ref.shape[2])
            total = acc_sc[0:m, :] + part
            acc_sc[0:m, :] = total
            o_ref[0, 0:m, :] = total.astype(BF16)


def _expert_ffn(moe_i, tile_e, tile_r, tile_rows, n_valid, xs, wg, wu, wd, tf, n_steps):
    tm = EXPERT_ROWS
    nk = D_FF_EXPERT // tf
    cap = xs.shape[1]

    def k_eff(g, k, nv):
        return jnp.where(g < nv[0], k, nk - 1)

    grid_spec = pltpu.PrefetchScalarGridSpec(
        num_scalar_prefetch=4,
        grid=(n_steps, nk),
        in_specs=[
            pl.BlockSpec((1, tm, D_MODEL), lambda g, k, te, tr, rw, nv: (te[g], tr[g], 0)),
            pl.BlockSpec((None, 1, D_MODEL, tf), lambda g, k, te, tr, rw, nv: (moe_i, te[g], 0, k_eff(g, k, nv))),
            pl.BlockSpec((None, 1, D_MODEL, tf), lambda g, k, te, tr, rw, nv: (moe_i, te[g], 0, k_eff(g, k, nv))),
            pl.BlockSpec((None, 1, tf, D_MODEL), lambda g, k, te, tr, rw, nv: (moe_i, te[g], k_eff(g, k, nv), 0)),
        ],
        out_specs=pl.BlockSpec((1, tm, D_MODEL), lambda g, k, te, tr, rw, nv: (te[g], tr[g], 0)),
        scratch_shapes=[pltpu.VMEM((tm, D_MODEL), F32)],
    )
    return pl.pallas_call(
        _expert_kernel,
        grid_spec=grid_spec,
        out_shape=jax.ShapeDtypeStruct((N_EXPERTS, cap, D_MODEL), BF16),
        compiler_params=_cparams(2),
        name="moe_experts",
    )(tile_e, tile_r, tile_rows, n_valid, xs, wg, wu, wd)


def _combine_kernel(base_ref, cnt_ref, lim_ref, ys_ref, rel_ref, gate_ref, x1_ref, mod_ref, lng_ref, lnb_ref, o_ref,
                    slab, xslab, acc_sc, sems, xsem, *, ts):
    i = pl.program_id(0)
    last = pl.num_programs(0) - 1
    par = lax.rem(i, 2)
    n_chunks = ts // SLOT_CHUNK
    slot = lax.broadcasted_iota(jnp.int32, (ts, SLOT_CHUNK), 1).astype(F32)

    def slab_start(step, e, c):
        a = base_ref[step * N_EXPERTS + e] + c * SLOT_CHUNK
        start = jnp.maximum(jnp.minimum(a, lim_ref[e] - SLOT_CHUNK), 0)
        return pl.multiple_of(start, ROW_ALIGN), a - start

    def first_copy(step, e):
        start, _ = slab_start(step, e, 0)
        p = lax.rem(step, 2)
        return _group_copy(ys_ref.at[e, pl.ds(start, SLOT_CHUNK), :], slab.at[p, e], sems.at[p, e])

    def contribution(e, c, rows):
        _, shift = slab_start(i, e, c)
        offset = (shift - c * SLOT_CHUNK).astype(F32)
        onehot = jnp.where(rel_ref[:, e:e + 1] + offset == slot, 1.0, 0.0).astype(BF16)
        return _dot(onehot, rows) * gate_ref[:, e:e + 1]

    @pl.when(i == 0)
    def _():
        for e in range(N_EXPERTS):
            first_copy(i, e).start()

    @pl.when(i < last)
    def _():
        for e in range(N_EXPERTS):
            first_copy(i + 1, e).start()

    for e in range(N_EXPERTS):
        first_copy(i, e).wait()
    f = contribution(0, 0, slab[par, 0])
    for e in range(1, N_EXPERTS):
        f = f + contribution(e, 0, slab[par, e])
    acc_sc[...] = f

    for e in range(N_EXPERTS):
        n = cnt_ref[i * N_EXPERTS + e]
        for c in range(1, n_chunks):
            @pl.when(n > c * SLOT_CHUNK)
            def _(e=e, c=c):
                start, _ = slab_start(i, e, c)
                cp = _group_copy(ys_ref.at[e, pl.ds(start, SLOT_CHUNK), :], xslab, xsem)
                cp.start()
                cp.wait()
                acc_sc[...] += contribution(e, c, xslab[...])

    gate2 = mod_ref[0, 5:6, :]
    o_ref[...] = _layer_norm(DEEPNORM_ALPHA * x1_ref[...] + gate2 * acc_sc[...], lng_ref[...], lnb_ref[...])


def _combine(l, base, cnt, lim, ys, rel, gates, x1, mod, ln_g, ln_b, ts, tiles_per_batch):
    t = x1.shape[0]
    grid_spec = pltpu.PrefetchScalarGridSpec(
        num_scalar_prefetch=3,
        grid=(t // ts,),
        in_specs=[
            pl.BlockSpec(memory_space=pl.ANY),
            pl.BlockSpec((ts, LANES), lambda i, *_: (i, 0)),
            pl.BlockSpec((ts, LANES), lambda i, *_: (i, 0)),
            pl.BlockSpec((ts, D_MODEL), lambda i, *_: (i, 0)),
            pl.BlockSpec((None, 1, 8, D_MODEL), lambda i, *_: (l, i // tiles_per_batch, 0, 0)),
            pl.BlockSpec((None, 1, D_MODEL), lambda i, *_: (l, 0, 0)),
            pl.BlockSpec((None, 1, D_MODEL), lambda i, *_: (l, 0, 0)),
        ],
        out_specs=pl.BlockSpec((ts, D_MODEL), lambda i, *_: (i, 0)),
        scratch_shapes=[
            pltpu.VMEM((2, N_EXPERTS, SLOT_CHUNK, D_MODEL), BF16),
            pltpu.VMEM((SLOT_CHUNK, D_MODEL), BF16),
            pltpu.VMEM((ts, D_MODEL), F32),
            pltpu.SemaphoreType.DMA((2, N_EXPERTS)),
            pltpu.SemaphoreType.DMA(()),
        ],
    )
    return pl.pallas_call(
        functools.partial(_combine_kernel, ts=ts),
        grid_spec=grid_spec,
        out_shape=jax.ShapeDtypeStruct((t, D_MODEL), F32),
        compiler_params=_cparams(1),
        name="moe_combine",
    )(base, cnt, lim, ys, rel, gates, x1, mod, ln_g, ln_b)


def _expert_tile_plan(base, cnt, nt, n_steps):
    tm = EXPERT_ROWS
    i32 = lambda v: v.astype(jnp.int32)
    last_base = base.reshape(nt, N_EXPERTS)[-1]
    last_cnt = cnt.reshape(nt, N_EXPERTS)[-1]
    total = last_base + ((last_cnt + (ROW_ALIGN - 1)) // ROW_ALIGN) * ROW_ALIGN
    tiles = jnp.maximum((total + tm - 1) // tm, 1)
    rest = total - (tiles - 1) * tm
    sizes = jnp.asarray(EXPERT_ROW_SIZES, jnp.int32)
    last_rows = sizes[jnp.sum((rest[:, None] > sizes[None, :]).astype(jnp.int32), axis=1)]
    ends = jnp.cumsum(tiles)
    n_valid = ends[-1]
    g = jnp.minimum(jnp.arange(n_steps, dtype=jnp.int32), n_valid - 1)
    tile_e = jnp.sum((g[:, None] >= ends[None, :]).astype(jnp.int32), axis=1)
    tile_r = g - (ends - tiles)[tile_e]
    tile_rows = jnp.where(tile_r == tiles[tile_e] - 1, last_rows[tile_e], tm)
    lim = (tiles - 1) * tm + last_rows
    return i32(tile_e), i32(tile_r), i32(tile_rows), i32(n_valid.reshape(1)), i32(lim)


def _top2_moe(l, moe_i, h2, x1, mod, routing, wg, wu, wd, ln_g, ln_b, ts, tiles_per_batch):
    rel, gates, relt, base3, cnt3 = routing
    t = h2.shape[0]
    nt = t // ts
    tm, tf = EXPERT_ROWS, 512
    base = base3[:, 0, :N_EXPERTS].reshape(-1)
    cnt = cnt3[:, 0, :N_EXPERTS].reshape(-1)
    zero_rows = -(-tm // SLOT_CHUNK) * SLOT_CHUNK
    rows_max = t + ROW_ALIGN * nt
    cap = -(-(rows_max + SLOT_CHUNK + zero_rows) // tm) * tm
    rows_all = 2 * t + ROW_ALIGN * N_EXPERTS * nt
    n_steps = -(-rows_all // tm) + N_EXPERTS
    xs = _dispatch(base, cnt, h2, relt, ts, cap, zero_rows)
    tile_e, tile_r, tile_rows, n_valid, lim = _expert_tile_plan(base, cnt, nt, n_steps)
    ys = _expert_ffn(moe_i, tile_e, tile_r, tile_rows, n_valid, xs, wg, wu, wd, tf, n_steps)
    return _combine(l, base, cnt, lim, ys, rel, gates, x1, mod, ln_g, ln_b, ts, tiles_per_batch)


def kernel(x, c, positions, w_in, q_norm_g, kv_norm_g, w_uq, w_ukv, conv_w, mix_norm_g, w_o, w_ada, b_ada,
           ln1_g, ln1_b, ln2_g, ln2_b, ffn_w_gate, ffn_w_up, ffn_w_down, moe_router_w, moe_router_b,
           moe_w_gate, moe_w_up, moe_w_down):
    b, s, _ = x.shape
    n_layers = w_in.shape[0]
    ts = min(512, s)
    t = b * s
    tiles_per_batch = s // ts

    bp = -(-b // 16) * 16
    c_pad = jnp.pad(c, ((0, bp - b), (0, 0)))
    mod = _ada_mod(c_pad, w_ada, b_ada)
    mod = jnp.pad(jnp.transpose(mod[:, :, :b, :], (0, 2, 1, 3)), ((0, 0), (0, 0), (0, 2), (0, 0)))
    rope_tab = _rope_tables(positions)

    rows = lambda v: v.reshape(v.shape[0], 1, -1)
    o3 = Q_LORA + KV_LORA + QK_ROPE
    w_in_p = jnp.concatenate(
        [w_in[:, :, :o3], jnp.zeros((n_layers, D_MODEL, D_IN_PAD - w_in.shape[2]), F32), w_in[:, :, o3:]], axis=2
    ).astype(BF16)
    w_uq_p = jnp.pad(
        w_uq.reshape(n_layers, Q_LORA, N_HEADS, QK_NOPE + QK_ROPE),
        ((0, 0), (0, 0), (0, 0), (0, QK_PAD - QK_NOPE - QK_ROPE))
    ).reshape(n_layers, Q_LORA, N_HEADS * QK_PAD).astype(BF16)
    w_ukv_b = w_ukv.astype(BF16)
    w_o_b = w_o.astype(BF16)
    conv_w_p = jnp.pad(conv_w, ((0, 0), (0, 8 - CONV_WIDTH), (0, 0)))
    g_attn, g_conv = rows(mix_norm_g[:, :D_ATTN]), rows(mix_norm_g[:, D_ATTN:])
    ffn_wg, ffn_wu, ffn_wd = ffn_w_gate.astype(BF16), ffn_w_up.astype(BF16), ffn_w_down.astype(BF16)
    rw_hi = moe_router_w.astype(BF16)
    rw_lo = (moe_router_w - rw_hi.astype(F32)).astype(BF16)
    rw = jnp.pad(jnp.concatenate([rw_hi, rw_lo], axis=2), ((0, 0), (0, 0), (0, LANES - 2 * N_EXPERTS)))
    rb = rows(jnp.pad(moe_router_b, ((0, 0), (0, LANES - N_EXPERTS))))

    for l in range(n_layers):
        qt, k, vt, convn = _inproj(l, x, mod, w_in_p, w_uq_p, w_ukv_b, rows(q_norm_g), rows(kv_norm_g), conv_w_p,
                                   g_conv, rope_tab, ts)
        attn = _attention(qt, k, vt, ts)
        common = (l, attn, convn, x, mod, w_o_b, g_attn, rows(ln1_g), rows(ln1_b), ts)
        i = l // 2
        if l % 2 == 0:
            x1, h2 = _outproj(*common)
            x = _dense_ffn(l, i, h2.reshape(t, D_MODEL), x1.reshape(t, D_MODEL), mod, ffn_wg, ffn_wu, ffn_wd,
                           rows(ln2_g), rows(ln2_b), ts, tiles_per_batch)
        else:
            x1, h2, *routing = _outproj(*common, router=(i, rw, rb))
            x = _top2_moe(l, i, h2.reshape(t, D_MODEL), x1.reshape(t, D_MODEL), mod, routing,
                          moe_w_gate, moe_w_up, moe_w_down, rows(ln2_g), rows(ln2_b), ts, tiles_per_batch)
        x = x.reshape(b, s, D_MODEL)
    return x
```

```python
import functools

import jax
import jax.numpy as jnp
from jax import lax
from jax.experimental import pallas as pl
from jax.experimental.pallas import tpu as pltpu

F32 = jnp.float32
BF16 = jnp.bfloat16

D_MODEL = 1024
N_HEADS = 4
QK_NOPE = 128
QK_ROPE = 64
V_HEAD = 128
Q_LORA = 256
KV_LORA = 128
D_ATTN = N_HEADS * V_HEAD
D_CONV = D_MODEL - D_ATTN
CONV_WIDTH = 3
ROPE_BASE = 10000.0
D_FF = 2816
N_EXPERTS = 8
D_FF_EXPERT = 3584
RMS_EPS = 1e-6
LN_EPS = 1e-5
DEPTH = 2
DEEPNORM_ALPHA = (2 * DEPTH) ** 0.25
SM_SCALE = (QK_NOPE + QK_ROPE) ** -0.5
LOG2E = 1.4426950408889634
Q_SCALE = SM_SCALE * LOG2E

LANES = 128
QK_PAD = 2 * LANES
D_IN_PAD = 2048
ROW_ALIGN = 16
SLOT_CHUNK = 256
EXPERT_ROWS = 1152
EXPERT_ROW_SIZES = (384, 768, 1152)
VMEM_LIMIT = 56 * 1024 * 1024
NEG_BIG = -1e30


def _cparams(n_axes):
    return pltpu.CompilerParams(dimension_semantics=("arbitrary",) * n_axes, vmem_limit_bytes=VMEM_LIMIT)


def _dot(a, b):
    return jnp.dot(a, b, preferred_element_type=F32)


def _rms(x, g):
    return x * lax.rsqrt(jnp.mean(x * x, axis=-1, keepdims=True) + RMS_EPS) * g


def _layer_norm(x, g, b):
    mu = jnp.mean(x, axis=-1, keepdims=True)
    xc = x - mu
    var = jnp.mean(xc * xc, axis=-1, keepdims=True)
    return xc * lax.rsqrt(var + LN_EPS) * g + b


def _silu(x):
    return x * jax.nn.sigmoid(x)


def _ada_kernel(c_ref, w_ref, b_ref, o_ref):
    ca = _silu(c_ref[...]).astype(BF16)
    o_ref[0, 0] = _dot(ca, w_ref[0].astype(BF16)) + b_ref[0]


def _ada_mod(c_pad, w_ada, b_ada):
    n_layers = w_ada.shape[0]
    bp = c_pad.shape[0]
    return pl.pallas_call(
        _ada_kernel,
        grid=(n_layers, 6),
        in_specs=[
            pl.BlockSpec((bp, D_MODEL), lambda l, j: (0, 0)),
            pl.BlockSpec((1, D_MODEL, D_MODEL), lambda l, j: (l, 0, j)),
            pl.BlockSpec((1, 1, D_MODEL), lambda l, j: (l, 0, j)),
        ],
        out_specs=pl.BlockSpec((1, 1, bp, D_MODEL), lambda l, j: (l, j, 0, 0)),
        out_shape=jax.ShapeDtypeStruct((n_layers, 6, bp, D_MODEL), F32),
        compiler_params=_cparams(2),
        name="ada_mod",
    )(c_pad, w_ada, b_ada.reshape(n_layers, 1, 6 * D_MODEL))


ROPE_HALF = QK_ROPE // 2


def _rope_kernel(pos_ref, f_ref, o_ref):
    ang = pos_ref[0].astype(F32) * f_ref[...]
    o_ref[0, 0:ROPE_HALF, :] = jnp.cos(ang)
    o_ref[0, ROPE_HALF:QK_ROPE, :] = jnp.sin(ang)


def _rope_tables(positions, ts):
    nt = positions.size // ts
    inv_freq = ROPE_BASE ** (-jnp.arange(0, QK_ROPE, 2, dtype=F32) / QK_ROPE)
    return pl.pallas_call(
        _rope_kernel,
        grid=(nt,),
        in_specs=[pl.BlockSpec((1, 1, ts), lambda i: (i, 0, 0)), pl.BlockSpec((ROPE_HALF, 1), lambda i: (0, 0))],
        out_specs=pl.BlockSpec((1, QK_ROPE, ts), lambda i: (i, 0, 0)),
        out_shape=jax.ShapeDtypeStruct((nt, QK_ROPE, ts), F32),
        compiler_params=_cparams(1),
        name="rope_tables",
    )(positions.reshape(nt, 1, ts), inv_freq.reshape(ROPE_HALF, 1))


def _inproj_kernel(x_ref, mod_ref, win_ref, wuqt_ref, wk_ref, wvt_ref, qg_ref, kvg_ref, cw_ref, cg_ref, rope_ref,
                   qt_ref, k_ref, vt_ref, cv_ref, cu_ext, *, ts):
    j = pl.program_id(1)

    @pl.when(j == 0)
    def _():
        cu_ext[0:8, :] = jnp.zeros((8, D_CONV), F32)

    @pl.when(j > 0)
    def _():
        cu_ext[0:8, :] = cu_ext[ts:ts + 8, :]

    x = x_ref[0]
    shift = mod_ref[0, 0:1, :]
    scale = mod_ref[0, 1:2, :]
    h = (x * (1.0 + scale) + shift).astype(BF16)
    z_lat = _dot(h, win_ref[:, 0:512])
    c_q = z_lat[:, 0:Q_LORA]
    c_kv = z_lat[:, Q_LORA:Q_LORA + KV_LORA]
    k_rope = z_lat[:, 384:512]
    z_gate = _dot(h, win_ref[:, 512:1536])
    gate_b = z_gate[:, 0:D_CONV]
    gate_c = z_gate[:, D_CONV:2 * D_CONV]
    u = _dot(h, win_ref[:, 1536:2048])
    cqn = _rms(c_q, qg_ref[...])
    ckvn = _rms(c_kv, kvg_ref[...])
    q_t = _dot(wuqt_ref[...], jnp.transpose(cqn).astype(BF16))
    k_nope = _dot(ckvn.astype(BF16), wk_ref[...])
    v_t = _dot(wvt_ref[...], jnp.transpose(ckvn).astype(BF16))

    cos = rope_ref[0, 0:ROPE_HALF, :]
    sin = rope_ref[0, ROPE_HALF:QK_ROPE, :]

    def rope_t(block):
        x1, x2 = block[0:ROPE_HALF, :], block[ROPE_HALF:QK_ROPE, :]
        return x1 * cos - x2 * sin, x2 * cos + x1 * sin

    k1, k2 = rope_t(jnp.transpose(k_rope)[0:QK_ROPE, :])
    k_rot = jnp.transpose(jnp.concatenate([k1, k2, jnp.zeros((LANES - QK_ROPE, ts), F32)], axis=0)).astype(BF16)
    for hh in range(N_HEADS):
        qo = hh * QK_PAD
        r0 = qo + QK_NOPE
        q1, q2 = rope_t(q_t[r0:r0 + QK_ROPE, :])
        qt_ref[0, hh, 0:QK_NOPE, :] = (q_t[qo:r0, :] * Q_SCALE).astype(BF16)
        qt_ref[0, hh, QK_NOPE:QK_NOPE + ROPE_HALF, :] = (q1 * Q_SCALE).astype(BF16)
        qt_ref[0, hh, QK_NOPE + ROPE_HALF:QK_NOPE + QK_ROPE, :] = (q2 * Q_SCALE).astype(BF16)
        qt_ref[0, hh, QK_NOPE + QK_ROPE:QK_PAD, :] = jnp.zeros((QK_PAD - QK_NOPE - QK_ROPE, ts), BF16)
        k_ref[0, hh, :, 0:LANES] = k_nope[:, hh * QK_NOPE:(hh + 1) * QK_NOPE].astype(BF16)
        k_ref[0, hh, :, LANES:QK_PAD] = k_rot
        vt_ref[0, hh, 0] = v_t[hh * V_HEAD:(hh + 1) * V_HEAD, :].astype(BF16)

    cu_ext[8:ts + 8, :] = gate_c * u
    conv = (cw_ref[2:3, :] * cu_ext[8:ts + 8, :]
            + cw_ref[1:2, :] * cu_ext[pl.ds(7, ts), :]
            + cw_ref[0:1, :] * cu_ext[pl.ds(6, ts), :])
    conv = gate_b * conv
    cv_ref[0] = _rms(conv, cg_ref[...]).astype(BF16)


def _inproj(l, x, mod, w_in_p, w_uq_t, w_k, w_v_t, qg, kvg, conv_w, conv_g, rope_tab, ts):
    b, s, _ = x.shape
    nsb = s // ts
    const = lambda shape: pl.BlockSpec((None,) + shape, lambda bi, j: (l,) + (0,) * len(shape))
    return pl.pallas_call(
        functools.partial(_inproj_kernel, ts=ts),
        grid=(b, nsb),
        in_specs=[
            pl.BlockSpec((1, ts, D_MODEL), lambda bi, j: (bi, j, 0)),
            pl.BlockSpec((None, 1, 8, D_MODEL), lambda bi, j: (l, bi, 0, 0)),
            const((D_MODEL, D_IN_PAD)),
            const((N_HEADS * QK_PAD, Q_LORA)),
            const((KV_LORA, N_HEADS * QK_NOPE)),
            const((N_HEADS * V_HEAD, KV_LORA)),
            const((1, Q_LORA)),
            const((1, KV_LORA)),
            const((8, D_CONV)),
            const((1, D_CONV)),
            pl.BlockSpec((1, QK_ROPE, ts), lambda bi, j: (bi * nsb + j, 0, 0)),
        ],
        out_specs=[
            pl.BlockSpec((1, N_HEADS, QK_PAD, ts), lambda bi, j: (bi, 0, 0, j)),
            pl.BlockSpec((1, N_HEADS, ts, QK_PAD), lambda bi, j: (bi, 0, j, 0)),
            pl.BlockSpec((1, N_HEADS, 1, V_HEAD, ts), lambda bi, j: (bi, 0, j, 0, 0)),
            pl.BlockSpec((1, ts, D_CONV), lambda bi, j: (bi, j, 0)),
        ],
        out_shape=[
            jax.ShapeDtypeStruct((b, N_HEADS, QK_PAD, s), BF16),
            jax.ShapeDtypeStruct((b, N_HEADS, s, QK_PAD), BF16),
            jax.ShapeDtypeStruct((b, N_HEADS, nsb, V_HEAD, ts), BF16),
            jax.ShapeDtypeStruct((b, s, D_CONV), BF16),
        ],
        scratch_shapes=[pltpu.VMEM((ts + 8, D_CONV), F32)],
        compiler_params=_cparams(2),
        name="inproj",
    )(x, mod, w_in_p, w_uq_t, w_k, w_v_t, qg, kvg, conv_w, conv_g, rope_tab)


Q_CHUNK = 256


K_ROWS = 16


def _attn_kernel(qt_ref, k_ref, vt_ref, o_ref, *scratch, tq):
    i = pl.program_id(1)
    n_chunks = tq // Q_CHUNK
    per_chain = 5
    chains = [(hh, c) + tuple(scratch[per_chain * (hh * n_chunks + c):per_chain * (hh * n_chunks + c + 1)])
              for hh in range(N_HEADS) for c in range(n_chunks)]
    for _, _, m_sc, l_sc, acc_sc, _, _ in chains:
        m_sc[...] = jnp.full(m_sc.shape, NEG_BIG, F32)
        l_sc[...] = jnp.zeros(l_sc.shape, F32)
        acc_sc[...] = jnp.zeros(acc_sc.shape, F32)

    def scores(chain, jk, buf):
        hh, c = chain[0], chain[1]
        kj = k_ref[0, hh, pl.ds(pl.multiple_of(jk * tq, tq), tq), :]
        chain[5 + buf][...] = _dot(kj, qt_ref[0, hh, :, c * Q_CHUNK:(c + 1) * Q_CHUNK])

    def softmax_values(chain, jk, buf, diagonal):
        hh, c, m_sc, l_sc, acc_sc = chain[:5]
        s_sc = chain[5 + buf]
        nk = (c + 1) * Q_CHUNK if diagonal else tq
        if diagonal:
            d0 = nk - Q_CHUNK
            key = lax.broadcasted_iota(jnp.int32, (Q_CHUNK, Q_CHUNK), 0)
            qry = lax.broadcasted_iota(jnp.int32, (Q_CHUNK, Q_CHUNK), 1)
            s_sc[d0:nk, :] = jnp.where(key <= qry, s_sc[d0:nk, :], NEG_BIG)
        blk_max = s_sc[0:K_ROWS, :]
        for r in range(K_ROWS, nk, K_ROWS):
            blk_max = jnp.maximum(blk_max, s_sc[r:r + K_ROWS, :])
        m_prev = m_sc[...]
        m_new = jnp.maximum(m_prev, jnp.max(blk_max, axis=0, keepdims=True))
        alpha = jnp.exp2(m_prev - m_new)
        p_sum = jnp.zeros((K_ROWS, Q_CHUNK), F32)
        acc = alpha * acc_sc[...]
        for kt in range(0, nk, Q_CHUNK):
            parts = []
            for r in range(kt, kt + Q_CHUNK, K_ROWS):
                p = jnp.exp2(s_sc[r:r + K_ROWS, :] - m_new)
                p_sum = p_sum + p
                parts.append(p.astype(BF16))
            acc = acc + _dot(vt_ref[0, hh, jk, :, kt:kt + Q_CHUNK], jnp.concatenate(parts, axis=0))
        l_sc[...] = alpha * l_sc[...] + jnp.sum(p_sum, axis=0, keepdims=True)
        acc_sc[...] = acc
        m_sc[...] = m_new

    def stage(cur, cur_buf, diagonal, nxt=None):
        for chain in chains:
            if nxt is not None:
                scores(chain, nxt, 1 - cur_buf)
            softmax_values(chain, cur, cur_buf, diagonal)

    for chain in chains:
        scores(chain, 0, 0)

    def pair(jj, carry):
        stage(2 * jj, 0, False, nxt=2 * jj + 1)
        stage(2 * jj + 1, 1, False, nxt=2 * jj + 2)
        return carry

    lax.fori_loop(0, i // 2, pair, 0)

    @pl.when(lax.rem(i, 2) == 0)
    def _():
        stage(i, 0, True)

    @pl.when(lax.rem(i, 2) == 1)
    def _():
        stage(i - 1, 0, False, nxt=i)
        stage(i, 1, True)

    for hh, c, _, l_sc, acc_sc, _, _ in chains:
        o_ref[0, c * Q_CHUNK:(c + 1) * Q_CHUNK, hh * V_HEAD:(hh + 1) * V_HEAD] = (
            jnp.transpose(acc_sc[...] / l_sc[...]).astype(BF16))


def _attention(qt, k, vt, tq):
    b, nh, s, _ = k.shape
    return pl.pallas_call(
        functools.partial(_attn_kernel, tq=tq),
        grid=(b, s // tq),
        in_specs=[
            pl.BlockSpec((1, nh, QK_PAD, tq), lambda bi, i: (bi, 0, 0, i)),
            pl.BlockSpec((1, nh, s, QK_PAD), lambda bi, i: (bi, 0, 0, 0)),
            pl.BlockSpec((1, nh, s // tq, V_HEAD, tq), lambda bi, i: (bi, 0, 0, 0, 0)),
        ],
        out_specs=pl.BlockSpec((1, tq, D_ATTN), lambda bi, i: (bi, i, 0)),
        out_shape=jax.ShapeDtypeStruct((b, s, D_ATTN), BF16),
        scratch_shapes=[
            pltpu.VMEM((1, Q_CHUNK), F32), pltpu.VMEM((1, Q_CHUNK), F32), pltpu.VMEM((V_HEAD, Q_CHUNK), F32),
            pltpu.VMEM((tq, Q_CHUNK), F32), pltpu.VMEM((tq, Q_CHUNK), F32),
        ] * (nh * (tq // Q_CHUNK)),
        compiler_params=_cparams(2),
        name="attention",
    )(qt, k, vt)


def _outproj_core(attn_ref, cv_ref, x_ref, mod_ref, wo_ref, ga_ref, lng_ref, lnb_ref):
    a = attn_ref[0].astype(F32)
    an = _rms(a, ga_ref[...]).astype(BF16)
    y = _dot(an, wo_ref[0:D_ATTN, :]) + _dot(cv_ref[0], wo_ref[D_ATTN:D_MODEL, :])
    gate1 = mod_ref[0, 2:3, :]
    x1 = _layer_norm(DEEPNORM_ALPHA * x_ref[0] + gate1 * y, lng_ref[...], lnb_ref[...])
    h2 = x1 * (1.0 + mod_ref[0, 4:5, :]) + mod_ref[0, 3:4, :]
    return x1, h2


def _outproj_kernel(attn_ref, cv_ref, x_ref, mod_ref, wo_ref, ga_ref, lng_ref, lnb_ref, x1_ref):
    x1, _ = _outproj_core(attn_ref, cv_ref, x_ref, mod_ref, wo_ref, ga_ref, lng_ref, lnb_ref)
    x1_ref[0] = x1


def _outproj_router_kernel(attn_ref, cv_ref, x_ref, mod_ref, wo_ref, ga_ref, lng_ref, lnb_ref, rw_ref, rb_ref,
                           x1_ref, h2_ref, rel_ref, gate_ref, relt_ref, base_ref, cnt_ref, run_sc, *, ts):
    first = jnp.logical_and(pl.program_id(0) == 0, pl.program_id(1) == 0)

    @pl.when(first)
    def _():
        run_sc[...] = jnp.zeros(run_sc.shape, F32)

    x1, h2 = _outproj_core(attn_ref, cv_ref, x_ref, mod_ref, wo_ref, ga_ref, lng_ref, lnb_ref)
    x1_ref[0] = x1
    h2_hi = h2.astype(BF16)
    h2_ref[0] = h2_hi

    lane = lax.broadcasted_iota(jnp.int32, (ts, LANES), 1)
    h2_lo = (h2 - h2_hi.astype(F32)).astype(BF16)
    prod = _dot(h2_hi, rw_ref[...]) + _dot(h2_lo, rw_ref[...])
    logits = prod + pltpu.roll(prod, LANES - N_EXPERTS, 1) + rb_ref[...]
    logits = jnp.where(lane < N_EXPERTS, logits, NEG_BIG)
    v1 = jnp.max(logits, axis=-1, keepdims=True)
    i1 = jnp.min(jnp.where(logits == v1, lane, LANES), axis=-1, keepdims=True)
    rest = jnp.where(lane == i1, NEG_BIG, logits)
    v2 = jnp.max(rest, axis=-1, keepdims=True)
    i2 = jnp.min(jnp.where(rest == v2, lane, LANES), axis=-1, keepdims=True)
    e21 = jnp.exp(v2 - v1)
    g1 = 1.0 / (1.0 + e21)
    g2 = e21 / (1.0 + e21)
    is1 = lane == i1
    is2 = lane == i2
    sel = jnp.where(jnp.logical_or(is1, is2), 1.0, 0.0)
    gate_ref[...] = jnp.where(is1, g1, 0.0) + jnp.where(is2, g2, 0.0)

    r_i = lax.broadcasted_iota(jnp.int32, (ts, ts), 0)
    c_i = lax.broadcasted_iota(jnp.int32, (ts, ts), 1)
    tri = jnp.where(c_i < r_i, 1.0, 0.0).astype(BF16)
    rank = _dot(tri, sel.astype(BF16))
    rel = jnp.where(sel > 0.0, rank, -1.0)
    rel_ref[...] = rel
    relt_ref[0] = jnp.transpose(rel)[0:N_EXPERTS, :]

    n_tile = jnp.sum(sel, axis=0, keepdims=True)
    n_pad = jnp.floor((n_tile + (ROW_ALIGN - 1)) * (1.0 / ROW_ALIGN)) * ROW_ALIGN
    base_ref[0] = run_sc[...].astype(jnp.int32)
    cnt_ref[0] = n_tile.astype(jnp.int32)
    run_sc[...] = run_sc[...] + n_pad


def _outproj(l, attn, convn, x, mod, w_o, g_attn, ln_g, ln_b, ts, router=None):
    b, s, _ = x.shape
    nsb = s // ts
    const = lambda shape, idx=l: pl.BlockSpec((None,) + shape, lambda bi, j: (idx,) + (0,) * len(shape))
    tile3 = lambda w: pl.BlockSpec((1, ts, w), lambda bi, j: (bi, j, 0))
    in_specs = [
        tile3(D_ATTN), tile3(D_CONV), tile3(D_MODEL),
        pl.BlockSpec((None, 1, 8, D_MODEL), lambda bi, j: (l, bi, 0, 0)),
        const((D_MODEL, D_MODEL)), const((1, D_ATTN)), const((1, D_MODEL)), const((1, D_MODEL)),
    ]
    out_specs = [tile3(D_MODEL), tile3(D_MODEL)]
    out_shape = [jax.ShapeDtypeStruct((b, s, D_MODEL), F32), jax.ShapeDtypeStruct((b, s, D_MODEL), BF16)]
    args = [attn, convn, x, mod, w_o, g_attn, ln_g, ln_b]
    if router is None:
        return pl.pallas_call(
            _outproj_kernel, grid=(b, nsb), in_specs=in_specs, out_specs=out_specs[0], out_shape=out_shape[0],
            compiler_params=_cparams(2), name="outproj",
        )(*args)
    moe_i, rw, rb = router
    nt = b * nsb
    flat = lambda w: pl.BlockSpec((ts, w), lambda bi, j: (bi * nsb + j, 0))
    per_tile = lambda r, w: pl.BlockSpec((1, r, w), lambda bi, j: (bi * nsb + j, 0, 0))
    in_specs += [const((D_MODEL, LANES), moe_i), const((1, LANES), moe_i)]
    out_specs += [flat(LANES), flat(LANES), per_tile(N_EXPERTS, ts), per_tile(1, LANES), per_tile(1, LANES)]
    out_shape += [
        jax.ShapeDtypeStruct((nt * ts, LANES), F32), jax.ShapeDtypeStruct((nt * ts, LANES), F32),
        jax.ShapeDtypeStruct((nt, N_EXPERTS, ts), F32),
        jax.ShapeDtypeStruct((nt, 1, LANES), jnp.int32), jax.ShapeDtypeStruct((nt, 1, LANES), jnp.int32),
    ]
    return pl.pallas_call(
        functools.partial(_outproj_router_kernel, ts=ts), grid=(b, nsb), in_specs=in_specs, out_specs=out_specs,
        out_shape=out_shape, scratch_shapes=[pltpu.VMEM((1, LANES), F32)],
        compiler_params=_cparams(2), name="outproj_router",
    )(*args, rw, rb)


FF_CHUNK = 256


def _swiglu(x, wg, wu, wd, width):
    out = None
    pending = None
    for c in range(0, width, FF_CHUNK):
        w = min(FF_CHUNK, width - c)
        g = _dot(x, wg(c, w))
        u = _dot(x, wu(c, w))
        if pending is not None:
            d = _dot(pending[0], wd(*pending[1]))
            out = d if out is None else out + d
        pending = ((_silu(g) * u).astype(BF16), (c, w))
    d = _dot(pending[0], wd(*pending[1]))
    return d if out is None else out + d


def _ffn_kernel(x1_ref, mod_ref, wg_ref, wu_ref, wd_ref, lng_ref, lnb_ref, o_ref):
    x1 = x1_ref[...]
    h = (x1 * (1.0 + mod_ref[0, 4:5, :]) + mod_ref[0, 3:4, :]).astype(BF16)
    f = _swiglu(h, lambda c, w: wg_ref[:, c:c + w], lambda c, w: wu_ref[:, c:c + w],
                lambda c, w: wd_ref[c:c + w, :], D_FF)
    gate2 = mod_ref[0, 5:6, :]
    o_ref[...] = _layer_norm(DEEPNORM_ALPHA * x1 + gate2 * f, lng_ref[...], lnb_ref[...])


def _dense_ffn(l, dense_i, x1, mod, wg, wu, wd, ln_g, ln_b, tm, tiles_per_batch):
    t = x1.shape[0]
    const = lambda shape, idx: pl.BlockSpec((None,) + shape, lambda i: (idx,) + (0,) * len(shape),
                                            pipeline_mode=pl.Buffered(1))
    return pl.pallas_call(
        _ffn_kernel,
        grid=(t // tm,),
        in_specs=[
            pl.BlockSpec((tm, D_MODEL), lambda i: (i, 0)),
            pl.BlockSpec((None, 1, 8, D_MODEL), lambda i: (l, i // tiles_per_batch, 0, 0)),
            const((D_MODEL, D_FF), dense_i), const((D_MODEL, D_FF), dense_i), const((D_FF, D_MODEL), dense_i),
            const((1, D_MODEL), l), const((1, D_MODEL), l),
        ],
        out_specs=pl.BlockSpec((tm, D_MODEL), lambda i: (i, 0)),
        out_shape=jax.ShapeDtypeStruct((t, D_MODEL), F32),
        compiler_params=_cparams(1),
        name="dense_ffn",
    )(x1, mod, wg, wu, wd, ln_g, ln_b)


def _group_copy(src, dst, sem):
    return pltpu.make_async_copy(src, dst, sem)


def _dispatch_kernel(base_ref, cnt_ref, h_ref, relt_ref, xs_ref, stage, extra, zbuf, sems, xsem, zsem, *,
                     ts, zero_rows):
    i = pl.program_id(0)
    last = pl.num_programs(0) - 1
    par = lax.rem(i, 2)
    n_chunks = ts // SLOT_CHUNK
    h = h_ref[...]
    slot = lax.broadcasted_iota(jnp.int32, (SLOT_CHUNK, ts), 0).astype(F32)

    def onehot(e, c):
        return jnp.where(relt_ref[0, e:e + 1, :] == slot + float(c * SLOT_CHUNK), 1.0, 0.0).astype(BF16)

    def first_copy(step, e):
        a = pl.multiple_of(base_ref[step * N_EXPERTS + e], ROW_ALIGN)
        p = lax.rem(step, 2)
        return _group_copy(stage.at[p, e], xs_ref.at[e, pl.ds(a, SLOT_CHUNK), :], sems.at[p, e])

    for e in range(N_EXPERTS):
        stage[par, e] = _dot(onehot(e, 0), h).astype(BF16)

    @pl.when(i > 0)
    def _():
        for e in range(N_EXPERTS):
            first_copy(i - 1, e).wait()

    for e in range(N_EXPERTS):
        first_copy(i, e).start()

    for e in range(N_EXPERTS):
        n = cnt_ref[i * N_EXPERTS + e]
        for c in range(1, n_chunks):
            @pl.when(n > c * SLOT_CHUNK)
            def _(e=e, c=c):
                a = pl.multiple_of(base_ref[i * N_EXPERTS + e], ROW_ALIGN)
                extra[...] = _dot(onehot(e, c), h).astype(BF16)
                cp = _group_copy(extra, xs_ref.at[e, pl.ds(a + c * SLOT_CHUNK, SLOT_CHUNK), :], xsem)
                cp.start()
                cp.wait()

    @pl.when(i == last)
    def _():
        for e in range(N_EXPERTS):
            first_copy(i, e).wait()
        zbuf[...] = jnp.zeros(zbuf.shape, BF16)
        for e in range(N_EXPERTS):
            n = cnt_ref[i * N_EXPERTS + e]
            end = base_ref[i * N_EXPERTS + e] + ((n + (ROW_ALIGN - 1)) // ROW_ALIGN) * ROW_ALIGN
            end = pl.multiple_of(end, ROW_ALIGN)
            copies = [
                _group_copy(zbuf, xs_ref.at[e, pl.ds(end + r * SLOT_CHUNK, SLOT_CHUNK), :], zsem.at[e])
                for r in range(zero_rows // SLOT_CHUNK)
            ]
            for cp in copies:
                cp.start()
            for cp in copies:
                cp.wait()


def _dispatch(base, cnt, h2, relt, ts, cap, zero_rows):
    t = h2.shape[0]
    nt = t // ts
    n_chunks = ts // SLOT_CHUNK
    grid_spec = pltpu.PrefetchScalarGridSpec(
        num_scalar_prefetch=2,
        grid=(nt,),
        in_specs=[
            pl.BlockSpec((ts, D_MODEL), lambda i, b_, c_: (i, 0)),
            pl.BlockSpec((1, N_EXPERTS, ts), lambda i, b_, c_: (i, 0, 0)),
        ],
        out_specs=pl.BlockSpec(memory_space=pl.ANY),
        scratch_shapes=[
            pltpu.VMEM((2, N_EXPERTS, SLOT_CHUNK, D_MODEL), BF16),
            pltpu.VMEM((SLOT_CHUNK, D_MODEL), BF16),
            pltpu.VMEM((SLOT_CHUNK, D_MODEL), BF16),
            pltpu.SemaphoreType.DMA((2, N_EXPERTS)),
            pltpu.SemaphoreType.DMA(()),
            pltpu.SemaphoreType.DMA((N_EXPERTS,)),
        ],
    )
    return pl.pallas_call(
        functools.partial(_dispatch_kernel, ts=ts, zero_rows=zero_rows),
        grid_spec=grid_spec,
        out_shape=jax.ShapeDtypeStruct((N_EXPERTS, cap, D_MODEL), BF16),
        compiler_params=_cparams(1),
        name="moe_dispatch",
    )(base, cnt, h2, relt)


def _expert_kernel(te_ref, tr_ref, rows_ref, nv_ref, x_ref, wg_ref, wu_ref, wd_ref, o_ref, acc_sc):
    g_i = pl.program_id(0)
    k = pl.program_id(1)

    valid = g_i < nv_ref[0]

    @pl.when(jnp.logical_and(valid, k == 0))
    def _():
        acc_sc[...] = jnp.zeros(acc_sc.shape, F32)

    for m in EXPERT_ROW_SIZES:
        @pl.when(jnp.logical_and(valid, rows_ref[g_i] == m))
        def _(m=m):
            part = _swiglu(x_ref[0, 0:m, :], lambda c, w: wg_ref[0, :, c:c + w].astype(BF16),
                           lambda c, w: wu_ref[0, :, c:c + w].astype(BF16),
                           lambda c, w: wd_ref[0, c:c + w, :].astype(BF16), wg_ref.shape[2])
            total = acc_sc[0:m, :] + part
            acc_sc[0:m, :] = total
            o_ref[0, 0:m, :] = total.astype(BF16)


def _expert_ffn(moe_i, tile_e, tile_r, tile_rows, n_valid, xs, wg, wu, wd, tf, n_steps):
    tm = EXPERT_ROWS
    nk = D_FF_EXPERT // tf
    cap = xs.shape[1]

    def k_eff(g, k, nv):
        return jnp.where(g < nv[0], k, nk - 1)

    grid_spec = pltpu.PrefetchScalarGridSpec(
        num_scalar_prefetch=4,
        grid=(n_steps, nk),
        in_specs=[
            pl.BlockSpec((1, tm, D_MODEL), lambda g, k, te, tr, rw, nv: (te[g], tr[g], 0)),
            pl.BlockSpec((None, 1, D_MODEL, tf), lambda g, k, te, tr, rw, nv: (moe_i, te[g], 0, k_eff(g, k, nv))),
            pl.BlockSpec((None, 1, D_MODEL, tf), lambda g, k, te, tr, rw, nv: (moe_i, te[g], 0, k_eff(g, k, nv))),
            pl.BlockSpec((None, 1, tf, D_MODEL), lambda g, k, te, tr, rw, nv: (moe_i, te[g], k_eff(g, k, nv), 0)),
        ],
        out_specs=pl.BlockSpec((1, tm, D_MODEL), lambda g, k, te, tr, rw, nv: (te[g], tr[g], 0)),
        scratch_shapes=[pltpu.VMEM((tm, D_MODEL), F32)],
    )
    return pl.pallas_call(
        _expert_kernel,
        grid_spec=grid_spec,
        out_shape=jax.ShapeDtypeStruct((N_EXPERTS, cap, D_MODEL), BF16),
        compiler_params=_cparams(2),
        name="moe_experts",
    )(tile_e, tile_r, tile_rows, n_valid, xs, wg, wu, wd)


def _combine_kernel(base_ref, cnt_ref, lim_ref, ys_ref, rel_ref, gate_ref, x1_ref, mod_ref, lng_ref, lnb_ref, o_ref,
                    slab, xslab, acc_sc, sems, xsem, *, ts):
    i = pl.program_id(0)
    last = pl.num_programs(0) - 1
    par = lax.rem(i, 2)
    n_chunks = ts // SLOT_CHUNK
    slot = lax.broadcasted_iota(jnp.int32, (ts, SLOT_CHUNK), 1).astype(F32)

    def slab_start(step, e, c):
        a = base_ref[step * N_EXPERTS + e] + c * SLOT_CHUNK
        start = jnp.maximum(jnp.minimum(a, lim_ref[e] - SLOT_CHUNK), 0)
        return pl.multiple_of(start, ROW_ALIGN), a - start

    def first_copy(step, e):
        start, _ = slab_start(step, e, 0)
        p = lax.rem(step, 2)
        return _group_copy(ys_ref.at[e, pl.ds(start, SLOT_CHUNK), :], slab.at[p, e], sems.at[p, e])

    def contribution(e, c, rows):
        _, shift = slab_start(i, e, c)
        offset = (shift - c * SLOT_CHUNK).astype(F32)
        onehot = jnp.where(rel_ref[:, e:e + 1] + offset == slot, 1.0, 0.0).astype(BF16)
        return _dot(onehot, rows) * gate_ref[:, e:e + 1]

    @pl.when(i == 0)
    def _():
        for e in range(N_EXPERTS):
            first_copy(i, e).start()

    @pl.when(i < last)
    def _():
        for e in range(N_EXPERTS):
            first_copy(i + 1, e).start()

    for e in range(N_EXPERTS):
        first_copy(i, e).wait()
    f = contribution(0, 0, slab[par, 0])
    for e in range(1, N_EXPERTS):
        f = f + contribution(e, 0, slab[par, e])
    acc_sc[...] = f

    for e in range(N_EXPERTS):
        n = cnt_ref[i * N_EXPERTS + e]
        for c in range(1, n_chunks):
            @pl.when(n > c * SLOT_CHUNK)
            def _(e=e, c=c):
                start, _ = slab_start(i, e, c)
                cp = _group_copy(ys_ref.at[e, pl.ds(start, SLOT_CHUNK), :], xslab, xsem)
                cp.start()
                cp.wait()
                acc_sc[...] += contribution(e, c, xslab[...])

    gate2 = mod_ref[0, 5:6, :]
    o_ref[...] = _layer_norm(DEEPNORM_ALPHA * x1_ref[...] + gate2 * acc_sc[...], lng_ref[...], lnb_ref[...])


def _combine(l, base, cnt, lim, ys, rel, gates, x1, mod, ln_g, ln_b, ts, tiles_per_batch):
    t = x1.shape[0]
    grid_spec = pltpu.PrefetchScalarGridSpec(
        num_scalar_prefetch=3,
        grid=(t // ts,),
        in_specs=[
            pl.BlockSpec(memory_space=pl.ANY),
            pl.BlockSpec((ts, LANES), lambda i, *_: (i, 0)),
            pl.BlockSpec((ts, LANES), lambda i, *_: (i, 0)),
            pl.BlockSpec((ts, D_MODEL), lambda i, *_: (i, 0)),
            pl.BlockSpec((None, 1, 8, D_MODEL), lambda i, *_: (l, i // tiles_per_batch, 0, 0)),
            pl.BlockSpec((None, 1, D_MODEL), lambda i, *_: (l, 0, 0)),
            pl.BlockSpec((None, 1, D_MODEL), lambda i, *_: (l, 0, 0)),
        ],
        out_specs=pl.BlockSpec((ts, D_MODEL), lambda i, *_: (i, 0)),
        scratch_shapes=[
            pltpu.VMEM((2, N_EXPERTS, SLOT_CHUNK, D_MODEL), BF16),
            pltpu.VMEM((SLOT_CHUNK, D_MODEL), BF16),
            pltpu.VMEM((ts, D_MODEL), F32),
            pltpu.SemaphoreType.DMA((2, N_EXPERTS)),
            pltpu.SemaphoreType.DMA(()),
        ],
    )
    return pl.pallas_call(
        functools.partial(_combine_kernel, ts=ts),
        grid_spec=grid_spec,
        out_shape=jax.ShapeDtypeStruct((t, D_MODEL), F32),
        compiler_params=_cparams(1),
        name="moe_combine",
    )(base, cnt, lim, ys, rel, gates, x1, mod, ln_g, ln_b)


def _expert_tile_plan(base, cnt, nt, n_steps):
    tm = EXPERT_ROWS
    i32 = lambda v: v.astype(jnp.int32)
    last_base = base.reshape(nt, N_EXPERTS)[-1]
    last_cnt = cnt.reshape(nt, N_EXPERTS)[-1]
    total = last_base + ((last_cnt + (ROW_ALIGN - 1)) // ROW_ALIGN) * ROW_ALIGN
    tiles = jnp.maximum((total + tm - 1) // tm, 1)
    rest = total - (tiles - 1) * tm
    sizes = jnp.asarray(EXPERT_ROW_SIZES, jnp.int32)
    last_rows = sizes[jnp.sum((rest[:, None] > sizes[None, :]).astype(jnp.int32), axis=1)]
    ends = jnp.cumsum(tiles)
    n_valid = ends[-1]
    g = jnp.minimum(jnp.arange(n_steps, dtype=jnp.int32), n_valid - 1)
    tile_e = jnp.sum((g[:, None] >= ends[None, :]).astype(jnp.int32), axis=1)
    tile_r = g - (ends - tiles)[tile_e]
    tile_rows = jnp.where(tile_r == tiles[tile_e] - 1, last_rows[tile_e], tm)
    lim = (tiles - 1) * tm + last_rows
    return i32(tile_e), i32(tile_r), i32(tile_rows), i32(n_valid.reshape(1)), i32(lim)


def _top2_moe(l, moe_i, h2, x1, mod, routing, wg, wu, wd, ln_g, ln_b, ts, tiles_per_batch):
    rel, gates, relt, base3, cnt3 = routing
    t = h2.shape[0]
    nt = t // ts
    tm, tf = EXPERT_ROWS, 512
    base = base3[:, 0, :N_EXPERTS].reshape(-1)
    cnt = cnt3[:, 0, :N_EXPERTS].reshape(-1)
    zero_rows = -(-tm // SLOT_CHUNK) * SLOT_CHUNK
    rows_max = t + ROW_ALIGN * nt
    cap = -(-(rows_max + SLOT_CHUNK + zero_rows) // tm) * tm
    rows_all = 2 * t + ROW_ALIGN * N_EXPERTS * nt
    n_steps = -(-rows_all // tm) + N_EXPERTS
    xs = _dispatch(base, cnt, h2, relt, ts, cap, zero_rows)
    tile_e, tile_r, tile_rows, n_valid, lim = _expert_tile_plan(base, cnt, nt, n_steps)
    ys = _expert_ffn(moe_i, tile_e, tile_r, tile_rows, n_valid, xs, wg, wu, wd, tf, n_steps)
    return _combine(l, base, cnt, lim, ys, rel, gates, x1, mod, ln_g, ln_b, ts, tiles_per_batch)


def kernel(x, c, positions, w_in, q_norm_g, kv_norm_g, w_uq, w_ukv, conv_w, mix_norm_g, w_o, w_ada, b_ada,
           ln1_g, ln1_b, ln2_g, ln2_b, ffn_w_gate, ffn_w_up, ffn_w_down, moe_router_w, moe_router_b,
           moe_w_gate, moe_w_up, moe_w_down):
    b, s, _ = x.shape
    n_layers = w_in.shape[0]
    ts = min(512, s)
    t = b * s
    tiles_per_batch = s // ts

    bp = -(-b // 16) * 16
    c_pad = jnp.pad(c, ((0, bp - b), (0, 0)))
    mod = _ada_mod(c_pad, w_ada, b_ada)
    mod = jnp.pad(jnp.transpose(mod[:, :, :b, :], (0, 2, 1, 3)), ((0, 0), (0, 0), (0, 2), (0, 0)))
    rope_tab = _rope_tables(positions, ts)

    rows = lambda v: v.reshape(v.shape[0], 1, -1)
    o3 = Q_LORA + KV_LORA + QK_ROPE
    w_in_p = jnp.concatenate(
        [w_in[:, :, :o3], jnp.zeros((n_layers, D_MODEL, D_IN_PAD - w_in.shape[2]), F32), w_in[:, :, o3:]], axis=2
    ).astype(BF16)
    w_uq_t = jnp.transpose(jnp.pad(
        w_uq.reshape(n_layers, Q_LORA, N_HEADS, QK_NOPE + QK_ROPE),
        ((0, 0), (0, 0), (0, 0), (0, QK_PAD - QK_NOPE - QK_ROPE))
    ).reshape(n_layers, Q_LORA, N_HEADS * QK_PAD), (0, 2, 1)).astype(BF16)
    w_ukv_h = w_ukv.reshape(n_layers, KV_LORA, N_HEADS, QK_NOPE + V_HEAD)
    w_k = w_ukv_h[..., :QK_NOPE].reshape(n_layers, KV_LORA, N_HEADS * QK_NOPE).astype(BF16)
    w_v_t = jnp.transpose(w_ukv_h[..., QK_NOPE:].reshape(n_layers, KV_LORA, N_HEADS * V_HEAD), (0, 2, 1)).astype(BF16)
    w_o_b = w_o.astype(BF16)
    conv_w_p = jnp.pad(conv_w, ((0, 0), (0, 8 - CONV_WIDTH), (0, 0)))
    g_attn, g_conv = rows(mix_norm_g[:, :D_ATTN]), rows(mix_norm_g[:, D_ATTN:])
    ffn_wg, ffn_wu, ffn_wd = ffn_w_gate.astype(BF16), ffn_w_up.astype(BF16), ffn_w_down.astype(BF16)
    rw_hi = moe_router_w.astype(BF16)
    rw_lo = (moe_router_w - rw_hi.astype(F32)).astype(BF16)
    rw = jnp.pad(jnp.concatenate([rw_hi, rw_lo], axis=2), ((0, 0), (0, 0), (0, LANES - 2 * N_EXPERTS)))
    rb = rows(jnp.pad(moe_router_b, ((0, 0), (0, LANES - N_EXPERTS))))

    for l in range(n_layers):
        qt, k, vt, convn = _inproj(l, x, mod, w_in_p, w_uq_t, w_k, w_v_t, rows(q_norm_g), rows(kv_norm_g),
                                   conv_w_p, g_conv, rope_tab, ts)
        attn = _attention(qt, k, vt, ts)
        common = (l, attn, convn, x, mod, w_o_b, g_attn, rows(ln1_g), rows(ln1_b), ts)
        i = l // 2
        if l % 2 == 0:
            x1 = _outproj(*common)
            x = _dense_ffn(l, i, x1.reshape(t, D_MODEL), mod, ffn_wg, ffn_wu, ffn_wd, rows(ln2_g), rows(ln2_b),
                           ts, tiles_per_batch)
        else:
            x1, h2, *routing = _outproj(*common, router=(i, rw, rb))
            x = _top2_moe(l, i, h2.reshape(t, D_MODEL), x1.reshape(t, D_MODEL), mod, routing,
                          moe_w_gate, moe_w_up, moe_w_down, rows(ln2_g), rows(ln2_b), ts, tiles_per_batch)
        x = x.reshape(b, s, D_MODEL)
    return x
```

```python
import functools

import jax
import jax.numpy as jnp
from jax import lax
from jax.experimental import pallas as pl
from jax.experimental.pallas import tpu as pltpu

F32 = jnp.float32
BF16 = jnp.bfloat16

D_MODEL = 1024
N_HEADS = 4
QK_NOPE = 128
QK_ROPE = 64
V_HEAD = 128
Q_LORA = 256
KV_LORA = 128
D_ATTN = N_HEADS * V_HEAD
D_CONV = D_MODEL - D_ATTN
CONV_WIDTH = 3
ROPE_BASE = 10000.0
D_FF = 2816
N_EXPERTS = 8
D_FF_EXPERT = 3584
RMS_EPS = 1e-6
LN_EPS = 1e-5
DEPTH = 2
DEEPNORM_ALPHA = (2 * DEPTH) ** 0.25
SM_SCALE = (QK_NOPE + QK_ROPE) ** -0.5
LOG2E = 1.4426950408889634
Q_SCALE = SM_SCALE * LOG2E

LANES = 128
QK_PAD = 2 * LANES
D_IN_PAD = 2048
ROW_ALIGN = 16
SLOT_CHUNK = 256
EXPERT_ROW_STEP = 384
EXPERT_ROW_SIZES = tuple(EXPERT_ROW_STEP * n for n in range(1, 7))
EXPERT_ROWS = EXPERT_ROW_SIZES[-1]
VMEM_LIMIT = 56 * 1024 * 1024
NEG_BIG = -1e30


def _cparams(n_axes):
    return pltpu.CompilerParams(dimension_semantics=("arbitrary",) * n_axes, vmem_limit_bytes=VMEM_LIMIT)


def _dot(a, b):
    return jnp.dot(a, b, preferred_element_type=F32)


def _rms(x, g):
    return x * lax.rsqrt(jnp.mean(x * x, axis=-1, keepdims=True) + RMS_EPS) * g


def _layer_norm(x, g, b):
    mu = jnp.mean(x, axis=-1, keepdims=True)
    xc = x - mu
    var = jnp.mean(xc * xc, axis=-1, keepdims=True)
    return xc * lax.rsqrt(var + LN_EPS) * g + b


def _silu(x):
    return x * jax.nn.sigmoid(x)


def _ada_kernel(c_ref, w_ref, b_ref, o_ref):
    ca = _silu(c_ref[...]).astype(BF16)
    o_ref[0, 0] = _dot(ca, w_ref[0].astype(BF16)) + b_ref[0]


def _ada_mod(c_pad, w_ada, b_ada):
    n_layers = w_ada.shape[0]
    bp = c_pad.shape[0]
    return pl.pallas_call(
        _ada_kernel,
        grid=(n_layers, 6),
        in_specs=[
            pl.BlockSpec((bp, D_MODEL), lambda l, j: (0, 0)),
            pl.BlockSpec((1, D_MODEL, D_MODEL), lambda l, j: (l, 0, j)),
            pl.BlockSpec((1, 1, D_MODEL), lambda l, j: (l, 0, j)),
        ],
        out_specs=pl.BlockSpec((1, 1, bp, D_MODEL), lambda l, j: (l, j, 0, 0)),
        out_shape=jax.ShapeDtypeStruct((n_layers, 6, bp, D_MODEL), F32),
        compiler_params=_cparams(2),
        name="ada_mod",
    )(c_pad, w_ada, b_ada.reshape(n_layers, 1, 6 * D_MODEL))


ROPE_HALF = QK_ROPE // 2


def _rope_kernel(pos_ref, f_ref, o_ref):
    ang = pos_ref[0].astype(F32) * f_ref[...]
    o_ref[0, 0:ROPE_HALF, :] = jnp.cos(ang)
    o_ref[0, ROPE_HALF:QK_ROPE, :] = jnp.sin(ang)


def _rope_tables(positions, ts):
    nt = positions.size // ts
    inv_freq = ROPE_BASE ** (-jnp.arange(0, QK_ROPE, 2, dtype=F32) / QK_ROPE)
    return pl.pallas_call(
        _rope_kernel,
        grid=(nt,),
        in_specs=[pl.BlockSpec((1, 1, ts), lambda i: (i, 0, 0)), pl.BlockSpec((ROPE_HALF, 1), lambda i: (0, 0))],
        out_specs=pl.BlockSpec((1, QK_ROPE, ts), lambda i: (i, 0, 0)),
        out_shape=jax.ShapeDtypeStruct((nt, QK_ROPE, ts), F32),
        compiler_params=_cparams(1),
        name="rope_tables",
    )(positions.reshape(nt, 1, ts), inv_freq.reshape(ROPE_HALF, 1))


def _inproj_kernel(x_ref, mod_ref, win_ref, wuqt_ref, wk_ref, wvt_ref, qg_ref, kvg_ref, cw_ref, cg_ref, rope_ref,
                   qt_ref, k_ref, vt_ref, cv_ref, cu_ext, *, ts):
    j = pl.program_id(1)

    @pl.when(j == 0)
    def _():
        cu_ext[0:8, :] = jnp.zeros((8, D_CONV), F32)

    @pl.when(j > 0)
    def _():
        cu_ext[0:8, :] = cu_ext[ts:ts + 8, :]

    x = x_ref[0]
    shift = mod_ref[0, 0:1, :]
    scale = mod_ref[0, 1:2, :]
    h = (x * (1.0 + scale) + shift).astype(BF16)
    z_lat = _dot(h, win_ref[:, 0:512])
    c_q = z_lat[:, 0:Q_LORA]
    c_kv = z_lat[:, Q_LORA:Q_LORA + KV_LORA]
    k_rope = z_lat[:, 384:512]
    z_gate = _dot(h, win_ref[:, 512:1536])
    gate_b = z_gate[:, 0:D_CONV]
    gate_c = z_gate[:, D_CONV:2 * D_CONV]
    u = _dot(h, win_ref[:, 1536:2048])
    cqn = _rms(c_q, qg_ref[...])
    ckvn = _rms(c_kv, kvg_ref[...])
    q_t = _dot(wuqt_ref[...], jnp.transpose(cqn).astype(BF16))
    k_nope = _dot(ckvn.astype(BF16), wk_ref[...])
    v_t = _dot(wvt_ref[...], jnp.transpose(ckvn).astype(BF16))

    cos = rope_ref[0, 0:ROPE_HALF, :]
    sin = rope_ref[0, ROPE_HALF:QK_ROPE, :]

    def rope_t(block):
        x1, x2 = block[0:ROPE_HALF, :], block[ROPE_HALF:QK_ROPE, :]
        return x1 * cos - x2 * sin, x2 * cos + x1 * sin

    k1, k2 = rope_t(jnp.transpose(k_rope)[0:QK_ROPE, :])
    k_rot = jnp.transpose(jnp.concatenate([k1, k2, jnp.zeros((LANES - QK_ROPE, ts), F32)], axis=0)).astype(BF16)
    for hh in range(N_HEADS):
        qo = hh * QK_PAD
        r0 = qo + QK_NOPE
        q1, q2 = rope_t(q_t[r0:r0 + QK_ROPE, :])
        qt_ref[0, hh, 0:QK_NOPE, :] = (q_t[qo:r0, :] * Q_SCALE).astype(BF16)
        qt_ref[0, hh, QK_NOPE:QK_NOPE + ROPE_HALF, :] = (q1 * Q_SCALE).astype(BF16)
        qt_ref[0, hh, QK_NOPE + ROPE_HALF:QK_NOPE + QK_ROPE, :] = (q2 * Q_SCALE).astype(BF16)
        qt_ref[0, hh, QK_NOPE + QK_ROPE:QK_PAD, :] = jnp.zeros((QK_PAD - QK_NOPE - QK_ROPE, ts), BF16)
        k_ref[0, hh, :, 0:LANES] = k_nope[:, hh * QK_NOPE:(hh + 1) * QK_NOPE].astype(BF16)
        k_ref[0, hh, :, LANES:QK_PAD] = k_rot
        vt_ref[0, hh, 0] = v_t[hh * V_HEAD:(hh + 1) * V_HEAD, :].astype(BF16)

    cu_ext[8:ts + 8, :] = gate_c * u
    conv = (cw_ref[2:3, :] * cu_ext[8:ts + 8, :]
            + cw_ref[1:2, :] * cu_ext[pl.ds(7, ts), :]
            + cw_ref[0:1, :] * cu_ext[pl.ds(6, ts), :])
    conv = gate_b * conv
    cv_ref[0] = _rms(conv, cg_ref[...]).astype(BF16)


def _inproj(l, x, mod, w_in_p, w_uq_t, w_k, w_v_t, qg, kvg, conv_w, conv_g, rope_tab, ts):
    b, s, _ = x.shape
    nsb = s // ts
    const = lambda shape: pl.BlockSpec((None,) + shape, lambda bi, j: (l,) + (0,) * len(shape))
    return pl.pallas_call(
        functools.partial(_inproj_kernel, ts=ts),
        grid=(b, nsb),
        in_specs=[
            pl.BlockSpec((1, ts, D_MODEL), lambda bi, j: (bi, j, 0)),
            pl.BlockSpec((None, 1, 8, D_MODEL), lambda bi, j: (l, bi, 0, 0)),
            const((D_MODEL, D_IN_PAD)),
            const((N_HEADS * QK_PAD, Q_LORA)),
            const((KV_LORA, N_HEADS * QK_NOPE)),
            const((N_HEADS * V_HEAD, KV_LORA)),
            const((1, Q_LORA)),
            const((1, KV_LORA)),
            const((8, D_CONV)),
            const((1, D_CONV)),
            pl.BlockSpec((1, QK_ROPE, ts), lambda bi, j: (bi * nsb + j, 0, 0)),
        ],
        out_specs=[
            pl.BlockSpec((1, N_HEADS, QK_PAD, ts), lambda bi, j: (bi, 0, 0, j)),
            pl.BlockSpec((1, N_HEADS, ts, QK_PAD), lambda bi, j: (bi, 0, j, 0)),
            pl.BlockSpec((1, N_HEADS, 1, V_HEAD, ts), lambda bi, j: (bi, 0, j, 0, 0)),
            pl.BlockSpec((1, ts, D_CONV), lambda bi, j: (bi, j, 0)),
        ],
        out_shape=[
            jax.ShapeDtypeStruct((b, N_HEADS, QK_PAD, s), BF16),
            jax.ShapeDtypeStruct((b, N_HEADS, s, QK_PAD), BF16),
            jax.ShapeDtypeStruct((b, N_HEADS, nsb, V_HEAD, ts), BF16),
            jax.ShapeDtypeStruct((b, s, D_CONV), BF16),
        ],
        scratch_shapes=[pltpu.VMEM((ts + 8, D_CONV), F32)],
        compiler_params=_cparams(2),
        name="inproj",
    )(x, mod, w_in_p, w_uq_t, w_k, w_v_t, qg, kvg, conv_w, conv_g, rope_tab)


Q_CHUNK = 256


K_ROWS = 16


def _attn_kernel(qt_ref, k_ref, vt_ref, o_ref, *scratch, tq):
    i = pl.program_id(1)
    n_chunks = tq // Q_CHUNK
    per_chain = 5
    chains = [(hh, c) + tuple(scratch[per_chain * (hh * n_chunks + c):per_chain * (hh * n_chunks + c + 1)])
              for hh in range(N_HEADS) for c in range(n_chunks)]
    for _, _, m_sc, l_sc, acc_sc, _, _ in chains:
        m_sc[...] = jnp.full(m_sc.shape, NEG_BIG, F32)
        l_sc[...] = jnp.zeros(l_sc.shape, F32)
        acc_sc[...] = jnp.zeros(acc_sc.shape, F32)

    def scores(chain, jk, buf):
        hh, c = chain[0], chain[1]
        kj = k_ref[0, hh, pl.ds(pl.multiple_of(jk * tq, tq), tq), :]
        chain[5 + buf][...] = _dot(kj, qt_ref[0, hh, :, c * Q_CHUNK:(c + 1) * Q_CHUNK])

    def softmax_values(chain, jk, buf, diagonal):
        hh, c, m_sc, l_sc, acc_sc = chain[:5]
        s_sc = chain[5 + buf]
        nk = (c + 1) * Q_CHUNK if diagonal else tq
        if diagonal:
            d0 = nk - Q_CHUNK
            key = lax.broadcasted_iota(jnp.int32, (Q_CHUNK, Q_CHUNK), 0)
            qry = lax.broadcasted_iota(jnp.int32, (Q_CHUNK, Q_CHUNK), 1)
            s_sc[d0:nk, :] = jnp.where(key <= qry, s_sc[d0:nk, :], NEG_BIG)
        blk_max = s_sc[0:K_ROWS, :]
        for r in range(K_ROWS, nk, K_ROWS):
            blk_max = jnp.maximum(blk_max, s_sc[r:r + K_ROWS, :])
        m_prev = m_sc[...]
        m_new = jnp.maximum(m_prev, jnp.max(blk_max, axis=0, keepdims=True))
        alpha = jnp.exp2(m_prev - m_new)
        p_sum = jnp.zeros((K_ROWS, Q_CHUNK), F32)
        acc = alpha * acc_sc[...]
        for kt in range(0, nk, Q_CHUNK):
            parts = []
            for r in range(kt, kt + Q_CHUNK, K_ROWS):
                p = jnp.exp2(s_sc[r:r + K_ROWS, :] - m_new)
                p_sum = p_sum + p
                parts.append(p.astype(BF16))
            acc = acc + _dot(vt_ref[0, hh, jk, :, kt:kt + Q_CHUNK], jnp.concatenate(parts, axis=0))
        l_sc[...] = alpha * l_sc[...] + jnp.sum(p_sum, axis=0, keepdims=True)
        acc_sc[...] = acc
        m_sc[...] = m_new

    def stage(cur, cur_buf, diagonal, nxt=None):
        for chain in chains:
            if nxt is not None:
                scores(chain, nxt, 1 - cur_buf)
            softmax_values(chain, cur, cur_buf, diagonal)

    for chain in chains:
        scores(chain, 0, 0)

    def pair(jj, carry):
        stage(2 * jj, 0, False, nxt=2 * jj + 1)
        stage(2 * jj + 1, 1, False, nxt=2 * jj + 2)
        return carry

    lax.fori_loop(0, i // 2, pair, 0)

    @pl.when(lax.rem(i, 2) == 0)
    def _():
        stage(i, 0, True)

    @pl.when(lax.rem(i, 2) == 1)
    def _():
        stage(i - 1, 0, False, nxt=i)
        stage(i, 1, True)

    for hh, c, _, l_sc, acc_sc, _, _ in chains:
        o_ref[0, c * Q_CHUNK:(c + 1) * Q_CHUNK, hh * V_HEAD:(hh + 1) * V_HEAD] = (
            jnp.transpose(acc_sc[...] / l_sc[...]).astype(BF16))


def _attention(qt, k, vt, tq):
    b, nh, s, _ = k.shape
    return pl.pallas_call(
        functools.partial(_attn_kernel, tq=tq),
        grid=(b, s // tq),
        in_specs=[
            pl.BlockSpec((1, nh, QK_PAD, tq), lambda bi, i: (bi, 0, 0, i)),
            pl.BlockSpec((1, nh, s, QK_PAD), lambda bi, i: (bi, 0, 0, 0)),
            pl.BlockSpec((1, nh, s // tq, V_HEAD, tq), lambda bi, i: (bi, 0, 0, 0, 0)),
        ],
        out_specs=pl.BlockSpec((1, tq, D_ATTN), lambda bi, i: (bi, i, 0)),
        out_shape=jax.ShapeDtypeStruct((b, s, D_ATTN), BF16),
        scratch_shapes=[
            pltpu.VMEM((1, Q_CHUNK), F32), pltpu.VMEM((1, Q_CHUNK), F32), pltpu.VMEM((V_HEAD, Q_CHUNK), F32),
            pltpu.VMEM((tq, Q_CHUNK), F32), pltpu.VMEM((tq, Q_CHUNK), F32),
        ] * (nh * (tq // Q_CHUNK)),
        compiler_params=_cparams(2),
        name="attention",
    )(qt, k, vt)


def _outproj_core(attn_ref, cv_ref, x_ref, mod_ref, wo_ref, ga_ref, lng_ref, lnb_ref):
    a = attn_ref[0].astype(F32)
    an = _rms(a, ga_ref[...]).astype(BF16)
    y = _dot(an, wo_ref[0:D_ATTN, :]) + _dot(cv_ref[0], wo_ref[D_ATTN:D_MODEL, :])
    gate1 = mod_ref[0, 2:3, :]
    x1 = _layer_norm(DEEPNORM_ALPHA * x_ref[0] + gate1 * y, lng_ref[...], lnb_ref[...])
    h2 = x1 * (1.0 + mod_ref[0, 4:5, :]) + mod_ref[0, 3:4, :]
    return x1, h2


def _outproj_kernel(attn_ref, cv_ref, x_ref, mod_ref, wo_ref, ga_ref, lng_ref, lnb_ref, x1_ref):
    x1, _ = _outproj_core(attn_ref, cv_ref, x_ref, mod_ref, wo_ref, ga_ref, lng_ref, lnb_ref)
    x1_ref[0] = x1


def _outproj_router_kernel(attn_ref, cv_ref, x_ref, mod_ref, wo_ref, ga_ref, lng_ref, lnb_ref, rw_ref, rb_ref,
                           x1_ref, h2_ref, rel_ref, gate_ref, relt_ref, base_ref, cnt_ref, run_sc, *, ts):
    first = jnp.logical_and(pl.program_id(0) == 0, pl.program_id(1) == 0)

    @pl.when(first)
    def _():
        run_sc[...] = jnp.zeros(run_sc.shape, F32)

    x1, h2 = _outproj_core(attn_ref, cv_ref, x_ref, mod_ref, wo_ref, ga_ref, lng_ref, lnb_ref)
    x1_ref[0] = x1
    h2_hi = h2.astype(BF16)
    h2_ref[0] = h2_hi

    lane = lax.broadcasted_iota(jnp.int32, (ts, LANES), 1)
    h2_lo = (h2 - h2_hi.astype(F32)).astype(BF16)
    prod = _dot(h2_hi, rw_ref[...]) + _dot(h2_lo, rw_ref[...])
    logits = prod + pltpu.roll(prod, LANES - N_EXPERTS, 1) + rb_ref[...]
    logits = jnp.where(lane < N_EXPERTS, logits, NEG_BIG)
    v1 = jnp.max(logits, axis=-1, keepdims=True)
    i1 = jnp.min(jnp.where(logits == v1, lane, LANES), axis=-1, keepdims=True)
    rest = jnp.where(lane == i1, NEG_BIG, logits)
    v2 = jnp.max(rest, axis=-1, keepdims=True)
    i2 = jnp.min(jnp.where(rest == v2, lane, LANES), axis=-1, keepdims=True)
    e21 = jnp.exp(v2 - v1)
    g1 = 1.0 / (1.0 + e21)
    g2 = e21 / (1.0 + e21)
    is1 = lane == i1
    is2 = lane == i2
    sel = jnp.where(jnp.logical_or(is1, is2), 1.0, 0.0)
    gate_ref[...] = jnp.where(is1, g1, 0.0) + jnp.where(is2, g2, 0.0)

    r_i = lax.broadcasted_iota(jnp.int32, (ts, ts), 0)
    c_i = lax.broadcasted_iota(jnp.int32, (ts, ts), 1)
    tri = jnp.where(c_i < r_i, 1.0, 0.0).astype(BF16)
    rank = _dot(tri, sel.astype(BF16))
    rel = jnp.where(sel > 0.0, rank, -1.0)
    rel_ref[...] = rel
    relt_ref[0] = jnp.transpose(rel)[0:N_EXPERTS, :]

    n_tile = jnp.sum(sel, axis=0, keepdims=True)
    n_pad = jnp.floor((n_tile + (ROW_ALIGN - 1)) * (1.0 / ROW_ALIGN)) * ROW_ALIGN
    base_ref[0] = run_sc[...].astype(jnp.int32)
    cnt_ref[0] = n_tile.astype(jnp.int32)
    run_sc[...] = run_sc[...] + n_pad


def _outproj(l, attn, convn, x, mod, w_o, g_attn, ln_g, ln_b, ts, router=None):
    b, s, _ = x.shape
    nsb = s // ts
    const = lambda shape, idx=l: pl.BlockSpec((None,) + shape, lambda bi, j: (idx,) + (0,) * len(shape))
    tile3 = lambda w: pl.BlockSpec((1, ts, w), lambda bi, j: (bi, j, 0))
    in_specs = [
        tile3(D_ATTN), tile3(D_CONV), tile3(D_MODEL),
        pl.BlockSpec((None, 1, 8, D_MODEL), lambda bi, j: (l, bi, 0, 0)),
        const((D_MODEL, D_MODEL)), const((1, D_ATTN)), const((1, D_MODEL)), const((1, D_MODEL)),
    ]
    out_specs = [tile3(D_MODEL), tile3(D_MODEL)]
    out_shape = [jax.ShapeDtypeStruct((b, s, D_MODEL), F32), jax.ShapeDtypeStruct((b, s, D_MODEL), BF16)]
    args = [attn, convn, x, mod, w_o, g_attn, ln_g, ln_b]
    if router is None:
        return pl.pallas_call(
            _outproj_kernel, grid=(b, nsb), in_specs=in_specs, out_specs=out_specs[0], out_shape=out_shape[0],
            compiler_params=_cparams(2), name="outproj",
        )(*args)
    moe_i, rw, rb = router
    nt = b * nsb
    flat = lambda w: pl.BlockSpec((ts, w), lambda bi, j: (bi * nsb + j, 0))
    per_tile = lambda r, w: pl.BlockSpec((1, r, w), lambda bi, j: (bi * nsb + j, 0, 0))
    in_specs += [const((D_MODEL, LANES), moe_i), const((1, LANES), moe_i)]
    out_specs += [flat(LANES), flat(LANES), per_tile(N_EXPERTS, ts), per_tile(1, LANES), per_tile(1, LANES)]
    out_shape += [
        jax.ShapeDtypeStruct((nt * ts, LANES), F32), jax.ShapeDtypeStruct((nt * ts, LANES), F32),
        jax.ShapeDtypeStruct((nt, N_EXPERTS, ts), F32),
        jax.ShapeDtypeStruct((nt, 1, LANES), jnp.int32), jax.ShapeDtypeStruct((nt, 1, LANES), jnp.int32),
    ]
    return pl.pallas_call(
        functools.partial(_outproj_router_kernel, ts=ts), grid=(b, nsb), in_specs=in_specs, out_specs=out_specs,
        out_shape=out_shape, scratch_shapes=[pltpu.VMEM((1, LANES), F32)],
        compiler_params=_cparams(2), name="outproj_router",
    )(*args, rw, rb)


FF_CHUNK = 256


def _swiglu(x, wg, wu, wd, width):
    out = None
    pending = None
    for c in range(0, width, FF_CHUNK):
        w = min(FF_CHUNK, width - c)
        g = _dot(x, wg(c, w))
        u = _dot(x, wu(c, w))
        if pending is not None:
            d = _dot(pending[0], wd(*pending[1]))
            out = d if out is None else out + d
        pending = ((_silu(g) * u).astype(BF16), (c, w))
    d = _dot(pending[0], wd(*pending[1]))
    return d if out is None else out + d


def _ffn_kernel(x1_ref, mod_ref, wg_ref, wu_ref, wd_ref, lng_ref, lnb_ref, o_ref):
    x1 = x1_ref[...]
    h = (x1 * (1.0 + mod_ref[0, 4:5, :]) + mod_ref[0, 3:4, :]).astype(BF16)
    f = _swiglu(h, lambda c, w: wg_ref[:, c:c + w], lambda c, w: wu_ref[:, c:c + w],
                lambda c, w: wd_ref[c:c + w, :], D_FF)
    gate2 = mod_ref[0, 5:6, :]
    o_ref[...] = _layer_norm(DEEPNORM_ALPHA * x1 + gate2 * f, lng_ref[...], lnb_ref[...])


def _dense_ffn(l, dense_i, x1, mod, wg, wu, wd, ln_g, ln_b, tm, tiles_per_batch):
    t = x1.shape[0]
    const = lambda shape, idx: pl.BlockSpec((None,) + shape, lambda i: (idx,) + (0,) * len(shape),
                                            pipeline_mode=pl.Buffered(1))
    return pl.pallas_call(
        _ffn_kernel,
        grid=(t // tm,),
        in_specs=[
            pl.BlockSpec((tm, D_MODEL), lambda i: (i, 0)),
            pl.BlockSpec((None, 1, 8, D_MODEL), lambda i: (l, i // tiles_per_batch, 0, 0)),
            const((D_MODEL, D_FF), dense_i), const((D_MODEL, D_FF), dense_i), const((D_FF, D_MODEL), dense_i),
            const((1, D_MODEL), l), const((1, D_MODEL), l),
        ],
        out_specs=pl.BlockSpec((tm, D_MODEL), lambda i: (i, 0)),
        out_shape=jax.ShapeDtypeStruct((t, D_MODEL), F32),
        compiler_params=_cparams(1),
        name="dense_ffn",
    )(x1, mod, wg, wu, wd, ln_g, ln_b)


def _group_copy(src, dst, sem):
    return pltpu.make_async_copy(src, dst, sem)


def _dispatch_kernel(base_ref, cnt_ref, h_ref, relt_ref, xs_ref, stage, extra, zbuf, sems, xsem, zsem, *,
                     ts, zero_rows):
    i = pl.program_id(0)
    last = pl.num_programs(0) - 1
    par = lax.rem(i, 2)
    n_chunks = ts // SLOT_CHUNK
    h = h_ref[...]
    slot = lax.broadcasted_iota(jnp.int32, (SLOT_CHUNK, ts), 0).astype(F32)

    def onehot(e, c):
        return jnp.where(relt_ref[0, e:e + 1, :] == slot + float(c * SLOT_CHUNK), 1.0, 0.0).astype(BF16)

    def first_copy(step, e):
        a = pl.multiple_of(base_ref[step * N_EXPERTS + e], ROW_ALIGN)
        p = lax.rem(step, 2)
        return _group_copy(stage.at[p, e], xs_ref.at[e, pl.ds(a, SLOT_CHUNK), :], sems.at[p, e])

    for e in range(N_EXPERTS):
        stage[par, e] = _dot(onehot(e, 0), h).astype(BF16)

    @pl.when(i > 0)
    def _():
        for e in range(N_EXPERTS):
            first_copy(i - 1, e).wait()

    for e in range(N_EXPERTS):
        first_copy(i, e).start()

    for e in range(N_EXPERTS):
        n = cnt_ref[i * N_EXPERTS + e]
        for c in range(1, n_chunks):
            @pl.when(n > c * SLOT_CHUNK)
            def _(e=e, c=c):
                a = pl.multiple_of(base_ref[i * N_EXPERTS + e], ROW_ALIGN)
                extra[...] = _dot(onehot(e, c), h).astype(BF16)
                cp = _group_copy(extra, xs_ref.at[e, pl.ds(a + c * SLOT_CHUNK, SLOT_CHUNK), :], xsem)
                cp.start()
                cp.wait()

    @pl.when(i == last)
    def _():
        for e in range(N_EXPERTS):
            first_copy(i, e).wait()
        zbuf[...] = jnp.zeros(zbuf.shape, BF16)
        for e in range(N_EXPERTS):
            n = cnt_ref[i * N_EXPERTS + e]
            end = base_ref[i * N_EXPERTS + e] + ((n + (ROW_ALIGN - 1)) // ROW_ALIGN) * ROW_ALIGN
            end = pl.multiple_of(end, ROW_ALIGN)
            copies = [
                _group_copy(zbuf, xs_ref.at[e, pl.ds(end + r * SLOT_CHUNK, SLOT_CHUNK), :], zsem.at[e])
                for r in range(zero_rows // SLOT_CHUNK)
            ]
            for cp in copies:
                cp.start()
            for cp in copies:
                cp.wait()


def _dispatch(base, cnt, h2, relt, ts, cap, zero_rows):
    t = h2.shape[0]
    nt = t // ts
    n_chunks = ts // SLOT_CHUNK
    grid_spec = pltpu.PrefetchScalarGridSpec(
        num_scalar_prefetch=2,
        grid=(nt,),
        in_specs=[
            pl.BlockSpec((ts, D_MODEL), lambda i, b_, c_: (i, 0)),
            pl.BlockSpec((1, N_EXPERTS, ts), lambda i, b_, c_: (i, 0, 0)),
        ],
        out_specs=pl.BlockSpec(memory_space=pl.ANY),
        scratch_shapes=[
            pltpu.VMEM((2, N_EXPERTS, SLOT_CHUNK, D_MODEL), BF16),
            pltpu.VMEM((SLOT_CHUNK, D_MODEL), BF16),
            pltpu.VMEM((SLOT_CHUNK, D_MODEL), BF16),
            pltpu.SemaphoreType.DMA((2, N_EXPERTS)),
            pltpu.SemaphoreType.DMA(()),
            pltpu.SemaphoreType.DMA((N_EXPERTS,)),
        ],
    )
    return pl.pallas_call(
        functools.partial(_dispatch_kernel, ts=ts, zero_rows=zero_rows),
        grid_spec=grid_spec,
        out_shape=jax.ShapeDtypeStruct((N_EXPERTS, cap, D_MODEL), BF16),
        compiler_params=_cparams(1),
        name="moe_dispatch",
    )(base, cnt, h2, relt)


def _expert_kernel(te_ref, tr_ref, rows_ref, nv_ref, x_ref, wg_ref, wu_ref, wd_ref, o_ref, acc_sc):
    g_i = pl.program_id(0)
    k = pl.program_id(1)

    valid = g_i < nv_ref[0]

    @pl.when(jnp.logical_and(valid, k == 0))
    def _():
        acc_sc[...] = jnp.zeros(acc_sc.shape, F32)

    for m in EXPERT_ROW_SIZES:
        @pl.when(jnp.logical_and(valid, rows_ref[g_i] == m))
        def _(m=m):
            part = _swiglu(x_ref[0, 0:m, :], lambda c, w: wg_ref[0, :, c:c + w].astype(BF16),
                           lambda c, w: wu_ref[0, :, c:c + w].astype(BF16),
                           lambda c, w: wd_ref[0, c:c + w, :].astype(BF16), wg_ref.shape[2])
            total = acc_sc[0:m, :] + part
            acc_sc[0:m, :] = total
            o_ref[0, 0:m, :] = total.astype(BF16)


def _expert_ffn(moe_i, tile_e, tile_r, tile_rows, n_valid, xs, wg, wu, wd, tf, n_steps):
    tm = EXPERT_ROWS
    nk = D_FF_EXPERT // tf
    cap = xs.shape[1]

    def k_eff(g, k, nv):
        return jnp.where(g < nv[0], k, nk - 1)

    grid_spec = pltpu.PrefetchScalarGridSpec(
        num_scalar_prefetch=4,
        grid=(n_steps, nk),
        in_specs=[
            pl.BlockSpec((1, tm, D_MODEL), lambda g, k, te, tr, rw, nv: (te[g], tr[g], 0)),
            pl.BlockSpec((None, 1, D_MODEL, tf), lambda g, k, te, tr, rw, nv: (moe_i, te[g], 0, k_eff(g, k, nv))),
            pl.BlockSpec((None, 1, D_MODEL, tf), lambda g, k, te, tr, rw, nv: (moe_i, te[g], 0, k_eff(g, k, nv))),
            pl.BlockSpec((None, 1, tf, D_MODEL), lambda g, k, te, tr, rw, nv: (moe_i, te[g], k_eff(g, k, nv), 0)),
        ],
        out_specs=pl.BlockSpec((1, tm, D_MODEL), lambda g, k, te, tr, rw, nv: (te[g], tr[g], 0)),
        scratch_shapes=[pltpu.VMEM((tm, D_MODEL), F32)],
    )
    return pl.pallas_call(
        _expert_kernel,
        grid_spec=grid_spec,
        out_shape=jax.ShapeDtypeStruct((N_EXPERTS, cap, D_MODEL), BF16),
        compiler_params=_cparams(2),
        name="moe_experts",
    )(tile_e, tile_r, tile_rows, n_valid, xs, wg, wu, wd)


def _combine_kernel(base_ref, cnt_ref, lim_ref, ys_ref, rel_ref, gate_ref, x1_ref, mod_ref, lng_ref, lnb_ref, o_ref,
                    slab, xslab, acc_sc, sems, xsem, *, ts):
    i = pl.program_id(0)
    last = pl.num_programs(0) - 1
    par = lax.rem(i, 2)
    n_chunks = ts // SLOT_CHUNK
    slot = lax.broadcasted_iota(jnp.int32, (ts, SLOT_CHUNK), 1).astype(F32)

    def slab_start(step, e, c):
        a = base_ref[step * N_EXPERTS + e] + c * SLOT_CHUNK
        start = jnp.maximum(jnp.minimum(a, lim_ref[e] - SLOT_CHUNK), 0)
        return pl.multiple_of(start, ROW_ALIGN), a - start

    def first_copy(step, e):
        start, _ = slab_start(step, e, 0)
        p = lax.rem(step, 2)
        return _group_copy(ys_ref.at[e, pl.ds(start, SLOT_CHUNK), :], slab.at[p, e], sems.at[p, e])

    def contribution(e, c, rows):
        _, shift = slab_start(i, e, c)
        offset = (shift - c * SLOT_CHUNK).astype(F32)
        onehot = jnp.where(rel_ref[:, e:e + 1] + offset == slot, 1.0, 0.0).astype(BF16)
        return _dot(onehot, rows) * gate_ref[:, e:e + 1]

    @pl.when(i == 0)
    def _():
        for e in range(N_EXPERTS):
            first_copy(i, e).start()

    @pl.when(i < last)
    def _():
        for e in range(N_EXPERTS):
            first_copy(i + 1, e).start()

    for e in range(N_EXPERTS):
        first_copy(i, e).wait()
    f = contribution(0, 0, slab[par, 0])
    for e in range(1, N_EXPERTS):
        f = f + contribution(e, 0, slab[par, e])
    acc_sc[...] = f

    for e in range(N_EXPERTS):
        n = cnt_ref[i * N_EXPERTS + e]
        for c in range(1, n_chunks):
            @pl.when(n > c * SLOT_CHUNK)
            def _(e=e, c=c):
                start, _ = slab_start(i, e, c)
                cp = _group_copy(ys_ref.at[e, pl.ds(start, SLOT_CHUNK), :], xslab, xsem)
                cp.start()
                cp.wait()
                acc_sc[...] += contribution(e, c, xslab[...])

    gate2 = mod_ref[0, 5:6, :]
    o_ref[...] = _layer_norm(DEEPNORM_ALPHA * x1_ref[...] + gate2 * acc_sc[...], lng_ref[...], lnb_ref[...])


def _combine(l, base, cnt, lim, ys, rel, gates, x1, mod, ln_g, ln_b, ts, tiles_per_batch):
    t = x1.shape[0]
    grid_spec = pltpu.PrefetchScalarGridSpec(
        num_scalar_prefetch=3,
        grid=(t // ts,),
        in_specs=[
            pl.BlockSpec(memory_space=pl.ANY),
            pl.BlockSpec((ts, LANES), lambda i, *_: (i, 0)),
            pl.BlockSpec((ts, LANES), lambda i, *_: (i, 0)),
            pl.BlockSpec((ts, D_MODEL), lambda i, *_: (i, 0)),
            pl.BlockSpec((None, 1, 8, D_MODEL), lambda i, *_: (l, i // tiles_per_batch, 0, 0)),
            pl.BlockSpec((None, 1, D_MODEL), lambda i, *_: (l, 0, 0)),
            pl.BlockSpec((None, 1, D_MODEL), lambda i, *_: (l, 0, 0)),
        ],
        out_specs=pl.BlockSpec((ts, D_MODEL), lambda i, *_: (i, 0)),
        scratch_shapes=[
            pltpu.VMEM((2, N_EXPERTS, SLOT_CHUNK, D_MODEL), BF16),
            pltpu.VMEM((SLOT_CHUNK, D_MODEL), BF16),
            pltpu.VMEM((ts, D_MODEL), F32),
            pltpu.SemaphoreType.DMA((2, N_EXPERTS)),
            pltpu.SemaphoreType.DMA(()),
        ],
    )
    return pl.pallas_call(
        functools.partial(_combine_kernel, ts=ts),
        grid_spec=grid_spec,
        out_shape=jax.ShapeDtypeStruct((t, D_MODEL), F32),
        compiler_params=_cparams(1),
        name="moe_combine",
    )(base, cnt, lim, ys, rel, gates, x1, mod, ln_g, ln_b)


def _expert_tile_plan(base, cnt, nt, n_steps):
    tm = EXPERT_ROWS
    i32 = lambda v: v.astype(jnp.int32)
    last_base = base.reshape(nt, N_EXPERTS)[-1]
    last_cnt = cnt.reshape(nt, N_EXPERTS)[-1]
    total = last_base + ((last_cnt + (ROW_ALIGN - 1)) // ROW_ALIGN) * ROW_ALIGN
    tiles = jnp.maximum((total + tm - 1) // tm, 1)
    rest = total - (tiles - 1) * tm
    last_rows = jnp.full_like(rest, EXPERT_ROW_SIZES[0])
    for small, big in zip(EXPERT_ROW_SIZES[:-1], EXPERT_ROW_SIZES[1:]):
        last_rows = last_rows + jnp.where(rest > small, big - small, 0)
    ends = jnp.cumsum(tiles)
    n_valid = ends[-1]
    g = jnp.minimum(jnp.arange(n_steps, dtype=jnp.int32), n_valid - 1)
    tile_e = jnp.sum((g[:, None] >= ends[None, :]).astype(jnp.int32), axis=1)
    onehot = (tile_e[:, None] == jnp.arange(N_EXPERTS, dtype=jnp.int32)[None, :]).astype(jnp.int32)
    pick = lambda v: jnp.sum(onehot * v[None, :], axis=1)
    tile_r = g - pick(ends - tiles)
    tile_rows = jnp.where(tile_r == pick(tiles) - 1, pick(last_rows), tm)
    lim = (tiles - 1) * tm + last_rows
    return i32(tile_e), i32(tile_r), i32(tile_rows), i32(n_valid.reshape(1)), i32(lim)


def _top2_moe(l, moe_i, h2, x1, mod, routing, wg, wu, wd, ln_g, ln_b, ts, tiles_per_batch):
    rel, gates, relt, base3, cnt3 = routing
    t = h2.shape[0]
    nt = t // ts
    tm, tf = EXPERT_ROWS, 512
    base = base3[:, 0, :N_EXPERTS].reshape(-1)
    cnt = cnt3[:, 0, :N_EXPERTS].reshape(-1)
    zero_rows = -(-tm // SLOT_CHUNK) * SLOT_CHUNK
    rows_max = t + ROW_ALIGN * nt
    cap = -(-(rows_max + SLOT_CHUNK + zero_rows) // tm) * tm
    rows_all = 2 * t + ROW_ALIGN * N_EXPERTS * nt
    n_steps = -(-rows_all // tm) + N_EXPERTS
    xs = _dispatch(base, cnt, h2, relt, ts, cap, zero_rows)
    tile_e, tile_r, tile_rows, n_valid, lim = _expert_tile_plan(base, cnt, nt, n_steps)
    ys = _expert_ffn(moe_i, tile_e, tile_r, tile_rows, n_valid, xs, wg, wu, wd, tf, n_steps)
    return _combine(l, base, cnt, lim, ys, rel, gates, x1, mod, ln_g, ln_b, ts, tiles_per_batch)


def kernel(x, c, positions, w_in, q_norm_g, kv_norm_g, w_uq, w_ukv, conv_w, mix_norm_g, w_o, w_ada, b_ada,
           ln1_g, ln1_b, ln2_g, ln2_b, ffn_w_gate, ffn_w_up, ffn_w_down, moe_router_w, moe_router_b,
           moe_w_gate, moe_w_up, moe_w_down):
    b, s, _ = x.shape
    n_layers = w_in.shape[0]
    ts = min(512, s)
    t = b * s
    tiles_per_batch = s // ts

    bp = -(-b // 16) * 16
    c_pad = jnp.pad(c, ((0, bp - b), (0, 0)))
    mod = _ada_mod(c_pad, w_ada, b_ada)
    mod = jnp.pad(jnp.transpose(mod[:, :, :b, :], (0, 2, 1, 3)), ((0, 0), (0, 0), (0, 2), (0, 0)))
    rope_tab = _rope_tables(positions, ts)

    rows = lambda v: v.reshape(v.shape[0], 1, -1)
    o3 = Q_LORA + KV_LORA + QK_ROPE
    w_in_p = jnp.concatenate(
        [w_in[:, :, :o3], jnp.zeros((n_layers, D_MODEL, D_IN_PAD - w_in.shape[2]), F32), w_in[:, :, o3:]], axis=2
    ).astype(BF16)
    w_uq_t = jnp.transpose(jnp.pad(
        w_uq.reshape(n_layers, Q_LORA, N_HEADS, QK_NOPE + QK_ROPE),
        ((0, 0), (0, 0), (0, 0), (0, QK_PAD - QK_NOPE - QK_ROPE))
    ).reshape(n_layers, Q_LORA, N_HEADS * QK_PAD), (0, 2, 1)).astype(BF16)
    w_ukv_h = w_ukv.reshape(n_layers, KV_LORA, N_HEADS, QK_NOPE + V_HEAD)
    w_k = w_ukv_h[..., :QK_NOPE].reshape(n_layers, KV_LORA, N_HEADS * QK_NOPE).astype(BF16)
    w_v_t = jnp.transpose(w_ukv_h[..., QK_NOPE:].reshape(n_layers, KV_LORA, N_HEADS * V_HEAD), (0, 2, 1)).astype(BF16)
    w_o_b = w_o.astype(BF16)
    conv_w_p = jnp.pad(conv_w, ((0, 0), (0, 8 - CONV_WIDTH), (0, 0)))
    g_attn, g_conv = rows(mix_norm_g[:, :D_ATTN]), rows(mix_norm_g[:, D_ATTN:])
    ffn_wg, ffn_wu, ffn_wd = ffn_w_gate.astype(BF16), ffn_w_up.astype(BF16), ffn_w_down.astype(BF16)
    rw_hi = moe_router_w.astype(BF16)
    rw_lo = (moe_router_w - rw_hi.astype(F32)).astype(BF16)
    rw = jnp.pad(jnp.concatenate([rw_hi, rw_lo], axis=2), ((0, 0), (0, 0), (0, LANES - 2 * N_EXPERTS)))
    rb = rows(jnp.pad(moe_router_b, ((0, 0), (0, LANES - N_EXPERTS))))

    for l in range(n_layers):
        qt, k, vt, convn = _inproj(l, x, mod, w_in_p, w_uq_t, w_k, w_v_t, rows(q_norm_g), rows(kv_norm_g),
                                   conv_w_p, g_conv, rope_tab, ts)
        attn = _attention(qt, k, vt, ts)
        common = (l, attn, convn, x, mod, w_o_b, g_attn, rows(ln1_g), rows(ln1_b), ts)
        i = l // 2
        if l % 2 == 0:
            x1 = _outproj(*common)
            x = _dense_ffn(l, i, x1.reshape(t, D_MODEL), mod, ffn_wg, ffn_wu, ffn_wd, rows(ln2_g), rows(ln2_b),
                           ts, tiles_per_batch)
        else:
            x1, h2, *routing = _outproj(*common, router=(i, rw, rb))
            x = _top2_moe(l, i, h2.reshape(t, D_MODEL), x1.reshape(t, D_MODEL), mod, routing,
                          moe_w_gate, moe_w_up, moe_w_down, rows(ln2_g), rows(ln2_b), ts, tiles_per_batch)
        x = x.reshape(b, s, D_MODEL)
    return x
```

```python
import functools

import jax
import jax.numpy as jnp
from jax import lax
from jax.experimental import pallas as pl
from jax.experimental.pallas import tpu as pltpu

F32 = jnp.float32
BF16 = jnp.bfloat16

D_MODEL = 1024
N_HEADS = 4
QK_NOPE = 128
QK_ROPE = 64
V_HEAD = 128
Q_LORA = 256
KV_LORA = 128
D_ATTN = N_HEADS * V_HEAD
D_CONV = D_MODEL - D_ATTN
CONV_WIDTH = 3
ROPE_BASE = 10000.0
D_FF = 2816
N_EXPERTS = 8
TOP_K = 2
D_FF_EXPERT = 3584
RMS_EPS = 1e-6
LN_EPS = 1e-5
DEPTH = 2
DEEPNORM_ALPHA = (2 * DEPTH) ** 0.25
SM_SCALE = (QK_NOPE + QK_ROPE) ** -0.5
LOG2E = 1.4426950408889634
Q_SCALE = SM_SCALE * LOG2E

LANES = 128
QK_PAD = 2 * LANES
D_IN_PAD = 2048
ROW_ALIGN = 16
SLOT_CHUNK = 256
EXPERT_ROW_STEP = 384
EXPERT_ROW_SIZES = tuple(EXPERT_ROW_STEP * n for n in range(1, 7))
EXPERT_ROWS = EXPERT_ROW_SIZES[-1]
VMEM_LIMIT = 56 * 1024 * 1024
NEG_BIG = -1e30
NOT_ROUTED = -4096.0


def _cparams(n_axes):
    return pltpu.CompilerParams(dimension_semantics=("arbitrary",) * n_axes, vmem_limit_bytes=VMEM_LIMIT)


def _dot(a, b):
    return jnp.dot(a, b, preferred_element_type=F32)


def _rms(x, g):
    return x * lax.rsqrt(jnp.mean(x * x, axis=-1, keepdims=True) + RMS_EPS) * g


def _layer_norm(x, g, b):
    mu = jnp.mean(x, axis=-1, keepdims=True)
    xc = x - mu
    var = jnp.mean(xc * xc, axis=-1, keepdims=True)
    return xc * lax.rsqrt(var + LN_EPS) * g + b


def _silu(x):
    return x * jax.nn.sigmoid(x)


def _ada_kernel(c_ref, w_ref, b_ref, o_ref):
    ca = _silu(c_ref[...]).astype(BF16)
    o_ref[0, 0] = _dot(ca, w_ref[0].astype(BF16)) + b_ref[0]


def _ada_mod(c_pad, w_ada, b_ada):
    n_layers = w_ada.shape[0]
    bp = c_pad.shape[0]
    return pl.pallas_call(
        _ada_kernel,
        grid=(n_layers, 6),
        in_specs=[
            pl.BlockSpec((bp, D_MODEL), lambda l, j: (0, 0)),
            pl.BlockSpec((1, D_MODEL, D_MODEL), lambda l, j: (l, 0, j)),
            pl.BlockSpec((1, 1, D_MODEL), lambda l, j: (l, 0, j)),
        ],
        out_specs=pl.BlockSpec((1, 1, bp, D_MODEL), lambda l, j: (l, j, 0, 0)),
        out_shape=jax.ShapeDtypeStruct((n_layers, 6, bp, D_MODEL), F32),
        compiler_params=_cparams(2),
        name="ada_mod",
    )(c_pad, w_ada, b_ada.reshape(n_layers, 1, 6 * D_MODEL))


ROPE_HALF = QK_ROPE // 2


def _rope_kernel(pos_ref, f_ref, o_ref):
    ang = pos_ref[0].astype(F32) * f_ref[...]
    o_ref[0, 0:ROPE_HALF, :] = jnp.cos(ang)
    o_ref[0, ROPE_HALF:QK_ROPE, :] = jnp.sin(ang)


def _rope_tables(positions, ts):
    nt = positions.size // ts
    inv_freq = ROPE_BASE ** (-jnp.arange(0, QK_ROPE, 2, dtype=F32) / QK_ROPE)
    return pl.pallas_call(
        _rope_kernel,
        grid=(nt,),
        in_specs=[pl.BlockSpec((1, 1, ts), lambda i: (i, 0, 0)), pl.BlockSpec((ROPE_HALF, 1), lambda i: (0, 0))],
        out_specs=pl.BlockSpec((1, QK_ROPE, ts), lambda i: (i, 0, 0)),
        out_shape=jax.ShapeDtypeStruct((nt, QK_ROPE, ts), F32),
        compiler_params=_cparams(1),
        name="rope_tables",
    )(positions.reshape(nt, 1, ts), inv_freq.reshape(ROPE_HALF, 1))


def _inproj_kernel(x_ref, mod_ref, win_ref, wuqt_ref, wk_ref, wvt_ref, qg_ref, kvg_ref, cw_ref, cg_ref, rope_ref,
                   qt_ref, k_ref, vt_ref, cv_ref, cu_ext, *, ts):
    j = pl.program_id(1)

    @pl.when(j == 0)
    def _():
        cu_ext[0:8, :] = jnp.zeros((8, D_CONV), F32)

    @pl.when(j > 0)
    def _():
        cu_ext[0:8, :] = cu_ext[ts:ts + 8, :]

    x = x_ref[0]
    shift = mod_ref[0, 0:1, :]
    scale = mod_ref[0, 1:2, :]
    h = (x * (1.0 + scale) + shift).astype(BF16)
    z_lat = _dot(h, win_ref[:, 0:512])
    c_q = z_lat[:, 0:Q_LORA]
    c_kv = z_lat[:, Q_LORA:Q_LORA + KV_LORA]
    k_rope = z_lat[:, 384:512]
    z_gate = _dot(h, win_ref[:, 512:1536])
    gate_b = z_gate[:, 0:D_CONV]
    gate_c = z_gate[:, D_CONV:2 * D_CONV]
    u = _dot(h, win_ref[:, 1536:2048])
    cqn = _rms(c_q, qg_ref[...])
    ckvn = _rms(c_kv, kvg_ref[...])
    q_t = _dot(wuqt_ref[...], jnp.transpose(cqn).astype(BF16))
    k_nope = _dot(ckvn.astype(BF16), wk_ref[...])
    v_t = _dot(wvt_ref[...], jnp.transpose(ckvn).astype(BF16))

    cos = rope_ref[0, 0:ROPE_HALF, :]
    sin = rope_ref[0, ROPE_HALF:QK_ROPE, :]

    def rope_t(block):
        x1, x2 = block[0:ROPE_HALF, :], block[ROPE_HALF:QK_ROPE, :]
        return x1 * cos - x2 * sin, x2 * cos + x1 * sin

    k1, k2 = rope_t(jnp.transpose(k_rope)[0:QK_ROPE, :])
    k_rot = jnp.transpose(jnp.concatenate([k1, k2, jnp.zeros((LANES - QK_ROPE, ts), F32)], axis=0)).astype(BF16)
    for hh in range(N_HEADS):
        qo = hh * QK_PAD
        r0 = qo + QK_NOPE
        q1, q2 = rope_t(q_t[r0:r0 + QK_ROPE, :])
        qt_ref[0, hh, 0:QK_NOPE, :] = (q_t[qo:r0, :] * Q_SCALE).astype(BF16)
        qt_ref[0, hh, QK_NOPE:QK_NOPE + ROPE_HALF, :] = (q1 * Q_SCALE).astype(BF16)
        qt_ref[0, hh, QK_NOPE + ROPE_HALF:QK_NOPE + QK_ROPE, :] = (q2 * Q_SCALE).astype(BF16)
        qt_ref[0, hh, QK_NOPE + QK_ROPE:QK_PAD, :] = jnp.zeros((QK_PAD - QK_NOPE - QK_ROPE, ts), BF16)
        k_ref[0, hh, :, 0:LANES] = k_nope[:, hh * QK_NOPE:(hh + 1) * QK_NOPE].astype(BF16)
        k_ref[0, hh, :, LANES:QK_PAD] = k_rot
        vt_ref[0, hh, 0] = v_t[hh * V_HEAD:(hh + 1) * V_HEAD, :].astype(BF16)

    cu_ext[8:ts + 8, :] = gate_c * u
    conv = (cw_ref[2:3, :] * cu_ext[8:ts + 8, :]
            + cw_ref[1:2, :] * cu_ext[pl.ds(7, ts), :]
            + cw_ref[0:1, :] * cu_ext[pl.ds(6, ts), :])
    conv = gate_b * conv
    cv_ref[0] = _rms(conv, cg_ref[...]).astype(BF16)


def _inproj(l, x, mod, w_in_p, w_uq_t, w_k, w_v_t, qg, kvg, conv_w, conv_g, rope_tab, ts):
    b, s, _ = x.shape
    nsb = s // ts
    const = lambda shape: pl.BlockSpec((None,) + shape, lambda bi, j: (l,) + (0,) * len(shape))
    return pl.pallas_call(
        functools.partial(_inproj_kernel, ts=ts),
        grid=(b, nsb),
        in_specs=[
            pl.BlockSpec((1, ts, D_MODEL), lambda bi, j: (bi, j, 0)),
            pl.BlockSpec((None, 1, 8, D_MODEL), lambda bi, j: (l, bi, 0, 0)),
            const((D_MODEL, D_IN_PAD)),
            const((N_HEADS * QK_PAD, Q_LORA)),
            const((KV_LORA, N_HEADS * QK_NOPE)),
            const((N_HEADS * V_HEAD, KV_LORA)),
            const((1, Q_LORA)),
            const((1, KV_LORA)),
            const((8, D_CONV)),
            const((1, D_CONV)),
            pl.BlockSpec((1, QK_ROPE, ts), lambda bi, j: (bi * nsb + j, 0, 0)),
        ],
        out_specs=[
            pl.BlockSpec((1, N_HEADS, QK_PAD, ts), lambda bi, j: (bi, 0, 0, j)),
            pl.BlockSpec((1, N_HEADS, ts, QK_PAD), lambda bi, j: (bi, 0, j, 0)),
            pl.BlockSpec((1, N_HEADS, 1, V_HEAD, ts), lambda bi, j: (bi, 0, j, 0, 0)),
            pl.BlockSpec((1, ts, D_CONV), lambda bi, j: (bi, j, 0)),
        ],
        out_shape=[
            jax.ShapeDtypeStruct((b, N_HEADS, QK_PAD, s), BF16),
            jax.ShapeDtypeStruct((b, N_HEADS, s, QK_PAD), BF16),
            jax.ShapeDtypeStruct((b, N_HEADS, nsb, V_HEAD, ts), BF16),
            jax.ShapeDtypeStruct((b, s, D_CONV), BF16),
        ],
        scratch_shapes=[pltpu.VMEM((ts + 8, D_CONV), F32)],
        compiler_params=_cparams(2),
        name="inproj",
    )(x, mod, w_in_p, w_uq_t, w_k, w_v_t, qg, kvg, conv_w, conv_g, rope_tab)


Q_CHUNK = 256


K_ROWS = 16


def _attn_kernel(qt_ref, k_ref, vt_ref, o_ref, *scratch, tq):
    i = pl.program_id(1)
    n_chunks = tq // Q_CHUNK
    per_chain = 5
    chains = [(hh, c) + tuple(scratch[per_chain * (hh * n_chunks + c):per_chain * (hh * n_chunks + c + 1)])
              for hh in range(N_HEADS) for c in range(n_chunks)]
    for _, _, m_sc, l_sc, acc_sc, _, _ in chains:
        m_sc[...] = jnp.full(m_sc.shape, NEG_BIG, F32)
        l_sc[...] = jnp.zeros(l_sc.shape, F32)
        acc_sc[...] = jnp.zeros(acc_sc.shape, F32)

    def scores(chain, jk, buf):
        hh, c = chain[0], chain[1]
        kj = k_ref[0, hh, pl.ds(pl.multiple_of(jk * tq, tq), tq), :]
        chain[5 + buf][...] = _dot(kj, qt_ref[0, hh, :, c * Q_CHUNK:(c + 1) * Q_CHUNK])

    def softmax_values(chain, jk, buf, diagonal):
        hh, c, m_sc, l_sc, acc_sc = chain[:5]
        s_sc = chain[5 + buf]
        nk = (c + 1) * Q_CHUNK if diagonal else tq
        if diagonal:
            d0 = nk - Q_CHUNK
            key = lax.broadcasted_iota(jnp.int32, (Q_CHUNK, Q_CHUNK), 0)
            qry = lax.broadcasted_iota(jnp.int32, (Q_CHUNK, Q_CHUNK), 1)
            s_sc[d0:nk, :] = jnp.where(key <= qry, s_sc[d0:nk, :], NEG_BIG)
        blk_max = s_sc[0:K_ROWS, :]
        for r in range(K_ROWS, nk, K_ROWS):
            blk_max = jnp.maximum(blk_max, s_sc[r:r + K_ROWS, :])
        m_prev = m_sc[...]
        m_new = jnp.maximum(m_prev, jnp.max(blk_max, axis=0, keepdims=True))
        alpha = jnp.exp2(m_prev - m_new)
        p_sum = jnp.zeros((K_ROWS, Q_CHUNK), F32)
        acc = alpha * acc_sc[...]
        for kt in range(0, nk, Q_CHUNK):
            parts = []
            for r in range(kt, kt + Q_CHUNK, K_ROWS):
                p = jnp.exp2(s_sc[r:r + K_ROWS, :] - m_new)
                p_sum = p_sum + p
                parts.append(p.astype(BF16))
            acc = acc + _dot(vt_ref[0, hh, jk, :, kt:kt + Q_CHUNK], jnp.concatenate(parts, axis=0))
        l_sc[...] = alpha * l_sc[...] + jnp.sum(p_sum, axis=0, keepdims=True)
        acc_sc[...] = acc
        m_sc[...] = m_new

    def stage(cur, cur_buf, diagonal, nxt=None):
        for chain in chains:
            if nxt is not None:
                scores(chain, nxt, 1 - cur_buf)
            softmax_values(chain, cur, cur_buf, diagonal)

    for chain in chains:
        scores(chain, 0, 0)

    def pair(jj, carry):
        stage(2 * jj, 0, False, nxt=2 * jj + 1)
        stage(2 * jj + 1, 1, False, nxt=2 * jj + 2)
        return carry

    lax.fori_loop(0, i // 2, pair, 0)

    @pl.when(lax.rem(i, 2) == 0)
    def _():
        stage(i, 0, True)

    @pl.when(lax.rem(i, 2) == 1)
    def _():
        stage(i - 1, 0, False, nxt=i)
        stage(i, 1, True)

    for hh, c, _, l_sc, acc_sc, _, _ in chains:
        o_ref[0, c * Q_CHUNK:(c + 1) * Q_CHUNK, hh * V_HEAD:(hh + 1) * V_HEAD] = (
            jnp.transpose(acc_sc[...] / l_sc[...]).astype(BF16))


def _attention(qt, k, vt, tq):
    b, nh, s, _ = k.shape
    return pl.pallas_call(
        functools.partial(_attn_kernel, tq=tq),
        grid=(b, s // tq),
        in_specs=[
            pl.BlockSpec((1, nh, QK_PAD, tq), lambda bi, i: (bi, 0, 0, i)),
            pl.BlockSpec((1, nh, s, QK_PAD), lambda bi, i: (bi, 0, 0, 0)),
            pl.BlockSpec((1, nh, s // tq, V_HEAD, tq), lambda bi, i: (bi, 0, 0, 0, 0)),
        ],
        out_specs=pl.BlockSpec((1, tq, D_ATTN), lambda bi, i: (bi, i, 0)),
        out_shape=jax.ShapeDtypeStruct((b, s, D_ATTN), BF16),
        scratch_shapes=[
            pltpu.VMEM((1, Q_CHUNK), F32), pltpu.VMEM((1, Q_CHUNK), F32), pltpu.VMEM((V_HEAD, Q_CHUNK), F32),
            pltpu.VMEM((tq, Q_CHUNK), F32), pltpu.VMEM((tq, Q_CHUNK), F32),
        ] * (nh * (tq // Q_CHUNK)),
        compiler_params=_cparams(2),
        name="attention",
    )(qt, k, vt)


def _outproj_core(attn_ref, cv_ref, x_ref, mod_ref, wo_ref, ga_ref, lng_ref, lnb_ref):
    a = attn_ref[0].astype(F32)
    an = _rms(a, ga_ref[...]).astype(BF16)
    y = _dot(an, wo_ref[0:D_ATTN, :]) + _dot(cv_ref[0], wo_ref[D_ATTN:D_MODEL, :])
    gate1 = mod_ref[0, 2:3, :]
    x1 = _layer_norm(DEEPNORM_ALPHA * x_ref[0] + gate1 * y, lng_ref[...], lnb_ref[...])
    h2 = x1 * (1.0 + mod_ref[0, 4:5, :]) + mod_ref[0, 3:4, :]
    return x1, h2


def _outproj_kernel(attn_ref, cv_ref, x_ref, mod_ref, wo_ref, ga_ref, lng_ref, lnb_ref, x1_ref):
    x1, _ = _outproj_core(attn_ref, cv_ref, x_ref, mod_ref, wo_ref, ga_ref, lng_ref, lnb_ref)
    x1_ref[0] = x1


def _outproj_router_kernel(attn_ref, cv_ref, x_ref, mod_ref, wo_ref, ga_ref, lng_ref, lnb_ref, rw_ref, rb_ref,
                           x1_ref, h2_ref, rel_ref, gate_ref, relt_ref, base_ref, cnt_ref, run_sc, *, ts):
    first = jnp.logical_and(pl.program_id(0) == 0, pl.program_id(1) == 0)

    @pl.when(first)
    def _():
        run_sc[...] = jnp.zeros(run_sc.shape, F32)

    x1, h2 = _outproj_core(attn_ref, cv_ref, x_ref, mod_ref, wo_ref, ga_ref, lng_ref, lnb_ref)
    x1_ref[0] = x1
    h2_hi = h2.astype(BF16)
    h2_ref[0] = h2_hi

    lane = lax.broadcasted_iota(jnp.int32, (ts, LANES), 1)
    h2_lo = (h2 - h2_hi.astype(F32)).astype(BF16)
    prod = _dot(h2_hi, rw_ref[...]) + _dot(h2_lo, rw_ref[...])
    logits = prod + pltpu.roll(prod, LANES - N_EXPERTS, 1) + rb_ref[...]
    logits = jnp.where(lane < N_EXPERTS, logits, NEG_BIG)
    v1 = jnp.max(logits, axis=-1, keepdims=True)
    i1 = jnp.min(jnp.where(logits == v1, lane, LANES), axis=-1, keepdims=True)
    rest = jnp.where(lane == i1, NEG_BIG, logits)
    v2 = jnp.max(rest, axis=-1, keepdims=True)
    i2 = jnp.min(jnp.where(rest == v2, lane, LANES), axis=-1, keepdims=True)
    e21 = jnp.exp(v2 - v1)
    g1 = 1.0 / (1.0 + e21)
    g2 = e21 / (1.0 + e21)
    is1 = lane == i1
    is2 = lane == i2
    sel = jnp.where(jnp.logical_or(is1, is2), 1.0, 0.0)
    gate_ref[...] = jnp.where(is1, g1, 0.0) + jnp.where(is2, g2, 0.0)

    r_i = lax.broadcasted_iota(jnp.int32, (ts, ts), 0)
    c_i = lax.broadcasted_iota(jnp.int32, (ts, ts), 1)
    tri = jnp.where(c_i < r_i, 1.0, 0.0).astype(BF16)
    rank = _dot(tri, sel.astype(BF16))
    rel = jnp.where(sel > 0.0, rank, NOT_ROUTED)
    rel_ref[...] = rel
    relt_ref[0] = jnp.transpose(rel)[0:N_EXPERTS, :]

    n_tile = jnp.sum(sel, axis=0, keepdims=True)
    base_ref[0] = run_sc[...].astype(jnp.int32)
    cnt_ref[0] = n_tile.astype(jnp.int32)
    run_sc[...] = run_sc[...] + n_tile


def _outproj(l, attn, convn, x, mod, w_o, g_attn, ln_g, ln_b, ts, router=None):
    b, s, _ = x.shape
    nsb = s // ts
    const = lambda shape, idx=l: pl.BlockSpec((None,) + shape, lambda bi, j: (idx,) + (0,) * len(shape))
    tile3 = lambda w: pl.BlockSpec((1, ts, w), lambda bi, j: (bi, j, 0))
    in_specs = [
        tile3(D_ATTN), tile3(D_CONV), tile3(D_MODEL),
        pl.BlockSpec((None, 1, 8, D_MODEL), lambda bi, j: (l, bi, 0, 0)),
        const((D_MODEL, D_MODEL)), const((1, D_ATTN)), const((1, D_MODEL)), const((1, D_MODEL)),
    ]
    out_specs = [tile3(D_MODEL), tile3(D_MODEL)]
    out_shape = [jax.ShapeDtypeStruct((b, s, D_MODEL), F32), jax.ShapeDtypeStruct((b, s, D_MODEL), BF16)]
    args = [attn, convn, x, mod, w_o, g_attn, ln_g, ln_b]
    if router is None:
        return pl.pallas_call(
            _outproj_kernel, grid=(b, nsb), in_specs=in_specs, out_specs=out_specs[0], out_shape=out_shape[0],
            compiler_params=_cparams(2), name="outproj",
        )(*args)
    moe_i, rw, rb = router
    nt = b * nsb
    flat = lambda w: pl.BlockSpec((ts, w), lambda bi, j: (bi * nsb + j, 0))
    per_tile = lambda r, w: pl.BlockSpec((1, r, w), lambda bi, j: (bi * nsb + j, 0, 0))
    in_specs += [const((D_MODEL, LANES), moe_i), const((1, LANES), moe_i)]
    out_specs += [flat(LANES), flat(LANES), per_tile(N_EXPERTS, ts), per_tile(1, LANES), per_tile(1, LANES)]
    out_shape += [
        jax.ShapeDtypeStruct((nt * ts, LANES), F32), jax.ShapeDtypeStruct((nt * ts, LANES), F32),
        jax.ShapeDtypeStruct((nt, N_EXPERTS, ts), F32),
        jax.ShapeDtypeStruct((nt, 1, LANES), jnp.int32), jax.ShapeDtypeStruct((nt, 1, LANES), jnp.int32),
    ]
    return pl.pallas_call(
        functools.partial(_outproj_router_kernel, ts=ts), grid=(b, nsb), in_specs=in_specs, out_specs=out_specs,
        out_shape=out_shape, scratch_shapes=[pltpu.VMEM((1, LANES), F32)],
        compiler_params=_cparams(2), name="outproj_router",
    )(*args, rw, rb)


FF_CHUNK = 256


def _swiglu(x, wg, wu, wd, width):
    out = None
    pending = None
    for c in range(0, width, FF_CHUNK):
        w = min(FF_CHUNK, width - c)
        g = _dot(x, wg(c, w))
        u = _dot(x, wu(c, w))
        if pending is not None:
            d = _dot(pending[0], wd(*pending[1]))
            out = d if out is None else out + d
        pending = ((_silu(g) * u).astype(BF16), (c, w))
    d = _dot(pending[0], wd(*pending[1]))
    return d if out is None else out + d


def _ffn_kernel(x1_ref, mod_ref, wg_ref, wu_ref, wd_ref, lng_ref, lnb_ref, o_ref):
    x1 = x1_ref[...]
    h = (x1 * (1.0 + mod_ref[0, 4:5, :]) + mod_ref[0, 3:4, :]).astype(BF16)
    f = _swiglu(h, lambda c, w: wg_ref[:, c:c + w], lambda c, w: wu_ref[:, c:c + w],
                lambda c, w: wd_ref[c:c + w, :], D_FF)
    gate2 = mod_ref[0, 5:6, :]
    o_ref[...] = _layer_norm(DEEPNORM_ALPHA * x1 + gate2 * f, lng_ref[...], lnb_ref[...])


def _dense_ffn(l, dense_i, x1, mod, wg, wu, wd, ln_g, ln_b, tm, tiles_per_batch):
    t = x1.shape[0]
    const = lambda shape, idx: pl.BlockSpec((None,) + shape, lambda i: (idx,) + (0,) * len(shape),
                                            pipeline_mode=pl.Buffered(1))
    return pl.pallas_call(
        _ffn_kernel,
        grid=(t // tm,),
        in_specs=[
            pl.BlockSpec((tm, D_MODEL), lambda i: (i, 0)),
            pl.BlockSpec((None, 1, 8, D_MODEL), lambda i: (l, i // tiles_per_batch, 0, 0)),
            const((D_MODEL, D_FF), dense_i), const((D_MODEL, D_FF), dense_i), const((D_FF, D_MODEL), dense_i),
            const((1, D_MODEL), l), const((1, D_MODEL), l),
        ],
        out_specs=pl.BlockSpec((tm, D_MODEL), lambda i: (i, 0)),
        out_shape=jax.ShapeDtypeStruct((t, D_MODEL), F32),
        compiler_params=_cparams(1),
        name="dense_ffn",
    )(x1, mod, wg, wu, wd, ln_g, ln_b)


def _group_copy(src, dst, sem):
    return pltpu.make_async_copy(src, dst, sem)


def _floor_align(v):
    return pl.multiple_of(lax.shift_left(lax.shift_right_logical(v, 4), 4), ROW_ALIGN)


def _dispatch_kernel(base_ref, cnt_ref, h_ref, relt_ref, xs_ref, stage, carry, zbuf, sems, xsem, zsem, *,
                     ts, zero_rows):
    i = pl.program_id(0)
    last = pl.num_programs(0) - 1
    par = lax.rem(i, 2)
    n_chunks = ts // SLOT_CHUNK
    h = h_ref[...]
    slot = lax.broadcasted_iota(jnp.int32, (SLOT_CHUNK, ts), 0).astype(F32)

    def group(step, e):
        a = base_ref[step * N_EXPERTS + e]
        a16 = _floor_align(a)
        return a, a16, a - a16

    def onehot(e, c):
        _, _, off = group(i, e)
        row = relt_ref[0, e:e + 1, :] + off.astype(F32)
        return jnp.where(row == slot + float(c * SLOT_CHUNK), 1.0, 0.0).astype(BF16)

    def first_copy(step, e):
        _, a16, _ = group(step, e)
        p = lax.rem(step, 2)
        return _group_copy(stage.at[p, e, 0:SLOT_CHUNK], xs_ref.at[e, pl.ds(a16, SLOT_CHUNK), :], sems.at[p, e])

    @pl.when(i == 0)
    def _():
        carry[...] = jnp.zeros(carry.shape, BF16)
        stage[:, :, n_chunks * SLOT_CHUNK:, :] = jnp.zeros((2, N_EXPERTS, ROW_ALIGN, D_MODEL), BF16)

    for e in range(N_EXPERTS):
        rows = _dot(onehot(e, 0), h)
        stage[par, e, 0:ROW_ALIGN] = (rows[0:ROW_ALIGN] + carry[e].astype(F32)).astype(BF16)
        stage[par, e, ROW_ALIGN:SLOT_CHUNK] = rows[ROW_ALIGN:SLOT_CHUNK].astype(BF16)
        stage[par, e, SLOT_CHUNK:SLOT_CHUNK + ROW_ALIGN] = jnp.zeros((ROW_ALIGN, D_MODEL), BF16)

    @pl.when(i > 0)
    def _():
        for e in range(N_EXPERTS):
            first_copy(i - 1, e).wait()

    for e in range(N_EXPERTS):
        first_copy(i, e).start()

    for e in range(N_EXPERTS):
        _, a16, off = group(i, e)
        n = cnt_ref[i * N_EXPERTS + e]
        for c in range(1, n_chunks):
            @pl.when(off + n > c * SLOT_CHUNK)
            def _(e=e, c=c, a16=a16):
                lo = c * SLOT_CHUNK
                stage[par, e, lo:lo + SLOT_CHUNK] = _dot(onehot(e, c), h).astype(BF16)
                if c + 1 < n_chunks:
                    stage[par, e, lo + SLOT_CHUNK:lo + SLOT_CHUNK + ROW_ALIGN] = jnp.zeros((ROW_ALIGN, D_MODEL), BF16)
                cp = _group_copy(stage.at[par, e, lo:lo + SLOT_CHUNK],
                                 xs_ref.at[e, pl.ds(a16 + lo, SLOT_CHUNK), :], xsem)
                cp.start()
                cp.wait()

    for e in range(N_EXPERTS):
        a, a16, _ = group(i, e)
        nxt16 = _floor_align(a + cnt_ref[i * N_EXPERTS + e])
        carry[e] = stage[par, e, pl.ds(pl.multiple_of(nxt16 - a16, ROW_ALIGN), ROW_ALIGN), :]

    @pl.when(i == last)
    def _():
        for e in range(N_EXPERTS):
            first_copy(i, e).wait()
        zbuf[...] = jnp.zeros(zbuf.shape, BF16)
        for e in range(N_EXPERTS):
            n = cnt_ref[i * N_EXPERTS + e]
            end = _floor_align(base_ref[i * N_EXPERTS + e] + n + (ROW_ALIGN - 1))
            copies = [
                _group_copy(zbuf, xs_ref.at[e, pl.ds(end + r * SLOT_CHUNK, SLOT_CHUNK), :], zsem.at[e])
                for r in range(zero_rows // SLOT_CHUNK)
            ]
            for cp in copies:
                cp.start()
            for cp in copies:
                cp.wait()


def _dispatch(base, cnt, h2, relt, ts, cap, zero_rows):
    t = h2.shape[0]
    nt = t // ts
    n_chunks = ts // SLOT_CHUNK
    grid_spec = pltpu.PrefetchScalarGridSpec(
        num_scalar_prefetch=2,
        grid=(nt,),
        in_specs=[
            pl.BlockSpec((ts, D_MODEL), lambda i, b_, c_: (i, 0)),
            pl.BlockSpec((1, N_EXPERTS, ts), lambda i, b_, c_: (i, 0, 0)),
        ],
        out_specs=pl.BlockSpec(memory_space=pl.ANY),
        scratch_shapes=[
            pltpu.VMEM((2, N_EXPERTS, n_chunks * SLOT_CHUNK + ROW_ALIGN, D_MODEL), BF16),
            pltpu.VMEM((N_EXPERTS, ROW_ALIGN, D_MODEL), BF16),
            pltpu.VMEM((SLOT_CHUNK, D_MODEL), BF16),
            pltpu.SemaphoreType.DMA((2, N_EXPERTS)),
            pltpu.SemaphoreType.DMA(()),
            pltpu.SemaphoreType.DMA((N_EXPERTS,)),
        ],
    )
    return pl.pallas_call(
        functools.partial(_dispatch_kernel, ts=ts, zero_rows=zero_rows),
        grid_spec=grid_spec,
        out_shape=jax.ShapeDtypeStruct((N_EXPERTS, cap, D_MODEL), BF16),
        compiler_params=_cparams(1),
        name="moe_dispatch",
    )(base, cnt, h2, relt)


def _expert_kernel(te_ref, tr_ref, rows_ref, nv_ref, x_ref, wg_ref, wu_ref, wd_ref, o_ref, acc_sc):
    g_i = pl.program_id(0)
    k = pl.program_id(1)

    valid = g_i < nv_ref[0]

    @pl.when(jnp.logical_and(valid, k == 0))
    def _():
        acc_sc[...] = jnp.zeros(acc_sc.shape, F32)

    for m in EXPERT_ROW_SIZES:
        @pl.when(jnp.logical_and(valid, rows_ref[g_i] == m))
        def _(m=m):
            part = _swiglu(x_ref[0, 0:m, :], lambda c, w: wg_ref[0, :, c:c + w].astype(BF16),
                           lambda c, w: wu_ref[0, :, c:c + w].astype(BF16),
                           lambda c, w: wd_ref[0, c:c + w, :].astype(BF16), wg_ref.shape[2])
            total = acc_sc[0:m, :] + part
            acc_sc[0:m, :] = total
            o_ref[0, 0:m, :] = total.astype(BF16)


def _expert_ffn(moe_i, tile_e, tile_r, tile_rows, n_valid, xs, wg, wu, wd, tf, n_steps):
    tm = EXPERT_ROWS
    nk = D_FF_EXPERT // tf
    cap = xs.shape[1]

    def k_eff(g, k, nv):
        return jnp.where(g < nv[0], k, nk - 1)

    grid_spec = pltpu.PrefetchScalarGridSpec(
        num_scalar_prefetch=4,
        grid=(n_steps, nk),
        in_specs=[
            pl.BlockSpec((1, tm, D_MODEL), lambda g, k, te, tr, rw, nv: (te[g], tr[g], 0)),
            pl.BlockSpec((None, 1, D_MODEL, tf), lambda g, k, te, tr, rw, nv: (moe_i, te[g], 0, k_eff(g, k, nv))),
            pl.BlockSpec((None, 1, D_MODEL, tf), lambda g, k, te, tr, rw, nv: (moe_i, te[g], 0, k_eff(g, k, nv))),
            pl.BlockSpec((None, 1, tf, D_MODEL), lambda g, k, te, tr, rw, nv: (moe_i, te[g], k_eff(g, k, nv), 0)),
        ],
        out_specs=pl.BlockSpec((1, tm, D_MODEL), lambda g, k, te, tr, rw, nv: (te[g], tr[g], 0)),
        scratch_shapes=[pltpu.VMEM((tm, D_MODEL), F32)],
    )
    return pl.pallas_call(
        _expert_kernel,
        grid_spec=grid_spec,
        out_shape=jax.ShapeDtypeStruct((N_EXPERTS, cap, D_MODEL), BF16),
        compiler_params=_cparams(2),
        name="moe_experts",
    )(tile_e, tile_r, tile_rows, n_valid, xs, wg, wu, wd)


def _combine_kernel(base_ref, cnt_ref, lim_ref, ys_ref, rel_ref, gate_ref, x1_ref, mod_ref, lng_ref, lnb_ref, o_ref,
                    slab, xslab, acc_sc, sems, xsem, *, ts):
    i = pl.program_id(0)
    last = pl.num_programs(0) - 1
    par = lax.rem(i, 2)
    n_chunks = ts // SLOT_CHUNK
    slot = lax.broadcasted_iota(jnp.int32, (ts, SLOT_CHUNK), 1).astype(F32)

    def slab_start(step, e, c):
        want = _floor_align(base_ref[step * N_EXPERTS + e]) + c * SLOT_CHUNK
        start = jnp.maximum(jnp.minimum(want, lim_ref[e] - SLOT_CHUNK), 0)
        return pl.multiple_of(start, ROW_ALIGN), want - start

    def first_copy(step, e):
        start, _ = slab_start(step, e, 0)
        p = lax.rem(step, 2)
        return _group_copy(ys_ref.at[e, pl.ds(start, SLOT_CHUNK), :], slab.at[p, e], sems.at[p, e])

    def contribution(e, c, rows):
        a = base_ref[i * N_EXPERTS + e]
        _, moved = slab_start(i, e, c)
        pos = rel_ref[:, e:e + 1] + (a - _floor_align(a)).astype(F32)
        in_chunk = jnp.where(pos >= float(c * SLOT_CHUNK), 1.0, 0.0)
        onehot = jnp.where(pos + (moved - c * SLOT_CHUNK).astype(F32) == slot, in_chunk, 0.0).astype(BF16)
        return _dot(onehot, rows) * gate_ref[:, e:e + 1]

    @pl.when(i == 0)
    def _():
        for e in range(N_EXPERTS):
            first_copy(i, e).start()

    @pl.when(i < last)
    def _():
        for e in range(N_EXPERTS):
            first_copy(i + 1, e).start()

    for e in range(N_EXPERTS):
        first_copy(i, e).wait()
    f = contribution(0, 0, slab[par, 0])
    for e in range(1, N_EXPERTS):
        f = f + contribution(e, 0, slab[par, e])
    acc_sc[...] = f

    for e in range(N_EXPERTS):
        a = base_ref[i * N_EXPERTS + e]
        rows_in_block = a - _floor_align(a) + cnt_ref[i * N_EXPERTS + e]
        for c in range(1, n_chunks):
            @pl.when(rows_in_block > c * SLOT_CHUNK)
            def _(e=e, c=c):
                start, _ = slab_start(i, e, c)
                cp = _group_copy(ys_ref.at[e, pl.ds(start, SLOT_CHUNK), :], xslab, xsem)
                cp.start()
                cp.wait()
                acc_sc[...] += contribution(e, c, xslab[...])

    gate2 = mod_ref[0, 5:6, :]
    o_ref[...] = _layer_norm(DEEPNORM_ALPHA * x1_ref[...] + gate2 * acc_sc[...], lng_ref[...], lnb_ref[...])


def _combine(l, base, cnt, lim, ys, rel, gates, x1, mod, ln_g, ln_b, ts, tiles_per_batch):
    t = x1.shape[0]
    grid_spec = pltpu.PrefetchScalarGridSpec(
        num_scalar_prefetch=3,
        grid=(t // ts,),
        in_specs=[
            pl.BlockSpec(memory_space=pl.ANY),
            pl.BlockSpec((ts, LANES), lambda i, *_: (i, 0)),
            pl.BlockSpec((ts, LANES), lambda i, *_: (i, 0)),
            pl.BlockSpec((ts, D_MODEL), lambda i, *_: (i, 0)),
            pl.BlockSpec((None, 1, 8, D_MODEL), lambda i, *_: (l, i // tiles_per_batch, 0, 0)),
            pl.BlockSpec((None, 1, D_MODEL), lambda i, *_: (l, 0, 0)),
            pl.BlockSpec((None, 1, D_MODEL), lambda i, *_: (l, 0, 0)),
        ],
        out_specs=pl.BlockSpec((ts, D_MODEL), lambda i, *_: (i, 0)),
        scratch_shapes=[
            pltpu.VMEM((2, N_EXPERTS, SLOT_CHUNK, D_MODEL), BF16),
            pltpu.VMEM((SLOT_CHUNK, D_MODEL), BF16),
            pltpu.VMEM((ts, D_MODEL), F32),
            pltpu.SemaphoreType.DMA((2, N_EXPERTS)),
            pltpu.SemaphoreType.DMA(()),
        ],
    )
    return pl.pallas_call(
        functools.partial(_combine_kernel, ts=ts),
        grid_spec=grid_spec,
        out_shape=jax.ShapeDtypeStruct((t, D_MODEL), F32),
        compiler_params=_cparams(1),
        name="moe_combine",
    )(base, cnt, lim, ys, rel, gates, x1, mod, ln_g, ln_b)


def _expert_tile_plan(base, cnt, nt, n_steps):
    tm = EXPERT_ROWS
    i32 = lambda v: v.astype(jnp.int32)
    last_base = base.reshape(nt, N_EXPERTS)[-1]
    last_cnt = cnt.reshape(nt, N_EXPERTS)[-1]
    total = last_base + last_cnt
    tiles = jnp.maximum((total + tm - 1) // tm, 1)
    rest = total - (tiles - 1) * tm
    last_rows = jnp.full_like(rest, EXPERT_ROW_SIZES[0])
    for small, big in zip(EXPERT_ROW_SIZES[:-1], EXPERT_ROW_SIZES[1:]):
        last_rows = last_rows + jnp.where(rest > small, big - small, 0)
    ends = jnp.cumsum(tiles)
    n_valid = ends[-1]
    g = jnp.minimum(jnp.arange(n_steps, dtype=jnp.int32), n_valid - 1)
    tile_e = jnp.sum((g[:, None] >= ends[None, :]).astype(jnp.int32), axis=1)
    onehot = (tile_e[:, None] == jnp.arange(N_EXPERTS, dtype=jnp.int32)[None, :]).astype(jnp.int32)
    pick = lambda v: jnp.sum(onehot * v[None, :], axis=1)
    tile_r = g - pick(ends - tiles)
    tile_rows = jnp.where(tile_r == pick(tiles) - 1, pick(last_rows), tm)
    lim = (tiles - 1) * tm + last_rows
    return i32(tile_e), i32(tile_r), i32(tile_rows), i32(n_valid.reshape(1)), i32(lim)


def _top2_moe(l, moe_i, h2, x1, mod, routing, wg, wu, wd, ln_g, ln_b, ts, tiles_per_batch):
    rel, gates, relt, base3, cnt3 = routing
    t = h2.shape[0]
    nt = t // ts
    tm, tf = EXPERT_ROWS, 512
    base = base3[:, 0, :N_EXPERTS].reshape(-1)
    cnt = cnt3[:, 0, :N_EXPERTS].reshape(-1)
    zero_rows = -(-tm // SLOT_CHUNK) * SLOT_CHUNK
    cap = -(-(t + ROW_ALIGN + ts + zero_rows) // tm) * tm
    n_steps = -(-TOP_K * t // tm) + N_EXPERTS
    xs = _dispatch(base, cnt, h2, relt, ts, cap, zero_rows)
    tile_e, tile_r, tile_rows, n_valid, lim = _expert_tile_plan(base, cnt, nt, n_steps)
    ys = _expert_ffn(moe_i, tile_e, tile_r, tile_rows, n_valid, xs, wg, wu, wd, tf, n_steps)
    return _combine(l, base, cnt, lim, ys, rel, gates, x1, mod, ln_g, ln_b, ts, tiles_per_batch)


def kernel(x, c, positions, w_in, q_norm_g, kv_norm_g, w_uq, w_ukv, conv_w, mix_norm_g, w_o, w_ada, b_ada,
           ln1_g, ln1_b, ln2_g, ln2_b, ffn_w_gate, ffn_w_up, ffn_w_down, moe_router_w, moe_router_b,
           moe_w_gate, moe_w_up, moe_w_down):
    b, s, _ = x.shape
    n_layers = w_in.shape[0]
    ts = min(512, s)
    t = b * s
    tiles_per_batch = s // ts

    bp = -(-b // 16) * 16
    c_pad = jnp.pad(c, ((0, bp - b), (0, 0)))
    mod = _ada_mod(c_pad, w_ada, b_ada)
    mod = jnp.pad(jnp.transpose(mod[:, :, :b, :], (0, 2, 1, 3)), ((0, 0), (0, 0), (0, 2), (0, 0)))
    rope_tab = _rope_tables(positions, ts)

    rows = lambda v: v.reshape(v.shape[0], 1, -1)
    o3 = Q_LORA + KV_LORA + QK_ROPE
    w_in_p = jnp.concatenate(
        [w_in[:, :, :o3], jnp.zeros((n_layers, D_MODEL, D_IN_PAD - w_in.shape[2]), F32), w_in[:, :, o3:]], axis=2
    ).astype(BF16)
    w_uq_t = jnp.transpose(jnp.pad(
        w_uq.reshape(n_layers, Q_LORA, N_HEADS, QK_NOPE + QK_ROPE),
        ((0, 0), (0, 0), (0, 0), (0, QK_PAD - QK_NOPE - QK_ROPE))
    ).reshape(n_layers, Q_LORA, N_HEADS * QK_PAD), (0, 2, 1)).astype(BF16)
    w_ukv_h = w_ukv.reshape(n_layers, KV_LORA, N_HEADS, QK_NOPE + V_HEAD)
    w_k = w_ukv_h[..., :QK_NOPE].reshape(n_layers, KV_LORA, N_HEADS * QK_NOPE).astype(BF16)
    w_v_t = jnp.transpose(w_ukv_h[..., QK_NOPE:].reshape(n_layers, KV_LORA, N_HEADS * V_HEAD), (0, 2, 1)).astype(BF16)
    w_o_b = w_o.astype(BF16)
    conv_w_p = jnp.pad(conv_w, ((0, 0), (0, 8 - CONV_WIDTH), (0, 0)))
    g_attn, g_conv = rows(mix_norm_g[:, :D_ATTN]), rows(mix_norm_g[:, D_ATTN:])
    ffn_wg, ffn_wu, ffn_wd = ffn_w_gate.astype(BF16), ffn_w_up.astype(BF16), ffn_w_down.astype(BF16)
    rw_hi = moe_router_w.astype(BF16)
    rw_lo = (moe_router_w - rw_hi.astype(F32)).astype(BF16)
    rw = jnp.pad(jnp.concatenate([rw_hi, rw_lo], axis=2), ((0, 0), (0, 0), (0, LANES - 2 * N_EXPERTS)))
    rb = rows(jnp.pad(moe_router_b, ((0, 0), (0, LANES - N_EXPERTS))))

    for l in range(n_layers):
        qt, k, vt, convn = _inproj(l, x, mod, w_in_p, w_uq_t, w_k, w_v_t, rows(q_norm_g), rows(kv_norm_g),
                                   conv_w_p, g_conv, rope_tab, ts)
        attn = _attention(qt, k, vt, ts)
        common = (l, attn, convn, x, mod, w_o_b, g_attn, rows(ln1_g), rows(ln1_b), ts)
        i = l // 2
        if l % 2 == 0:
            x1 = _outproj(*common)
            x = _dense_ffn(l, i, x1.reshape(t, D_MODEL), mod, ffn_wg, ffn_wu, ffn_wd, rows(ln2_g), rows(ln2_b),
                           ts, tiles_per_batch)
        else:
            x1, h2, *routing = _outproj(*common, router=(i, rw, rb))
            x = _top2_moe(l, i, h2.reshape(t, D_MODEL), x1.reshape(t, D_MODEL), mod, routing,
                          moe_w_gate, moe_w_up, moe_w_down, rows(ln2_g), rows(ln2_b), ts, tiles_per_batch)
        x = x.reshape(b, s, D_MODEL)
    return x
```

```python
import functools

import jax
import jax.numpy as jnp
from jax import lax
from jax.experimental import pallas as pl
from jax.experimental.pallas import tpu as pltpu

F32 = jnp.float32
BF16 = jnp.bfloat16

D_MODEL = 1024
N_HEADS = 4
QK_NOPE = 128
QK_ROPE = 64
V_HEAD = 128
Q_LORA = 256
KV_LORA = 128
D_ATTN = N_HEADS * V_HEAD
D_CONV = D_MODEL - D_ATTN
CONV_WIDTH = 3
ROPE_BASE = 10000.0
D_FF = 2816
N_EXPERTS = 8
TOP_K = 2
D_FF_EXPERT = 3584
RMS_EPS = 1e-6
LN_EPS = 1e-5
DEPTH = 2
DEEPNORM_ALPHA = (2 * DEPTH) ** 0.25
SM_SCALE = (QK_NOPE + QK_ROPE) ** -0.5
LOG2E = 1.4426950408889634
Q_SCALE = SM_SCALE * LOG2E

LANES = 128
QK_PAD = 2 * LANES
D_IN_PAD = 2048
ROW_ALIGN = 16
SLOT_CHUNK = 256
EXPERT_ROW_STEP = 384
EXPERT_ROW_SIZES = tuple(EXPERT_ROW_STEP * n for n in range(1, 7))
EXPERT_ROWS = EXPERT_ROW_SIZES[-1]
VMEM_LIMIT = 56 * 1024 * 1024
NEG_BIG = -1e30
NOT_ROUTED = -4096.0


def _cparams(n_axes):
    return pltpu.CompilerParams(dimension_semantics=("arbitrary",) * n_axes, vmem_limit_bytes=VMEM_LIMIT)


def _dot(a, b):
    return jnp.dot(a, b, preferred_element_type=F32)


def _rms(x, g):
    return x * lax.rsqrt(jnp.mean(x * x, axis=-1, keepdims=True) + RMS_EPS) * g


def _layer_norm(x, g, b):
    mu = jnp.mean(x, axis=-1, keepdims=True)
    xc = x - mu
    var = jnp.mean(xc * xc, axis=-1, keepdims=True)
    return xc * lax.rsqrt(var + LN_EPS) * g + b


def _silu(x):
    return x * jax.nn.sigmoid(x)


def _ada_kernel(c_ref, w_ref, b_ref, o_ref):
    ca = _silu(c_ref[...]).astype(BF16)
    o_ref[0, 0] = _dot(ca, w_ref[0].astype(BF16)) + b_ref[0]


def _ada_mod(c_pad, w_ada, b_ada):
    n_layers = w_ada.shape[0]
    bp = c_pad.shape[0]
    return pl.pallas_call(
        _ada_kernel,
        grid=(n_layers, 6),
        in_specs=[
            pl.BlockSpec((bp, D_MODEL), lambda l, j: (0, 0)),
            pl.BlockSpec((1, D_MODEL, D_MODEL), lambda l, j: (l, 0, j)),
            pl.BlockSpec((1, 1, D_MODEL), lambda l, j: (l, 0, j)),
        ],
        out_specs=pl.BlockSpec((1, 1, bp, D_MODEL), lambda l, j: (l, j, 0, 0)),
        out_shape=jax.ShapeDtypeStruct((n_layers, 6, bp, D_MODEL), F32),
        compiler_params=_cparams(2),
        name="ada_mod",
    )(c_pad, w_ada, b_ada.reshape(n_layers, 1, 6 * D_MODEL))


ROPE_HALF = QK_ROPE // 2


def _rope_kernel(pos_ref, f_ref, o_ref):
    ang = pos_ref[0].astype(F32) * f_ref[...]
    o_ref[0, 0:ROPE_HALF, :] = jnp.cos(ang)
    o_ref[0, ROPE_HALF:QK_ROPE, :] = jnp.sin(ang)


def _rope_tables(positions, ts):
    nt = positions.size // ts
    inv_freq = ROPE_BASE ** (-jnp.arange(0, QK_ROPE, 2, dtype=F32) / QK_ROPE)
    return pl.pallas_call(
        _rope_kernel,
        grid=(nt,),
        in_specs=[pl.BlockSpec((1, 1, ts), lambda i: (i, 0, 0)), pl.BlockSpec((ROPE_HALF, 1), lambda i: (0, 0))],
        out_specs=pl.BlockSpec((1, QK_ROPE, ts), lambda i: (i, 0, 0)),
        out_shape=jax.ShapeDtypeStruct((nt, QK_ROPE, ts), F32),
        compiler_params=_cparams(1),
        name="rope_tables",
    )(positions.reshape(nt, 1, ts), inv_freq.reshape(ROPE_HALF, 1))


def _inproj_kernel(x_ref, mod_ref, win_ref, wuqt_ref, wk_ref, wvt_ref, qg_ref, kvg_ref, cw_ref, cg_ref, rope_ref,
                   qt_ref, k_ref, vt_ref, cv_ref, cu_ext, *, ts):
    j = pl.program_id(1)

    @pl.when(j == 0)
    def _():
        cu_ext[0:8, :] = jnp.zeros((8, D_CONV), F32)

    @pl.when(j > 0)
    def _():
        cu_ext[0:8, :] = cu_ext[ts:ts + 8, :]

    x = x_ref[0]
    shift = mod_ref[0, 0:1, :]
    scale = mod_ref[0, 1:2, :]
    h = (x * (1.0 + scale) + shift).astype(BF16)
    z_lat = _dot(h, win_ref[:, 0:512])
    c_q = z_lat[:, 0:Q_LORA]
    c_kv = z_lat[:, Q_LORA:Q_LORA + KV_LORA]
    k_rope = z_lat[:, 384:512]
    z_gate = _dot(h, win_ref[:, 512:1536])
    gate_b = z_gate[:, 0:D_CONV]
    gate_c = z_gate[:, D_CONV:2 * D_CONV]
    u = _dot(h, win_ref[:, 1536:2048])
    cqn = _rms(c_q, qg_ref[...])
    ckvn = _rms(c_kv, kvg_ref[...])
    q_t = _dot(wuqt_ref[...], jnp.transpose(cqn).astype(BF16))
    k_nope = _dot(ckvn.astype(BF16), wk_ref[...])
    v_t = _dot(wvt_ref[...], jnp.transpose(ckvn).astype(BF16))

    cos = rope_ref[0, 0:ROPE_HALF, :]
    sin = rope_ref[0, ROPE_HALF:QK_ROPE, :]

    def rope_t(block):
        x1, x2 = block[0:ROPE_HALF, :], block[ROPE_HALF:QK_ROPE, :]
        return x1 * cos - x2 * sin, x2 * cos + x1 * sin

    k1, k2 = rope_t(jnp.transpose(k_rope)[0:QK_ROPE, :])
    k_rot = jnp.transpose(jnp.concatenate([k1, k2, jnp.zeros((LANES - QK_ROPE, ts), F32)], axis=0)).astype(BF16)
    for hh in range(N_HEADS):
        qo = hh * QK_PAD
        r0 = qo + QK_NOPE
        q1, q2 = rope_t(q_t[r0:r0 + QK_ROPE, :])
        qt_ref[0, hh, 0:QK_NOPE, :] = (q_t[qo:r0, :] * Q_SCALE).astype(BF16)
        qt_ref[0, hh, QK_NOPE:QK_NOPE + ROPE_HALF, :] = (q1 * Q_SCALE).astype(BF16)
        qt_ref[0, hh, QK_NOPE + ROPE_HALF:QK_NOPE + QK_ROPE, :] = (q2 * Q_SCALE).astype(BF16)
        qt_ref[0, hh, QK_NOPE + QK_ROPE:QK_PAD, :] = jnp.zeros((QK_PAD - QK_NOPE - QK_ROPE, ts), BF16)
        k_ref[0, hh, :, 0:LANES] = k_nope[:, hh * QK_NOPE:(hh + 1) * QK_NOPE].astype(BF16)
        k_ref[0, hh, :, LANES:QK_PAD] = k_rot
        vt_ref[0, hh, 0] = v_t[hh * V_HEAD:(hh + 1) * V_HEAD, :].astype(BF16)

    cu_ext[8:ts + 8, :] = gate_c * u
    conv = (cw_ref[2:3, :] * cu_ext[8:ts + 8, :]
            + cw_ref[1:2, :] * cu_ext[pl.ds(7, ts), :]
            + cw_ref[0:1, :] * cu_ext[pl.ds(6, ts), :])
    conv = gate_b * conv
    cv_ref[0] = _rms(conv, cg_ref[...]).astype(BF16)


def _inproj(l, x, mod, w_in_p, w_uq_t, w_k, w_v_t, qg, kvg, conv_w, conv_g, rope_tab, ts):
    b, s, _ = x.shape
    nsb = s // ts
    const = lambda shape: pl.BlockSpec((None,) + shape, lambda bi, j: (l,) + (0,) * len(shape))
    return pl.pallas_call(
        functools.partial(_inproj_kernel, ts=ts),
        grid=(b, nsb),
        in_specs=[
            pl.BlockSpec((1, ts, D_MODEL), lambda bi, j: (bi, j, 0)),
            pl.BlockSpec((None, 1, 8, D_MODEL), lambda bi, j: (l, bi, 0, 0)),
            const((D_MODEL, D_IN_PAD)),
            const((N_HEADS * QK_PAD, Q_LORA)),
            const((KV_LORA, N_HEADS * QK_NOPE)),
            const((N_HEADS * V_HEAD, KV_LORA)),
            const((1, Q_LORA)),
            const((1, KV_LORA)),
            const((8, D_CONV)),
            const((1, D_CONV)),
            pl.BlockSpec((1, QK_ROPE, ts), lambda bi, j: (bi * nsb + j, 0, 0)),
        ],
        out_specs=[
            pl.BlockSpec((1, N_HEADS, QK_PAD, ts), lambda bi, j: (bi, 0, 0, j)),
            pl.BlockSpec((1, N_HEADS, ts, QK_PAD), lambda bi, j: (bi, 0, j, 0)),
            pl.BlockSpec((1, N_HEADS, 1, V_HEAD, ts), lambda bi, j: (bi, 0, j, 0, 0)),
            pl.BlockSpec((1, ts, D_CONV), lambda bi, j: (bi, j, 0)),
        ],
        out_shape=[
            jax.ShapeDtypeStruct((b, N_HEADS, QK_PAD, s), BF16),
            jax.ShapeDtypeStruct((b, N_HEADS, s, QK_PAD), BF16),
            jax.ShapeDtypeStruct((b, N_HEADS, nsb, V_HEAD, ts), BF16),
            jax.ShapeDtypeStruct((b, s, D_CONV), BF16),
        ],
        scratch_shapes=[pltpu.VMEM((ts + 8, D_CONV), F32)],
        compiler_params=_cparams(2),
        name="inproj",
    )(x, mod, w_in_p, w_uq_t, w_k, w_v_t, qg, kvg, conv_w, conv_g, rope_tab)


Q_CHUNK = 256


K_ROWS = 16


def _attn_kernel(qt_ref, k_ref, vt_ref, o_ref, *scratch, tq):
    i = pl.program_id(1)
    n_chunks = tq // Q_CHUNK
    per_chain = 5
    chains = [(hh, c) + tuple(scratch[per_chain * (hh * n_chunks + c):per_chain * (hh * n_chunks + c + 1)])
              for hh in range(N_HEADS) for c in range(n_chunks)]
    for _, _, m_sc, l_sc, acc_sc, _, _ in chains:
        m_sc[...] = jnp.full(m_sc.shape, NEG_BIG, F32)
        l_sc[...] = jnp.zeros(l_sc.shape, F32)
        acc_sc[...] = jnp.zeros(acc_sc.shape, F32)

    def scores(chain, jk, buf):
        hh, c = chain[0], chain[1]
        kj = k_ref[0, hh, pl.ds(pl.multiple_of(jk * tq, tq), tq), :]
        chain[5 + buf][...] = _dot(kj, qt_ref[0, hh, :, c * Q_CHUNK:(c + 1) * Q_CHUNK])

    def softmax_values(chain, jk, buf, diagonal):
        hh, c, m_sc, l_sc, acc_sc = chain[:5]
        s_sc = chain[5 + buf]
        nk = (c + 1) * Q_CHUNK if diagonal else tq
        if diagonal:
            d0 = nk - Q_CHUNK
            key = lax.broadcasted_iota(jnp.int32, (Q_CHUNK, Q_CHUNK), 0)
            qry = lax.broadcasted_iota(jnp.int32, (Q_CHUNK, Q_CHUNK), 1)
            s_sc[d0:nk, :] = jnp.where(key <= qry, s_sc[d0:nk, :], NEG_BIG)
        blk_max = s_sc[0:K_ROWS, :]
        for r in range(K_ROWS, nk, K_ROWS):
            blk_max = jnp.maximum(blk_max, s_sc[r:r + K_ROWS, :])
        m_prev = m_sc[...]
        m_new = jnp.maximum(m_prev, jnp.max(blk_max, axis=0, keepdims=True))
        alpha = jnp.exp2(m_prev - m_new)
        p_sum = jnp.zeros((K_ROWS, Q_CHUNK), F32)
        acc = alpha * acc_sc[...]
        for kt in range(0, nk, Q_CHUNK):
            parts = []
            for r in range(kt, kt + Q_CHUNK, K_ROWS):
                p = jnp.exp2(s_sc[r:r + K_ROWS, :] - m_new)
                p_sum = p_sum + p
                parts.append(p.astype(BF16))
            acc = acc + _dot(vt_ref[0, hh, jk, :, kt:kt + Q_CHUNK], jnp.concatenate(parts, axis=0))
        l_sc[...] = alpha * l_sc[...] + jnp.sum(p_sum, axis=0, keepdims=True)
        acc_sc[...] = acc
        m_sc[...] = m_new

    def stage(cur, cur_buf, diagonal, nxt=None):
        for chain in chains:
            if nxt is not None:
                scores(chain, nxt, 1 - cur_buf)
            softmax_values(chain, cur, cur_buf, diagonal)

    for chain in chains:
        scores(chain, 0, 0)

    def pair(jj, carry):
        stage(2 * jj, 0, False, nxt=2 * jj + 1)
        stage(2 * jj + 1, 1, False, nxt=2 * jj + 2)
        return carry

    lax.fori_loop(0, i // 2, pair, 0)

    @pl.when(lax.rem(i, 2) == 0)
    def _():
        stage(i, 0, True)

    @pl.when(lax.rem(i, 2) == 1)
    def _():
        stage(i - 1, 0, False, nxt=i)
        stage(i, 1, True)

    for hh, c, _, l_sc, acc_sc, _, _ in chains:
        o_ref[0, c * Q_CHUNK:(c + 1) * Q_CHUNK, hh * V_HEAD:(hh + 1) * V_HEAD] = (
            jnp.transpose(acc_sc[...] / l_sc[...]).astype(BF16))


def _attention(qt, k, vt, tq):
    b, nh, s, _ = k.shape
    return pl.pallas_call(
        functools.partial(_attn_kernel, tq=tq),
        grid=(b, s // tq),
        in_specs=[
            pl.BlockSpec((1, nh, QK_PAD, tq), lambda bi, i: (bi, 0, 0, i)),
            pl.BlockSpec((1, nh, s, QK_PAD), lambda bi, i: (bi, 0, 0, 0)),
            pl.BlockSpec((1, nh, s // tq, V_HEAD, tq), lambda bi, i: (bi, 0, 0, 0, 0)),
        ],
        out_specs=pl.BlockSpec((1, tq, D_ATTN), lambda bi, i: (bi, i, 0)),
        out_shape=jax.ShapeDtypeStruct((b, s, D_ATTN), BF16),
        scratch_shapes=[
            pltpu.VMEM((1, Q_CHUNK), F32), pltpu.VMEM((1, Q_CHUNK), F32), pltpu.VMEM((V_HEAD, Q_CHUNK), F32),
            pltpu.VMEM((tq, Q_CHUNK), F32), pltpu.VMEM((tq, Q_CHUNK), F32),
        ] * (nh * (tq // Q_CHUNK)),
        compiler_params=_cparams(2),
        name="attention",
    )(qt, k, vt)


OUT_ROWS = 256


def _row_chunks(ts):
    return [slice(r, r + OUT_ROWS) for r in range(0, ts, OUT_ROWS)]


def _outproj_matmuls(attn_ref, cv_ref, wo_ref, ga_ref, rows):
    an = _rms(attn_ref[0, rows, :].astype(F32), ga_ref[...]).astype(BF16)
    return _dot(an, wo_ref[0:D_ATTN, :]) + _dot(cv_ref[0, rows, :], wo_ref[D_ATTN:D_MODEL, :])


def _outproj_norm(y, x_ref, mod_ref, lng_ref, lnb_ref, rows):
    x1 = _layer_norm(DEEPNORM_ALPHA * x_ref[0, rows, :] + mod_ref[0, 2:3, :] * y, lng_ref[...], lnb_ref[...])
    h2 = x1 * (1.0 + mod_ref[0, 4:5, :]) + mod_ref[0, 3:4, :]
    return x1, h2


def _outproj_kernel(attn_ref, cv_ref, x_ref, mod_ref, wo_ref, ga_ref, lng_ref, lnb_ref, x1_ref, *, ts):
    ys = [_outproj_matmuls(attn_ref, cv_ref, wo_ref, ga_ref, rows) for rows in _row_chunks(ts)]
    for rows, y in zip(_row_chunks(ts), ys):
        x1_ref[0, rows, :], _ = _outproj_norm(y, x_ref, mod_ref, lng_ref, lnb_ref, rows)


def _outproj_router_kernel(attn_ref, cv_ref, x_ref, mod_ref, wo_ref, ga_ref, lng_ref, lnb_ref, rw_ref, rb_ref,
                           x1_ref, h2_ref, rel_ref, gate_ref, relt_ref, base_ref, cnt_ref, run_sc, *, ts):
    first = jnp.logical_and(pl.program_id(0) == 0, pl.program_id(1) == 0)

    @pl.when(first)
    def _():
        run_sc[...] = jnp.zeros(run_sc.shape, F32)

    chunks = _row_chunks(ts)
    ys = [_outproj_matmuls(attn_ref, cv_ref, wo_ref, ga_ref, rows) for rows in chunks]
    lane = lax.broadcasted_iota(jnp.int32, (OUT_ROWS, LANES), 1)
    r_i = lax.broadcasted_iota(jnp.int32, (OUT_ROWS, OUT_ROWS), 0)
    c_i = lax.broadcasted_iota(jnp.int32, (OUT_ROWS, OUT_ROWS), 1)
    tri = jnp.where(c_i < r_i, 1.0, 0.0).astype(BF16)
    n_tile = jnp.zeros((1, LANES), F32)
    for rows, y in zip(chunks, ys):
        x1, h2 = _outproj_norm(y, x_ref, mod_ref, lng_ref, lnb_ref, rows)
        x1_ref[0, rows, :] = x1
        h2_hi = h2.astype(BF16)
        h2_ref[0, rows, :] = h2_hi

        h2_lo = (h2 - h2_hi.astype(F32)).astype(BF16)
        prod = _dot(h2_hi, rw_ref[...]) + _dot(h2_lo, rw_ref[...])
        logits = prod + pltpu.roll(prod, LANES - N_EXPERTS, 1) + rb_ref[...]
        logits = jnp.where(lane < N_EXPERTS, logits, NEG_BIG)
        v1 = jnp.max(logits, axis=-1, keepdims=True)
        i1 = jnp.min(jnp.where(logits == v1, lane, LANES), axis=-1, keepdims=True)
        rest = jnp.where(lane == i1, NEG_BIG, logits)
        v2 = jnp.max(rest, axis=-1, keepdims=True)
        i2 = jnp.min(jnp.where(rest == v2, lane, LANES), axis=-1, keepdims=True)
        e21 = jnp.exp(v2 - v1)
        g1 = 1.0 / (1.0 + e21)
        g2 = e21 / (1.0 + e21)
        is1 = lane == i1
        is2 = lane == i2
        sel = jnp.where(jnp.logical_or(is1, is2), 1.0, 0.0)
        gate_ref[rows, :] = jnp.where(is1, g1, 0.0) + jnp.where(is2, g2, 0.0)

        rank = _dot(tri, sel.astype(BF16)) + n_tile
        rel = jnp.where(sel > 0.0, rank, NOT_ROUTED)
        rel_ref[rows, :] = rel
        relt_ref[0, :, rows] = jnp.transpose(rel)[0:N_EXPERTS, :]
        n_tile = n_tile + jnp.sum(sel, axis=0, keepdims=True)

    base_ref[0] = run_sc[...].astype(jnp.int32)
    cnt_ref[0] = n_tile.astype(jnp.int32)
    run_sc[...] = run_sc[...] + n_tile


def _outproj(l, attn, convn, x, mod, w_o, g_attn, ln_g, ln_b, ts, router=None):
    b, s, _ = x.shape
    nsb = s // ts
    const = lambda shape, idx=l: pl.BlockSpec((None,) + shape, lambda bi, j: (idx,) + (0,) * len(shape))
    tile3 = lambda w: pl.BlockSpec((1, ts, w), lambda bi, j: (bi, j, 0))
    in_specs = [
        tile3(D_ATTN), tile3(D_CONV), tile3(D_MODEL),
        pl.BlockSpec((None, 1, 8, D_MODEL), lambda bi, j: (l, bi, 0, 0)),
        const((D_MODEL, D_MODEL)), const((1, D_ATTN)), const((1, D_MODEL)), const((1, D_MODEL)),
    ]
    out_specs = [tile3(D_MODEL), tile3(D_MODEL)]
    out_shape = [jax.ShapeDtypeStruct((b, s, D_MODEL), F32), jax.ShapeDtypeStruct((b, s, D_MODEL), BF16)]
    args = [attn, convn, x, mod, w_o, g_attn, ln_g, ln_b]
    if router is None:
        return pl.pallas_call(
            functools.partial(_outproj_kernel, ts=ts), grid=(b, nsb), in_specs=in_specs, out_specs=out_specs[0],
            out_shape=out_shape[0],
            compiler_params=_cparams(2), name="outproj",
        )(*args)
    moe_i, rw, rb = router
    nt = b * nsb
    flat = lambda w: pl.BlockSpec((ts, w), lambda bi, j: (bi * nsb + j, 0))
    per_tile = lambda r, w: pl.BlockSpec((1, r, w), lambda bi, j: (bi * nsb + j, 0, 0))
    in_specs += [const((D_MODEL, LANES), moe_i), const((1, LANES), moe_i)]
    out_specs += [flat(LANES), flat(LANES), per_tile(N_EXPERTS, ts), per_tile(1, LANES), per_tile(1, LANES)]
    out_shape += [
        jax.ShapeDtypeStruct((nt * ts, LANES), F32), jax.ShapeDtypeStruct((nt * ts, LANES), F32),
        jax.ShapeDtypeStruct((nt, N_EXPERTS, ts), F32),
        jax.ShapeDtypeStruct((nt, 1, LANES), jnp.int32), jax.ShapeDtypeStruct((nt, 1, LANES), jnp.int32),
    ]
    return pl.pallas_call(
        functools.partial(_outproj_router_kernel, ts=ts), grid=(b, nsb), in_specs=in_specs, out_specs=out_specs,
        out_shape=out_shape, scratch_shapes=[pltpu.VMEM((1, LANES), F32)],
        compiler_params=_cparams(2), name="outproj_router",
    )(*args, rw, rb)


FF_CHUNK = 256


def _swiglu(x, wg, wu, wd, width):
    out = None
    pending = None
    for c in range(0, width, FF_CHUNK):
        w = min(FF_CHUNK, width - c)
        g = _dot(x, wg(c, w))
        u = _dot(x, wu(c, w))
        if pending is not None:
            d = _dot(pending[0], wd(*pending[1]))
            out = d if out is None else out + d
        pending = ((_silu(g) * u).astype(BF16), (c, w))
    d = _dot(pending[0], wd(*pending[1]))
    return d if out is None else out + d


def _ffn_kernel(x1_ref, mod_ref, wg_ref, wu_ref, wd_ref, lng_ref, lnb_ref, o_ref):
    x1 = x1_ref[...]
    h = (x1 * (1.0 + mod_ref[0, 4:5, :]) + mod_ref[0, 3:4, :]).astype(BF16)
    f = _swiglu(h, lambda c, w: wg_ref[:, c:c + w], lambda c, w: wu_ref[:, c:c + w],
                lambda c, w: wd_ref[c:c + w, :], D_FF)
    gate2 = mod_ref[0, 5:6, :]
    o_ref[...] = _layer_norm(DEEPNORM_ALPHA * x1 + gate2 * f, lng_ref[...], lnb_ref[...])


def _dense_ffn(l, dense_i, x1, mod, wg, wu, wd, ln_g, ln_b, tm, tiles_per_batch):
    t = x1.shape[0]
    const = lambda shape, idx: pl.BlockSpec((None,) + shape, lambda i: (idx,) + (0,) * len(shape),
                                            pipeline_mode=pl.Buffered(1))
    return pl.pallas_call(
        _ffn_kernel,
        grid=(t // tm,),
        in_specs=[
            pl.BlockSpec((tm, D_MODEL), lambda i: (i, 0)),
            pl.BlockSpec((None, 1, 8, D_MODEL), lambda i: (l, i // tiles_per_batch, 0, 0)),
            const((D_MODEL, D_FF), dense_i), const((D_MODEL, D_FF), dense_i), const((D_FF, D_MODEL), dense_i),
            const((1, D_MODEL), l), const((1, D_MODEL), l),
        ],
        out_specs=pl.BlockSpec((tm, D_MODEL), lambda i: (i, 0)),
        out_shape=jax.ShapeDtypeStruct((t, D_MODEL), F32),
        compiler_params=_cparams(1),
        name="dense_ffn",
    )(x1, mod, wg, wu, wd, ln_g, ln_b)


def _group_copy(src, dst, sem):
    return pltpu.make_async_copy(src, dst, sem)


def _floor_align(v):
    return pl.multiple_of(lax.shift_left(lax.shift_right_logical(v, 4), 4), ROW_ALIGN)


def _dispatch_kernel(base_ref, cnt_ref, h_ref, relt_ref, xs_ref, stage, carry, zbuf, sems, xsem, zsem, *,
                     ts, zero_rows):
    i = pl.program_id(0)
    last = pl.num_programs(0) - 1
    par = lax.rem(i, 2)
    n_chunks = ts // SLOT_CHUNK
    h = h_ref[...]
    slot = lax.broadcasted_iota(jnp.int32, (SLOT_CHUNK, ts), 0).astype(F32)

    def group(step, e):
        a = base_ref[step * N_EXPERTS + e]
        a16 = _floor_align(a)
        return a, a16, a - a16

    def onehot(e, c):
        _, _, off = group(i, e)
        row = relt_ref[0, e:e + 1, :] + off.astype(F32)
        return jnp.where(row == slot + float(c * SLOT_CHUNK), 1.0, 0.0).astype(BF16)

    def first_copy(step, e):
        _, a16, _ = group(step, e)
        p = lax.rem(step, 2)
        return _group_copy(stage.at[p, e, 0:SLOT_CHUNK], xs_ref.at[e, pl.ds(a16, SLOT_CHUNK), :], sems.at[p, e])

    @pl.when(i == 0)
    def _():
        carry[...] = jnp.zeros(carry.shape, BF16)
        stage[:, :, n_chunks * SLOT_CHUNK:, :] = jnp.zeros((2, N_EXPERTS, ROW_ALIGN, D_MODEL), BF16)

    for e in range(N_EXPERTS):
        rows = _dot(onehot(e, 0), h)
        stage[par, e, 0:ROW_ALIGN] = (rows[0:ROW_ALIGN] + carry[e].astype(F32)).astype(BF16)
        stage[par, e, ROW_ALIGN:SLOT_CHUNK] = rows[ROW_ALIGN:SLOT_CHUNK].astype(BF16)
        stage[par, e, SLOT_CHUNK:SLOT_CHUNK + ROW_ALIGN] = jnp.zeros((ROW_ALIGN, D_MODEL), BF16)

    @pl.when(i > 0)
    def _():
        for e in range(N_EXPERTS):
            first_copy(i - 1, e).wait()

    for e in range(N_EXPERTS):
        first_copy(i, e).start()

    for e in range(N_EXPERTS):
        _, a16, off = group(i, e)
        n = cnt_ref[i * N_EXPERTS + e]
        for c in range(1, n_chunks):
            @pl.when(off + n > c * SLOT_CHUNK)
            def _(e=e, c=c, a16=a16):
                lo = c * SLOT_CHUNK
                stage[par, e, lo:lo + SLOT_CHUNK] = _dot(onehot(e, c), h).astype(BF16)
                if c + 1 < n_chunks:
                    stage[par, e, lo + SLOT_CHUNK:lo + SLOT_CHUNK + ROW_ALIGN] = jnp.zeros((ROW_ALIGN, D_MODEL), BF16)
                cp = _group_copy(stage.at[par, e, lo:lo + SLOT_CHUNK],
                                 xs_ref.at[e, pl.ds(a16 + lo, SLOT_CHUNK), :], xsem)
                cp.start()
                cp.wait()

    for e in range(N_EXPERTS):
        a, a16, _ = group(i, e)
        nxt16 = _floor_align(a + cnt_ref[i * N_EXPERTS + e])
        carry[e] = stage[par, e, pl.ds(pl.multiple_of(nxt16 - a16, ROW_ALIGN), ROW_ALIGN), :]

    @pl.when(i == last)
    def _():
        for e in range(N_EXPERTS):
            first_copy(i, e).wait()
        zbuf[...] = jnp.zeros(zbuf.shape, BF16)
        for e in range(N_EXPERTS):
            n = cnt_ref[i * N_EXPERTS + e]
            end = _floor_align(base_ref[i * N_EXPERTS + e] + n + (ROW_ALIGN - 1))
            copies = [
                _group_copy(zbuf, xs_ref.at[e, pl.ds(end + r * SLOT_CHUNK, SLOT_CHUNK), :], zsem.at[e])
                for r in range(zero_rows // SLOT_CHUNK)
            ]
            for cp in copies:
                cp.start()
            for cp in copies:
                cp.wait()


def _dispatch(base, cnt, h2, relt, ts, cap, zero_rows):
    t = h2.shape[0]
    nt = t // ts
    n_chunks = ts // SLOT_CHUNK
    grid_spec = pltpu.PrefetchScalarGridSpec(
        num_scalar_prefetch=2,
        grid=(nt,),
        in_specs=[
            pl.BlockSpec((ts, D_MODEL), lambda i, b_, c_: (i, 0)),
            pl.BlockSpec((1, N_EXPERTS, ts), lambda i, b_, c_: (i, 0, 0)),
        ],
        out_specs=pl.BlockSpec(memory_space=pl.ANY),
        scratch_shapes=[
            pltpu.VMEM((2, N_EXPERTS, n_chunks * SLOT_CHUNK + ROW_ALIGN, D_MODEL), BF16),
            pltpu.VMEM((N_EXPERTS, ROW_ALIGN, D_MODEL), BF16),
            pltpu.VMEM((SLOT_CHUNK, D_MODEL), BF16),
            pltpu.SemaphoreType.DMA((2, N_EXPERTS)),
            pltpu.SemaphoreType.DMA(()),
            pltpu.SemaphoreType.DMA((N_EXPERTS,)),
        ],
    )
    return pl.pallas_call(
        functools.partial(_dispatch_kernel, ts=ts, zero_rows=zero_rows),
        grid_spec=grid_spec,
        out_shape=jax.ShapeDtypeStruct((N_EXPERTS, cap, D_MODEL), BF16),
        compiler_params=_cparams(1),
        name="moe_dispatch",
    )(base, cnt, h2, relt)


def _expert_kernel(te_ref, tr_ref, rows_ref, nv_ref, x_ref, wg_ref, wu_ref, wd_ref, o_ref, acc_sc):
    g_i = pl.program_id(0)
    k = pl.program_id(1)

    valid = g_i < nv_ref[0]

    @pl.when(jnp.logical_and(valid, k == 0))
    def _():
        acc_sc[...] = jnp.zeros(acc_sc.shape, F32)

    for m in EXPERT_ROW_SIZES:
        @pl.when(jnp.logical_and(valid, rows_ref[g_i] == m))
        def _(m=m):
            part = _swiglu(x_ref[0, 0:m, :], lambda c, w: wg_ref[0, :, c:c + w].astype(BF16),
                           lambda c, w: wu_ref[0, :, c:c + w].astype(BF16),
                           lambda c, w: wd_ref[0, c:c + w, :].astype(BF16), wg_ref.shape[2])
            total = acc_sc[0:m, :] + part
            acc_sc[0:m, :] = total
            o_ref[0, 0:m, :] = total.astype(BF16)


def _expert_ffn(moe_i, tile_e, tile_r, tile_rows, n_valid, xs, wg, wu, wd, tf, n_steps):
    tm = EXPERT_ROWS
    nk = D_FF_EXPERT // tf
    cap = xs.shape[1]

    def k_eff(g, k, nv):
        return jnp.where(g < nv[0], k, nk - 1)

    grid_spec = pltpu.PrefetchScalarGridSpec(
        num_scalar_prefetch=4,
        grid=(n_steps, nk),
        in_specs=[
            pl.BlockSpec((1, tm, D_MODEL), lambda g, k, te, tr, rw, nv: (te[g], tr[g], 0)),
            pl.BlockSpec((None, 1, D_MODEL, tf), lambda g, k, te, tr, rw, nv: (moe_i, te[g], 0, k_eff(g, k, nv))),
            pl.BlockSpec((None, 1, D_MODEL, tf), lambda g, k, te, tr, rw, nv: (moe_i, te[g], 0, k_eff(g, k, nv))),
            pl.BlockSpec((None, 1, tf, D_MODEL), lambda g, k, te, tr, rw, nv: (moe_i, te[g], k_eff(g, k, nv), 0)),
        ],
        out_specs=pl.BlockSpec((1, tm, D_MODEL), lambda g, k, te, tr, rw, nv: (te[g], tr[g], 0)),
        scratch_shapes=[pltpu.VMEM((tm, D_MODEL), F32)],
    )
    return pl.pallas_call(
        _expert_kernel,
        grid_spec=grid_spec,
        out_shape=jax.ShapeDtypeStruct((N_EXPERTS, cap, D_MODEL), BF16),
        compiler_params=_cparams(2),
        name="moe_experts",
    )(tile_e, tile_r, tile_rows, n_valid, xs, wg, wu, wd)


def _combine_kernel(base_ref, cnt_ref, lim_ref, ys_ref, rel_ref, gate_ref, x1_ref, mod_ref, lng_ref, lnb_ref, o_ref,
                    slab, xslab, acc_sc, sems, xsem, *, ts):
    i = pl.program_id(0)
    last = pl.num_programs(0) - 1
    par = lax.rem(i, 2)
    n_chunks = ts // SLOT_CHUNK
    slot = lax.broadcasted_iota(jnp.int32, (ts, SLOT_CHUNK), 1).astype(F32)

    def slab_start(step, e, c):
        want = _floor_align(base_ref[step * N_EXPERTS + e]) + c * SLOT_CHUNK
        start = jnp.maximum(jnp.minimum(want, lim_ref[e] - SLOT_CHUNK), 0)
        return pl.multiple_of(start, ROW_ALIGN), want - start

    def first_copy(step, e):
        start, _ = slab_start(step, e, 0)
        p = lax.rem(step, 2)
        return _group_copy(ys_ref.at[e, pl.ds(start, SLOT_CHUNK), :], slab.at[p, e], sems.at[p, e])

    def contribution(e, c, rows):
        a = base_ref[i * N_EXPERTS + e]
        _, moved = slab_start(i, e, c)
        pos = rel_ref[:, e:e + 1] + (a - _floor_align(a)).astype(F32)
        in_chunk = jnp.where(pos >= float(c * SLOT_CHUNK), 1.0, 0.0)
        onehot = jnp.where(pos + (moved - c * SLOT_CHUNK).astype(F32) == slot, in_chunk, 0.0).astype(BF16)
        return _dot(onehot, rows) * gate_ref[:, e:e + 1]

    @pl.when(i == 0)
    def _():
        for e in range(N_EXPERTS):
            first_copy(i, e).start()

    @pl.when(i < last)
    def _():
        for e in range(N_EXPERTS):
            first_copy(i + 1, e).start()

    for e in range(N_EXPERTS):
        first_copy(i, e).wait()
    f = contribution(0, 0, slab[par, 0])
    for e in range(1, N_EXPERTS):
        f = f + contribution(e, 0, slab[par, e])
    acc_sc[...] = f

    for e in range(N_EXPERTS):
        a = base_ref[i * N_EXPERTS + e]
        rows_in_block = a - _floor_align(a) + cnt_ref[i * N_EXPERTS + e]
        for c in range(1, n_chunks):
            @pl.when(rows_in_block > c * SLOT_CHUNK)
            def _(e=e, c=c):
                start, _ = slab_start(i, e, c)
                cp = _group_copy(ys_ref.at[e, pl.ds(start, SLOT_CHUNK), :], xslab, xsem)
                cp.start()
                cp.wait()
                acc_sc[...] += contribution(e, c, xslab[...])

    gate2 = mod_ref[0, 5:6, :]
    o_ref[...] = _layer_norm(DEEPNORM_ALPHA * x1_ref[...] + gate2 * acc_sc[...], lng_ref[...], lnb_ref[...])


def _combine(l, base, cnt, lim, ys, rel, gates, x1, mod, ln_g, ln_b, ts, tiles_per_batch):
    t = x1.shape[0]
    grid_spec = pltpu.PrefetchScalarGridSpec(
        num_scalar_prefetch=3,
        grid=(t // ts,),
        in_specs=[
            pl.BlockSpec(memory_space=pl.ANY),
            pl.BlockSpec((ts, LANES), lambda i, *_: (i, 0)),
            pl.BlockSpec((ts, LANES), lambda i, *_: (i, 0)),
            pl.BlockSpec((ts, D_MODEL), lambda i, *_: (i, 0)),
            pl.BlockSpec((None, 1, 8, D_MODEL), lambda i, *_: (l, i // tiles_per_batch, 0, 0)),
            pl.BlockSpec((None, 1, D_MODEL), lambda i, *_: (l, 0, 0)),
            pl.BlockSpec((None, 1, D_MODEL), lambda i, *_: (l, 0, 0)),
        ],
        out_specs=pl.BlockSpec((ts, D_MODEL), lambda i, *_: (i, 0)),
        scratch_shapes=[
            pltpu.VMEM((2, N_EXPERTS, SLOT_CHUNK, D_MODEL), BF16),
            pltpu.VMEM((SLOT_CHUNK, D_MODEL), BF16),
            pltpu.VMEM((ts, D_MODEL), F32),
            pltpu.SemaphoreType.DMA((2, N_EXPERTS)),
            pltpu.SemaphoreType.DMA(()),
        ],
    )
    return pl.pallas_call(
        functools.partial(_combine_kernel, ts=ts),
        grid_spec=grid_spec,
        out_shape=jax.ShapeDtypeStruct((t, D_MODEL), F32),
        compiler_params=_cparams(1),
        name="moe_combine",
    )(base, cnt, lim, ys, rel, gates, x1, mod, ln_g, ln_b)


def _expert_tile_plan(base, cnt, nt, n_steps):
    tm = EXPERT_ROWS
    i32 = lambda v: v.astype(jnp.int32)
    last_base = base.reshape(nt, N_EXPERTS)[-1]
    last_cnt = cnt.reshape(nt, N_EXPERTS)[-1]
    total = last_base + last_cnt
    tiles = jnp.maximum((total + tm - 1) // tm, 1)
    rest = total - (tiles - 1) * tm
    last_rows = jnp.full_like(rest, EXPERT_ROW_SIZES[0])
    for small, big in zip(EXPERT_ROW_SIZES[:-1], EXPERT_ROW_SIZES[1:]):
        last_rows = last_rows + jnp.where(rest > small, big - small, 0)
    ends = jnp.cumsum(tiles)
    n_valid = ends[-1]
    g = jnp.minimum(jnp.arange(n_steps, dtype=jnp.int32), n_valid - 1)
    tile_e = jnp.sum((g[:, None] >= ends[None, :]).astype(jnp.int32), axis=1)
    onehot = (tile_e[:, None] == jnp.arange(N_EXPERTS, dtype=jnp.int32)[None, :]).astype(jnp.int32)
    pick = lambda v: jnp.sum(onehot * v[None, :], axis=1)
    tile_r = g - pick(ends - tiles)
    tile_rows = jnp.where(tile_r == pick(tiles) - 1, pick(last_rows), tm)
    lim = (tiles - 1) * tm + last_rows
    return i32(tile_e), i32(tile_r), i32(tile_rows), i32(n_valid.reshape(1)), i32(lim)


def _top2_moe(l, moe_i, h2, x1, mod, routing, wg, wu, wd, ln_g, ln_b, ts, tiles_per_batch):
    rel, gates, relt, base3, cnt3 = routing
    t = h2.shape[0]
    nt = t // ts
    tm, tf = EXPERT_ROWS, 512
    base = base3[:, 0, :N_EXPERTS].reshape(-1)
    cnt = cnt3[:, 0, :N_EXPERTS].reshape(-1)
    zero_rows = -(-tm // SLOT_CHUNK) * SLOT_CHUNK
    cap = -(-(t + ROW_ALIGN + ts + zero_rows) // tm) * tm
    n_steps = -(-TOP_K * t // tm) + N_EXPERTS
    xs = _dispatch(base, cnt, h2, relt, ts, cap, zero_rows)
    tile_e, tile_r, tile_rows, n_valid, lim = _expert_tile_plan(base, cnt, nt, n_steps)
    ys = _expert_ffn(moe_i, tile_e, tile_r, tile_rows, n_valid, xs, wg, wu, wd, tf, n_steps)
    return _combine(l, base, cnt, lim, ys, rel, gates, x1, mod, ln_g, ln_b, ts, tiles_per_batch)


def kernel(x, c, positions, w_in, q_norm_g, kv_norm_g, w_uq, w_ukv, conv_w, mix_norm_g, w_o, w_ada, b_ada,
           ln1_g, ln1_b, ln2_g, ln2_b, ffn_w_gate, ffn_w_up, ffn_w_down, moe_router_w, moe_router_b,
           moe_w_gate, moe_w_up, moe_w_down):
    b, s, _ = x.shape
    n_layers = w_in.shape[0]
    ts = min(512, s)
    t = b * s
    tiles_per_batch = s // ts

    bp = -(-b // 16) * 16
    c_pad = jnp.pad(c, ((0, bp - b), (0, 0)))
    mod = _ada_mod(c_pad, w_ada, b_ada)
    mod = jnp.pad(jnp.transpose(mod[:, :, :b, :], (0, 2, 1, 3)), ((0, 0), (0, 0), (0, 2), (0, 0)))
    rope_tab = _rope_tables(positions, ts)

    rows = lambda v: v.reshape(v.shape[0], 1, -1)
    o3 = Q_LORA + KV_LORA + QK_ROPE
    w_in_p = jnp.concatenate(
        [w_in[:, :, :o3], jnp.zeros((n_layers, D_MODEL, D_IN_PAD - w_in.shape[2]), F32), w_in[:, :, o3:]], axis=2
    ).astype(BF16)
    w_uq_t = jnp.transpose(jnp.pad(
        w_uq.reshape(n_layers, Q_LORA, N_HEADS, QK_NOPE + QK_ROPE),
        ((0, 0), (0, 0), (0, 0), (0, QK_PAD - QK_NOPE - QK_ROPE))
    ).reshape(n_layers, Q_LORA, N_HEADS * QK_PAD), (0, 2, 1)).astype(BF16)
    w_ukv_h = w_ukv.reshape(n_layers, KV_LORA, N_HEADS, QK_NOPE + V_HEAD)
    w_k = w_ukv_h[..., :QK_NOPE].reshape(n_layers, KV_LORA, N_HEADS * QK_NOPE).astype(BF16)
    w_v_t = jnp.transpose(w_ukv_h[..., QK_NOPE:].reshape(n_layers, KV_LORA, N_HEADS * V_HEAD), (0, 2, 1)).astype(BF16)
    w_o_b = w_o.astype(BF16)
    conv_w_p = jnp.pad(conv_w, ((0, 0), (0, 8 - CONV_WIDTH), (0, 0)))
    g_attn, g_conv = rows(mix_norm_g[:, :D_ATTN]), rows(mix_norm_g[:, D_ATTN:])
    ffn_wg, ffn_wu, ffn_wd = ffn_w_gate.astype(BF16), ffn_w_up.astype(BF16), ffn_w_down.astype(BF16)
    rw_hi = moe_router_w.astype(BF16)
    rw_lo = (moe_router_w - rw_hi.astype(F32)).astype(BF16)
    rw = jnp.pad(jnp.concatenate([rw_hi, rw_lo], axis=2), ((0, 0), (0, 0), (0, LANES - 2 * N_EXPERTS)))
    rb = rows(jnp.pad(moe_router_b, ((0, 0), (0, LANES - N_EXPERTS))))

    for l in range(n_layers):
        qt, k, vt, convn = _inproj(l, x, mod, w_in_p, w_uq_t, w_k, w_v_t, rows(q_norm_g), rows(kv_norm_g),
                                   conv_w_p, g_conv, rope_tab, ts)
        attn = _attention(qt, k, vt, ts)
        common = (l, attn, convn, x, mod, w_o_b, g_attn, rows(ln1_g), rows(ln1_b), ts)
        i = l // 2
        if l % 2 == 0:
            x1 = _outproj(*common)
            x = _dense_ffn(l, i, x1.reshape(t, D_MODEL), mod, ffn_wg, ffn_wu, ffn_wd, rows(ln2_g), rows(ln2_b),
                           ts, tiles_per_batch)
        else:
            x1, h2, *routing = _outproj(*common, router=(i, rw, rb))
            x = _top2_moe(l, i, h2.reshape(t, D_MODEL), x1.reshape(t, D_MODEL), mod, routing,
                          moe_w_gate, moe_w_up, moe_w_down, rows(ln2_g), rows(ln2_b), ts, tiles_per_batch)
        x = x.reshape(b, s, D_MODEL)
    return x
```

```python
import functools

import jax
import jax.numpy as jnp
from jax import lax
from jax.experimental import pallas as pl
from jax.experimental.pallas import tpu as pltpu

F32 = jnp.float32
BF16 = jnp.bfloat16

D_MODEL = 1024
N_HEADS = 4
QK_NOPE = 128
QK_ROPE = 64
V_HEAD = 128
Q_LORA = 256
KV_LORA = 128
D_ATTN = N_HEADS * V_HEAD
D_CONV = D_MODEL - D_ATTN
CONV_WIDTH = 3
ROPE_BASE = 10000.0
D_FF = 2816
N_EXPERTS = 8
TOP_K = 2
D_FF_EXPERT = 3584
RMS_EPS = 1e-6
LN_EPS = 1e-5
DEPTH = 2
DEEPNORM_ALPHA = (2 * DEPTH) ** 0.25
SM_SCALE = (QK_NOPE + QK_ROPE) ** -0.5
LOG2E = 1.4426950408889634
Q_SCALE = SM_SCALE * LOG2E

LANES = 128
QK_PAD = 2 * LANES
D_IN_PAD = 2048
ROW_ALIGN = 16
SLOT_CHUNK = 256
EXPERT_ROW_STEP = 192
EXPERT_ROW_SIZES = tuple(EXPERT_ROW_STEP * n for n in range(2, 13))
EXPERT_ROWS = EXPERT_ROW_SIZES[-1]
VMEM_LIMIT = 56 * 1024 * 1024
NEG_BIG = -1e30
NOT_ROUTED = -4096.0


def _cparams(n_axes):
    return pltpu.CompilerParams(dimension_semantics=("arbitrary",) * n_axes, vmem_limit_bytes=VMEM_LIMIT)


def _dot(a, b):
    return jnp.dot(a, b, preferred_element_type=F32)


def _rms(x, g):
    return x * lax.rsqrt(jnp.mean(x * x, axis=-1, keepdims=True) + RMS_EPS) * g


def _layer_norm(x, g, b):
    mu = jnp.mean(x, axis=-1, keepdims=True)
    xc = x - mu
    var = jnp.mean(xc * xc, axis=-1, keepdims=True)
    return xc * lax.rsqrt(var + LN_EPS) * g + b


def _silu(x):
    return x * jax.nn.sigmoid(x)


def _ada_kernel(c_ref, w_ref, b_ref, o_ref):
    ca = _silu(c_ref[...]).astype(BF16)
    o_ref[0, 0] = _dot(ca, w_ref[0].astype(BF16)) + b_ref[0]


def _ada_mod(c_pad, w_ada, b_ada):
    n_layers = w_ada.shape[0]
    bp = c_pad.shape[0]
    return pl.pallas_call(
        _ada_kernel,
        grid=(n_layers, 6),
        in_specs=[
            pl.BlockSpec((bp, D_MODEL), lambda l, j: (0, 0)),
            pl.BlockSpec((1, D_MODEL, D_MODEL), lambda l, j: (l, 0, j)),
            pl.BlockSpec((1, 1, D_MODEL), lambda l, j: (l, 0, j)),
        ],
        out_specs=pl.BlockSpec((1, 1, bp, D_MODEL), lambda l, j: (l, j, 0, 0)),
        out_shape=jax.ShapeDtypeStruct((n_layers, 6, bp, D_MODEL), F32),
        compiler_params=_cparams(2),
        name="ada_mod",
    )(c_pad, w_ada, b_ada.reshape(n_layers, 1, 6 * D_MODEL))


ROPE_HALF = QK_ROPE // 2


def _rope_kernel(pos_ref, f_ref, o_ref):
    ang = pos_ref[0].astype(F32) * f_ref[...]
    o_ref[0, 0:ROPE_HALF, :] = jnp.cos(ang)
    o_ref[0, ROPE_HALF:QK_ROPE, :] = jnp.sin(ang)


def _rope_tables(positions, ts):
    nt = positions.size // ts
    inv_freq = ROPE_BASE ** (-jnp.arange(0, QK_ROPE, 2, dtype=F32) / QK_ROPE)
    return pl.pallas_call(
        _rope_kernel,
        grid=(nt,),
        in_specs=[pl.BlockSpec((1, 1, ts), lambda i: (i, 0, 0)), pl.BlockSpec((ROPE_HALF, 1), lambda i: (0, 0))],
        out_specs=pl.BlockSpec((1, QK_ROPE, ts), lambda i: (i, 0, 0)),
        out_shape=jax.ShapeDtypeStruct((nt, QK_ROPE, ts), F32),
        compiler_params=_cparams(1),
        name="rope_tables",
    )(positions.reshape(nt, 1, ts), inv_freq.reshape(ROPE_HALF, 1))


def _inproj_kernel(x_ref, mod_ref, win_ref, wuqt_ref, wk_ref, wvt_ref, qg_ref, kvg_ref, cw_ref, cg_ref, rope_ref,
                   qt_ref, k_ref, vt_ref, cv_ref, cu_ext, *, ts):
    j = pl.program_id(1)

    @pl.when(j == 0)
    def _():
        cu_ext[0:8, :] = jnp.zeros((8, D_CONV), F32)

    @pl.when(j > 0)
    def _():
        cu_ext[0:8, :] = cu_ext[ts:ts + 8, :]

    x = x_ref[0]
    shift = mod_ref[0, 0:1, :]
    scale = mod_ref[0, 1:2, :]
    h = (x * (1.0 + scale) + shift).astype(BF16)
    z_lat = _dot(h, win_ref[:, 0:512])
    c_q = z_lat[:, 0:Q_LORA]
    c_kv = z_lat[:, Q_LORA:Q_LORA + KV_LORA]
    k_rope = z_lat[:, 384:512]
    z_gate = _dot(h, win_ref[:, 512:1536])
    gate_b = z_gate[:, 0:D_CONV]
    gate_c = z_gate[:, D_CONV:2 * D_CONV]
    u = _dot(h, win_ref[:, 1536:2048])
    cqn = _rms(c_q, qg_ref[...])
    ckvn = _rms(c_kv, kvg_ref[...])
    q_t = _dot(wuqt_ref[...], jnp.transpose(cqn).astype(BF16))
    k_nope = _dot(ckvn.astype(BF16), wk_ref[...])
    v_t = _dot(wvt_ref[...], jnp.transpose(ckvn).astype(BF16))

    cos = rope_ref[0, 0:ROPE_HALF, :]
    sin = rope_ref[0, ROPE_HALF:QK_ROPE, :]

    def rope_t(block):
        x1, x2 = block[0:ROPE_HALF, :], block[ROPE_HALF:QK_ROPE, :]
        return x1 * cos - x2 * sin, x2 * cos + x1 * sin

    k1, k2 = rope_t(jnp.transpose(k_rope)[0:QK_ROPE, :])
    k_rot = jnp.transpose(jnp.concatenate([k1, k2, jnp.zeros((LANES - QK_ROPE, ts), F32)], axis=0)).astype(BF16)
    for hh in range(N_HEADS):
        qo = hh * QK_PAD
        r0 = qo + QK_NOPE
        q1, q2 = rope_t(q_t[r0:r0 + QK_ROPE, :])
        qt_ref[0, hh, 0:QK_NOPE, :] = (q_t[qo:r0, :] * Q_SCALE).astype(BF16)
        qt_ref[0, hh, QK_NOPE:QK_NOPE + ROPE_HALF, :] = (q1 * Q_SCALE).astype(BF16)
        qt_ref[0, hh, QK_NOPE + ROPE_HALF:QK_NOPE + QK_ROPE, :] = (q2 * Q_SCALE).astype(BF16)
        qt_ref[0, hh, QK_NOPE + QK_ROPE:QK_PAD, :] = jnp.zeros((QK_PAD - QK_NOPE - QK_ROPE, ts), BF16)
        k_ref[0, hh, :, 0:LANES] = k_nope[:, hh * QK_NOPE:(hh + 1) * QK_NOPE].astype(BF16)
        k_ref[0, hh, :, LANES:QK_PAD] = k_rot
        vt_ref[0, hh, 0] = v_t[hh * V_HEAD:(hh + 1) * V_HEAD, :].astype(BF16)

    cu_ext[8:ts + 8, :] = gate_c * u
    conv = (cw_ref[2:3, :] * cu_ext[8:ts + 8, :]
            + cw_ref[1:2, :] * cu_ext[pl.ds(7, ts), :]
            + cw_ref[0:1, :] * cu_ext[pl.ds(6, ts), :])
    conv = gate_b * conv
    cv_ref[0] = _rms(conv, cg_ref[...]).astype(BF16)


def _inproj(l, x, mod, w_in_p, w_uq_t, w_k, w_v_t, qg, kvg, conv_w, conv_g, rope_tab, ts):
    b, s, _ = x.shape
    nsb = s // ts
    const = lambda shape: pl.BlockSpec((None,) + shape, lambda bi, j: (l,) + (0,) * len(shape))
    return pl.pallas_call(
        functools.partial(_inproj_kernel, ts=ts),
        grid=(b, nsb),
        in_specs=[
            pl.BlockSpec((1, ts, D_MODEL), lambda bi, j: (bi, j, 0)),
            pl.BlockSpec((None, 1, 8, D_MODEL), lambda bi, j: (l, bi, 0, 0)),
            const((D_MODEL, D_IN_PAD)),
            const((N_HEADS * QK_PAD, Q_LORA)),
            const((KV_LORA, N_HEADS * QK_NOPE)),
            const((N_HEADS * V_HEAD, KV_LORA)),
            const((1, Q_LORA)),
            const((1, KV_LORA)),
            const((8, D_CONV)),
            const((1, D_CONV)),
            pl.BlockSpec((1, QK_ROPE, ts), lambda bi, j: (bi * nsb + j, 0, 0)),
        ],
        out_specs=[
            pl.BlockSpec((1, N_HEADS, QK_PAD, ts), lambda bi, j: (bi, 0, 0, j)),
            pl.BlockSpec((1, N_HEADS, ts, QK_PAD), lambda bi, j: (bi, 0, j, 0)),
            pl.BlockSpec((1, N_HEADS, 1, V_HEAD, ts), lambda bi, j: (bi, 0, j, 0, 0)),
            pl.BlockSpec((1, ts, D_CONV), lambda bi, j: (bi, j, 0)),
        ],
        out_shape=[
            jax.ShapeDtypeStruct((b, N_HEADS, QK_PAD, s), BF16),
            jax.ShapeDtypeStruct((b, N_HEADS, s, QK_PAD), BF16),
            jax.ShapeDtypeStruct((b, N_HEADS, nsb, V_HEAD, ts), BF16),
            jax.ShapeDtypeStruct((b, s, D_CONV), BF16),
        ],
        scratch_shapes=[pltpu.VMEM((ts + 8, D_CONV), F32)],
        compiler_params=_cparams(2),
        name="inproj",
    )(x, mod, w_in_p, w_uq_t, w_k, w_v_t, qg, kvg, conv_w, conv_g, rope_tab)


Q_CHUNK = 256


K_ROWS = 16


def _attn_kernel(qt_ref, k_ref, vt_ref, o_ref, *scratch, tq):
    i = pl.program_id(1)
    n_chunks = tq // Q_CHUNK
    per_chain = 5
    chains = [(hh, c) + tuple(scratch[per_chain * (hh * n_chunks + c):per_chain * (hh * n_chunks + c + 1)])
              for hh in range(N_HEADS) for c in range(n_chunks)]
    for _, _, m_sc, l_sc, acc_sc, _, _ in chains:
        m_sc[...] = jnp.full(m_sc.shape, NEG_BIG, F32)
        l_sc[...] = jnp.zeros(l_sc.shape, F32)
        acc_sc[...] = jnp.zeros(acc_sc.shape, F32)

    def scores(chain, jk, buf):
        hh, c = chain[0], chain[1]
        kj = k_ref[0, hh, pl.ds(pl.multiple_of(jk * tq, tq), tq), :]
        chain[5 + buf][...] = _dot(kj, qt_ref[0, hh, :, c * Q_CHUNK:(c + 1) * Q_CHUNK])

    def softmax_values(chain, jk, buf, diagonal):
        hh, c, m_sc, l_sc, acc_sc = chain[:5]
        s_sc = chain[5 + buf]
        nk = (c + 1) * Q_CHUNK if diagonal else tq
        if diagonal:
            d0 = nk - Q_CHUNK
            key = lax.broadcasted_iota(jnp.int32, (Q_CHUNK, Q_CHUNK), 0)
            qry = lax.broadcasted_iota(jnp.int32, (Q_CHUNK, Q_CHUNK), 1)
            s_sc[d0:nk, :] = jnp.where(key <= qry, s_sc[d0:nk, :], NEG_BIG)
        blk_max = s_sc[0:K_ROWS, :]
        for r in range(K_ROWS, nk, K_ROWS):
            blk_max = jnp.maximum(blk_max, s_sc[r:r + K_ROWS, :])
        m_prev = m_sc[...]
        m_new = jnp.maximum(m_prev, jnp.max(blk_max, axis=0, keepdims=True))
        alpha = jnp.exp2(m_prev - m_new)
        p_sum = jnp.zeros((K_ROWS, Q_CHUNK), F32)
        acc = alpha * acc_sc[...]
        for kt in range(0, nk, Q_CHUNK):
            parts = []
            for r in range(kt, kt + Q_CHUNK, K_ROWS):
                p = jnp.exp2(s_sc[r:r + K_ROWS, :] - m_new)
                p_sum = p_sum + p
                parts.append(p.astype(BF16))
            acc = acc + _dot(vt_ref[0, hh, jk, :, kt:kt + Q_CHUNK], jnp.concatenate(parts, axis=0))
        l_sc[...] = alpha * l_sc[...] + jnp.sum(p_sum, axis=0, keepdims=True)
        acc_sc[...] = acc
        m_sc[...] = m_new

    def stage(cur, cur_buf, diagonal, nxt=None):
        for chain in chains:
            if nxt is not None:
                scores(chain, nxt, 1 - cur_buf)
            softmax_values(chain, cur, cur_buf, diagonal)

    for chain in chains:
        scores(chain, 0, 0)

    def pair(jj, carry):
        stage(2 * jj, 0, False, nxt=2 * jj + 1)
        stage(2 * jj + 1, 1, False, nxt=2 * jj + 2)
        return carry

    lax.fori_loop(0, i // 2, pair, 0)

    @pl.when(lax.rem(i, 2) == 0)
    def _():
        stage(i, 0, True)

    @pl.when(lax.rem(i, 2) == 1)
    def _():
        stage(i - 1, 0, False, nxt=i)
        stage(i, 1, True)

    for hh, c, _, l_sc, acc_sc, _, _ in chains:
        o_ref[0, c * Q_CHUNK:(c + 1) * Q_CHUNK, hh * V_HEAD:(hh + 1) * V_HEAD] = (
            jnp.transpose(acc_sc[...] / l_sc[...]).astype(BF16))


def _attention(qt, k, vt, tq):
    b, nh, s, _ = k.shape
    return pl.pallas_call(
        functools.partial(_attn_kernel, tq=tq),
        grid=(b, s // tq),
        in_specs=[
            pl.BlockSpec((1, nh, QK_PAD, tq), lambda bi, i: (bi, 0, 0, i)),
            pl.BlockSpec((1, nh, s, QK_PAD), lambda bi, i: (bi, 0, 0, 0)),
            pl.BlockSpec((1, nh, s // tq, V_HEAD, tq), lambda bi, i: (bi, 0, 0, 0, 0)),
        ],
        out_specs=pl.BlockSpec((1, tq, D_ATTN), lambda bi, i: (bi, i, 0)),
        out_shape=jax.ShapeDtypeStruct((b, s, D_ATTN), BF16),
        scratch_shapes=[
            pltpu.VMEM((1, Q_CHUNK), F32), pltpu.VMEM((1, Q_CHUNK), F32), pltpu.VMEM((V_HEAD, Q_CHUNK), F32),
            pltpu.VMEM((tq, Q_CHUNK), F32), pltpu.VMEM((tq, Q_CHUNK), F32),
        ] * (nh * (tq // Q_CHUNK)),
        compiler_params=_cparams(2),
        name="attention",
    )(qt, k, vt)


OUT_ROWS = 256


def _row_chunks(ts):
    return [slice(r, r + OUT_ROWS) for r in range(0, ts, OUT_ROWS)]


def _outproj_matmuls(attn_ref, cv_ref, wo_ref, ga_ref, rows):
    an = _rms(attn_ref[0, rows, :].astype(F32), ga_ref[...]).astype(BF16)
    return _dot(an, wo_ref[0:D_ATTN, :]) + _dot(cv_ref[0, rows, :], wo_ref[D_ATTN:D_MODEL, :])


def _outproj_norm(y, x_ref, mod_ref, lng_ref, lnb_ref, rows):
    x1 = _layer_norm(DEEPNORM_ALPHA * x_ref[0, rows, :] + mod_ref[0, 2:3, :] * y, lng_ref[...], lnb_ref[...])
    h2 = x1 * (1.0 + mod_ref[0, 4:5, :]) + mod_ref[0, 3:4, :]
    return x1, h2


def _outproj_kernel(attn_ref, cv_ref, x_ref, mod_ref, wo_ref, ga_ref, lng_ref, lnb_ref, x1_ref, *, ts):
    ys = [_outproj_matmuls(attn_ref, cv_ref, wo_ref, ga_ref, rows) for rows in _row_chunks(ts)]
    for rows, y in zip(_row_chunks(ts), ys):
        x1_ref[0, rows, :], _ = _outproj_norm(y, x_ref, mod_ref, lng_ref, lnb_ref, rows)


def _outproj_router_kernel(attn_ref, cv_ref, x_ref, mod_ref, wo_ref, ga_ref, lng_ref, lnb_ref, rw_ref, rb_ref,
                           x1_ref, h2_ref, rel_ref, gate_ref, relt_ref, base_ref, cnt_ref, run_sc, *, ts):
    first = jnp.logical_and(pl.program_id(0) == 0, pl.program_id(1) == 0)

    @pl.when(first)
    def _():
        run_sc[...] = jnp.zeros(run_sc.shape, F32)

    chunks = _row_chunks(ts)
    ys = [_outproj_matmuls(attn_ref, cv_ref, wo_ref, ga_ref, rows) for rows in chunks]
    lane = lax.broadcasted_iota(jnp.int32, (OUT_ROWS, LANES), 1)
    r_i = lax.broadcasted_iota(jnp.int32, (OUT_ROWS, OUT_ROWS), 0)
    c_i = lax.broadcasted_iota(jnp.int32, (OUT_ROWS, OUT_ROWS), 1)
    tri = jnp.where(c_i < r_i, 1.0, 0.0).astype(BF16)
    n_tile = jnp.zeros((1, LANES), F32)
    for rows, y in zip(chunks, ys):
        x1, h2 = _outproj_norm(y, x_ref, mod_ref, lng_ref, lnb_ref, rows)
        x1_ref[0, rows, :] = x1
        h2_hi = h2.astype(BF16)
        h2_ref[0, rows, :] = h2_hi

        h2_lo = (h2 - h2_hi.astype(F32)).astype(BF16)
        prod = _dot(h2_hi, rw_ref[...]) + _dot(h2_lo, rw_ref[...])
        logits = prod + pltpu.roll(prod, LANES - N_EXPERTS, 1) + rb_ref[...]
        logits = jnp.where(lane < N_EXPERTS, logits, NEG_BIG)
        v1 = jnp.max(logits, axis=-1, keepdims=True)
        i1 = jnp.min(jnp.where(logits == v1, lane, LANES), axis=-1, keepdims=True)
        rest = jnp.where(lane == i1, NEG_BIG, logits)
        v2 = jnp.max(rest, axis=-1, keepdims=True)
        i2 = jnp.min(jnp.where(rest == v2, lane, LANES), axis=-1, keepdims=True)
        e21 = jnp.exp(v2 - v1)
        g1 = 1.0 / (1.0 + e21)
        g2 = e21 / (1.0 + e21)
        is1 = lane == i1
        is2 = lane == i2
        sel = jnp.where(jnp.logical_or(is1, is2), 1.0, 0.0)
        gate_ref[rows, :] = jnp.where(is1, g1, 0.0) + jnp.where(is2, g2, 0.0)

        rank = _dot(tri, sel.astype(BF16)) + n_tile
        rel = jnp.where(sel > 0.0, rank, NOT_ROUTED)
        rel_ref[rows, :] = rel
        relt_ref[0, :, rows] = jnp.transpose(rel)[0:N_EXPERTS, :]
        n_tile = n_tile + jnp.sum(sel, axis=0, keepdims=True)

    base_ref[0] = run_sc[...].astype(jnp.int32)
    cnt_ref[0] = n_tile.astype(jnp.int32)
    run_sc[...] = run_sc[...] + n_tile


def _outproj(l, attn, convn, x, mod, w_o, g_attn, ln_g, ln_b, ts, router=None):
    b, s, _ = x.shape
    nsb = s // ts
    const = lambda shape, idx=l: pl.BlockSpec((None,) + shape, lambda bi, j: (idx,) + (0,) * len(shape))
    tile3 = lambda w: pl.BlockSpec((1, ts, w), lambda bi, j: (bi, j, 0))
    in_specs = [
        tile3(D_ATTN), tile3(D_CONV), tile3(D_MODEL),
        pl.BlockSpec((None, 1, 8, D_MODEL), lambda bi, j: (l, bi, 0, 0)),
        const((D_MODEL, D_MODEL)), const((1, D_ATTN)), const((1, D_MODEL)), const((1, D_MODEL)),
    ]
    out_specs = [tile3(D_MODEL), tile3(D_MODEL)]
    out_shape = [jax.ShapeDtypeStruct((b, s, D_MODEL), F32), jax.ShapeDtypeStruct((b, s, D_MODEL), BF16)]
    args = [attn, convn, x, mod, w_o, g_attn, ln_g, ln_b]
    if router is None:
        return pl.pallas_call(
            functools.partial(_outproj_kernel, ts=ts), grid=(b, nsb), in_specs=in_specs, out_specs=out_specs[0],
            out_shape=out_shape[0],
            compiler_params=_cparams(2), name="outproj",
        )(*args)
    moe_i, rw, rb = router
    nt = b * nsb
    flat = lambda w: pl.BlockSpec((ts, w), lambda bi, j: (bi * nsb + j, 0))
    per_tile = lambda r, w: pl.BlockSpec((1, r, w), lambda bi, j: (bi * nsb + j, 0, 0))
    in_specs += [const((D_MODEL, LANES), moe_i), const((1, LANES), moe_i)]
    out_specs += [flat(LANES), flat(LANES), per_tile(N_EXPERTS, ts), per_tile(1, LANES), per_tile(1, LANES)]
    out_shape += [
        jax.ShapeDtypeStruct((nt * ts, LANES), F32), jax.ShapeDtypeStruct((nt * ts, LANES), F32),
        jax.ShapeDtypeStruct((nt, N_EXPERTS, ts), F32),
        jax.ShapeDtypeStruct((nt, 1, LANES), jnp.int32), jax.ShapeDtypeStruct((nt, 1, LANES), jnp.int32),
    ]
    return pl.pallas_call(
        functools.partial(_outproj_router_kernel, ts=ts), grid=(b, nsb), in_specs=in_specs, out_specs=out_specs,
        out_shape=out_shape, scratch_shapes=[pltpu.VMEM((1, LANES), F32)],
        compiler_params=_cparams(2), name="outproj_router",
    )(*args, rw, rb)


FF_CHUNK = 256


def _swiglu(x, wg, wu, wd, width):
    out = None
    pending = None
    for c in range(0, width, FF_CHUNK):
        w = min(FF_CHUNK, width - c)
        g = _dot(x, wg(c, w))
        u = _dot(x, wu(c, w))
        if pending is not None:
            d = _dot(pending[0], wd(*pending[1]))
            out = d if out is None else out + d
        pending = ((_silu(g) * u).astype(BF16), (c, w))
    d = _dot(pending[0], wd(*pending[1]))
    return d if out is None else out + d


def _ffn_kernel(x1_ref, mod_ref, wg_ref, wu_ref, wd_ref, lng_ref, lnb_ref, o_ref):
    x1 = x1_ref[...]
    h = (x1 * (1.0 + mod_ref[0, 4:5, :]) + mod_ref[0, 3:4, :]).astype(BF16)
    f = _swiglu(h, lambda c, w: wg_ref[:, c:c + w], lambda c, w: wu_ref[:, c:c + w],
                lambda c, w: wd_ref[c:c + w, :], D_FF)
    gate2 = mod_ref[0, 5:6, :]
    o_ref[...] = _layer_norm(DEEPNORM_ALPHA * x1 + gate2 * f, lng_ref[...], lnb_ref[...])


def _dense_ffn(l, dense_i, x1, mod, wg, wu, wd, ln_g, ln_b, tm, tiles_per_batch):
    t = x1.shape[0]
    const = lambda shape, idx: pl.BlockSpec((None,) + shape, lambda i: (idx,) + (0,) * len(shape),
                                            pipeline_mode=pl.Buffered(1))
    return pl.pallas_call(
        _ffn_kernel,
        grid=(t // tm,),
        in_specs=[
            pl.BlockSpec((tm, D_MODEL), lambda i: (i, 0)),
            pl.BlockSpec((None, 1, 8, D_MODEL), lambda i: (l, i // tiles_per_batch, 0, 0)),
            const((D_MODEL, D_FF), dense_i), const((D_MODEL, D_FF), dense_i), const((D_FF, D_MODEL), dense_i),
            const((1, D_MODEL), l), const((1, D_MODEL), l),
        ],
        out_specs=pl.BlockSpec((tm, D_MODEL), lambda i: (i, 0)),
        out_shape=jax.ShapeDtypeStruct((t, D_MODEL), F32),
        compiler_params=_cparams(1),
        name="dense_ffn",
    )(x1, mod, wg, wu, wd, ln_g, ln_b)


def _group_copy(src, dst, sem):
    return pltpu.make_async_copy(src, dst, sem)


DISPATCH_CHUNK = 192


def _dispatch_chunks(ts):
    return -(-(ts + ROW_ALIGN - 1) // DISPATCH_CHUNK)


def _floor_align(v):
    return pl.multiple_of(lax.shift_left(lax.shift_right_logical(v, 4), 4), ROW_ALIGN)


def _dispatch_kernel(base_ref, cnt_ref, h_ref, relt_ref, xs_ref, stage, carry, zbuf, sems, xsem, zsem, *,
                     ts, zero_rows):
    i = pl.program_id(0)
    last = pl.num_programs(0) - 1
    par = lax.rem(i, 2)
    ch = DISPATCH_CHUNK
    n_chunks = _dispatch_chunks(ts)
    h = h_ref[...]
    slot = lax.broadcasted_iota(jnp.int32, (ch, ts), 0).astype(F32)

    def group(step, e):
        a = base_ref[step * N_EXPERTS + e]
        a16 = _floor_align(a)
        return a, a16, a - a16

    def onehot(e, c):
        _, _, off = group(i, e)
        row = relt_ref[0, e:e + 1, :] + off.astype(F32)
        return jnp.where(row == slot + float(c * ch), 1.0, 0.0).astype(BF16)

    def first_copy(step, e):
        _, a16, _ = group(step, e)
        p = lax.rem(step, 2)
        return _group_copy(stage.at[p, e, 0:ch], xs_ref.at[e, pl.ds(a16, ch), :], sems.at[p, e])

    @pl.when(i == 0)
    def _():
        carry[...] = jnp.zeros(carry.shape, BF16)
        stage[:, :, n_chunks * ch:, :] = jnp.zeros((2, N_EXPERTS, ROW_ALIGN, D_MODEL), BF16)

    for e in range(N_EXPERTS):
        rows = _dot(onehot(e, 0), h)
        stage[par, e, 0:ROW_ALIGN] = (rows[0:ROW_ALIGN] + carry[e].astype(F32)).astype(BF16)
        stage[par, e, ROW_ALIGN:ch] = rows[ROW_ALIGN:ch].astype(BF16)
        stage[par, e, ch:ch + ROW_ALIGN] = jnp.zeros((ROW_ALIGN, D_MODEL), BF16)

    @pl.when(i > 0)
    def _():
        for e in range(N_EXPERTS):
            first_copy(i - 1, e).wait()

    for e in range(N_EXPERTS):
        first_copy(i, e).start()

    for e in range(N_EXPERTS):
        _, a16, off = group(i, e)
        n = cnt_ref[i * N_EXPERTS + e]
        for c in range(1, n_chunks):
            @pl.when(off + n > c * ch)
            def _(e=e, c=c, a16=a16):
                lo = c * ch
                stage[par, e, lo:lo + ch] = _dot(onehot(e, c), h).astype(BF16)
                if c + 1 < n_chunks:
                    stage[par, e, lo + ch:lo + ch + ROW_ALIGN] = jnp.zeros((ROW_ALIGN, D_MODEL), BF16)
                cp = _group_copy(stage.at[par, e, lo:lo + ch], xs_ref.at[e, pl.ds(a16 + lo, ch), :], xsem)
                cp.start()
                cp.wait()

    for e in range(N_EXPERTS):
        a, a16, _ = group(i, e)
        nxt16 = _floor_align(a + cnt_ref[i * N_EXPERTS + e])
        carry[e] = stage[par, e, pl.ds(pl.multiple_of(nxt16 - a16, ROW_ALIGN), ROW_ALIGN), :]

    @pl.when(i == last)
    def _():
        for e in range(N_EXPERTS):
            first_copy(i, e).wait()
        zbuf[...] = jnp.zeros(zbuf.shape, BF16)
        for e in range(N_EXPERTS):
            n = cnt_ref[i * N_EXPERTS + e]
            end = _floor_align(base_ref[i * N_EXPERTS + e] + n + (ROW_ALIGN - 1))
            copies = [
                _group_copy(zbuf, xs_ref.at[e, pl.ds(end + r * SLOT_CHUNK, SLOT_CHUNK), :], zsem.at[e])
                for r in range(zero_rows // SLOT_CHUNK)
            ]
            for cp in copies:
                cp.start()
            for cp in copies:
                cp.wait()


def _dispatch(base, cnt, h2, relt, ts, cap, zero_rows):
    t = h2.shape[0]
    nt = t // ts
    n_chunks = _dispatch_chunks(ts)
    grid_spec = pltpu.PrefetchScalarGridSpec(
        num_scalar_prefetch=2,
        grid=(nt,),
        in_specs=[
            pl.BlockSpec((ts, D_MODEL), lambda i, b_, c_: (i, 0)),
            pl.BlockSpec((1, N_EXPERTS, ts), lambda i, b_, c_: (i, 0, 0)),
        ],
        out_specs=pl.BlockSpec(memory_space=pl.ANY),
        scratch_shapes=[
            pltpu.VMEM((2, N_EXPERTS, n_chunks * DISPATCH_CHUNK + ROW_ALIGN, D_MODEL), BF16),
            pltpu.VMEM((N_EXPERTS, ROW_ALIGN, D_MODEL), BF16),
            pltpu.VMEM((SLOT_CHUNK, D_MODEL), BF16),
            pltpu.SemaphoreType.DMA((2, N_EXPERTS)),
            pltpu.SemaphoreType.DMA(()),
            pltpu.SemaphoreType.DMA((N_EXPERTS,)),
        ],
    )
    return pl.pallas_call(
        functools.partial(_dispatch_kernel, ts=ts, zero_rows=zero_rows),
        grid_spec=grid_spec,
        out_shape=jax.ShapeDtypeStruct((N_EXPERTS, cap, D_MODEL), BF16),
        compiler_params=_cparams(1),
        name="moe_dispatch",
    )(base, cnt, h2, relt)


def _expert_kernel(te_ref, tr_ref, rows_ref, nv_ref, x_ref, wg_ref, wu_ref, wd_ref, o_ref, acc_sc):
    g_i = pl.program_id(0)
    k = pl.program_id(1)

    valid = g_i < nv_ref[0]

    @pl.when(jnp.logical_and(valid, k == 0))
    def _():
        acc_sc[...] = jnp.zeros(acc_sc.shape, F32)

    for m in EXPERT_ROW_SIZES:
        @pl.when(jnp.logical_and(valid, rows_ref[g_i] == m))
        def _(m=m):
            part = _swiglu(x_ref[0, 0:m, :], lambda c, w: wg_ref[0, :, c:c + w].astype(BF16),
                           lambda c, w: wu_ref[0, :, c:c + w].astype(BF16),
                           lambda c, w: wd_ref[0, c:c + w, :].astype(BF16), wg_ref.shape[2])
            total = acc_sc[0:m, :] + part
            acc_sc[0:m, :] = total
            o_ref[0, 0:m, :] = total.astype(BF16)


def _expert_ffn(moe_i, tile_e, tile_r, tile_rows, n_valid, xs, wg, wu, wd, tf, n_steps):
    tm = EXPERT_ROWS
    nk = D_FF_EXPERT // tf
    cap = xs.shape[1]

    def k_eff(g, k, nv):
        return jnp.where(g < nv[0], k, nk - 1)

    grid_spec = pltpu.PrefetchScalarGridSpec(
        num_scalar_prefetch=4,
        grid=(n_steps, nk),
        in_specs=[
            pl.BlockSpec((1, tm, D_MODEL), lambda g, k, te, tr, rw, nv: (te[g], tr[g], 0)),
            pl.BlockSpec((None, 1, D_MODEL, tf), lambda g, k, te, tr, rw, nv: (moe_i, te[g], 0, k_eff(g, k, nv))),
            pl.BlockSpec((None, 1, D_MODEL, tf), lambda g, k, te, tr, rw, nv: (moe_i, te[g], 0, k_eff(g, k, nv))),
            pl.BlockSpec((None, 1, tf, D_MODEL), lambda g, k, te, tr, rw, nv: (moe_i, te[g], k_eff(g, k, nv), 0)),
        ],
        out_specs=pl.BlockSpec((1, tm, D_MODEL), lambda g, k, te, tr, rw, nv: (te[g], tr[g], 0)),
        scratch_shapes=[pltpu.VMEM((tm, D_MODEL), F32)],
    )
    return pl.pallas_call(
        _expert_kernel,
        grid_spec=grid_spec,
        out_shape=jax.ShapeDtypeStruct((N_EXPERTS, cap, D_MODEL), BF16),
        compiler_params=_cparams(2),
        name="moe_experts",
    )(tile_e, tile_r, tile_rows, n_valid, xs, wg, wu, wd)


def _combine_kernel(base_ref, cnt_ref, lim_ref, ys_ref, rel_ref, gate_ref, x1_ref, mod_ref, lng_ref, lnb_ref, o_ref,
                    slab, xslab, acc_sc, sems, xsem, *, ts):
    i = pl.program_id(0)
    last = pl.num_programs(0) - 1
    par = lax.rem(i, 2)
    n_chunks = ts // SLOT_CHUNK
    slot = lax.broadcasted_iota(jnp.int32, (ts, SLOT_CHUNK), 1).astype(F32)

    def slab_start(step, e, c):
        want = _floor_align(base_ref[step * N_EXPERTS + e]) + c * SLOT_CHUNK
        start = jnp.maximum(jnp.minimum(want, lim_ref[e] - SLOT_CHUNK), 0)
        return pl.multiple_of(start, ROW_ALIGN), want - start

    def first_copy(step, e):
        start, _ = slab_start(step, e, 0)
        p = lax.rem(step, 2)
        return _group_copy(ys_ref.at[e, pl.ds(start, SLOT_CHUNK), :], slab.at[p, e], sems.at[p, e])

    def contribution(e, c, rows):
        a = base_ref[i * N_EXPERTS + e]
        _, moved = slab_start(i, e, c)
        pos = rel_ref[:, e:e + 1] + (a - _floor_align(a)).astype(F32)
        in_chunk = jnp.where(pos >= float(c * SLOT_CHUNK), 1.0, 0.0)
        onehot = jnp.where(pos + (moved - c * SLOT_CHUNK).astype(F32) == slot, in_chunk, 0.0).astype(BF16)
        return _dot(onehot, rows) * gate_ref[:, e:e + 1]

    @pl.when(i == 0)
    def _():
        for e in range(N_EXPERTS):
            first_copy(i, e).start()

    @pl.when(i < last)
    def _():
        for e in range(N_EXPERTS):
            first_copy(i + 1, e).start()

    for e in range(N_EXPERTS):
        first_copy(i, e).wait()
    f = contribution(0, 0, slab[par, 0])
    for e in range(1, N_EXPERTS):
        f = f + contribution(e, 0, slab[par, e])
    acc_sc[...] = f

    for e in range(N_EXPERTS):
        a = base_ref[i * N_EXPERTS + e]
        rows_in_block = a - _floor_align(a) + cnt_ref[i * N_EXPERTS + e]
        for c in range(1, n_chunks):
            @pl.when(rows_in_block > c * SLOT_CHUNK)
            def _(e=e, c=c):
                start, _ = slab_start(i, e, c)
                cp = _group_copy(ys_ref.at[e, pl.ds(start, SLOT_CHUNK), :], xslab, xsem)
                cp.start()
                cp.wait()
                acc_sc[...] += contribution(e, c, xslab[...])

    gate2 = mod_ref[0, 5:6, :]
    o_ref[...] = _layer_norm(DEEPNORM_ALPHA * x1_ref[...] + gate2 * acc_sc[...], lng_ref[...], lnb_ref[...])


def _combine(l, base, cnt, lim, ys, rel, gates, x1, mod, ln_g, ln_b, ts, tiles_per_batch):
    t = x1.shape[0]
    grid_spec = pltpu.PrefetchScalarGridSpec(
        num_scalar_prefetch=3,
        grid=(t // ts,),
        in_specs=[
            pl.BlockSpec(memory_space=pl.ANY),
            pl.BlockSpec((ts, LANES), lambda i, *_: (i, 0)),
            pl.BlockSpec((ts, LANES), lambda i, *_: (i, 0)),
            pl.BlockSpec((ts, D_MODEL), lambda i, *_: (i, 0)),
            pl.BlockSpec((None, 1, 8, D_MODEL), lambda i, *_: (l, i // tiles_per_batch, 0, 0)),
            pl.BlockSpec((None, 1, D_MODEL), lambda i, *_: (l, 0, 0)),
            pl.BlockSpec((None, 1, D_MODEL), lambda i, *_: (l, 0, 0)),
        ],
        out_specs=pl.BlockSpec((ts, D_MODEL), lambda i, *_: (i, 0)),
        scratch_shapes=[
            pltpu.VMEM((2, N_EXPERTS, SLOT_CHUNK, D_MODEL), BF16),
            pltpu.VMEM((SLOT_CHUNK, D_MODEL), BF16),
            pltpu.VMEM((ts, D_MODEL), F32),
            pltpu.SemaphoreType.DMA((2, N_EXPERTS)),
            pltpu.SemaphoreType.DMA(()),
        ],
    )
    return pl.pallas_call(
        functools.partial(_combine_kernel, ts=ts),
        grid_spec=grid_spec,
        out_shape=jax.ShapeDtypeStruct((t, D_MODEL), F32),
        compiler_params=_cparams(1),
        name="moe_combine",
    )(base, cnt, lim, ys, rel, gates, x1, mod, ln_g, ln_b)


def _expert_tile_plan(base, cnt, nt, n_steps):
    tm = EXPERT_ROWS
    i32 = lambda v: v.astype(jnp.int32)
    last_base = base.reshape(nt, N_EXPERTS)[-1]
    last_cnt = cnt.reshape(nt, N_EXPERTS)[-1]
    total = last_base + last_cnt
    tiles = jnp.maximum((total + tm - 1) // tm, 1)
    rest = total - (tiles - 1) * tm
    last_rows = jnp.full_like(rest, EXPERT_ROW_SIZES[0])
    for small, big in zip(EXPERT_ROW_SIZES[:-1], EXPERT_ROW_SIZES[1:]):
        last_rows = last_rows + jnp.where(rest > small, big - small, 0)
    ends = jnp.cumsum(tiles)
    n_valid = ends[-1]
    g = jnp.minimum(jnp.arange(n_steps, dtype=jnp.int32), n_valid - 1)
    tile_e = jnp.sum((g[:, None] >= ends[None, :]).astype(jnp.int32), axis=1)
    onehot = (tile_e[:, None] == jnp.arange(N_EXPERTS, dtype=jnp.int32)[None, :]).astype(jnp.int32)
    pick = lambda v: jnp.sum(onehot * v[None, :], axis=1)
    tile_r = g - pick(ends - tiles)
    tile_rows = jnp.where(tile_r == pick(tiles) - 1, pick(last_rows), tm)
    lim = (tiles - 1) * tm + last_rows
    return i32(tile_e), i32(tile_r), i32(tile_rows), i32(n_valid.reshape(1)), i32(lim)


def _top2_moe(l, moe_i, h2, x1, mod, routing, wg, wu, wd, ln_g, ln_b, ts, tiles_per_batch):
    rel, gates, relt, base3, cnt3 = routing
    t = h2.shape[0]
    nt = t // ts
    tm, tf = EXPERT_ROWS, 512
    base = base3[:, 0, :N_EXPERTS].reshape(-1)
    cnt = cnt3[:, 0, :N_EXPERTS].reshape(-1)
    zero_rows = -(-tm // SLOT_CHUNK) * SLOT_CHUNK
    reach_rows = _dispatch_chunks(ts) * DISPATCH_CHUNK
    cap = -(-(t + ROW_ALIGN + reach_rows + zero_rows) // tm) * tm
    n_steps = -(-TOP_K * t // tm) + N_EXPERTS
    xs = _dispatch(base, cnt, h2, relt, ts, cap, zero_rows)
    tile_e, tile_r, tile_rows, n_valid, lim = _expert_tile_plan(base, cnt, nt, n_steps)
    ys = _expert_ffn(moe_i, tile_e, tile_r, tile_rows, n_valid, xs, wg, wu, wd, tf, n_steps)
    return _combine(l, base, cnt, lim, ys, rel, gates, x1, mod, ln_g, ln_b, ts, tiles_per_batch)


def kernel(x, c, positions, w_in, q_norm_g, kv_norm_g, w_uq, w_ukv, conv_w, mix_norm_g, w_o, w_ada, b_ada,
           ln1_g, ln1_b, ln2_g, ln2_b, ffn_w_gate, ffn_w_up, ffn_w_down, moe_router_w, moe_router_b,
           moe_w_gate, moe_w_up, moe_w_down):
    b, s, _ = x.shape
    n_layers = w_in.shape[0]
    ts = min(512, s)
    t = b * s
    tiles_per_batch = s // ts

    bp = -(-b // 16) * 16
    c_pad = jnp.pad(c, ((0, bp - b), (0, 0)))
    mod = _ada_mod(c_pad, w_ada, b_ada)
    mod = jnp.pad(jnp.transpose(mod[:, :, :b, :], (0, 2, 1, 3)), ((0, 0), (0, 0), (0, 2), (0, 0)))
    rope_tab = _rope_tables(positions, ts)

    rows = lambda v: v.reshape(v.shape[0], 1, -1)
    o3 = Q_LORA + KV_LORA + QK_ROPE
    w_in_p = jnp.concatenate(
        [w_in[:, :, :o3], jnp.zeros((n_layers, D_MODEL, D_IN_PAD - w_in.shape[2]), F32), w_in[:, :, o3:]], axis=2
    ).astype(BF16)
    w_uq_t = jnp.transpose(jnp.pad(
        w_uq.reshape(n_layers, Q_LORA, N_HEADS, QK_NOPE + QK_ROPE),
        ((0, 0), (0, 0), (0, 0), (0, QK_PAD - QK_NOPE - QK_ROPE))
    ).reshape(n_layers, Q_LORA, N_HEADS * QK_PAD), (0, 2, 1)).astype(BF16)
    w_ukv_h = w_ukv.reshape(n_layers, KV_LORA, N_HEADS, QK_NOPE + V_HEAD)
    w_k = w_ukv_h[..., :QK_NOPE].reshape(n_layers, KV_LORA, N_HEADS * QK_NOPE).astype(BF16)
    w_v_t = jnp.transpose(w_ukv_h[..., QK_NOPE:].reshape(n_layers, KV_LORA, N_HEADS * V_HEAD), (0, 2, 1)).astype(BF16)
    w_o_b = w_o.astype(BF16)
    conv_w_p = jnp.pad(conv_w, ((0, 0), (0, 8 - CONV_WIDTH), (0, 0)))
    g_attn, g_conv = rows(mix_norm_g[:, :D_ATTN]), rows(mix_norm_g[:, D_ATTN:])
    ffn_wg, ffn_wu, ffn_wd = ffn_w_gate.astype(BF16), ffn_w_up.astype(BF16), ffn_w_down.astype(BF16)
    rw_hi = moe_router_w.astype(BF16)
    rw_lo = (moe_router_w - rw_hi.astype(F32)).astype(BF16)
    rw = jnp.pad(jnp.concatenate([rw_hi, rw_lo], axis=2), ((0, 0), (0, 0), (0, LANES - 2 * N_EXPERTS)))
    rb = rows(jnp.pad(moe_router_b, ((0, 0), (0, LANES - N_EXPERTS))))

    for l in range(n_layers):
        qt, k, vt, convn = _inproj(l, x, mod, w_in_p, w_uq_t, w_k, w_v_t, rows(q_norm_g), rows(kv_norm_g),
                                   conv_w_p, g_conv, rope_tab, ts)
        attn = _attention(qt, k, vt, ts)
        common = (l, attn, convn, x, mod, w_o_b, g_attn, rows(ln1_g), rows(ln1_b), ts)
        i = l // 2
        if l % 2 == 0:
            x1 = _outproj(*common)
            x = _dense_ffn(l, i, x1.reshape(t, D_MODEL), mod, ffn_wg, ffn_wu, ffn_wd, rows(ln2_g), rows(ln2_b),
                           ts, tiles_per_batch)
        else:
            x1, h2, *routing = _outproj(*common, router=(i, rw, rb))
            x = _top2_moe(l, i, h2.reshape(t, D_MODEL), x1.reshape(t, D_MODEL), mod, routing,
                          moe_w_gate, moe_w_up, moe_w_down, rows(ln2_g), rows(ln2_b), ts, tiles_per_batch)
        x = x.reshape(b, s, D_MODEL)
    return x
```

```python
import functools

import jax
import jax.numpy as jnp
from jax import lax
from jax.experimental import pallas as pl
from jax.experimental.pallas import tpu as pltpu

F32 = jnp.float32
BF16 = jnp.bfloat16

D_MODEL = 1024
N_HEADS = 4
QK_NOPE = 128
QK_ROPE = 64
V_HEAD = 128
Q_LORA = 256
KV_LORA = 128
D_ATTN = N_HEADS * V_HEAD
D_CONV = D_MODEL - D_ATTN
CONV_WIDTH = 3
ROPE_BASE = 10000.0
D_FF = 2816
N_EXPERTS = 8
TOP_K = 2
D_FF_EXPERT = 3584
RMS_EPS = 1e-6
LN_EPS = 1e-5
DEPTH = 2
DEEPNORM_ALPHA = (2 * DEPTH) ** 0.25
SM_SCALE = (QK_NOPE + QK_ROPE) ** -0.5
LOG2E = 1.4426950408889634
Q_SCALE = SM_SCALE * LOG2E

LANES = 128
QK_PAD = 2 * LANES
D_IN_PAD = 2048
ROW_ALIGN = 16
SLOT_CHUNK = 256
EXPERT_ROW_STEP = 384
EXPERT_ROW_SIZES = tuple(EXPERT_ROW_STEP * n for n in range(1, 7))
EXPERT_ROWS = EXPERT_ROW_SIZES[-1]
VMEM_LIMIT = 56 * 1024 * 1024
NEG_BIG = -1e30
NOT_ROUTED = -4096.0


def _cparams(n_axes):
    return pltpu.CompilerParams(dimension_semantics=("arbitrary",) * n_axes, vmem_limit_bytes=VMEM_LIMIT)


def _dot(a, b):
    return jnp.dot(a, b, preferred_element_type=F32)


def _rms(x, g):
    return x * lax.rsqrt(jnp.mean(x * x, axis=-1, keepdims=True) + RMS_EPS) * g


def _layer_norm(x, g, b):
    mu = jnp.mean(x, axis=-1, keepdims=True)
    xc = x - mu
    var = jnp.mean(xc * xc, axis=-1, keepdims=True)
    return xc * lax.rsqrt(var + LN_EPS) * g + b


def _silu(x):
    return x * jax.nn.sigmoid(x)


def _ada_kernel(c_ref, w_ref, b_ref, o_ref):
    ca = _silu(c_ref[...]).astype(BF16)
    o_ref[0, 0] = _dot(ca, w_ref[0].astype(BF16)) + b_ref[0]


def _ada_mod(c_pad, w_ada, b_ada):
    n_layers = w_ada.shape[0]
    bp = c_pad.shape[0]
    return pl.pallas_call(
        _ada_kernel,
        grid=(n_layers, 6),
        in_specs=[
            pl.BlockSpec((bp, D_MODEL), lambda l, j: (0, 0)),
            pl.BlockSpec((1, D_MODEL, D_MODEL), lambda l, j: (l, 0, j)),
            pl.BlockSpec((1, 1, D_MODEL), lambda l, j: (l, 0, j)),
        ],
        out_specs=pl.BlockSpec((1, 1, bp, D_MODEL), lambda l, j: (l, j, 0, 0)),
        out_shape=jax.ShapeDtypeStruct((n_layers, 6, bp, D_MODEL), F32),
        compiler_params=_cparams(2),
        name="ada_mod",
    )(c_pad, w_ada, b_ada.reshape(n_layers, 1, 6 * D_MODEL))


ROPE_HALF = QK_ROPE // 2


def _rope_kernel(pos_ref, f_ref, o_ref):
    ang = pos_ref[0].astype(F32) * f_ref[...]
    o_ref[0, 0:ROPE_HALF, :] = jnp.cos(ang)
    o_ref[0, ROPE_HALF:QK_ROPE, :] = jnp.sin(ang)


def _rope_tables(positions, ts):
    nt = positions.size // ts
    inv_freq = ROPE_BASE ** (-jnp.arange(0, QK_ROPE, 2, dtype=F32) / QK_ROPE)
    return pl.pallas_call(
        _rope_kernel,
        grid=(nt,),
        in_specs=[pl.BlockSpec((1, 1, ts), lambda i: (i, 0, 0)), pl.BlockSpec((ROPE_HALF, 1), lambda i: (0, 0))],
        out_specs=pl.BlockSpec((1, QK_ROPE, ts), lambda i: (i, 0, 0)),
        out_shape=jax.ShapeDtypeStruct((nt, QK_ROPE, ts), F32),
        compiler_params=_cparams(1),
        name="rope_tables",
    )(positions.reshape(nt, 1, ts), inv_freq.reshape(ROPE_HALF, 1))


def _inproj_kernel(x_ref, mod_ref, win_ref, wuqt_ref, wk_ref, wvt_ref, qg_ref, kvg_ref, cw_ref, cg_ref, rope_ref,
                   qt_ref, k_ref, vt_ref, cv_ref, cu_ext, *, ts):
    j = pl.program_id(1)

    @pl.when(j == 0)
    def _():
        cu_ext[0:8, :] = jnp.zeros((8, D_CONV), F32)

    @pl.when(j > 0)
    def _():
        cu_ext[0:8, :] = cu_ext[ts:ts + 8, :]

    x = x_ref[0]
    shift = mod_ref[0, 0:1, :]
    scale = mod_ref[0, 1:2, :]
    h = (x * (1.0 + scale) + shift).astype(BF16)
    z_lat = _dot(h, win_ref[:, 0:512])
    c_q = z_lat[:, 0:Q_LORA]
    c_kv = z_lat[:, Q_LORA:Q_LORA + KV_LORA]
    k_rope = z_lat[:, 384:512]
    z_gate = _dot(h, win_ref[:, 512:1536])
    gate_b = z_gate[:, 0:D_CONV]
    gate_c = z_gate[:, D_CONV:2 * D_CONV]
    u = _dot(h, win_ref[:, 1536:2048])
    cqn = _rms(c_q, qg_ref[...])
    ckvn = _rms(c_kv, kvg_ref[...])
    q_t = _dot(wuqt_ref[...], jnp.transpose(cqn).astype(BF16))
    k_nope = _dot(ckvn.astype(BF16), wk_ref[...])
    v_t = _dot(wvt_ref[...], jnp.transpose(ckvn).astype(BF16))

    cos = rope_ref[0, 0:ROPE_HALF, :]
    sin = rope_ref[0, ROPE_HALF:QK_ROPE, :]

    def rope_t(block):
        x1, x2 = block[0:ROPE_HALF, :], block[ROPE_HALF:QK_ROPE, :]
        return x1 * cos - x2 * sin, x2 * cos + x1 * sin

    k1, k2 = rope_t(jnp.transpose(k_rope)[0:QK_ROPE, :])
    k_rot = jnp.transpose(jnp.concatenate([k1, k2, jnp.zeros((LANES - QK_ROPE, ts), F32)], axis=0)).astype(BF16)
    for hh in range(N_HEADS):
        qo = hh * QK_PAD
        r0 = qo + QK_NOPE
        q1, q2 = rope_t(q_t[r0:r0 + QK_ROPE, :])
        qt_ref[0, hh, 0:QK_NOPE, :] = (q_t[qo:r0, :] * Q_SCALE).astype(BF16)
        qt_ref[0, hh, QK_NOPE:QK_NOPE + ROPE_HALF, :] = (q1 * Q_SCALE).astype(BF16)
        qt_ref[0, hh, QK_NOPE + ROPE_HALF:QK_NOPE + QK_ROPE, :] = (q2 * Q_SCALE).astype(BF16)
        qt_ref[0, hh, QK_NOPE + QK_ROPE:QK_PAD, :] = jnp.zeros((QK_PAD - QK_NOPE - QK_ROPE, ts), BF16)
        k_ref[0, hh, :, 0:LANES] = k_nope[:, hh * QK_NOPE:(hh + 1) * QK_NOPE].astype(BF16)
        k_ref[0, hh, :, LANES:QK_PAD] = k_rot
        vt_ref[0, hh, 0] = v_t[hh * V_HEAD:(hh + 1) * V_HEAD, :].astype(BF16)

    cu_ext[8:ts + 8, :] = gate_c * u
    conv = (cw_ref[2:3, :] * cu_ext[8:ts + 8, :]
            + cw_ref[1:2, :] * cu_ext[pl.ds(7, ts), :]
            + cw_ref[0:1, :] * cu_ext[pl.ds(6, ts), :])
    conv = gate_b * conv
    cv_ref[0] = _rms(conv, cg_ref[...]).astype(BF16)


def _inproj(l, x, mod, w_in_p, w_uq_t, w_k, w_v_t, qg, kvg, conv_w, conv_g, rope_tab, ts):
    b, s, _ = x.shape
    nsb = s // ts
    const = lambda shape: pl.BlockSpec((None,) + shape, lambda bi, j: (l,) + (0,) * len(shape))
    return pl.pallas_call(
        functools.partial(_inproj_kernel, ts=ts),
        grid=(b, nsb),
        in_specs=[
            pl.BlockSpec((1, ts, D_MODEL), lambda bi, j: (bi, j, 0)),
            pl.BlockSpec((None, 1, 8, D_MODEL), lambda bi, j: (l, bi, 0, 0)),
            const((D_MODEL, D_IN_PAD)),
            const((N_HEADS * QK_PAD, Q_LORA)),
            const((KV_LORA, N_HEADS * QK_NOPE)),
            const((N_HEADS * V_HEAD, KV_LORA)),
            const((1, Q_LORA)),
            const((1, KV_LORA)),
            const((8, D_CONV)),
            const((1, D_CONV)),
            pl.BlockSpec((1, QK_ROPE, ts), lambda bi, j: (bi * nsb + j, 0, 0)),
        ],
        out_specs=[
            pl.BlockSpec((1, N_HEADS, QK_PAD, ts), lambda bi, j: (bi, 0, 0, j)),
            pl.BlockSpec((1, N_HEADS, ts, QK_PAD), lambda bi, j: (bi, 0, j, 0)),
            pl.BlockSpec((1, N_HEADS, 1, V_HEAD, ts), lambda bi, j: (bi, 0, j, 0, 0)),
            pl.BlockSpec((1, ts, D_CONV), lambda bi, j: (bi, j, 0)),
        ],
        out_shape=[
            jax.ShapeDtypeStruct((b, N_HEADS, QK_PAD, s), BF16),
            jax.ShapeDtypeStruct((b, N_HEADS, s, QK_PAD), BF16),
            jax.ShapeDtypeStruct((b, N_HEADS, nsb, V_HEAD, ts), BF16),
            jax.ShapeDtypeStruct((b, s, D_CONV), BF16),
        ],
        scratch_shapes=[pltpu.VMEM((ts + 8, D_CONV), F32)],
        compiler_params=_cparams(2),
        name="inproj",
    )(x, mod, w_in_p, w_uq_t, w_k, w_v_t, qg, kvg, conv_w, conv_g, rope_tab)


Q_CHUNK = 256


K_ROWS = 16


def _attn_kernel(qt_ref, k_ref, vt_ref, o_ref, *scratch, tq):
    i = pl.program_id(1)
    n_chunks = tq // Q_CHUNK
    per_chain = 5
    chains = [(hh, c) + tuple(scratch[per_chain * (hh * n_chunks + c):per_chain * (hh * n_chunks + c + 1)])
              for hh in range(N_HEADS) for c in range(n_chunks)]
    for _, _, m_sc, l_sc, acc_sc, _, _ in chains:
        m_sc[...] = jnp.full(m_sc.shape, NEG_BIG, F32)
        l_sc[...] = jnp.zeros(l_sc.shape, F32)
        acc_sc[...] = jnp.zeros(acc_sc.shape, F32)

    def scores(chain, jk, buf):
        hh, c = chain[0], chain[1]
        kj = k_ref[0, hh, pl.ds(pl.multiple_of(jk * tq, tq), tq), :]
        chain[5 + buf][...] = _dot(kj, qt_ref[0, hh, :, c * Q_CHUNK:(c + 1) * Q_CHUNK])

    def softmax_values(chain, jk, buf, diagonal):
        hh, c, m_sc, l_sc, acc_sc = chain[:5]
        s_sc = chain[5 + buf]
        nk = (c + 1) * Q_CHUNK if diagonal else tq
        if diagonal:
            d0 = nk - Q_CHUNK
            key = lax.broadcasted_iota(jnp.int32, (Q_CHUNK, Q_CHUNK), 0)
            qry = lax.broadcasted_iota(jnp.int32, (Q_CHUNK, Q_CHUNK), 1)
            s_sc[d0:nk, :] = jnp.where(key <= qry, s_sc[d0:nk, :], NEG_BIG)
        blk_max = s_sc[0:K_ROWS, :]
        for r in range(K_ROWS, nk, K_ROWS):
            blk_max = jnp.maximum(blk_max, s_sc[r:r + K_ROWS, :])
        m_prev = m_sc[...]
        m_new = jnp.maximum(m_prev, jnp.max(blk_max, axis=0, keepdims=True))
        alpha = jnp.exp2(m_prev - m_new)
        p_sum = jnp.zeros((K_ROWS, Q_CHUNK), F32)
        acc = alpha * acc_sc[...]
        for kt in range(0, nk, Q_CHUNK):
            parts = []
            for r in range(kt, kt + Q_CHUNK, K_ROWS):
                p = jnp.exp2(s_sc[r:r + K_ROWS, :] - m_new)
                p_sum = p_sum + p
                parts.append(p.astype(BF16))
            acc = acc + _dot(vt_ref[0, hh, jk, :, kt:kt + Q_CHUNK], jnp.concatenate(parts, axis=0))
        l_sc[...] = alpha * l_sc[...] + jnp.sum(p_sum, axis=0, keepdims=True)
        acc_sc[...] = acc
        m_sc[...] = m_new

    def stage(cur, cur_buf, diagonal, nxt=None):
        for chain in chains:
            if nxt is not None:
                scores(chain, nxt, 1 - cur_buf)
            softmax_values(chain, cur, cur_buf, diagonal)

    for chain in chains:
        scores(chain, 0, 0)

    def pair(jj, carry):
        stage(2 * jj, 0, False, nxt=2 * jj + 1)
        stage(2 * jj + 1, 1, False, nxt=2 * jj + 2)
        return carry

    lax.fori_loop(0, i // 2, pair, 0)

    @pl.when(lax.rem(i, 2) == 0)
    def _():
        stage(i, 0, True)

    @pl.when(lax.rem(i, 2) == 1)
    def _():
        stage(i - 1, 0, False, nxt=i)
        stage(i, 1, True)

    for hh, c, _, l_sc, acc_sc, _, _ in chains:
        o_ref[0, c * Q_CHUNK:(c + 1) * Q_CHUNK, hh * V_HEAD:(hh + 1) * V_HEAD] = (
            jnp.transpose(acc_sc[...] / l_sc[...]).astype(BF16))


def _attention(qt, k, vt, tq):
    b, nh, s, _ = k.shape
    return pl.pallas_call(
        functools.partial(_attn_kernel, tq=tq),
        grid=(b, s // tq),
        in_specs=[
            pl.BlockSpec((1, nh, QK_PAD, tq), lambda bi, i: (bi, 0, 0, i)),
            pl.BlockSpec((1, nh, s, QK_PAD), lambda bi, i: (bi, 0, 0, 0)),
            pl.BlockSpec((1, nh, s // tq, V_HEAD, tq), lambda bi, i: (bi, 0, 0, 0, 0)),
        ],
        out_specs=pl.BlockSpec((1, tq, D_ATTN), lambda bi, i: (bi, i, 0)),
        out_shape=jax.ShapeDtypeStruct((b, s, D_ATTN), BF16),
        scratch_shapes=[
            pltpu.VMEM((1, Q_CHUNK), F32), pltpu.VMEM((1, Q_CHUNK), F32), pltpu.VMEM((V_HEAD, Q_CHUNK), F32),
            pltpu.VMEM((tq, Q_CHUNK), F32), pltpu.VMEM((tq, Q_CHUNK), F32),
        ] * (nh * (tq // Q_CHUNK)),
        compiler_params=_cparams(2),
        name="attention",
    )(qt, k, vt)


OUT_ROWS = 256


def _row_chunks(ts):
    return [slice(r, r + OUT_ROWS) for r in range(0, ts, OUT_ROWS)]


def _outproj_matmuls(attn_ref, cv_ref, wo_ref, ga_ref, rows):
    an = _rms(attn_ref[0, rows, :].astype(F32), ga_ref[...]).astype(BF16)
    return _dot(an, wo_ref[0:D_ATTN, :]) + _dot(cv_ref[0, rows, :], wo_ref[D_ATTN:D_MODEL, :])


def _outproj_norm(y, x_ref, mod_ref, lng_ref, lnb_ref, rows):
    x1 = _layer_norm(DEEPNORM_ALPHA * x_ref[0, rows, :] + mod_ref[0, 2:3, :] * y, lng_ref[...], lnb_ref[...])
    h2 = x1 * (1.0 + mod_ref[0, 4:5, :]) + mod_ref[0, 3:4, :]
    return x1, h2


def _outproj_kernel(attn_ref, cv_ref, x_ref, mod_ref, wo_ref, ga_ref, lng_ref, lnb_ref, x1_ref, *, ts):
    ys = [_outproj_matmuls(attn_ref, cv_ref, wo_ref, ga_ref, rows) for rows in _row_chunks(ts)]
    for rows, y in zip(_row_chunks(ts), ys):
        x1_ref[0, rows, :], _ = _outproj_norm(y, x_ref, mod_ref, lng_ref, lnb_ref, rows)


def _outproj_router_kernel(attn_ref, cv_ref, x_ref, mod_ref, wo_ref, ga_ref, lng_ref, lnb_ref, rw_ref, rb_ref,
                           x1_ref, h2_ref, rel_ref, gate_ref, relt_ref, base_ref, cnt_ref, run_sc, *, ts):
    first = jnp.logical_and(pl.program_id(0) == 0, pl.program_id(1) == 0)

    @pl.when(first)
    def _():
        run_sc[...] = jnp.zeros(run_sc.shape, F32)

    chunks = _row_chunks(ts)
    ys = [_outproj_matmuls(attn_ref, cv_ref, wo_ref, ga_ref, rows) for rows in chunks]
    lane = lax.broadcasted_iota(jnp.int32, (OUT_ROWS, LANES), 1)
    r_i = lax.broadcasted_iota(jnp.int32, (OUT_ROWS, OUT_ROWS), 0)
    c_i = lax.broadcasted_iota(jnp.int32, (OUT_ROWS, OUT_ROWS), 1)
    tri = jnp.where(c_i < r_i, 1.0, 0.0).astype(BF16)
    n_tile = jnp.zeros((1, LANES), F32)
    for rows, y in zip(chunks, ys):
        x1, h2 = _outproj_norm(y, x_ref, mod_ref, lng_ref, lnb_ref, rows)
        x1_ref[0, rows, :] = x1
        h2_hi = h2.astype(BF16)
        h2_ref[0, rows, :] = h2_hi

        h2_lo = (h2 - h2_hi.astype(F32)).astype(BF16)
        prod = _dot(h2_hi, rw_ref[...]) + _dot(h2_lo, rw_ref[...])
        logits = prod + pltpu.roll(prod, LANES - N_EXPERTS, 1) + rb_ref[...]
        logits = jnp.where(lane < N_EXPERTS, logits, NEG_BIG)
        v1 = jnp.max(logits, axis=-1, keepdims=True)
        i1 = jnp.min(jnp.where(logits == v1, lane, LANES), axis=-1, keepdims=True)
        rest = jnp.where(lane == i1, NEG_BIG, logits)
        v2 = jnp.max(rest, axis=-1, keepdims=True)
        i2 = jnp.min(jnp.where(rest == v2, lane, LANES), axis=-1, keepdims=True)
        e21 = jnp.exp(v2 - v1)
        g1 = 1.0 / (1.0 + e21)
        g2 = e21 / (1.0 + e21)
        is1 = lane == i1
        is2 = lane == i2
        sel = jnp.where(jnp.logical_or(is1, is2), 1.0, 0.0)
        gate_ref[rows, :] = jnp.where(is1, g1, 0.0) + jnp.where(is2, g2, 0.0)

        rank = _dot(tri, sel.astype(BF16)) + n_tile
        rel = jnp.where(sel > 0.0, rank, NOT_ROUTED)
        rel_ref[rows, :] = rel
        relt_ref[0, :, rows] = jnp.transpose(rel)[0:N_EXPERTS, :]
        n_tile = n_tile + jnp.sum(sel, axis=0, keepdims=True)

    base_ref[0] = run_sc[...].astype(jnp.int32)
    cnt_ref[0] = n_tile.astype(jnp.int32)
    run_sc[...] = run_sc[...] + n_tile


def _outproj(l, attn, convn, x, mod, w_o, g_attn, ln_g, ln_b, ts, router=None):
    b, s, _ = x.shape
    nsb = s // ts
    const = lambda shape, idx=l: pl.BlockSpec((None,) + shape, lambda bi, j: (idx,) + (0,) * len(shape))
    tile3 = lambda w: pl.BlockSpec((1, ts, w), lambda bi, j: (bi, j, 0))
    in_specs = [
        tile3(D_ATTN), tile3(D_CONV), tile3(D_MODEL),
        pl.BlockSpec((None, 1, 8, D_MODEL), lambda bi, j: (l, bi, 0, 0)),
        const((D_MODEL, D_MODEL)), const((1, D_ATTN)), const((1, D_MODEL)), const((1, D_MODEL)),
    ]
    out_specs = [tile3(D_MODEL), tile3(D_MODEL)]
    out_shape = [jax.ShapeDtypeStruct((b, s, D_MODEL), F32), jax.ShapeDtypeStruct((b, s, D_MODEL), BF16)]
    args = [attn, convn, x, mod, w_o, g_attn, ln_g, ln_b]
    if router is None:
        return pl.pallas_call(
            functools.partial(_outproj_kernel, ts=ts), grid=(b, nsb), in_specs=in_specs, out_specs=out_specs[0],
            out_shape=out_shape[0],
            compiler_params=_cparams(2), name="outproj",
        )(*args)
    moe_i, rw, rb = router
    nt = b * nsb
    flat = lambda w: pl.BlockSpec((ts, w), lambda bi, j: (bi * nsb + j, 0))
    per_tile = lambda r, w: pl.BlockSpec((1, r, w), lambda bi, j: (bi * nsb + j, 0, 0))
    in_specs += [const((D_MODEL, LANES), moe_i), const((1, LANES), moe_i)]
    out_specs += [flat(LANES), flat(LANES), per_tile(N_EXPERTS, ts), per_tile(1, LANES), per_tile(1, LANES)]
    out_shape += [
        jax.ShapeDtypeStruct((nt * ts, LANES), F32), jax.ShapeDtypeStruct((nt * ts, LANES), F32),
        jax.ShapeDtypeStruct((nt, N_EXPERTS, ts), F32),
        jax.ShapeDtypeStruct((nt, 1, LANES), jnp.int32), jax.ShapeDtypeStruct((nt, 1, LANES), jnp.int32),
    ]
    return pl.pallas_call(
        functools.partial(_outproj_router_kernel, ts=ts), grid=(b, nsb), in_specs=in_specs, out_specs=out_specs,
        out_shape=out_shape, scratch_shapes=[pltpu.VMEM((1, LANES), F32)],
        compiler_params=_cparams(2), name="outproj_router",
    )(*args, rw, rb)


FF_CHUNK = 256


def _swiglu(x, wg, wu, wd, width):
    out = None
    pending = None
    for c in range(0, width, FF_CHUNK):
        w = min(FF_CHUNK, width - c)
        g = _dot(x, wg(c, w))
        u = _dot(x, wu(c, w))
        if pending is not None:
            d = _dot(pending[0], wd(*pending[1]))
            out = d if out is None else out + d
        pending = ((_silu(g) * u).astype(BF16), (c, w))
    d = _dot(pending[0], wd(*pending[1]))
    return d if out is None else out + d


def _ffn_kernel(x1_ref, mod_ref, wg_ref, wu_ref, wd_ref, lng_ref, lnb_ref, o_ref):
    x1 = x1_ref[...]
    h = (x1 * (1.0 + mod_ref[0, 4:5, :]) + mod_ref[0, 3:4, :]).astype(BF16)
    f = _swiglu(h, lambda c, w: wg_ref[:, c:c + w], lambda c, w: wu_ref[:, c:c + w],
                lambda c, w: wd_ref[c:c + w, :], D_FF)
    gate2 = mod_ref[0, 5:6, :]
    o_ref[...] = _layer_norm(DEEPNORM_ALPHA * x1 + gate2 * f, lng_ref[...], lnb_ref[...])


def _dense_ffn(l, dense_i, x1, mod, wg, wu, wd, ln_g, ln_b, tm, tiles_per_batch):
    t = x1.shape[0]
    const = lambda shape, idx: pl.BlockSpec((None,) + shape, lambda i: (idx,) + (0,) * len(shape),
                                            pipeline_mode=pl.Buffered(1))
    return pl.pallas_call(
        _ffn_kernel,
        grid=(t // tm,),
        in_specs=[
            pl.BlockSpec((tm, D_MODEL), lambda i: (i, 0)),
            pl.BlockSpec((None, 1, 8, D_MODEL), lambda i: (l, i // tiles_per_batch, 0, 0)),
            const((D_MODEL, D_FF), dense_i), const((D_MODEL, D_FF), dense_i), const((D_FF, D_MODEL), dense_i),
            const((1, D_MODEL), l), const((1, D_MODEL), l),
        ],
        out_specs=pl.BlockSpec((tm, D_MODEL), lambda i: (i, 0)),
        out_shape=jax.ShapeDtypeStruct((t, D_MODEL), F32),
        compiler_params=_cparams(1),
        name="dense_ffn",
    )(x1, mod, wg, wu, wd, ln_g, ln_b)


def _group_copy(src, dst, sem):
    return pltpu.make_async_copy(src, dst, sem)


DISPATCH_CHUNK = 256


def _dispatch_chunks(ts):
    return -(-(ts + ROW_ALIGN - 1) // DISPATCH_CHUNK)


def _floor_align(v):
    return pl.multiple_of(lax.shift_left(lax.shift_right_logical(v, 4), 4), ROW_ALIGN)


def _dispatch_kernel(base_ref, cnt_ref, h_ref, relt_ref, xs_ref, stage, carry, zbuf, sems, xsem, zsem, *,
                     ts, zero_rows):
    i = pl.program_id(0)
    last = pl.num_programs(0) - 1
    par = lax.rem(i, 2)
    ch = DISPATCH_CHUNK
    n_chunks = _dispatch_chunks(ts)
    h = h_ref[...]
    slot = lax.broadcasted_iota(jnp.int32, (ch, ts), 0).astype(F32)

    def group(step, e):
        a = base_ref[step * N_EXPERTS + e]
        a16 = _floor_align(a)
        return a, a16, a - a16

    def onehot(e, c):
        _, _, off = group(i, e)
        row = relt_ref[0, e:e + 1, :] + off.astype(F32)
        return jnp.where(row == slot + float(c * ch), 1.0, 0.0).astype(BF16)

    def first_copy(step, e):
        _, a16, _ = group(step, e)
        p = lax.rem(step, 2)
        return _group_copy(stage.at[p, e, 0:ch], xs_ref.at[e, pl.ds(a16, ch), :], sems.at[p, e])

    @pl.when(i == 0)
    def _():
        carry[...] = jnp.zeros(carry.shape, BF16)
        stage[:, :, n_chunks * ch:, :] = jnp.zeros((2, N_EXPERTS, ROW_ALIGN, D_MODEL), BF16)

    for e in range(N_EXPERTS):
        rows = _dot(onehot(e, 0), h)
        stage[par, e, 0:ROW_ALIGN] = (rows[0:ROW_ALIGN] + carry[e].astype(F32)).astype(BF16)
        stage[par, e, ROW_ALIGN:ch] = rows[ROW_ALIGN:ch].astype(BF16)
        stage[par, e, ch:ch + ROW_ALIGN] = jnp.zeros((ROW_ALIGN, D_MODEL), BF16)

    @pl.when(i > 0)
    def _():
        for e in range(N_EXPERTS):
            first_copy(i - 1, e).wait()

    for e in range(N_EXPERTS):
        first_copy(i, e).start()

    for e in range(N_EXPERTS):
        _, a16, off = group(i, e)
        n = cnt_ref[i * N_EXPERTS + e]
        for c in range(1, n_chunks):
            @pl.when(off + n > c * ch)
            def _(e=e, c=c, a16=a16):
                lo = c * ch
                stage[par, e, lo:lo + ch] = _dot(onehot(e, c), h).astype(BF16)
                if c + 1 < n_chunks:
                    stage[par, e, lo + ch:lo + ch + ROW_ALIGN] = jnp.zeros((ROW_ALIGN, D_MODEL), BF16)
                cp = _group_copy(stage.at[par, e, lo:lo + ch], xs_ref.at[e, pl.ds(a16 + lo, ch), :], xsem)
                cp.start()
                cp.wait()

    for e in range(N_EXPERTS):
        a, a16, _ = group(i, e)
        nxt16 = _floor_align(a + cnt_ref[i * N_EXPERTS + e])
        carry[e] = stage[par, e, pl.ds(pl.multiple_of(nxt16 - a16, ROW_ALIGN), ROW_ALIGN), :]

    @pl.when(i == last)
    def _():
        for e in range(N_EXPERTS):
            first_copy(i, e).wait()
        zbuf[...] = jnp.zeros(zbuf.shape, BF16)
        for e in range(N_EXPERTS):
            n = cnt_ref[i * N_EXPERTS + e]
            end = _floor_align(base_ref[i * N_EXPERTS + e] + n + (ROW_ALIGN - 1))
            copies = [
                _group_copy(zbuf, xs_ref.at[e, pl.ds(end + r * SLOT_CHUNK, SLOT_CHUNK), :], zsem.at[e])
                for r in range(zero_rows // SLOT_CHUNK)
            ]
            for cp in copies:
                cp.start()
            for cp in copies:
                cp.wait()


def _dispatch(base, cnt, h2, relt, ts, cap, zero_rows):
    t = h2.shape[0]
    nt = t // ts
    n_chunks = _dispatch_chunks(ts)
    grid_spec = pltpu.PrefetchScalarGridSpec(
        num_scalar_prefetch=2,
        grid=(nt,),
        in_specs=[
            pl.BlockSpec((ts, D_MODEL), lambda i, b_, c_: (i, 0)),
            pl.BlockSpec((1, N_EXPERTS, ts), lambda i, b_, c_: (i, 0, 0)),
        ],
        out_specs=pl.BlockSpec(memory_space=pl.ANY),
        scratch_shapes=[
            pltpu.VMEM((2, N_EXPERTS, n_chunks * DISPATCH_CHUNK + ROW_ALIGN, D_MODEL), BF16),
            pltpu.VMEM((N_EXPERTS, ROW_ALIGN, D_MODEL), BF16),
            pltpu.VMEM((SLOT_CHUNK, D_MODEL), BF16),
            pltpu.SemaphoreType.DMA((2, N_EXPERTS)),
            pltpu.SemaphoreType.DMA(()),
            pltpu.SemaphoreType.DMA((N_EXPERTS,)),
        ],
    )
    return pl.pallas_call(
        functools.partial(_dispatch_kernel, ts=ts, zero_rows=zero_rows),
        grid_spec=grid_spec,
        out_shape=jax.ShapeDtypeStruct((N_EXPERTS, cap, D_MODEL), BF16),
        compiler_params=_cparams(1),
        name="moe_dispatch",
    )(base, cnt, h2, relt)


def _expert_kernel(te_ref, tr_ref, rows_ref, nv_ref, x_ref, wg_ref, wu_ref, wd_ref, o_ref, acc_sc):
    g_i = pl.program_id(0)
    k = pl.program_id(1)

    valid = g_i < nv_ref[0]

    @pl.when(jnp.logical_and(valid, k == 0))
    def _():
        acc_sc[...] = jnp.zeros(acc_sc.shape, F32)

    for m in EXPERT_ROW_SIZES:
        @pl.when(jnp.logical_and(valid, rows_ref[g_i] == m))
        def _(m=m):
            part = _swiglu(x_ref[0, 0:m, :], lambda c, w: wg_ref[0, :, c:c + w].astype(BF16),
                           lambda c, w: wu_ref[0, :, c:c + w].astype(BF16),
                           lambda c, w: wd_ref[0, c:c + w, :].astype(BF16), wg_ref.shape[2])
            total = acc_sc[0:m, :] + part
            acc_sc[0:m, :] = total
            o_ref[0, 0:m, :] = total.astype(BF16)


def _expert_ffn(moe_i, tile_e, tile_r, tile_rows, n_valid, xs, wg, wu, wd, tf, n_steps):
    tm = EXPERT_ROWS
    nk = D_FF_EXPERT // tf
    cap = xs.shape[1]

    def k_eff(g, k, nv):
        return jnp.where(g < nv[0], k, nk - 1)

    grid_spec = pltpu.PrefetchScalarGridSpec(
        num_scalar_prefetch=4,
        grid=(n_steps, nk),
        in_specs=[
            pl.BlockSpec((1, tm, D_MODEL), lambda g, k, te, tr, rw, nv: (te[g], tr[g], 0)),
            pl.BlockSpec((None, 1, D_MODEL, tf), lambda g, k, te, tr, rw, nv: (moe_i, te[g], 0, k_eff(g, k, nv))),
            pl.BlockSpec((None, 1, D_MODEL, tf), lambda g, k, te, tr, rw, nv: (moe_i, te[g], 0, k_eff(g, k, nv))),
            pl.BlockSpec((None, 1, tf, D_MODEL), lambda g, k, te, tr, rw, nv: (moe_i, te[g], k_eff(g, k, nv), 0)),
        ],
        out_specs=pl.BlockSpec((1, tm, D_MODEL), lambda g, k, te, tr, rw, nv: (te[g], tr[g], 0)),
        scratch_shapes=[pltpu.VMEM((tm, D_MODEL), F32)],
    )
    return pl.pallas_call(
        _expert_kernel,
        grid_spec=grid_spec,
        out_shape=jax.ShapeDtypeStruct((N_EXPERTS, cap, D_MODEL), BF16),
        compiler_params=_cparams(2),
        name="moe_experts",
    )(tile_e, tile_r, tile_rows, n_valid, xs, wg, wu, wd)


def _combine_kernel(base_ref, cnt_ref, lim_ref, ys_ref, rel_ref, gate_ref, x1_ref, mod_ref, lng_ref, lnb_ref, o_ref,
                    slab, xslab, acc_sc, sems, xsem, *, ts):
    i = pl.program_id(0)
    last = pl.num_programs(0) - 1
    par = lax.rem(i, 2)
    n_chunks = -(-(ts + ROW_ALIGN - 1) // SLOT_CHUNK)
    slot = lax.broadcasted_iota(jnp.int32, (ts, SLOT_CHUNK), 1).astype(F32)

    def slab_start(step, e, c):
        want = _floor_align(base_ref[step * N_EXPERTS + e]) + c * SLOT_CHUNK
        start = jnp.maximum(jnp.minimum(want, lim_ref[e] - SLOT_CHUNK), 0)
        return pl.multiple_of(start, ROW_ALIGN), want - start

    def first_copy(step, e):
        start, _ = slab_start(step, e, 0)
        p = lax.rem(step, 2)
        return _group_copy(ys_ref.at[e, pl.ds(start, SLOT_CHUNK), :], slab.at[p, e], sems.at[p, e])

    def contribution(e, c, rows):
        a = base_ref[i * N_EXPERTS + e]
        _, moved = slab_start(i, e, c)
        pos = rel_ref[:, e:e + 1] + (a - _floor_align(a)).astype(F32)
        in_chunk = jnp.where(pos >= float(c * SLOT_CHUNK), 1.0, 0.0)
        onehot = jnp.where(pos + (moved - c * SLOT_CHUNK).astype(F32) == slot, in_chunk, 0.0).astype(BF16)
        return _dot(onehot, rows) * gate_ref[:, e:e + 1]

    @pl.when(i == 0)
    def _():
        for e in range(N_EXPERTS):
            first_copy(i, e).start()

    @pl.when(i < last)
    def _():
        for e in range(N_EXPERTS):
            first_copy(i + 1, e).start()

    for e in range(N_EXPERTS):
        first_copy(i, e).wait()
    f = contribution(0, 0, slab[par, 0])
    for e in range(1, N_EXPERTS):
        f = f + contribution(e, 0, slab[par, e])
    acc_sc[...] = f

    for e in range(N_EXPERTS):
        a = base_ref[i * N_EXPERTS + e]
        rows_in_block = a - _floor_align(a) + cnt_ref[i * N_EXPERTS + e]
        for c in range(1, n_chunks):
            @pl.when(rows_in_block > c * SLOT_CHUNK)
            def _(e=e, c=c):
                start, _ = slab_start(i, e, c)
                cp = _group_copy(ys_ref.at[e, pl.ds(start, SLOT_CHUNK), :], xslab, xsem)
                cp.start()
                cp.wait()
                acc_sc[...] += contribution(e, c, xslab[...])

    gate2 = mod_ref[0, 5:6, :]
    o_ref[...] = _layer_norm(DEEPNORM_ALPHA * x1_ref[...] + gate2 * acc_sc[...], lng_ref[...], lnb_ref[...])


def _combine(l, base, cnt, lim, ys, rel, gates, x1, mod, ln_g, ln_b, ts, tiles_per_batch):
    t = x1.shape[0]
    grid_spec = pltpu.PrefetchScalarGridSpec(
        num_scalar_prefetch=3,
        grid=(t // ts,),
        in_specs=[
            pl.BlockSpec(memory_space=pl.ANY),
            pl.BlockSpec((ts, LANES), lambda i, *_: (i, 0)),
            pl.BlockSpec((ts, LANES), lambda i, *_: (i, 0)),
            pl.BlockSpec((ts, D_MODEL), lambda i, *_: (i, 0)),
            pl.BlockSpec((None, 1, 8, D_MODEL), lambda i, *_: (l, i // tiles_per_batch, 0, 0)),
            pl.BlockSpec((None, 1, D_MODEL), lambda i, *_: (l, 0, 0)),
            pl.BlockSpec((None, 1, D_MODEL), lambda i, *_: (l, 0, 0)),
        ],
        out_specs=pl.BlockSpec((ts, D_MODEL), lambda i, *_: (i, 0)),
        scratch_shapes=[
            pltpu.VMEM((2, N_EXPERTS, SLOT_CHUNK, D_MODEL), BF16),
            pltpu.VMEM((SLOT_CHUNK, D_MODEL), BF16),
            pltpu.VMEM((ts, D_MODEL), F32),
            pltpu.SemaphoreType.DMA((2, N_EXPERTS)),
            pltpu.SemaphoreType.DMA(()),
        ],
    )
    return pl.pallas_call(
        functools.partial(_combine_kernel, ts=ts),
        grid_spec=grid_spec,
        out_shape=jax.ShapeDtypeStruct((t, D_MODEL), F32),
        compiler_params=_cparams(1),
        name="moe_combine",
    )(base, cnt, lim, ys, rel, gates, x1, mod, ln_g, ln_b)


def _expert_tile_plan(base, cnt, nt, n_steps):
    tm = EXPERT_ROWS
    i32 = lambda v: v.astype(jnp.int32)
    last_base = base.reshape(nt, N_EXPERTS)[-1]
    last_cnt = cnt.reshape(nt, N_EXPERTS)[-1]
    total = last_base + last_cnt
    tiles = jnp.maximum((total + tm - 1) // tm, 1)
    rest = total - (tiles - 1) * tm
    last_rows = jnp.full_like(rest, EXPERT_ROW_SIZES[0])
    for small, big in zip(EXPERT_ROW_SIZES[:-1], EXPERT_ROW_SIZES[1:]):
        last_rows = last_rows + jnp.where(rest > small, big - small, 0)
    ends = jnp.cumsum(tiles)
    n_valid = ends[-1]
    g = jnp.minimum(jnp.arange(n_steps, dtype=jnp.int32), n_valid - 1)
    tile_e = jnp.sum((g[:, None] >= ends[None, :]).astype(jnp.int32), axis=1)
    onehot = (tile_e[:, None] == jnp.arange(N_EXPERTS, dtype=jnp.int32)[None, :]).astype(jnp.int32)
    pick = lambda v: jnp.sum(onehot * v[None, :], axis=1)
    tile_r = g - pick(ends - tiles)
    tile_rows = jnp.where(tile_r == pick(tiles) - 1, pick(last_rows), tm)
    lim = (tiles - 1) * tm + last_rows
    return i32(tile_e), i32(tile_r), i32(tile_rows), i32(n_valid.reshape(1)), i32(lim)


def _top2_moe(l, moe_i, h2, x1, mod, routing, wg, wu, wd, ln_g, ln_b, ts, tiles_per_batch):
    rel, gates, relt, base3, cnt3 = routing
    t = h2.shape[0]
    nt = t // ts
    tm, tf = EXPERT_ROWS, 512
    base = base3[:, 0, :N_EXPERTS].reshape(-1)
    cnt = cnt3[:, 0, :N_EXPERTS].reshape(-1)
    zero_rows = -(-tm // SLOT_CHUNK) * SLOT_CHUNK
    reach_rows = _dispatch_chunks(ts) * DISPATCH_CHUNK
    cap = -(-(t + ROW_ALIGN + reach_rows + zero_rows) // tm) * tm
    n_steps = -(-TOP_K * t // tm) + N_EXPERTS
    xs = _dispatch(base, cnt, h2, relt, ts, cap, zero_rows)
    tile_e, tile_r, tile_rows, n_valid, lim = _expert_tile_plan(base, cnt, nt, n_steps)
    ys = _expert_ffn(moe_i, tile_e, tile_r, tile_rows, n_valid, xs, wg, wu, wd, tf, n_steps)
    return _combine(l, base, cnt, lim, ys, rel, gates, x1, mod, ln_g, ln_b, ts, tiles_per_batch)


def kernel(x, c, positions, w_in, q_norm_g, kv_norm_g, w_uq, w_ukv, conv_w, mix_norm_g, w_o, w_ada, b_ada,
           ln1_g, ln1_b, ln2_g, ln2_b, ffn_w_gate, ffn_w_up, ffn_w_down, moe_router_w, moe_router_b,
           moe_w_gate, moe_w_up, moe_w_down):
    b, s, _ = x.shape
    n_layers = w_in.shape[0]
    ts = min(512, s)
    t = b * s
    tiles_per_batch = s // ts

    bp = -(-b // 16) * 16
    c_pad = jnp.pad(c, ((0, bp - b), (0, 0)))
    mod = _ada_mod(c_pad, w_ada, b_ada)
    mod = jnp.pad(jnp.transpose(mod[:, :, :b, :], (0, 2, 1, 3)), ((0, 0), (0, 0), (0, 2), (0, 0)))
    rope_tab = _rope_tables(positions, ts)

    rows = lambda v: v.reshape(v.shape[0], 1, -1)
    o3 = Q_LORA + KV_LORA + QK_ROPE
    w_in_p = jnp.concatenate(
        [w_in[:, :, :o3], jnp.zeros((n_layers, D_MODEL, D_IN_PAD - w_in.shape[2]), F32), w_in[:, :, o3:]], axis=2
    ).astype(BF16)
    w_uq_t = jnp.transpose(jnp.pad(
        w_uq.reshape(n_layers, Q_LORA, N_HEADS, QK_NOPE + QK_ROPE),
        ((0, 0), (0, 0), (0, 0), (0, QK_PAD - QK_NOPE - QK_ROPE))
    ).reshape(n_layers, Q_LORA, N_HEADS * QK_PAD), (0, 2, 1)).astype(BF16)
    w_ukv_h = w_ukv.reshape(n_layers, KV_LORA, N_HEADS, QK_NOPE + V_HEAD)
    w_k = w_ukv_h[..., :QK_NOPE].reshape(n_layers, KV_LORA, N_HEADS * QK_NOPE).astype(BF16)
    w_v_t = jnp.transpose(w_ukv_h[..., QK_NOPE:].reshape(n_layers, KV_LORA, N_HEADS * V_HEAD), (0, 2, 1)).astype(BF16)
    w_o_b = w_o.astype(BF16)
    conv_w_p = jnp.pad(conv_w, ((0, 0), (0, 8 - CONV_WIDTH), (0, 0)))
    g_attn, g_conv = rows(mix_norm_g[:, :D_ATTN]), rows(mix_norm_g[:, D_ATTN:])
    ffn_wg, ffn_wu, ffn_wd = ffn_w_gate.astype(BF16), ffn_w_up.astype(BF16), ffn_w_down.astype(BF16)
    rw_hi = moe_router_w.astype(BF16)
    rw_lo = (moe_router_w - rw_hi.astype(F32)).astype(BF16)
    rw = jnp.pad(jnp.concatenate([rw_hi, rw_lo], axis=2), ((0, 0), (0, 0), (0, LANES - 2 * N_EXPERTS)))
    rb = rows(jnp.pad(moe_router_b, ((0, 0), (0, LANES - N_EXPERTS))))

    for l in range(n_layers):
        qt, k, vt, convn = _inproj(l, x, mod, w_in_p, w_uq_t, w_k, w_v_t, rows(q_norm_g), rows(kv_norm_g),
                                   conv_w_p, g_conv, rope_tab, ts)
        attn = _attention(qt, k, vt, ts)
        common = (l, attn, convn, x, mod, w_o_b, g_attn, rows(ln1_g), rows(ln1_b), ts)
        i = l // 2
        if l % 2 == 0:
            x1 = _outproj(*common)
            x = _dense_ffn(l, i, x1.reshape(t, D_MODEL), mod, ffn_wg, ffn_wu, ffn_wd, rows(ln2_g), rows(ln2_b),
                           ts, tiles_per_batch)
        else:
            x1, h2, *routing = _outproj(*common, router=(i, rw, rb))
            x = _top2_moe(l, i, h2.reshape(t, D_MODEL), x1.reshape(t, D_MODEL), mod, routing,
                          moe_w_gate, moe_w_up, moe_w_down, rows(ln2_g), rows(ln2_b), ts, tiles_per_batch)
        x = x.reshape(b, s, D_MODEL)
    return x
```

```python
import functools

import jax
import jax.numpy as jnp
from jax import lax
from jax.experimental import pallas as pl
from jax.experimental.pallas import tpu as pltpu

F32 = jnp.float32
BF16 = jnp.bfloat16

D_MODEL = 1024
N_HEADS = 4
QK_NOPE = 128
QK_ROPE = 64
V_HEAD = 128
Q_LORA = 256
KV_LORA = 128
D_ATTN = N_HEADS * V_HEAD
D_CONV = D_MODEL - D_ATTN
CONV_WIDTH = 3
ROPE_BASE = 10000.0
D_FF = 2816
N_EXPERTS = 8
TOP_K = 2
D_FF_EXPERT = 3584
RMS_EPS = 1e-6
LN_EPS = 1e-5
DEPTH = 2
DEEPNORM_ALPHA = (2 * DEPTH) ** 0.25
SM_SCALE = (QK_NOPE + QK_ROPE) ** -0.5
LOG2E = 1.4426950408889634
Q_SCALE = SM_SCALE * LOG2E

LANES = 128
QK_PAD = 2 * LANES
D_IN_PAD = 2048
TOKEN_TILE = 512
EXPERT_FF_TILE = 512
ROW_ALIGN = 16
SLOT_CHUNK = 256
EXPERT_ROW_STEP = 384
EXPERT_ROW_SIZES = tuple(EXPERT_ROW_STEP * n for n in range(1, 7))
EXPERT_ROWS = EXPERT_ROW_SIZES[-1]
VMEM_LIMIT = 56 * 1024 * 1024
NEG_BIG = -1e30
NOT_ROUTED = -4096.0


def _cparams(n_axes):
    return pltpu.CompilerParams(dimension_semantics=("arbitrary",) * n_axes, vmem_limit_bytes=VMEM_LIMIT)


def _dot(a, b):
    return jnp.dot(a, b, preferred_element_type=F32)


def _rms(x, g):
    return x * lax.rsqrt(jnp.mean(x * x, axis=-1, keepdims=True) + RMS_EPS) * g


def _layer_norm(x, g, b):
    mu = jnp.mean(x, axis=-1, keepdims=True)
    xc = x - mu
    var = jnp.mean(xc * xc, axis=-1, keepdims=True)
    return xc * lax.rsqrt(var + LN_EPS) * g + b


def _silu(x):
    return x * jax.nn.sigmoid(x)


def _ada_kernel(c_ref, w_ref, b_ref, o_ref):
    ca = _silu(c_ref[...]).astype(BF16)
    o_ref[0, 0] = _dot(ca, w_ref[0].astype(BF16)) + b_ref[0]


def _ada_mod(c_pad, w_ada, b_ada):
    n_layers = w_ada.shape[0]
    bp = c_pad.shape[0]
    return pl.pallas_call(
        _ada_kernel,
        grid=(n_layers, 6),
        in_specs=[
            pl.BlockSpec((bp, D_MODEL), lambda l, j: (0, 0)),
            pl.BlockSpec((1, D_MODEL, D_MODEL), lambda l, j: (l, 0, j)),
            pl.BlockSpec((1, 1, D_MODEL), lambda l, j: (l, 0, j)),
        ],
        out_specs=pl.BlockSpec((1, 1, bp, D_MODEL), lambda l, j: (l, j, 0, 0)),
        out_shape=jax.ShapeDtypeStruct((n_layers, 6, bp, D_MODEL), F32),
        compiler_params=_cparams(2),
        name="ada_mod",
    )(c_pad, w_ada, b_ada.reshape(n_layers, 1, 6 * D_MODEL))


ROPE_HALF = QK_ROPE // 2


def _rope_kernel(pos_ref, f_ref, o_ref):
    ang = pos_ref[0].astype(F32) * f_ref[...]
    o_ref[0, 0:ROPE_HALF, :] = jnp.cos(ang)
    o_ref[0, ROPE_HALF:QK_ROPE, :] = jnp.sin(ang)


def _rope_tables(positions, ts):
    nt = positions.size // ts
    inv_freq = ROPE_BASE ** (-jnp.arange(0, QK_ROPE, 2, dtype=F32) / QK_ROPE)
    return pl.pallas_call(
        _rope_kernel,
        grid=(nt,),
        in_specs=[pl.BlockSpec((1, 1, ts), lambda i: (i, 0, 0)), pl.BlockSpec((ROPE_HALF, 1), lambda i: (0, 0))],
        out_specs=pl.BlockSpec((1, QK_ROPE, ts), lambda i: (i, 0, 0)),
        out_shape=jax.ShapeDtypeStruct((nt, QK_ROPE, ts), F32),
        compiler_params=_cparams(1),
        name="rope_tables",
    )(positions.reshape(nt, 1, ts), inv_freq.reshape(ROPE_HALF, 1))


def _inproj_kernel(x_ref, mod_ref, win_ref, wuqt_ref, wk_ref, wvt_ref, qg_ref, kvg_ref, cw_ref, cg_ref, rope_ref,
                   qt_ref, k_ref, vt_ref, cv_ref, cu_ext, *, ts):
    j = pl.program_id(1)

    @pl.when(j == 0)
    def _():
        cu_ext[0:8, :] = jnp.zeros((8, D_CONV), F32)

    @pl.when(j > 0)
    def _():
        cu_ext[0:8, :] = cu_ext[ts:ts + 8, :]

    x = x_ref[0]
    shift = mod_ref[0, 0:1, :]
    scale = mod_ref[0, 1:2, :]
    h = (x * (1.0 + scale) + shift).astype(BF16)
    z_lat = _dot(h, win_ref[:, 0:512])
    c_q = z_lat[:, 0:Q_LORA]
    c_kv = z_lat[:, Q_LORA:Q_LORA + KV_LORA]
    k_rope = z_lat[:, 384:512]
    z_gate = _dot(h, win_ref[:, 512:1536])
    gate_b = z_gate[:, 0:D_CONV]
    gate_c = z_gate[:, D_CONV:2 * D_CONV]
    u = _dot(h, win_ref[:, 1536:2048])
    cqn = _rms(c_q, qg_ref[...])
    ckvn = _rms(c_kv, kvg_ref[...])
    q_t = _dot(wuqt_ref[...], jnp.transpose(cqn).astype(BF16))
    k_nope = _dot(ckvn.astype(BF16), wk_ref[...])
    v_t = _dot(wvt_ref[...], jnp.transpose(ckvn).astype(BF16))

    cos = rope_ref[0, 0:ROPE_HALF, :]
    sin = rope_ref[0, ROPE_HALF:QK_ROPE, :]

    def rope_t(block):
        x1, x2 = block[0:ROPE_HALF, :], block[ROPE_HALF:QK_ROPE, :]
        return x1 * cos - x2 * sin, x2 * cos + x1 * sin

    k1, k2 = rope_t(jnp.transpose(k_rope)[0:QK_ROPE, :])
    k_rot = jnp.transpose(jnp.concatenate([k1, k2, jnp.zeros((LANES - QK_ROPE, ts), F32)], axis=0)).astype(BF16)
    for hh in range(N_HEADS):
        qo = hh * QK_PAD
        r0 = qo + QK_NOPE
        q1, q2 = rope_t(q_t[r0:r0 + QK_ROPE, :])
        qt_ref[0, hh, 0:QK_NOPE, :] = (q_t[qo:r0, :] * Q_SCALE).astype(BF16)
        qt_ref[0, hh, QK_NOPE:QK_NOPE + ROPE_HALF, :] = (q1 * Q_SCALE).astype(BF16)
        qt_ref[0, hh, QK_NOPE + ROPE_HALF:QK_NOPE + QK_ROPE, :] = (q2 * Q_SCALE).astype(BF16)
        qt_ref[0, hh, QK_NOPE + QK_ROPE:QK_PAD, :] = jnp.zeros((QK_PAD - QK_NOPE - QK_ROPE, ts), BF16)
        k_ref[0, hh, :, 0:LANES] = k_nope[:, hh * QK_NOPE:(hh + 1) * QK_NOPE].astype(BF16)
        k_ref[0, hh, :, LANES:QK_PAD] = k_rot
        vt_ref[0, hh, 0] = v_t[hh * V_HEAD:(hh + 1) * V_HEAD, :].astype(BF16)

    cu_ext[8:ts + 8, :] = gate_c * u
    conv = (cw_ref[2:3, :] * cu_ext[8:ts + 8, :]
            + cw_ref[1:2, :] * cu_ext[pl.ds(7, ts), :]
            + cw_ref[0:1, :] * cu_ext[pl.ds(6, ts), :])
    conv = gate_b * conv
    cv_ref[0] = _rms(conv, cg_ref[...]).astype(BF16)


def _inproj(l, x, mod, w_in_p, w_uq_t, w_k, w_v_t, qg, kvg, conv_w, conv_g, rope_tab, ts):
    b, s, _ = x.shape
    nsb = s // ts
    const = lambda shape: pl.BlockSpec((None,) + shape, lambda bi, j: (l,) + (0,) * len(shape))
    return pl.pallas_call(
        functools.partial(_inproj_kernel, ts=ts),
        grid=(b, nsb),
        in_specs=[
            pl.BlockSpec((1, ts, D_MODEL), lambda bi, j: (bi, j, 0)),
            pl.BlockSpec((None, 1, 8, D_MODEL), lambda bi, j: (l, bi, 0, 0)),
            const((D_MODEL, D_IN_PAD)),
            const((N_HEADS * QK_PAD, Q_LORA)),
            const((KV_LORA, N_HEADS * QK_NOPE)),
            const((N_HEADS * V_HEAD, KV_LORA)),
            const((1, Q_LORA)),
            const((1, KV_LORA)),
            const((8, D_CONV)),
            const((1, D_CONV)),
            pl.BlockSpec((1, QK_ROPE, ts), lambda bi, j: (bi * nsb + j, 0, 0)),
        ],
        out_specs=[
            pl.BlockSpec((1, N_HEADS, QK_PAD, ts), lambda bi, j: (bi, 0, 0, j)),
            pl.BlockSpec((1, N_HEADS, ts, QK_PAD), lambda bi, j: (bi, 0, j, 0)),
            pl.BlockSpec((1, N_HEADS, 1, V_HEAD, ts), lambda bi, j: (bi, 0, j, 0, 0)),
            pl.BlockSpec((1, ts, D_CONV), lambda bi, j: (bi, j, 0)),
        ],
        out_shape=[
            jax.ShapeDtypeStruct((b, N_HEADS, QK_PAD, s), BF16),
            jax.ShapeDtypeStruct((b, N_HEADS, s, QK_PAD), BF16),
            jax.ShapeDtypeStruct((b, N_HEADS, nsb, V_HEAD, ts), BF16),
            jax.ShapeDtypeStruct((b, s, D_CONV), BF16),
        ],
        scratch_shapes=[pltpu.VMEM((ts + 8, D_CONV), F32)],
        compiler_params=_cparams(2),
        name="inproj",
    )(x, mod, w_in_p, w_uq_t, w_k, w_v_t, qg, kvg, conv_w, conv_g, rope_tab)


Q_CHUNK = 256


K_ROWS = 16


def _attn_kernel(qt_ref, k_ref, vt_ref, o_ref, *scratch, tq):
    i = pl.program_id(1)
    n_chunks = tq // Q_CHUNK
    per_chain = 5
    chains = [(hh, c) + tuple(scratch[per_chain * (hh * n_chunks + c):per_chain * (hh * n_chunks + c + 1)])
              for hh in range(N_HEADS) for c in range(n_chunks)]
    for _, _, m_sc, l_sc, acc_sc, _, _ in chains:
        m_sc[...] = jnp.full(m_sc.shape, NEG_BIG, F32)
        l_sc[...] = jnp.zeros(l_sc.shape, F32)
        acc_sc[...] = jnp.zeros(acc_sc.shape, F32)

    def scores(chain, jk, buf):
        hh, c = chain[0], chain[1]
        kj = k_ref[0, hh, pl.ds(pl.multiple_of(jk * tq, tq), tq), :]
        chain[5 + buf][...] = _dot(kj, qt_ref[0, hh, :, c * Q_CHUNK:(c + 1) * Q_CHUNK])

    def softmax_values(chain, jk, buf, diagonal):
        hh, c, m_sc, l_sc, acc_sc = chain[:5]
        s_sc = chain[5 + buf]
        nk = (c + 1) * Q_CHUNK if diagonal else tq
        if diagonal:
            d0 = nk - Q_CHUNK
            key = lax.broadcasted_iota(jnp.int32, (Q_CHUNK, Q_CHUNK), 0)
            qry = lax.broadcasted_iota(jnp.int32, (Q_CHUNK, Q_CHUNK), 1)
            s_sc[d0:nk, :] = jnp.where(key <= qry, s_sc[d0:nk, :], NEG_BIG)
        blk_max = s_sc[0:K_ROWS, :]
        for r in range(K_ROWS, nk, K_ROWS):
            blk_max = jnp.maximum(blk_max, s_sc[r:r + K_ROWS, :])
        m_prev = m_sc[...]
        m_new = jnp.maximum(m_prev, jnp.max(blk_max, axis=0, keepdims=True))
        alpha = jnp.exp2(m_prev - m_new)
        p_sum = jnp.zeros((K_ROWS, Q_CHUNK), F32)
        acc = alpha * acc_sc[...]
        for kt in range(0, nk, Q_CHUNK):
            parts = []
            for r in range(kt, kt + Q_CHUNK, K_ROWS):
                p = jnp.exp2(s_sc[r:r + K_ROWS, :] - m_new)
                p_sum = p_sum + p
                parts.append(p.astype(BF16))
            acc = acc + _dot(vt_ref[0, hh, jk, :, kt:kt + Q_CHUNK], jnp.concatenate(parts, axis=0))
        l_sc[...] = alpha * l_sc[...] + jnp.sum(p_sum, axis=0, keepdims=True)
        acc_sc[...] = acc
        m_sc[...] = m_new

    def stage(cur, cur_buf, diagonal, nxt=None):
        for chain in chains:
            if nxt is not None:
                scores(chain, nxt, 1 - cur_buf)
            softmax_values(chain, cur, cur_buf, diagonal)

    for chain in chains:
        scores(chain, 0, 0)

    def pair(jj, carry):
        stage(2 * jj, 0, False, nxt=2 * jj + 1)
        stage(2 * jj + 1, 1, False, nxt=2 * jj + 2)
        return carry

    lax.fori_loop(0, i // 2, pair, 0)

    @pl.when(lax.rem(i, 2) == 0)
    def _():
        stage(i, 0, True)

    @pl.when(lax.rem(i, 2) == 1)
    def _():
        stage(i - 1, 0, False, nxt=i)
        stage(i, 1, True)

    for hh, c, _, l_sc, acc_sc, _, _ in chains:
        o_ref[0, c * Q_CHUNK:(c + 1) * Q_CHUNK, hh * V_HEAD:(hh + 1) * V_HEAD] = (
            jnp.transpose(acc_sc[...] / l_sc[...]).astype(BF16))


def _attention(qt, k, vt, tq):
    b, nh, s, _ = k.shape
    return pl.pallas_call(
        functools.partial(_attn_kernel, tq=tq),
        grid=(b, s // tq),
        in_specs=[
            pl.BlockSpec((1, nh, QK_PAD, tq), lambda bi, i: (bi, 0, 0, i)),
            pl.BlockSpec((1, nh, s, QK_PAD), lambda bi, i: (bi, 0, 0, 0)),
            pl.BlockSpec((1, nh, s // tq, V_HEAD, tq), lambda bi, i: (bi, 0, 0, 0, 0)),
        ],
        out_specs=pl.BlockSpec((1, tq, D_ATTN), lambda bi, i: (bi, i, 0)),
        out_shape=jax.ShapeDtypeStruct((b, s, D_ATTN), BF16),
        scratch_shapes=[
            pltpu.VMEM((1, Q_CHUNK), F32), pltpu.VMEM((1, Q_CHUNK), F32), pltpu.VMEM((V_HEAD, Q_CHUNK), F32),
            pltpu.VMEM((tq, Q_CHUNK), F32), pltpu.VMEM((tq, Q_CHUNK), F32),
        ] * (nh * (tq // Q_CHUNK)),
        compiler_params=_cparams(2),
        name="attention",
    )(qt, k, vt)


OUT_ROWS = 256


def _row_chunks(ts):
    return [slice(r, r + OUT_ROWS) for r in range(0, ts, OUT_ROWS)]


def _outproj_matmuls(attn_ref, cv_ref, wo_ref, ga_ref, rows):
    an = _rms(attn_ref[0, rows, :].astype(F32), ga_ref[...]).astype(BF16)
    return _dot(an, wo_ref[0:D_ATTN, :]) + _dot(cv_ref[0, rows, :], wo_ref[D_ATTN:D_MODEL, :])


def _outproj_norm(y, x_ref, mod_ref, lng_ref, lnb_ref, rows):
    x1 = _layer_norm(DEEPNORM_ALPHA * x_ref[0, rows, :] + mod_ref[0, 2:3, :] * y, lng_ref[...], lnb_ref[...])
    h2 = x1 * (1.0 + mod_ref[0, 4:5, :]) + mod_ref[0, 3:4, :]
    return x1, h2


def _outproj_router_kernel(attn_ref, cv_ref, x_ref, mod_ref, wo_ref, ga_ref, lng_ref, lnb_ref, rw_ref, rb_ref,
                           x1_ref, h2_ref, rel_ref, gate_ref, relt_ref, base_ref, cnt_ref, run_sc, *, ts):
    first = jnp.logical_and(pl.program_id(0) == 0, pl.program_id(1) == 0)

    @pl.when(first)
    def _():
        run_sc[...] = jnp.zeros(run_sc.shape, F32)

    chunks = _row_chunks(ts)
    ys = [_outproj_matmuls(attn_ref, cv_ref, wo_ref, ga_ref, rows) for rows in chunks]
    lane = lax.broadcasted_iota(jnp.int32, (OUT_ROWS, LANES), 1)
    r_i = lax.broadcasted_iota(jnp.int32, (OUT_ROWS, OUT_ROWS), 0)
    c_i = lax.broadcasted_iota(jnp.int32, (OUT_ROWS, OUT_ROWS), 1)
    tri = jnp.where(c_i < r_i, 1.0, 0.0).astype(BF16)
    n_tile = jnp.zeros((1, LANES), F32)
    for rows, y in zip(chunks, ys):
        x1, h2 = _outproj_norm(y, x_ref, mod_ref, lng_ref, lnb_ref, rows)
        x1_ref[0, rows, :] = x1
        h2_hi = h2.astype(BF16)
        h2_ref[0, rows, :] = h2_hi

        h2_lo = (h2 - h2_hi.astype(F32)).astype(BF16)
        prod = _dot(h2_hi, rw_ref[...]) + _dot(h2_lo, rw_ref[...])
        logits = prod + pltpu.roll(prod, LANES - N_EXPERTS, 1) + rb_ref[...]
        logits = jnp.where(lane < N_EXPERTS, logits, NEG_BIG)
        v1 = jnp.max(logits, axis=-1, keepdims=True)
        i1 = jnp.min(jnp.where(logits == v1, lane, LANES), axis=-1, keepdims=True)
        rest = jnp.where(lane == i1, NEG_BIG, logits)
        v2 = jnp.max(rest, axis=-1, keepdims=True)
        i2 = jnp.min(jnp.where(rest == v2, lane, LANES), axis=-1, keepdims=True)
        e21 = jnp.exp(v2 - v1)
        g1 = 1.0 / (1.0 + e21)
        g2 = e21 / (1.0 + e21)
        is1 = lane == i1
        is2 = lane == i2
        sel = jnp.where(jnp.logical_or(is1, is2), 1.0, 0.0)
        gate_ref[rows, :] = jnp.where(is1, g1, 0.0) + jnp.where(is2, g2, 0.0)

        rank = _dot(tri, sel.astype(BF16)) + n_tile
        rel = jnp.where(sel > 0.0, rank, NOT_ROUTED)
        rel_ref[rows, :] = rel
        relt_ref[0, :, rows] = jnp.transpose(rel)[0:N_EXPERTS, :]
        n_tile = n_tile + jnp.sum(sel, axis=0, keepdims=True)

    base_ref[0] = run_sc[...].astype(jnp.int32)
    cnt_ref[0] = n_tile.astype(jnp.int32)
    run_sc[...] = run_sc[...] + n_tile


def _outproj_router(l, moe_i, attn, convn, x, mod, w_o, g_attn, ln_g, ln_b, rw, rb, ts):
    b, s, _ = x.shape
    nsb = s // ts
    const = lambda shape, idx=l: pl.BlockSpec((None,) + shape, lambda bi, j: (idx,) + (0,) * len(shape))
    tile3 = lambda w: pl.BlockSpec((1, ts, w), lambda bi, j: (bi, j, 0))
    in_specs = [
        tile3(D_ATTN), tile3(D_CONV), tile3(D_MODEL),
        pl.BlockSpec((None, 1, 8, D_MODEL), lambda bi, j: (l, bi, 0, 0)),
        const((D_MODEL, D_MODEL)), const((1, D_ATTN)), const((1, D_MODEL)), const((1, D_MODEL)),
    ]
    out_specs = [tile3(D_MODEL), tile3(D_MODEL)]
    out_shape = [jax.ShapeDtypeStruct((b, s, D_MODEL), F32), jax.ShapeDtypeStruct((b, s, D_MODEL), BF16)]
    args = [attn, convn, x, mod, w_o, g_attn, ln_g, ln_b]
    nt = b * nsb
    flat = lambda w: pl.BlockSpec((ts, w), lambda bi, j: (bi * nsb + j, 0))
    per_tile = lambda r, w: pl.BlockSpec((1, r, w), lambda bi, j: (bi * nsb + j, 0, 0))
    in_specs += [const((D_MODEL, LANES), moe_i), const((1, LANES), moe_i)]
    out_specs += [flat(LANES), flat(LANES), per_tile(N_EXPERTS, ts), per_tile(1, LANES), per_tile(1, LANES)]
    out_shape += [
        jax.ShapeDtypeStruct((nt * ts, LANES), F32), jax.ShapeDtypeStruct((nt * ts, LANES), F32),
        jax.ShapeDtypeStruct((nt, N_EXPERTS, ts), F32),
        jax.ShapeDtypeStruct((nt, 1, LANES), jnp.int32), jax.ShapeDtypeStruct((nt, 1, LANES), jnp.int32),
    ]
    return pl.pallas_call(
        functools.partial(_outproj_router_kernel, ts=ts), grid=(b, nsb), in_specs=in_specs, out_specs=out_specs,
        out_shape=out_shape, scratch_shapes=[pltpu.VMEM((1, LANES), F32)],
        compiler_params=_cparams(2), name="outproj_router",
    )(*args, rw, rb)


FF_CHUNK = 256


def _swiglu(x, wg, wu, wd, width):
    out = None
    pending = None
    for c in range(0, width, FF_CHUNK):
        w = min(FF_CHUNK, width - c)
        g = _dot(x, wg(c, w))
        u = _dot(x, wu(c, w))
        if pending is not None:
            d = _dot(pending[0], wd(*pending[1]))
            out = d if out is None else out + d
        pending = ((_silu(g) * u).astype(BF16), (c, w))
    d = _dot(pending[0], wd(*pending[1]))
    return d if out is None else out + d


def _outproj_ffn_kernel(attn_ref, cv_ref, x_ref, mod_ref, wo_ref, ga_ref, ln1g_ref, ln1b_ref,
                        wg_ref, wu_ref, wd_ref, ln2g_ref, ln2b_ref, o_ref, *, ts):
    chunks = _row_chunks(ts)
    ys = [_outproj_matmuls(attn_ref, cv_ref, wo_ref, ga_ref, rows) for rows in chunks]
    normed = [_outproj_norm(y, x_ref, mod_ref, ln1g_ref, ln1b_ref, rows) for rows, y in zip(chunks, ys)]
    x1 = jnp.concatenate([n[0] for n in normed], axis=0)
    h = jnp.concatenate([n[1].astype(BF16) for n in normed], axis=0)
    f = _swiglu(h, lambda c, w: wg_ref[:, c:c + w], lambda c, w: wu_ref[:, c:c + w],
                lambda c, w: wd_ref[c:c + w, :], D_FF)
    gate2 = mod_ref[0, 5:6, :]
    o_ref[0] = _layer_norm(DEEPNORM_ALPHA * x1 + gate2 * f, ln2g_ref[...], ln2b_ref[...])


def _outproj_dense_ffn(l, dense_i, attn, convn, x, mod, w_o, g_attn, ln1_g, ln1_b, wg, wu, wd, ln2_g, ln2_b, ts):
    b, s, _ = x.shape
    const = lambda shape, idx=l: pl.BlockSpec((None,) + shape, lambda bi, j: (idx,) + (0,) * len(shape),
                                              pipeline_mode=pl.Buffered(1))
    tile3 = lambda w: pl.BlockSpec((1, ts, w), lambda bi, j: (bi, j, 0))
    return pl.pallas_call(
        functools.partial(_outproj_ffn_kernel, ts=ts),
        grid=(b, s // ts),
        in_specs=[
            tile3(D_ATTN), tile3(D_CONV), tile3(D_MODEL),
            pl.BlockSpec((None, 1, 8, D_MODEL), lambda bi, j: (l, bi, 0, 0)),
            const((D_MODEL, D_MODEL)), const((1, D_ATTN)), const((1, D_MODEL)), const((1, D_MODEL)),
            const((D_MODEL, D_FF), dense_i), const((D_MODEL, D_FF), dense_i), const((D_FF, D_MODEL), dense_i),
            const((1, D_MODEL)), const((1, D_MODEL)),
        ],
        out_specs=tile3(D_MODEL),
        out_shape=jax.ShapeDtypeStruct((b, s, D_MODEL), F32),
        compiler_params=_cparams(2),
        name="outproj_dense_ffn",
    )(attn, convn, x, mod, w_o, g_attn, ln1_g, ln1_b, wg, wu, wd, ln2_g, ln2_b)


def _group_copy(src, dst, sem):
    return pltpu.make_async_copy(src, dst, sem)


DISPATCH_CHUNK = 256


def _dispatch_chunks(ts):
    return -(-(ts + ROW_ALIGN - 1) // DISPATCH_CHUNK)


def _floor_align(v):
    return pl.multiple_of(lax.shift_left(lax.shift_right_logical(v, 4), 4), ROW_ALIGN)


def _dispatch_kernel(base_ref, cnt_ref, h_ref, relt_ref, xs_ref, stage, carry, zbuf, sems, xsem, zsem, *,
                     ts, zero_rows):
    i = pl.program_id(0)
    last = pl.num_programs(0) - 1
    par = lax.rem(i, 2)
    ch = DISPATCH_CHUNK
    n_chunks = _dispatch_chunks(ts)
    h = h_ref[...]
    slot = lax.broadcasted_iota(jnp.int32, (ch, ts), 0).astype(F32)

    def group(step, e):
        a = base_ref[step * N_EXPERTS + e]
        a16 = _floor_align(a)
        return a, a16, a - a16

    def onehot(e, c):
        _, _, off = group(i, e)
        row = relt_ref[0, e:e + 1, :] + off.astype(F32)
        return jnp.where(row == slot + float(c * ch), 1.0, 0.0).astype(BF16)

    def first_copy(step, e):
        _, a16, _ = group(step, e)
        p = lax.rem(step, 2)
        return _group_copy(stage.at[p, e, 0:ch], xs_ref.at[e, pl.ds(a16, ch), :], sems.at[p, e])

    @pl.when(i == 0)
    def _():
        carry[...] = jnp.zeros(carry.shape, BF16)
        stage[:, :, n_chunks * ch:, :] = jnp.zeros((2, N_EXPERTS, ROW_ALIGN, D_MODEL), BF16)

    for e in range(N_EXPERTS):
        rows = _dot(onehot(e, 0), h)
        stage[par, e, 0:ROW_ALIGN] = (rows[0:ROW_ALIGN] + carry[e].astype(F32)).astype(BF16)
        stage[par, e, ROW_ALIGN:ch] = rows[ROW_ALIGN:ch].astype(BF16)
        stage[par, e, ch:ch + ROW_ALIGN] = jnp.zeros((ROW_ALIGN, D_MODEL), BF16)

    @pl.when(i > 0)
    def _():
        for e in range(N_EXPERTS):
            first_copy(i - 1, e).wait()

    for e in range(N_EXPERTS):
        first_copy(i, e).start()

    for e in range(N_EXPERTS):
        _, a16, off = group(i, e)
        n = cnt_ref[i * N_EXPERTS + e]
        for c in range(1, n_chunks):
            @pl.when(off + n > c * ch)
            def _(e=e, c=c, a16=a16):
                lo = c * ch
                stage[par, e, lo:lo + ch] = _dot(onehot(e, c), h).astype(BF16)
                if c + 1 < n_chunks:
                    stage[par, e, lo + ch:lo + ch + ROW_ALIGN] = jnp.zeros((ROW_ALIGN, D_MODEL), BF16)
                cp = _group_copy(stage.at[par, e, lo:lo + ch], xs_ref.at[e, pl.ds(a16 + lo, ch), :], xsem)
                cp.start()
                cp.wait()

    for e in range(N_EXPERTS):
        a, a16, _ = group(i, e)
        nxt16 = _floor_align(a + cnt_ref[i * N_EXPERTS + e])
        carry[e] = stage[par, e, pl.ds(pl.multiple_of(nxt16 - a16, ROW_ALIGN), ROW_ALIGN), :]

    @pl.when(i == last)
    def _():
        for e in range(N_EXPERTS):
            first_copy(i, e).wait()
        zbuf[...] = jnp.zeros(zbuf.shape, BF16)
        for e in range(N_EXPERTS):
            n = cnt_ref[i * N_EXPERTS + e]
            end = _floor_align(base_ref[i * N_EXPERTS + e] + n + (ROW_ALIGN - 1))
            copies = [
                _group_copy(zbuf, xs_ref.at[e, pl.ds(end + r * SLOT_CHUNK, SLOT_CHUNK), :], zsem.at[e])
                for r in range(zero_rows // SLOT_CHUNK)
            ]
            for cp in copies:
                cp.start()
            for cp in copies:
                cp.wait()


def _dispatch(base, cnt, h2, relt, ts, cap, zero_rows):
    t = h2.shape[0]
    nt = t // ts
    n_chunks = _dispatch_chunks(ts)
    grid_spec = pltpu.PrefetchScalarGridSpec(
        num_scalar_prefetch=2,
        grid=(nt,),
        in_specs=[
            pl.BlockSpec((ts, D_MODEL), lambda i, b_, c_: (i, 0)),
            pl.BlockSpec((1, N_EXPERTS, ts), lambda i, b_, c_: (i, 0, 0)),
        ],
        out_specs=pl.BlockSpec(memory_space=pl.ANY),
        scratch_shapes=[
            pltpu.VMEM((2, N_EXPERTS, n_chunks * DISPATCH_CHUNK + ROW_ALIGN, D_MODEL), BF16),
            pltpu.VMEM((N_EXPERTS, ROW_ALIGN, D_MODEL), BF16),
            pltpu.VMEM((SLOT_CHUNK, D_MODEL), BF16),
            pltpu.SemaphoreType.DMA((2, N_EXPERTS)),
            pltpu.SemaphoreType.DMA(()),
            pltpu.SemaphoreType.DMA((N_EXPERTS,)),
        ],
    )
    return pl.pallas_call(
        functools.partial(_dispatch_kernel, ts=ts, zero_rows=zero_rows),
        grid_spec=grid_spec,
        out_shape=jax.ShapeDtypeStruct((N_EXPERTS, cap, D_MODEL), BF16),
        compiler_params=_cparams(1),
        name="moe_dispatch",
    )(base, cnt, h2, relt)


def _expert_kernel(te_ref, tr_ref, rows_ref, nv_ref, x_ref, wg_ref, wu_ref, wd_ref, o_ref, acc_sc):
    g_i = pl.program_id(0)
    k = pl.program_id(1)

    valid = g_i < nv_ref[0]

    @pl.when(jnp.logical_and(valid, k == 0))
    def _():
        acc_sc[...] = jnp.zeros(acc_sc.shape, F32)

    for m in EXPERT_ROW_SIZES:
        @pl.when(jnp.logical_and(valid, rows_ref[g_i] == m))
        def _(m=m):
            part = _swiglu(x_ref[0, 0:m, :], lambda c, w: wg_ref[0, :, c:c + w].astype(BF16),
                           lambda c, w: wu_ref[0, :, c:c + w].astype(BF16),
                           lambda c, w: wd_ref[0, c:c + w, :].astype(BF16), wg_ref.shape[2])
            total = acc_sc[0:m, :] + part
            acc_sc[0:m, :] = total
            o_ref[0, 0:m, :] = total.astype(BF16)


def _expert_ffn(moe_i, tile_e, tile_r, tile_rows, n_valid, xs, wg, wu, wd, tf, n_steps):
    tm = EXPERT_ROWS
    nk = D_FF_EXPERT // tf
    cap = xs.shape[1]

    def k_eff(g, k, nv):
        return jnp.where(g < nv[0], k, nk - 1)

    grid_spec = pltpu.PrefetchScalarGridSpec(
        num_scalar_prefetch=4,
        grid=(n_steps, nk),
        in_specs=[
            pl.BlockSpec((1, tm, D_MODEL), lambda g, k, te, tr, rw, nv: (te[g], tr[g], 0)),
            pl.BlockSpec((None, 1, D_MODEL, tf), lambda g, k, te, tr, rw, nv: (moe_i, te[g], 0, k_eff(g, k, nv))),
            pl.BlockSpec((None, 1, D_MODEL, tf), lambda g, k, te, tr, rw, nv: (moe_i, te[g], 0, k_eff(g, k, nv))),
            pl.BlockSpec((None, 1, tf, D_MODEL), lambda g, k, te, tr, rw, nv: (moe_i, te[g], k_eff(g, k, nv), 0)),
        ],
        out_specs=pl.BlockSpec((1, tm, D_MODEL), lambda g, k, te, tr, rw, nv: (te[g], tr[g], 0)),
        scratch_shapes=[pltpu.VMEM((tm, D_MODEL), F32)],
    )
    return pl.pallas_call(
        _expert_kernel,
        grid_spec=grid_spec,
        out_shape=jax.ShapeDtypeStruct((N_EXPERTS, cap, D_MODEL), BF16),
        compiler_params=_cparams(2),
        name="moe_experts",
    )(tile_e, tile_r, tile_rows, n_valid, xs, wg, wu, wd)


def _combine_kernel(base_ref, cnt_ref, lim_ref, ys_ref, rel_ref, gate_ref, x1_ref, mod_ref, lng_ref, lnb_ref, o_ref,
                    slab, xslab, acc_sc, sems, xsem, *, ts):
    i = pl.program_id(0)
    last = pl.num_programs(0) - 1
    par = lax.rem(i, 2)
    n_chunks = -(-(ts + ROW_ALIGN - 1) // SLOT_CHUNK)
    slot = lax.broadcasted_iota(jnp.int32, (ts, SLOT_CHUNK), 1).astype(F32)

    def slab_start(step, e, c):
        want = _floor_align(base_ref[step * N_EXPERTS + e]) + c * SLOT_CHUNK
        start = jnp.maximum(jnp.minimum(want, lim_ref[e] - SLOT_CHUNK), 0)
        return pl.multiple_of(start, ROW_ALIGN), want - start

    def first_copy(step, e):
        start, _ = slab_start(step, e, 0)
        p = lax.rem(step, 2)
        return _group_copy(ys_ref.at[e, pl.ds(start, SLOT_CHUNK), :], slab.at[p, e], sems.at[p, e])

    def contribution(e, c, rows):
        a = base_ref[i * N_EXPERTS + e]
        _, moved = slab_start(i, e, c)
        pos = rel_ref[:, e:e + 1] + (a - _floor_align(a)).astype(F32)
        in_chunk = jnp.where(pos >= float(c * SLOT_CHUNK), 1.0, 0.0)
        onehot = jnp.where(pos + (moved - c * SLOT_CHUNK).astype(F32) == slot, in_chunk, 0.0).astype(BF16)
        return _dot(onehot, rows) * gate_ref[:, e:e + 1]

    @pl.when(i == 0)
    def _():
        for e in range(N_EXPERTS):
            first_copy(i, e).start()

    @pl.when(i < last)
    def _():
        for e in range(N_EXPERTS):
            first_copy(i + 1, e).start()

    for e in range(N_EXPERTS):
        first_copy(i, e).wait()
    f = contribution(0, 0, slab[par, 0])
    for e in range(1, N_EXPERTS):
        f = f + contribution(e, 0, slab[par, e])
    acc_sc[...] = f

    for e in range(N_EXPERTS):
        a = base_ref[i * N_EXPERTS + e]
        rows_in_block = a - _floor_align(a) + cnt_ref[i * N_EXPERTS + e]
        for c in range(1, n_chunks):
            @pl.when(rows_in_block > c * SLOT_CHUNK)
            def _(e=e, c=c):
                start, _ = slab_start(i, e, c)
                cp = _group_copy(ys_ref.at[e, pl.ds(start, SLOT_CHUNK), :], xslab, xsem)
                cp.start()
                cp.wait()
                acc_sc[...] += contribution(e, c, xslab[...])

    gate2 = mod_ref[0, 5:6, :]
    o_ref[...] = _layer_norm(DEEPNORM_ALPHA * x1_ref[...] + gate2 * acc_sc[...], lng_ref[...], lnb_ref[...])


def _combine(l, base, cnt, lim, ys, rel, gates, x1, mod, ln_g, ln_b, ts, tiles_per_batch):
    t = x1.shape[0]
    grid_spec = pltpu.PrefetchScalarGridSpec(
        num_scalar_prefetch=3,
        grid=(t // ts,),
        in_specs=[
            pl.BlockSpec(memory_space=pl.ANY),
            pl.BlockSpec((ts, LANES), lambda i, *_: (i, 0)),
            pl.BlockSpec((ts, LANES), lambda i, *_: (i, 0)),
            pl.BlockSpec((ts, D_MODEL), lambda i, *_: (i, 0)),
            pl.BlockSpec((None, 1, 8, D_MODEL), lambda i, *_: (l, i // tiles_per_batch, 0, 0)),
            pl.BlockSpec((None, 1, D_MODEL), lambda i, *_: (l, 0, 0)),
            pl.BlockSpec((None, 1, D_MODEL), lambda i, *_: (l, 0, 0)),
        ],
        out_specs=pl.BlockSpec((ts, D_MODEL), lambda i, *_: (i, 0)),
        scratch_shapes=[
            pltpu.VMEM((2, N_EXPERTS, SLOT_CHUNK, D_MODEL), BF16),
            pltpu.VMEM((SLOT_CHUNK, D_MODEL), BF16),
            pltpu.VMEM((ts, D_MODEL), F32),
            pltpu.SemaphoreType.DMA((2, N_EXPERTS)),
            pltpu.SemaphoreType.DMA(()),
        ],
    )
    return pl.pallas_call(
        functools.partial(_combine_kernel, ts=ts),
        grid_spec=grid_spec,
        out_shape=jax.ShapeDtypeStruct((t, D_MODEL), F32),
        compiler_params=_cparams(1),
        name="moe_combine",
    )(base, cnt, lim, ys, rel, gates, x1, mod, ln_g, ln_b)


def _expert_tile_plan(base, cnt, nt, n_steps):
    tm = EXPERT_ROWS
    i32 = lambda v: v.astype(jnp.int32)
    last_base = base.reshape(nt, N_EXPERTS)[-1]
    last_cnt = cnt.reshape(nt, N_EXPERTS)[-1]
    total = last_base + last_cnt
    tiles = jnp.maximum((total + tm - 1) // tm, 1)
    rest = total - (tiles - 1) * tm
    last_rows = jnp.full_like(rest, EXPERT_ROW_SIZES[0])
    for small, big in zip(EXPERT_ROW_SIZES[:-1], EXPERT_ROW_SIZES[1:]):
        last_rows = last_rows + jnp.where(rest > small, big - small, 0)
    ends = jnp.cumsum(tiles)
    n_valid = ends[-1]
    g = jnp.minimum(jnp.arange(n_steps, dtype=jnp.int32), n_valid - 1)
    tile_e = jnp.sum((g[:, None] >= ends[None, :]).astype(jnp.int32), axis=1)
    onehot = (tile_e[:, None] == jnp.arange(N_EXPERTS, dtype=jnp.int32)[None, :]).astype(jnp.int32)
    pick = lambda v: jnp.sum(onehot * v[None, :], axis=1)
    tile_r = g - pick(ends - tiles)
    tile_rows = jnp.where(tile_r == pick(tiles) - 1, pick(last_rows), tm)
    lim = (tiles - 1) * tm + last_rows
    return i32(tile_e), i32(tile_r), i32(tile_rows), i32(n_valid.reshape(1)), i32(lim)


def _top2_moe(l, moe_i, h2, x1, mod, routing, wg, wu, wd, ln_g, ln_b, ts, tiles_per_batch):
    rel, gates, relt, base3, cnt3 = routing
    t = h2.shape[0]
    nt = t // ts
    tm, tf = EXPERT_ROWS, EXPERT_FF_TILE
    base = base3[:, 0, :N_EXPERTS].reshape(-1)
    cnt = cnt3[:, 0, :N_EXPERTS].reshape(-1)
    zero_rows = -(-tm // SLOT_CHUNK) * SLOT_CHUNK
    reach_rows = _dispatch_chunks(ts) * DISPATCH_CHUNK
    cap = -(-(t + ROW_ALIGN + reach_rows + zero_rows) // tm) * tm
    n_steps = -(-TOP_K * t // tm) + N_EXPERTS
    xs = _dispatch(base, cnt, h2, relt, ts, cap, zero_rows)
    tile_e, tile_r, tile_rows, n_valid, lim = _expert_tile_plan(base, cnt, nt, n_steps)
    ys = _expert_ffn(moe_i, tile_e, tile_r, tile_rows, n_valid, xs, wg, wu, wd, tf, n_steps)
    return _combine(l, base, cnt, lim, ys, rel, gates, x1, mod, ln_g, ln_b, ts, tiles_per_batch)


def kernel(x, c, positions, w_in, q_norm_g, kv_norm_g, w_uq, w_ukv, conv_w, mix_norm_g, w_o, w_ada, b_ada,
           ln1_g, ln1_b, ln2_g, ln2_b, ffn_w_gate, ffn_w_up, ffn_w_down, moe_router_w, moe_router_b,
           moe_w_gate, moe_w_up, moe_w_down):
    b, s, _ = x.shape
    n_layers = w_in.shape[0]
    ts = min(TOKEN_TILE, s)
    t = b * s
    tiles_per_batch = s // ts

    bp = -(-b // ROW_ALIGN) * ROW_ALIGN
    c_pad = jnp.pad(c, ((0, bp - b), (0, 0)))
    mod = _ada_mod(c_pad, w_ada, b_ada)
    mod = jnp.pad(jnp.transpose(mod[:, :, :b, :], (0, 2, 1, 3)), ((0, 0), (0, 0), (0, 2), (0, 0)))
    rope_tab = _rope_tables(positions, ts)

    rows = lambda v: v.reshape(v.shape[0], 1, -1)
    o3 = Q_LORA + KV_LORA + QK_ROPE
    w_in_p = jnp.concatenate(
        [w_in[:, :, :o3], jnp.zeros((n_layers, D_MODEL, D_IN_PAD - w_in.shape[2]), F32), w_in[:, :, o3:]], axis=2
    ).astype(BF16)
    w_uq_t = jnp.transpose(jnp.pad(
        w_uq.reshape(n_layers, Q_LORA, N_HEADS, QK_NOPE + QK_ROPE),
        ((0, 0), (0, 0), (0, 0), (0, QK_PAD - QK_NOPE - QK_ROPE))
    ).reshape(n_layers, Q_LORA, N_HEADS * QK_PAD), (0, 2, 1)).astype(BF16)
    w_ukv_h = w_ukv.reshape(n_layers, KV_LORA, N_HEADS, QK_NOPE + V_HEAD)
    w_k = w_ukv_h[..., :QK_NOPE].reshape(n_layers, KV_LORA, N_HEADS * QK_NOPE).astype(BF16)
    w_v_t = jnp.transpose(w_ukv_h[..., QK_NOPE:].reshape(n_layers, KV_LORA, N_HEADS * V_HEAD), (0, 2, 1)).astype(BF16)
    w_o_b = w_o.astype(BF16)
    conv_w_p = jnp.pad(conv_w, ((0, 0), (0, 8 - CONV_WIDTH), (0, 0)))
    g_attn, g_conv = rows(mix_norm_g[:, :D_ATTN]), rows(mix_norm_g[:, D_ATTN:])
    ffn_wg, ffn_wu, ffn_wd = ffn_w_gate.astype(BF16), ffn_w_up.astype(BF16), ffn_w_down.astype(BF16)
    rw_hi = moe_router_w.astype(BF16)
    rw_lo = (moe_router_w - rw_hi.astype(F32)).astype(BF16)
    rw = jnp.pad(jnp.concatenate([rw_hi, rw_lo], axis=2), ((0, 0), (0, 0), (0, LANES - 2 * N_EXPERTS)))
    rb = rows(jnp.pad(moe_router_b, ((0, 0), (0, LANES - N_EXPERTS))))

    for l in range(n_layers):
        qt, k, vt, convn = _inproj(l, x, mod, w_in_p, w_uq_t, w_k, w_v_t, rows(q_norm_g), rows(kv_norm_g),
                                   conv_w_p, g_conv, rope_tab, ts)
        attn = _attention(qt, k, vt, ts)
        i = l // 2
        if l % 2 == 0:
            x = _outproj_dense_ffn(l, i, attn, convn, x, mod, w_o_b, g_attn, rows(ln1_g), rows(ln1_b),
                                   ffn_wg, ffn_wu, ffn_wd, rows(ln2_g), rows(ln2_b), ts)
        else:
            x1, h2, *routing = _outproj_router(l, i, attn, convn, x, mod, w_o_b, g_attn, rows(ln1_g), rows(ln1_b),
                                               rw, rb, ts)
            x = _top2_moe(l, i, h2.reshape(t, D_MODEL), x1.reshape(t, D_MODEL), mod, routing,
                          moe_w_gate, moe_w_up, moe_w_down, rows(ln2_g), rows(ln2_b), ts, tiles_per_batch)
            x = x.reshape(b, s, D_MODEL)
    return x
```

```python
import functools

import jax
import jax.numpy as jnp
from jax import lax
from jax.experimental import pallas as pl
from jax.experimental.pallas import tpu as pltpu

F32 = jnp.float32
BF16 = jnp.bfloat16

D_MODEL = 1024
N_HEADS = 4
QK_NOPE = 128
QK_ROPE = 64
V_HEAD = 128
Q_LORA = 256
KV_LORA = 128
D_ATTN = N_HEADS * V_HEAD
D_CONV = D_MODEL - D_ATTN
CONV_WIDTH = 3
ROPE_BASE = 10000.0
D_FF = 2816
N_EXPERTS = 8
TOP_K = 2
D_FF_EXPERT = 3584
RMS_EPS = 1e-6
LN_EPS = 1e-5
DEPTH = 2
DEEPNORM_ALPHA = (2 * DEPTH) ** 0.25
SM_SCALE = (QK_NOPE + QK_ROPE) ** -0.5
LOG2E = 1.4426950408889634
Q_SCALE = SM_SCALE * LOG2E

LANES = 128
QK_PAD = 2 * LANES
LATENT_PAD = 512
TOKEN_TILE = 512
EXPERT_FF_TILE = 512
ROW_ALIGN = 16
SLOT_CHUNK = 256
EXPERT_ROW_STEP = 384
EXPERT_ROW_SIZES = tuple(EXPERT_ROW_STEP * n for n in range(1, 7))
EXPERT_ROWS = EXPERT_ROW_SIZES[-1]
VMEM_LIMIT = 56 * 1024 * 1024
NEG_BIG = -1e30
NOT_ROUTED = -4096.0


def _cparams(n_axes):
    return pltpu.CompilerParams(dimension_semantics=("arbitrary",) * n_axes, vmem_limit_bytes=VMEM_LIMIT)


def _dot(a, b):
    return jnp.dot(a, b, preferred_element_type=F32)


def _rms(x, g):
    return x * lax.rsqrt(jnp.mean(x * x, axis=-1, keepdims=True) + RMS_EPS) * g


def _layer_norm(x, g, b):
    mu = jnp.mean(x, axis=-1, keepdims=True)
    xc = x - mu
    var = jnp.mean(xc * xc, axis=-1, keepdims=True)
    return xc * lax.rsqrt(var + LN_EPS) * g + b


def _silu(x):
    return x * jax.nn.sigmoid(x)


def _ada_kernel(c_ref, w_ref, b_ref, o_ref):
    ca = _silu(c_ref[...]).astype(BF16)
    o_ref[0, 0] = _dot(ca, w_ref[0].astype(BF16)) + b_ref[0]


def _ada_mod(c_pad, w_ada, b_ada):
    n_layers = w_ada.shape[0]
    bp = c_pad.shape[0]
    return pl.pallas_call(
        _ada_kernel,
        grid=(n_layers, 6),
        in_specs=[
            pl.BlockSpec((bp, D_MODEL), lambda l, j: (0, 0)),
            pl.BlockSpec((1, D_MODEL, D_MODEL), lambda l, j: (l, 0, j)),
            pl.BlockSpec((1, 1, D_MODEL), lambda l, j: (l, 0, j)),
        ],
        out_specs=pl.BlockSpec((1, 1, bp, D_MODEL), lambda l, j: (l, j, 0, 0)),
        out_shape=jax.ShapeDtypeStruct((n_layers, 6, bp, D_MODEL), F32),
        compiler_params=_cparams(2),
        name="ada_mod",
    )(c_pad, w_ada, b_ada.reshape(n_layers, 1, 6 * D_MODEL))


ROPE_HALF = QK_ROPE // 2


def _rope_kernel(pos_ref, f_ref, o_ref):
    ang = pos_ref[0].astype(F32) * f_ref[...]
    o_ref[0, 0:ROPE_HALF, :] = jnp.cos(ang)
    o_ref[0, ROPE_HALF:QK_ROPE, :] = jnp.sin(ang)


def _rope_tables(positions, ts):
    nt = positions.size // ts
    inv_freq = ROPE_BASE ** (-jnp.arange(0, QK_ROPE, 2, dtype=F32) / QK_ROPE)
    return pl.pallas_call(
        _rope_kernel,
        grid=(nt,),
        in_specs=[pl.BlockSpec((1, 1, ts), lambda i: (i, 0, 0)), pl.BlockSpec((ROPE_HALF, 1), lambda i: (0, 0))],
        out_specs=pl.BlockSpec((1, QK_ROPE, ts), lambda i: (i, 0, 0)),
        out_shape=jax.ShapeDtypeStruct((nt, QK_ROPE, ts), F32),
        compiler_params=_cparams(1),
        name="rope_tables",
    )(positions.reshape(nt, 1, ts), inv_freq.reshape(ROPE_HALF, 1))


def _inproj_kernel(x_ref, mod_ref, wlat_ref, wgate_ref, wu_ref, wuqt_ref, wk_ref, wvt_ref, qg_ref, kvg_ref, cw_ref,
                   cg_ref, rope_ref, qt_ref, k_ref, vt_ref, cv_ref, cu_ext, *, ts):
    j = pl.program_id(1)

    @pl.when(j == 0)
    def _():
        cu_ext[0:8, :] = jnp.zeros((8, D_CONV), F32)

    @pl.when(j > 0)
    def _():
        cu_ext[0:8, :] = cu_ext[ts:ts + 8, :]

    x = x_ref[0]
    shift = mod_ref[0, 0:1, :]
    scale = mod_ref[0, 1:2, :]
    h = (x * (1.0 + scale) + shift).astype(BF16)
    z_lat = _dot(h, wlat_ref[...])
    c_q = z_lat[:, 0:Q_LORA]
    c_kv = z_lat[:, Q_LORA:Q_LORA + KV_LORA]
    k_rope = z_lat[:, Q_LORA + KV_LORA:LATENT_PAD]
    z_gate = _dot(h, wgate_ref[...])
    gate_b = z_gate[:, 0:D_CONV]
    gate_c = z_gate[:, D_CONV:2 * D_CONV]
    u = _dot(h, wu_ref[...])
    cqn = _rms(c_q, qg_ref[...])
    ckvn = _rms(c_kv, kvg_ref[...])
    q_t = _dot(wuqt_ref[...], jnp.transpose(cqn).astype(BF16))
    k_nope = _dot(ckvn.astype(BF16), wk_ref[...])
    v_t = _dot(wvt_ref[...], jnp.transpose(ckvn).astype(BF16))

    cos = rope_ref[0, 0:ROPE_HALF, :]
    sin = rope_ref[0, ROPE_HALF:QK_ROPE, :]

    def rope_t(block):
        x1, x2 = block[0:ROPE_HALF, :], block[ROPE_HALF:QK_ROPE, :]
        return x1 * cos - x2 * sin, x2 * cos + x1 * sin

    k1, k2 = rope_t(jnp.transpose(k_rope)[0:QK_ROPE, :])
    k_rot = jnp.transpose(jnp.concatenate([k1, k2, jnp.zeros((LANES - QK_ROPE, ts), F32)], axis=0)).astype(BF16)
    for hh in range(N_HEADS):
        qo = hh * QK_PAD
        r0 = qo + QK_NOPE
        q1, q2 = rope_t(q_t[r0:r0 + QK_ROPE, :])
        qt_ref[0, hh, 0:QK_NOPE, :] = (q_t[qo:r0, :] * Q_SCALE).astype(BF16)
        qt_ref[0, hh, QK_NOPE:QK_NOPE + ROPE_HALF, :] = (q1 * Q_SCALE).astype(BF16)
        qt_ref[0, hh, QK_NOPE + ROPE_HALF:QK_NOPE + QK_ROPE, :] = (q2 * Q_SCALE).astype(BF16)
        qt_ref[0, hh, QK_NOPE + QK_ROPE:QK_PAD, :] = jnp.zeros((QK_PAD - QK_NOPE - QK_ROPE, ts), BF16)
        k_ref[0, hh, :, 0:LANES] = k_nope[:, hh * QK_NOPE:(hh + 1) * QK_NOPE].astype(BF16)
        k_ref[0, hh, :, LANES:QK_PAD] = k_rot
        vt_ref[0, hh, 0] = v_t[hh * V_HEAD:(hh + 1) * V_HEAD, :].astype(BF16)

    cu_ext[8:ts + 8, :] = gate_c * u
    conv = (cw_ref[2:3, :] * cu_ext[8:ts + 8, :]
            + cw_ref[1:2, :] * cu_ext[pl.ds(7, ts), :]
            + cw_ref[0:1, :] * cu_ext[pl.ds(6, ts), :])
    conv = gate_b * conv
    cv_ref[0] = _rms(conv, cg_ref[...]).astype(BF16)


def _inproj(l, x, mod, w_lat, w_gate, w_u, w_uq_t, w_k, w_v_t, qg, kvg, conv_w, conv_g, rope_tab, ts):
    b, s, _ = x.shape
    nsb = s // ts
    const = lambda shape: pl.BlockSpec((None,) + shape, lambda bi, j: (l,) + (0,) * len(shape))
    return pl.pallas_call(
        functools.partial(_inproj_kernel, ts=ts),
        grid=(b, nsb),
        in_specs=[
            pl.BlockSpec((1, ts, D_MODEL), lambda bi, j: (bi, j, 0)),
            pl.BlockSpec((None, 1, 8, D_MODEL), lambda bi, j: (l, bi, 0, 0)),
            const((D_MODEL, LATENT_PAD)),
            const((D_MODEL, 2 * D_CONV)),
            const((D_MODEL, D_CONV)),
            const((N_HEADS * QK_PAD, Q_LORA)),
            const((KV_LORA, N_HEADS * QK_NOPE)),
            const((N_HEADS * V_HEAD, KV_LORA)),
            const((1, Q_LORA)),
            const((1, KV_LORA)),
            const((8, D_CONV)),
            const((1, D_CONV)),
            pl.BlockSpec((1, QK_ROPE, ts), lambda bi, j: (bi * nsb + j, 0, 0)),
        ],
        out_specs=[
            pl.BlockSpec((1, N_HEADS, QK_PAD, ts), lambda bi, j: (bi, 0, 0, j)),
            pl.BlockSpec((1, N_HEADS, ts, QK_PAD), lambda bi, j: (bi, 0, j, 0)),
            pl.BlockSpec((1, N_HEADS, 1, V_HEAD, ts), lambda bi, j: (bi, 0, j, 0, 0)),
            pl.BlockSpec((1, ts, D_CONV), lambda bi, j: (bi, j, 0)),
        ],
        out_shape=[
            jax.ShapeDtypeStruct((b, N_HEADS, QK_PAD, s), BF16),
            jax.ShapeDtypeStruct((b, N_HEADS, s, QK_PAD), BF16),
            jax.ShapeDtypeStruct((b, N_HEADS, nsb, V_HEAD, ts), BF16),
            jax.ShapeDtypeStruct((b, s, D_CONV), BF16),
        ],
        scratch_shapes=[pltpu.VMEM((ts + 8, D_CONV), F32)],
        compiler_params=_cparams(2),
        name="inproj",
    )(x, mod, w_lat, w_gate, w_u, w_uq_t, w_k, w_v_t, qg, kvg, conv_w, conv_g, rope_tab)


Q_CHUNK = 256


K_ROWS = 16


def _attn_kernel(qt_ref, k_ref, vt_ref, o_ref, *scratch, tq):
    i = pl.program_id(1)
    n_chunks = tq // Q_CHUNK
    per_chain = 5
    chains = [(hh, c) + tuple(scratch[per_chain * (hh * n_chunks + c):per_chain * (hh * n_chunks + c + 1)])
              for hh in range(N_HEADS) for c in range(n_chunks)]
    for _, _, m_sc, l_sc, acc_sc, _, _ in chains:
        m_sc[...] = jnp.full(m_sc.shape, NEG_BIG, F32)
        l_sc[...] = jnp.zeros(l_sc.shape, F32)
        acc_sc[...] = jnp.zeros(acc_sc.shape, F32)

    def scores(chain, jk, buf):
        hh, c = chain[0], chain[1]
        kj = k_ref[0, hh, pl.ds(pl.multiple_of(jk * tq, tq), tq), :]
        chain[5 + buf][...] = _dot(kj, qt_ref[0, hh, :, c * Q_CHUNK:(c + 1) * Q_CHUNK])

    def softmax_values(chain, jk, buf, diagonal):
        hh, c, m_sc, l_sc, acc_sc = chain[:5]
        s_sc = chain[5 + buf]
        nk = (c + 1) * Q_CHUNK if diagonal else tq
        if diagonal:
            d0 = nk - Q_CHUNK
            key = lax.broadcasted_iota(jnp.int32, (Q_CHUNK, Q_CHUNK), 0)
            qry = lax.broadcasted_iota(jnp.int32, (Q_CHUNK, Q_CHUNK), 1)
            s_sc[d0:nk, :] = jnp.where(key <= qry, s_sc[d0:nk, :], NEG_BIG)
        blk_max = s_sc[0:K_ROWS, :]
        for r in range(K_ROWS, nk, K_ROWS):
            blk_max = jnp.maximum(blk_max, s_sc[r:r + K_ROWS, :])
        m_prev = m_sc[...]
        m_new = jnp.maximum(m_prev, jnp.max(blk_max, axis=0, keepdims=True))
        alpha = jnp.exp2(m_prev - m_new)
        p_sum = jnp.zeros((K_ROWS, Q_CHUNK), F32)
        acc = alpha * acc_sc[...]
        for kt in range(0, nk, Q_CHUNK):
            parts = []
            for r in range(kt, kt + Q_CHUNK, K_ROWS):
                p = jnp.exp2(s_sc[r:r + K_ROWS, :] - m_new)
                p_sum = p_sum + p
                parts.append(p.astype(BF16))
            acc = acc + _dot(vt_ref[0, hh, jk, :, kt:kt + Q_CHUNK], jnp.concatenate(parts, axis=0))
        l_sc[...] = alpha * l_sc[...] + jnp.sum(p_sum, axis=0, keepdims=True)
        acc_sc[...] = acc
        m_sc[...] = m_new

    def stage(cur, cur_buf, diagonal, nxt=None):
        for chain in chains:
            if nxt is not None:
                scores(chain, nxt, 1 - cur_buf)
            softmax_values(chain, cur, cur_buf, diagonal)

    for chain in chains:
        scores(chain, 0, 0)

    def pair(jj, carry):
        stage(2 * jj, 0, False, nxt=2 * jj + 1)
        stage(2 * jj + 1, 1, False, nxt=2 * jj + 2)
        return carry

    lax.fori_loop(0, i // 2, pair, 0)

    @pl.when(lax.rem(i, 2) == 0)
    def _():
        stage(i, 0, True)

    @pl.when(lax.rem(i, 2) == 1)
    def _():
        stage(i - 1, 0, False, nxt=i)
        stage(i, 1, True)

    for hh, c, _, l_sc, acc_sc, _, _ in chains:
        o_ref[0, c * Q_CHUNK:(c + 1) * Q_CHUNK, hh * V_HEAD:(hh + 1) * V_HEAD] = (
            jnp.transpose(acc_sc[...] / l_sc[...]).astype(BF16))


def _attention(qt, k, vt, tq):
    b, nh, s, _ = k.shape
    return pl.pallas_call(
        functools.partial(_attn_kernel, tq=tq),
        grid=(b, s // tq),
        in_specs=[
            pl.BlockSpec((1, nh, QK_PAD, tq), lambda bi, i: (bi, 0, 0, i)),
            pl.BlockSpec((1, nh, s, QK_PAD), lambda bi, i: (bi, 0, 0, 0)),
            pl.BlockSpec((1, nh, s // tq, V_HEAD, tq), lambda bi, i: (bi, 0, 0, 0, 0)),
        ],
        out_specs=pl.BlockSpec((1, tq, D_ATTN), lambda bi, i: (bi, i, 0)),
        out_shape=jax.ShapeDtypeStruct((b, s, D_ATTN), BF16),
        scratch_shapes=[
            pltpu.VMEM((1, Q_CHUNK), F32), pltpu.VMEM((1, Q_CHUNK), F32), pltpu.VMEM((V_HEAD, Q_CHUNK), F32),
            pltpu.VMEM((tq, Q_CHUNK), F32), pltpu.VMEM((tq, Q_CHUNK), F32),
        ] * (nh * (tq // Q_CHUNK)),
        compiler_params=_cparams(2),
        name="attention",
    )(qt, k, vt)


OUT_ROWS = 256


def _row_chunks(ts):
    return [slice(r, r + OUT_ROWS) for r in range(0, ts, OUT_ROWS)]


def _outproj_matmuls(attn_ref, cv_ref, wo_ref, ga_ref, rows):
    an = _rms(attn_ref[0, rows, :].astype(F32), ga_ref[...]).astype(BF16)
    return _dot(an, wo_ref[0:D_ATTN, :]) + _dot(cv_ref[0, rows, :], wo_ref[D_ATTN:D_MODEL, :])


def _outproj_norm(y, x_ref, mod_ref, lng_ref, lnb_ref, rows):
    x1 = _layer_norm(DEEPNORM_ALPHA * x_ref[0, rows, :] + mod_ref[0, 2:3, :] * y, lng_ref[...], lnb_ref[...])
    h2 = x1 * (1.0 + mod_ref[0, 4:5, :]) + mod_ref[0, 3:4, :]
    return x1, h2


def _outproj_router_kernel(attn_ref, cv_ref, x_ref, mod_ref, wo_ref, ga_ref, lng_ref, lnb_ref, rw_ref, rb_ref,
                           x1_ref, h2_ref, rel_ref, gate_ref, relt_ref, base_ref, cnt_ref, run_sc, *, ts):
    first = jnp.logical_and(pl.program_id(0) == 0, pl.program_id(1) == 0)

    @pl.when(first)
    def _():
        run_sc[...] = jnp.zeros(run_sc.shape, F32)

    chunks = _row_chunks(ts)
    ys = [_outproj_matmuls(attn_ref, cv_ref, wo_ref, ga_ref, rows) for rows in chunks]
    lane = lax.broadcasted_iota(jnp.int32, (OUT_ROWS, LANES), 1)
    r_i = lax.broadcasted_iota(jnp.int32, (OUT_ROWS, OUT_ROWS), 0)
    c_i = lax.broadcasted_iota(jnp.int32, (OUT_ROWS, OUT_ROWS), 1)
    tri = jnp.where(c_i < r_i, 1.0, 0.0).astype(BF16)
    n_tile = jnp.zeros((1, LANES), F32)
    for rows, y in zip(chunks, ys):
        x1, h2 = _outproj_norm(y, x_ref, mod_ref, lng_ref, lnb_ref, rows)
        x1_ref[0, rows, :] = x1
        h2_hi = h2.astype(BF16)
        h2_ref[0, rows, :] = h2_hi

        h2_lo = (h2 - h2_hi.astype(F32)).astype(BF16)
        prod = _dot(h2_hi, rw_ref[...]) + _dot(h2_lo, rw_ref[...])
        logits = prod + pltpu.roll(prod, LANES - N_EXPERTS, 1) + rb_ref[...]
        logits = jnp.where(lane < N_EXPERTS, logits, NEG_BIG)
        v1 = jnp.max(logits, axis=-1, keepdims=True)
        i1 = jnp.min(jnp.where(logits == v1, lane, LANES), axis=-1, keepdims=True)
        rest = jnp.where(lane == i1, NEG_BIG, logits)
        v2 = jnp.max(rest, axis=-1, keepdims=True)
        i2 = jnp.min(jnp.where(rest == v2, lane, LANES), axis=-1, keepdims=True)
        e21 = jnp.exp(v2 - v1)
        g1 = 1.0 / (1.0 + e21)
        g2 = e21 / (1.0 + e21)
        is1 = lane == i1
        is2 = lane == i2
        sel = jnp.where(jnp.logical_or(is1, is2), 1.0, 0.0)
        gate_ref[rows, :] = jnp.where(is1, g1, 0.0) + jnp.where(is2, g2, 0.0)

        rank = _dot(tri, sel.astype(BF16)) + n_tile
        rel = jnp.where(sel > 0.0, rank, NOT_ROUTED)
        rel_ref[rows, :] = rel
        relt_ref[0, :, rows] = jnp.transpose(rel)[0:N_EXPERTS, :]
        n_tile = n_tile + jnp.sum(sel, axis=0, keepdims=True)

    base_ref[0] = run_sc[...].astype(jnp.int32)
    cnt_ref[0] = n_tile.astype(jnp.int32)
    run_sc[...] = run_sc[...] + n_tile


def _outproj_router(l, moe_i, attn, convn, x, mod, w_o, g_attn, ln_g, ln_b, rw, rb, ts):
    b, s, _ = x.shape
    nsb = s // ts
    const = lambda shape, idx=l: pl.BlockSpec((None,) + shape, lambda bi, j: (idx,) + (0,) * len(shape))
    tile3 = lambda w: pl.BlockSpec((1, ts, w), lambda bi, j: (bi, j, 0))
    in_specs = [
        tile3(D_ATTN), tile3(D_CONV), tile3(D_MODEL),
        pl.BlockSpec((None, 1, 8, D_MODEL), lambda bi, j: (l, bi, 0, 0)),
        const((D_MODEL, D_MODEL)), const((1, D_ATTN)), const((1, D_MODEL)), const((1, D_MODEL)),
    ]
    out_specs = [tile3(D_MODEL), tile3(D_MODEL)]
    out_shape = [jax.ShapeDtypeStruct((b, s, D_MODEL), F32), jax.ShapeDtypeStruct((b, s, D_MODEL), BF16)]
    args = [attn, convn, x, mod, w_o, g_attn, ln_g, ln_b]
    nt = b * nsb
    flat = lambda w: pl.BlockSpec((ts, w), lambda bi, j: (bi * nsb + j, 0))
    per_tile = lambda r, w: pl.BlockSpec((1, r, w), lambda bi, j: (bi * nsb + j, 0, 0))
    in_specs += [const((D_MODEL, LANES), moe_i), const((1, LANES), moe_i)]
    out_specs += [flat(LANES), flat(LANES), per_tile(N_EXPERTS, ts), per_tile(1, LANES), per_tile(1, LANES)]
    out_shape += [
        jax.ShapeDtypeStruct((nt * ts, LANES), F32), jax.ShapeDtypeStruct((nt * ts, LANES), F32),
        jax.ShapeDtypeStruct((nt, N_EXPERTS, ts), F32),
        jax.ShapeDtypeStruct((nt, 1, LANES), jnp.int32), jax.ShapeDtypeStruct((nt, 1, LANES), jnp.int32),
    ]
    return pl.pallas_call(
        functools.partial(_outproj_router_kernel, ts=ts), grid=(b, nsb), in_specs=in_specs, out_specs=out_specs,
        out_shape=out_shape, scratch_shapes=[pltpu.VMEM((1, LANES), F32)],
        compiler_params=_cparams(2), name="outproj_router",
    )(*args, rw, rb)


FF_CHUNK = 256


def _swiglu(x, wg, wu, wd, width):
    out = None
    pending = None
    for c in range(0, width, FF_CHUNK):
        w = min(FF_CHUNK, width - c)
        g = _dot(x, wg(c, w))
        u = _dot(x, wu(c, w))
        if pending is not None:
            d = _dot(pending[0], wd(*pending[1]))
            out = d if out is None else out + d
        pending = ((_silu(g) * u).astype(BF16), (c, w))
    d = _dot(pending[0], wd(*pending[1]))
    return d if out is None else out + d


def _outproj_ffn_kernel(attn_ref, cv_ref, x_ref, mod_ref, wo_ref, ga_ref, ln1g_ref, ln1b_ref,
                        wg_ref, wu_ref, wd_ref, ln2g_ref, ln2b_ref, o_ref, *, ts):
    chunks = _row_chunks(ts)
    ys = [_outproj_matmuls(attn_ref, cv_ref, wo_ref, ga_ref, rows) for rows in chunks]
    normed = [_outproj_norm(y, x_ref, mod_ref, ln1g_ref, ln1b_ref, rows) for rows, y in zip(chunks, ys)]
    x1 = jnp.concatenate([n[0] for n in normed], axis=0)
    h = jnp.concatenate([n[1].astype(BF16) for n in normed], axis=0)
    f = _swiglu(h, lambda c, w: wg_ref[:, c:c + w], lambda c, w: wu_ref[:, c:c + w],
                lambda c, w: wd_ref[c:c + w, :], D_FF)
    gate2 = mod_ref[0, 5:6, :]
    o_ref[0] = _layer_norm(DEEPNORM_ALPHA * x1 + gate2 * f, ln2g_ref[...], ln2b_ref[...])


def _outproj_dense_ffn(l, dense_i, attn, convn, x, mod, w_o, g_attn, ln1_g, ln1_b, wg, wu, wd, ln2_g, ln2_b, ts):
    b, s, _ = x.shape
    const = lambda shape, idx=l: pl.BlockSpec((None,) + shape, lambda bi, j: (idx,) + (0,) * len(shape),
                                              pipeline_mode=pl.Buffered(1))
    tile3 = lambda w: pl.BlockSpec((1, ts, w), lambda bi, j: (bi, j, 0))
    return pl.pallas_call(
        functools.partial(_outproj_ffn_kernel, ts=ts),
        grid=(b, s // ts),
        in_specs=[
            tile3(D_ATTN), tile3(D_CONV), tile3(D_MODEL),
            pl.BlockSpec((None, 1, 8, D_MODEL), lambda bi, j: (l, bi, 0, 0)),
            const((D_MODEL, D_MODEL)), const((1, D_ATTN)), const((1, D_MODEL)), const((1, D_MODEL)),
            const((D_MODEL, D_FF), dense_i), const((D_MODEL, D_FF), dense_i), const((D_FF, D_MODEL), dense_i),
            const((1, D_MODEL)), const((1, D_MODEL)),
        ],
        out_specs=tile3(D_MODEL),
        out_shape=jax.ShapeDtypeStruct((b, s, D_MODEL), F32),
        compiler_params=_cparams(2),
        name="outproj_dense_ffn",
    )(attn, convn, x, mod, w_o, g_attn, ln1_g, ln1_b, wg, wu, wd, ln2_g, ln2_b)


def _group_copy(src, dst, sem):
    return pltpu.make_async_copy(src, dst, sem)


DISPATCH_CHUNK = 256


def _dispatch_chunks(ts):
    return -(-(ts + ROW_ALIGN - 1) // DISPATCH_CHUNK)


def _floor_align(v):
    return pl.multiple_of(lax.shift_left(lax.shift_right_logical(v, 4), 4), ROW_ALIGN)


def _dispatch_kernel(base_ref, cnt_ref, h_ref, relt_ref, xs_ref, stage, carry, zbuf, sems, xsem, zsem, *,
                     ts, zero_rows):
    i = pl.program_id(0)
    last = pl.num_programs(0) - 1
    par = lax.rem(i, 2)
    ch = DISPATCH_CHUNK
    n_chunks = _dispatch_chunks(ts)
    h = h_ref[...]
    slot = lax.broadcasted_iota(jnp.int32, (ch, ts), 0).astype(F32)

    def group(step, e):
        a = base_ref[step * N_EXPERTS + e]
        a16 = _floor_align(a)
        return a, a16, a - a16

    def onehot(e, c):
        _, _, off = group(i, e)
        row = relt_ref[0, e:e + 1, :] + off.astype(F32)
        return jnp.where(row == slot + float(c * ch), 1.0, 0.0).astype(BF16)

    def first_copy(step, e):
        _, a16, _ = group(step, e)
        p = lax.rem(step, 2)
        return _group_copy(stage.at[p, e, 0:ch], xs_ref.at[e, pl.ds(a16, ch), :], sems.at[p, e])

    @pl.when(i == 0)
    def _():
        carry[...] = jnp.zeros(carry.shape, BF16)
        stage[:, :, n_chunks * ch:, :] = jnp.zeros((2, N_EXPERTS, ROW_ALIGN, D_MODEL), BF16)

    for e in range(N_EXPERTS):
        rows = _dot(onehot(e, 0), h)
        stage[par, e, 0:ROW_ALIGN] = (rows[0:ROW_ALIGN] + carry[e].astype(F32)).astype(BF16)
        stage[par, e, ROW_ALIGN:ch] = rows[ROW_ALIGN:ch].astype(BF16)
        stage[par, e, ch:ch + ROW_ALIGN] = jnp.zeros((ROW_ALIGN, D_MODEL), BF16)

    @pl.when(i > 0)
    def _():
        for e in range(N_EXPERTS):
            first_copy(i - 1, e).wait()

    for e in range(N_EXPERTS):
        first_copy(i, e).start()

    for e in range(N_EXPERTS):
        _, a16, off = group(i, e)
        n = cnt_ref[i * N_EXPERTS + e]
        for c in range(1, n_chunks):
            @pl.when(off + n > c * ch)
            def _(e=e, c=c, a16=a16):
                lo = c * ch
                stage[par, e, lo:lo + ch] = _dot(onehot(e, c), h).astype(BF16)
                if c + 1 < n_chunks:
                    stage[par, e, lo + ch:lo + ch + ROW_ALIGN] = jnp.zeros((ROW_ALIGN, D_MODEL), BF16)
                cp = _group_copy(stage.at[par, e, lo:lo + ch], xs_ref.at[e, pl.ds(a16 + lo, ch), :], xsem)
                cp.start()
                cp.wait()

    for e in range(N_EXPERTS):
        a, a16, _ = group(i, e)
        nxt16 = _floor_align(a + cnt_ref[i * N_EXPERTS + e])
        carry[e] = stage[par, e, pl.ds(pl.multiple_of(nxt16 - a16, ROW_ALIGN), ROW_ALIGN), :]

    @pl.when(i == last)
    def _():
        for e in range(N_EXPERTS):
            first_copy(i, e).wait()
        zbuf[...] = jnp.zeros(zbuf.shape, BF16)
        for e in range(N_EXPERTS):
            n = cnt_ref[i * N_EXPERTS + e]
            end = _floor_align(base_ref[i * N_EXPERTS + e] + n + (ROW_ALIGN - 1))
            copies = [
                _group_copy(zbuf, xs_ref.at[e, pl.ds(end + r * SLOT_CHUNK, SLOT_CHUNK), :], zsem.at[e])
                for r in range(zero_rows // SLOT_CHUNK)
            ]
            for cp in copies:
                cp.start()
            for cp in copies:
                cp.wait()


def _dispatch(base, cnt, h2, relt, ts, cap, zero_rows):
    t = h2.shape[0]
    nt = t // ts
    n_chunks = _dispatch_chunks(ts)
    grid_spec = pltpu.PrefetchScalarGridSpec(
        num_scalar_prefetch=2,
        grid=(nt,),
        in_specs=[
            pl.BlockSpec((ts, D_MODEL), lambda i, b_, c_: (i, 0)),
            pl.BlockSpec((1, N_EXPERTS, ts), lambda i, b_, c_: (i, 0, 0)),
        ],
        out_specs=pl.BlockSpec(memory_space=pl.ANY),
        scratch_shapes=[
            pltpu.VMEM((2, N_EXPERTS, n_chunks * DISPATCH_CHUNK + ROW_ALIGN, D_MODEL), BF16),
            pltpu.VMEM((N_EXPERTS, ROW_ALIGN, D_MODEL), BF16),
            pltpu.VMEM((SLOT_CHUNK, D_MODEL), BF16),
            pltpu.SemaphoreType.DMA((2, N_EXPERTS)),
            pltpu.SemaphoreType.DMA(()),
            pltpu.SemaphoreType.DMA((N_EXPERTS,)),
        ],
    )
    return pl.pallas_call(
        functools.partial(_dispatch_kernel, ts=ts, zero_rows=zero_rows),
        grid_spec=grid_spec,
        out_shape=jax.ShapeDtypeStruct((N_EXPERTS, cap, D_MODEL), BF16),
        compiler_params=_cparams(1),
        name="moe_dispatch",
    )(base, cnt, h2, relt)


def _expert_kernel(te_ref, tr_ref, rows_ref, nv_ref, x_ref, wg_ref, wu_ref, wd_ref, o_ref, acc_sc):
    g_i = pl.program_id(0)
    k = pl.program_id(1)

    valid = g_i < nv_ref[0]

    @pl.when(jnp.logical_and(valid, k == 0))
    def _():
        acc_sc[...] = jnp.zeros(acc_sc.shape, F32)

    for m in EXPERT_ROW_SIZES:
        @pl.when(jnp.logical_and(valid, rows_ref[g_i] == m))
        def _(m=m):
            part = _swiglu(x_ref[0, 0:m, :], lambda c, w: wg_ref[0, :, c:c + w].astype(BF16),
                           lambda c, w: wu_ref[0, :, c:c + w].astype(BF16),
                           lambda c, w: wd_ref[0, c:c + w, :].astype(BF16), wg_ref.shape[2])
            total = acc_sc[0:m, :] + part
            acc_sc[0:m, :] = total
            o_ref[0, 0:m, :] = total.astype(BF16)


def _expert_ffn(moe_i, tile_e, tile_r, tile_rows, n_valid, xs, wg, wu, wd, tf, n_steps):
    tm = EXPERT_ROWS
    nk = D_FF_EXPERT // tf
    cap = xs.shape[1]

    def k_eff(g, k, nv):
        return jnp.where(g < nv[0], k, nk - 1)

    grid_spec = pltpu.PrefetchScalarGridSpec(
        num_scalar_prefetch=4,
        grid=(n_steps, nk),
        in_specs=[
            pl.BlockSpec((1, tm, D_MODEL), lambda g, k, te, tr, rw, nv: (te[g], tr[g], 0)),
            pl.BlockSpec((None, 1, D_MODEL, tf), lambda g, k, te, tr, rw, nv: (moe_i, te[g], 0, k_eff(g, k, nv))),
            pl.BlockSpec((None, 1, D_MODEL, tf), lambda g, k, te, tr, rw, nv: (moe_i, te[g], 0, k_eff(g, k, nv))),
            pl.BlockSpec((None, 1, tf, D_MODEL), lambda g, k, te, tr, rw, nv: (moe_i, te[g], k_eff(g, k, nv), 0)),
        ],
        out_specs=pl.BlockSpec((1, tm, D_MODEL), lambda g, k, te, tr, rw, nv: (te[g], tr[g], 0)),
        scratch_shapes=[pltpu.VMEM((tm, D_MODEL), F32)],
    )
    return pl.pallas_call(
        _expert_kernel,
        grid_spec=grid_spec,
        out_shape=jax.ShapeDtypeStruct((N_EXPERTS, cap, D_MODEL), BF16),
        compiler_params=_cparams(2),
        name="moe_experts",
    )(tile_e, tile_r, tile_rows, n_valid, xs, wg, wu, wd)


def _combine_kernel(base_ref, cnt_ref, lim_ref, ys_ref, rel_ref, gate_ref, x1_ref, mod_ref, lng_ref, lnb_ref, o_ref,
                    slab, xslab, acc_sc, sems, xsem, *, ts):
    i = pl.program_id(0)
    last = pl.num_programs(0) - 1
    par = lax.rem(i, 2)
    n_chunks = -(-(ts + ROW_ALIGN - 1) // SLOT_CHUNK)
    slot = lax.broadcasted_iota(jnp.int32, (ts, SLOT_CHUNK), 1).astype(F32)

    def slab_start(step, e, c):
        want = _floor_align(base_ref[step * N_EXPERTS + e]) + c * SLOT_CHUNK
        start = jnp.maximum(jnp.minimum(want, lim_ref[e] - SLOT_CHUNK), 0)
        return pl.multiple_of(start, ROW_ALIGN), want - start

    def first_copy(step, e):
        start, _ = slab_start(step, e, 0)
        p = lax.rem(step, 2)
        return _group_copy(ys_ref.at[e, pl.ds(start, SLOT_CHUNK), :], slab.at[p, e], sems.at[p, e])

    def contribution(e, c, rows):
        a = base_ref[i * N_EXPERTS + e]
        _, moved = slab_start(i, e, c)
        pos = rel_ref[:, e:e + 1] + (a - _floor_align(a)).astype(F32)
        in_chunk = jnp.where(pos >= float(c * SLOT_CHUNK), 1.0, 0.0)
        onehot = jnp.where(pos + (moved - c * SLOT_CHUNK).astype(F32) == slot, in_chunk, 0.0).astype(BF16)
        return _dot(onehot, rows) * gate_ref[:, e:e + 1]

    @pl.when(i == 0)
    def _():
        for e in range(N_EXPERTS):
            first_copy(i, e).start()

    @pl.when(i < last)
    def _():
        for e in range(N_EXPERTS):
            first_copy(i + 1, e).start()

    for e in range(N_EXPERTS):
        first_copy(i, e).wait()
    f = contribution(0, 0, slab[par, 0])
    for e in range(1, N_EXPERTS):
        f = f + contribution(e, 0, slab[par, e])
    acc_sc[...] = f

    for e in range(N_EXPERTS):
        a = base_ref[i * N_EXPERTS + e]
        rows_in_block = a - _floor_align(a) + cnt_ref[i * N_EXPERTS + e]
        for c in range(1, n_chunks):
            @pl.when(rows_in_block > c * SLOT_CHUNK)
            def _(e=e, c=c):
                start, _ = slab_start(i, e, c)
                cp = _group_copy(ys_ref.at[e, pl.ds(start, SLOT_CHUNK), :], xslab, xsem)
                cp.start()
                cp.wait()
                acc_sc[...] += contribution(e, c, xslab[...])

    gate2 = mod_ref[0, 5:6, :]
    o_ref[...] = _layer_norm(DEEPNORM_ALPHA * x1_ref[...] + gate2 * acc_sc[...], lng_ref[...], lnb_ref[...])


def _combine(l, base, cnt, lim, ys, rel, gates, x1, mod, ln_g, ln_b, ts, tiles_per_batch):
    t = x1.shape[0]
    grid_spec = pltpu.PrefetchScalarGridSpec(
        num_scalar_prefetch=3,
        grid=(t // ts,),
        in_specs=[
            pl.BlockSpec(memory_space=pl.ANY),
            pl.BlockSpec((ts, LANES), lambda i, *_: (i, 0)),
            pl.BlockSpec((ts, LANES), lambda i, *_: (i, 0)),
            pl.BlockSpec((ts, D_MODEL), lambda i, *_: (i, 0)),
            pl.BlockSpec((None, 1, 8, D_MODEL), lambda i, *_: (l, i // tiles_per_batch, 0, 0)),
            pl.BlockSpec((None, 1, D_MODEL), lambda i, *_: (l, 0, 0)),
            pl.BlockSpec((None, 1, D_MODEL), lambda i, *_: (l, 0, 0)),
        ],
        out_specs=pl.BlockSpec((ts, D_MODEL), lambda i, *_: (i, 0)),
        scratch_shapes=[
            pltpu.VMEM((2, N_EXPERTS, SLOT_CHUNK, D_MODEL), BF16),
            pltpu.VMEM((SLOT_CHUNK, D_MODEL), BF16),
            pltpu.VMEM((ts, D_MODEL), F32),
            pltpu.SemaphoreType.DMA((2, N_EXPERTS)),
            pltpu.SemaphoreType.DMA(()),
        ],
    )
    return pl.pallas_call(
        functools.partial(_combine_kernel, ts=ts),
        grid_spec=grid_spec,
        out_shape=jax.ShapeDtypeStruct((t, D_MODEL), F32),
        compiler_params=_cparams(1),
        name="moe_combine",
    )(base, cnt, lim, ys, rel, gates, x1, mod, ln_g, ln_b)


def _expert_tile_plan(base, cnt, nt, n_steps):
    tm = EXPERT_ROWS
    i32 = lambda v: v.astype(jnp.int32)
    last_base = base.reshape(nt, N_EXPERTS)[-1]
    last_cnt = cnt.reshape(nt, N_EXPERTS)[-1]
    total = last_base + last_cnt
    tiles = jnp.maximum((total + tm - 1) // tm, 1)
    rest = total - (tiles - 1) * tm
    last_rows = jnp.full_like(rest, EXPERT_ROW_SIZES[0])
    for small, big in zip(EXPERT_ROW_SIZES[:-1], EXPERT_ROW_SIZES[1:]):
        last_rows = last_rows + jnp.where(rest > small, big - small, 0)
    ends = jnp.cumsum(tiles)
    n_valid = ends[-1]
    g = jnp.minimum(jnp.arange(n_steps, dtype=jnp.int32), n_valid - 1)
    tile_e = jnp.sum((g[:, None] >= ends[None, :]).astype(jnp.int32), axis=1)
    onehot = (tile_e[:, None] == jnp.arange(N_EXPERTS, dtype=jnp.int32)[None, :]).astype(jnp.int32)
    pick = lambda v: jnp.sum(onehot * v[None, :], axis=1)
    tile_r = g - pick(ends - tiles)
    tile_rows = jnp.where(tile_r == pick(tiles) - 1, pick(last_rows), tm)
    lim = (tiles - 1) * tm + last_rows
    return i32(tile_e), i32(tile_r), i32(tile_rows), i32(n_valid.reshape(1)), i32(lim)


def _top2_moe(l, moe_i, h2, x1, mod, routing, wg, wu, wd, ln_g, ln_b, ts, tiles_per_batch):
    rel, gates, relt, base3, cnt3 = routing
    t = h2.shape[0]
    nt = t // ts
    tm, tf = EXPERT_ROWS, EXPERT_FF_TILE
    base = base3[:, 0, :N_EXPERTS].reshape(-1)
    cnt = cnt3[:, 0, :N_EXPERTS].reshape(-1)
    zero_rows = -(-tm // SLOT_CHUNK) * SLOT_CHUNK
    reach_rows = _dispatch_chunks(ts) * DISPATCH_CHUNK
    cap = -(-(t + ROW_ALIGN + reach_rows + zero_rows) // tm) * tm
    n_steps = -(-TOP_K * t // tm) + N_EXPERTS
    xs = _dispatch(base, cnt, h2, relt, ts, cap, zero_rows)
    tile_e, tile_r, tile_rows, n_valid, lim = _expert_tile_plan(base, cnt, nt, n_steps)
    ys = _expert_ffn(moe_i, tile_e, tile_r, tile_rows, n_valid, xs, wg, wu, wd, tf, n_steps)
    return _combine(l, base, cnt, lim, ys, rel, gates, x1, mod, ln_g, ln_b, ts, tiles_per_batch)


def kernel(x, c, positions, w_in, q_norm_g, kv_norm_g, w_uq, w_ukv, conv_w, mix_norm_g, w_o, w_ada, b_ada,
           ln1_g, ln1_b, ln2_g, ln2_b, ffn_w_gate, ffn_w_up, ffn_w_down, moe_router_w, moe_router_b,
           moe_w_gate, moe_w_up, moe_w_down):
    b, s, _ = x.shape
    n_layers = w_in.shape[0]
    ts = min(TOKEN_TILE, s)
    t = b * s
    tiles_per_batch = s // ts

    bp = -(-b // ROW_ALIGN) * ROW_ALIGN
    c_pad = jnp.pad(c, ((0, bp - b), (0, 0)))
    mod = _ada_mod(c_pad, w_ada, b_ada)
    mod = jnp.pad(jnp.transpose(mod[:, :, :b, :], (0, 2, 1, 3)), ((0, 0), (0, 0), (0, 2), (0, 0)))
    rope_tab = _rope_tables(positions, ts)

    rows = lambda v: v.reshape(v.shape[0], 1, -1)
    o3 = Q_LORA + KV_LORA + QK_ROPE
    w_lat = jnp.pad(w_in[:, :, :o3], ((0, 0), (0, 0), (0, LATENT_PAD - o3))).astype(BF16)
    w_gate = w_in[:, :, o3:o3 + 2 * D_CONV].astype(BF16)
    w_u = w_in[:, :, o3 + 2 * D_CONV:].astype(BF16)
    w_uq_t = jnp.transpose(jnp.pad(
        w_uq.reshape(n_layers, Q_LORA, N_HEADS, QK_NOPE + QK_ROPE),
        ((0, 0), (0, 0), (0, 0), (0, QK_PAD - QK_NOPE - QK_ROPE))
    ).reshape(n_layers, Q_LORA, N_HEADS * QK_PAD), (0, 2, 1)).astype(BF16)
    w_ukv_h = w_ukv.reshape(n_layers, KV_LORA, N_HEADS, QK_NOPE + V_HEAD)
    w_k = w_ukv_h[..., :QK_NOPE].reshape(n_layers, KV_LORA, N_HEADS * QK_NOPE).astype(BF16)
    w_v_t = jnp.transpose(w_ukv_h[..., QK_NOPE:].reshape(n_layers, KV_LORA, N_HEADS * V_HEAD), (0, 2, 1)).astype(BF16)
    w_o_b = w_o.astype(BF16)
    conv_w_p = jnp.pad(conv_w, ((0, 0), (0, 8 - CONV_WIDTH), (0, 0)))
    g_attn, g_conv = rows(mix_norm_g[:, :D_ATTN]), rows(mix_norm_g[:, D_ATTN:])
    ffn_wg, ffn_wu, ffn_wd = ffn_w_gate.astype(BF16), ffn_w_up.astype(BF16), ffn_w_down.astype(BF16)
    rw_hi = moe_router_w.astype(BF16)
    rw_lo = (moe_router_w - rw_hi.astype(F32)).astype(BF16)
    rw = jnp.pad(jnp.concatenate([rw_hi, rw_lo], axis=2), ((0, 0), (0, 0), (0, LANES - 2 * N_EXPERTS)))
    rb = rows(jnp.pad(moe_router_b, ((0, 0), (0, LANES - N_EXPERTS))))

    for l in range(n_layers):
        qt, k, vt, convn = _inproj(l, x, mod, w_lat, w_gate, w_u, w_uq_t, w_k, w_v_t, rows(q_norm_g), rows(kv_norm_g),
                                   conv_w_p, g_conv, rope_tab, ts)
        attn = _attention(qt, k, vt, ts)
        i = l // 2
        if l % 2 == 0:
            x = _outproj_dense_ffn(l, i, attn, convn, x, mod, w_o_b, g_attn, rows(ln1_g), rows(ln1_b),
                                   ffn_wg, ffn_wu, ffn_wd, rows(ln2_g), rows(ln2_b), ts)
        else:
            x1, h2, *routing = _outproj_router(l, i, attn, convn, x, mod, w_o_b, g_attn, rows(ln1_g), rows(ln1_b),
                                               rw, rb, ts)
            x = _top2_moe(l, i, h2.reshape(t, D_MODEL), x1.reshape(t, D_MODEL), mod, routing,
                          moe_w_gate, moe_w_up, moe_w_down, rows(ln2_g), rows(ln2_b), ts, tiles_per_batch)
            x = x.reshape(b, s, D_MODEL)
    return x
```

```python
import functools

import jax
import jax.numpy as jnp
from jax import lax
from jax.experimental import pallas as pl
from jax.experimental.pallas import tpu as pltpu

F32 = jnp.float32
BF16 = jnp.bfloat16

D_MODEL = 1024
N_HEADS = 4
QK_NOPE = 128
QK_ROPE = 64
V_HEAD = 128
Q_LORA = 256
KV_LORA = 128
D_ATTN = N_HEADS * V_HEAD
D_CONV = D_MODEL - D_ATTN
CONV_WIDTH = 3
ROPE_BASE = 10000.0
D_FF = 2816
N_EXPERTS = 8
TOP_K = 2
D_FF_EXPERT = 3584
RMS_EPS = 1e-6
LN_EPS = 1e-5
DEPTH = 2
DEEPNORM_ALPHA = (2 * DEPTH) ** 0.25
SM_SCALE = (QK_NOPE + QK_ROPE) ** -0.5
LOG2E = 1.4426950408889634
Q_SCALE = SM_SCALE * LOG2E

LANES = 128
QK_PAD = 2 * LANES
LATENT_PAD = 512
TOKEN_TILE = 512
EXPERT_FF_TILE = 512
ROW_ALIGN = 16
SLOT_CHUNK = 256
EXPERT_ROW_STEP = 384
EXPERT_ROW_SIZES = tuple(EXPERT_ROW_STEP * n for n in range(1, 7))
EXPERT_ROWS = EXPERT_ROW_SIZES[-1]
VMEM_LIMIT = 56 * 1024 * 1024
NEG_BIG = -1e30
NOT_ROUTED = -4096.0


def _cparams(n_axes):
    return pltpu.CompilerParams(dimension_semantics=("arbitrary",) * n_axes, vmem_limit_bytes=VMEM_LIMIT)


def _dot(a, b):
    return jnp.dot(a, b, preferred_element_type=F32)


def _rms(x, g):
    return x * lax.rsqrt(jnp.mean(x * x, axis=-1, keepdims=True) + RMS_EPS) * g


def _layer_norm(x, g, b):
    mu = jnp.mean(x, axis=-1, keepdims=True)
    xc = x - mu
    var = jnp.mean(xc * xc, axis=-1, keepdims=True)
    return xc * lax.rsqrt(var + LN_EPS) * g + b


def _silu(x):
    return x * jax.nn.sigmoid(x)


def _ada_kernel(c_ref, w_ref, b_ref, o_ref):
    ca = _silu(c_ref[...]).astype(BF16)
    o_ref[0, 0] = _dot(ca, w_ref[0].astype(BF16)) + b_ref[0]


def _ada_mod(c_pad, w_ada, b_ada):
    n_layers = w_ada.shape[0]
    bp = c_pad.shape[0]
    return pl.pallas_call(
        _ada_kernel,
        grid=(n_layers, 6),
        in_specs=[
            pl.BlockSpec((bp, D_MODEL), lambda l, j: (0, 0)),
            pl.BlockSpec((1, D_MODEL, D_MODEL), lambda l, j: (l, 0, j)),
            pl.BlockSpec((1, 1, D_MODEL), lambda l, j: (l, 0, j)),
        ],
        out_specs=pl.BlockSpec((1, 1, bp, D_MODEL), lambda l, j: (l, j, 0, 0)),
        out_shape=jax.ShapeDtypeStruct((n_layers, 6, bp, D_MODEL), F32),
        compiler_params=_cparams(2),
        name="ada_mod",
    )(c_pad, w_ada, b_ada.reshape(n_layers, 1, 6 * D_MODEL))


ROPE_HALF = QK_ROPE // 2


def _rope_kernel(pos_ref, f_ref, o_ref):
    ang = pos_ref[0].astype(F32) * f_ref[...]
    o_ref[0, 0:ROPE_HALF, :] = jnp.cos(ang)
    o_ref[0, ROPE_HALF:QK_ROPE, :] = jnp.sin(ang)


def _rope_tables(positions, ts):
    nt = positions.size // ts
    inv_freq = ROPE_BASE ** (-jnp.arange(0, QK_ROPE, 2, dtype=F32) / QK_ROPE)
    return pl.pallas_call(
        _rope_kernel,
        grid=(nt,),
        in_specs=[pl.BlockSpec((1, 1, ts), lambda i: (i, 0, 0)), pl.BlockSpec((ROPE_HALF, 1), lambda i: (0, 0))],
        out_specs=pl.BlockSpec((1, QK_ROPE, ts), lambda i: (i, 0, 0)),
        out_shape=jax.ShapeDtypeStruct((nt, QK_ROPE, ts), F32),
        compiler_params=_cparams(1),
        name="rope_tables",
    )(positions.reshape(nt, 1, ts), inv_freq.reshape(ROPE_HALF, 1))


def _inproj_kernel(x_ref, mod_ref, wlat_ref, wgate_ref, wu_ref, wuqt_ref, wk_ref, wvt_ref, qg_ref, kvg_ref, cw_ref,
                   cg_ref, rope_ref, qt_ref, k_ref, vt_ref, cv_ref, cu_ext, *, ts):
    j = pl.program_id(1)

    @pl.when(j == 0)
    def _():
        cu_ext[0:8, :] = jnp.zeros((8, D_CONV), F32)

    @pl.when(j > 0)
    def _():
        cu_ext[0:8, :] = cu_ext[ts:ts + 8, :]

    x = x_ref[0]
    shift = mod_ref[0, 0:1, :]
    scale = mod_ref[0, 1:2, :]
    h = (x * (1.0 + scale) + shift).astype(BF16)
    z_lat = _dot(h, wlat_ref[...])
    c_q = z_lat[:, 0:Q_LORA]
    c_kv = z_lat[:, Q_LORA:Q_LORA + KV_LORA]
    k_rope = z_lat[:, Q_LORA + KV_LORA:LATENT_PAD]
    z_gate = _dot(h, wgate_ref[...])
    gate_b = z_gate[:, 0:D_CONV]
    gate_c = z_gate[:, D_CONV:2 * D_CONV]
    u = _dot(h, wu_ref[...])
    cqn = _rms(c_q, qg_ref[...])
    ckvn = _rms(c_kv, kvg_ref[...])
    q_t = _dot(wuqt_ref[...], jnp.transpose(cqn).astype(BF16))
    k_nope = _dot(ckvn.astype(BF16), wk_ref[...])
    v_t = _dot(wvt_ref[...], jnp.transpose(ckvn).astype(BF16))

    cos = rope_ref[0, 0:ROPE_HALF, :]
    sin = rope_ref[0, ROPE_HALF:QK_ROPE, :]

    def rope_t(block):
        x1, x2 = block[0:ROPE_HALF, :], block[ROPE_HALF:QK_ROPE, :]
        return x1 * cos - x2 * sin, x2 * cos + x1 * sin

    k1, k2 = rope_t(jnp.transpose(k_rope)[0:QK_ROPE, :])
    k_rot = jnp.transpose(jnp.concatenate([k1, k2, jnp.zeros((LANES - QK_ROPE, ts), F32)], axis=0)).astype(BF16)
    for hh in range(N_HEADS):
        qo = hh * QK_PAD
        r0 = qo + QK_NOPE
        q1, q2 = rope_t(q_t[r0:r0 + QK_ROPE, :])
        qt_ref[0, hh, 0:QK_NOPE, :] = (q_t[qo:r0, :] * Q_SCALE).astype(BF16)
        qt_ref[0, hh, QK_NOPE:QK_NOPE + ROPE_HALF, :] = (q1 * Q_SCALE).astype(BF16)
        qt_ref[0, hh, QK_NOPE + ROPE_HALF:QK_NOPE + QK_ROPE, :] = (q2 * Q_SCALE).astype(BF16)
        qt_ref[0, hh, QK_NOPE + QK_ROPE:QK_PAD, :] = jnp.zeros((QK_PAD - QK_NOPE - QK_ROPE, ts), BF16)
        k_ref[0, hh, :, 0:LANES] = k_nope[:, hh * QK_NOPE:(hh + 1) * QK_NOPE].astype(BF16)
        k_ref[0, hh, :, LANES:QK_PAD] = k_rot
        vt_ref[0, hh, 0] = v_t[hh * V_HEAD:(hh + 1) * V_HEAD, :].astype(BF16)

    cu_ext[8:ts + 8, :] = gate_c * u
    conv = (cw_ref[2:3, :] * cu_ext[8:ts + 8, :]
            + cw_ref[1:2, :] * cu_ext[pl.ds(7, ts), :]
            + cw_ref[0:1, :] * cu_ext[pl.ds(6, ts), :])
    conv = gate_b * conv
    cv_ref[0] = _rms(conv, cg_ref[...]).astype(BF16)


def _inproj(l, x, mod, w_lat, w_gate, w_u, w_uq_t, w_k, w_v_t, qg, kvg, conv_w, conv_g, rope_tab, ts):
    b, s, _ = x.shape
    nsb = s // ts
    const = lambda shape: pl.BlockSpec((None,) + shape, lambda bi, j: (l,) + (0,) * len(shape))
    return pl.pallas_call(
        functools.partial(_inproj_kernel, ts=ts),
        grid=(b, nsb),
        in_specs=[
            pl.BlockSpec((1, ts, D_MODEL), lambda bi, j: (bi, j, 0)),
            pl.BlockSpec((None, 1, 8, D_MODEL), lambda bi, j: (l, bi, 0, 0)),
            const((D_MODEL, LATENT_PAD)),
            const((D_MODEL, 2 * D_CONV)),
            const((D_MODEL, D_CONV)),
            const((N_HEADS * QK_PAD, Q_LORA)),
            const((KV_LORA, N_HEADS * QK_NOPE)),
            const((N_HEADS * V_HEAD, KV_LORA)),
            const((1, Q_LORA)),
            const((1, KV_LORA)),
            const((8, D_CONV)),
            const((1, D_CONV)),
            pl.BlockSpec((1, QK_ROPE, ts), lambda bi, j: (bi * nsb + j, 0, 0)),
        ],
        out_specs=[
            pl.BlockSpec((1, N_HEADS, QK_PAD, ts), lambda bi, j: (bi, 0, 0, j)),
            pl.BlockSpec((1, N_HEADS, ts, QK_PAD), lambda bi, j: (bi, 0, j, 0)),
            pl.BlockSpec((1, N_HEADS, 1, V_HEAD, ts), lambda bi, j: (bi, 0, j, 0, 0)),
            pl.BlockSpec((1, ts, D_CONV), lambda bi, j: (bi, j, 0)),
        ],
        out_shape=[
            jax.ShapeDtypeStruct((b, N_HEADS, QK_PAD, s), BF16),
            jax.ShapeDtypeStruct((b, N_HEADS, s, QK_PAD), BF16),
            jax.ShapeDtypeStruct((b, N_HEADS, nsb, V_HEAD, ts), BF16),
            jax.ShapeDtypeStruct((b, s, D_CONV), BF16),
        ],
        scratch_shapes=[pltpu.VMEM((ts + 8, D_CONV), F32)],
        compiler_params=_cparams(2),
        name="inproj",
    )(x, mod, w_lat, w_gate, w_u, w_uq_t, w_k, w_v_t, qg, kvg, conv_w, conv_g, rope_tab)


Q_CHUNK = 256


K_ROWS = 16


def _attn_kernel(qt_ref, k_ref, vt_ref, o_ref, *scratch, tq):
    i = pl.program_id(1)
    n_chunks = tq // Q_CHUNK
    per_chain = 5
    chains = [(hh, c) + tuple(scratch[per_chain * (hh * n_chunks + c):per_chain * (hh * n_chunks + c + 1)])
              for hh in range(N_HEADS) for c in range(n_chunks)]
    for _, _, m_sc, l_sc, acc_sc, _, _ in chains:
        m_sc[...] = jnp.full(m_sc.shape, NEG_BIG, F32)
        l_sc[...] = jnp.zeros(l_sc.shape, F32)
        acc_sc[...] = jnp.zeros(acc_sc.shape, F32)

    def scores(chain, jk, buf):
        hh, c = chain[0], chain[1]
        kj = k_ref[0, hh, pl.ds(pl.multiple_of(jk * tq, tq), tq), :]
        chain[5 + buf][...] = _dot(kj, qt_ref[0, hh, :, c * Q_CHUNK:(c + 1) * Q_CHUNK])

    def softmax_values(chain, jk, buf, diagonal):
        hh, c, m_sc, l_sc, acc_sc = chain[:5]
        s_sc = chain[5 + buf]
        nk = (c + 1) * Q_CHUNK if diagonal else tq
        if diagonal:
            d0 = nk - Q_CHUNK
            key = lax.broadcasted_iota(jnp.int32, (Q_CHUNK, Q_CHUNK), 0)
            qry = lax.broadcasted_iota(jnp.int32, (Q_CHUNK, Q_CHUNK), 1)
            s_sc[d0:nk, :] = jnp.where(key <= qry, s_sc[d0:nk, :], NEG_BIG)
        blk_max = s_sc[0:K_ROWS, :]
        for r in range(K_ROWS, nk, K_ROWS):
            blk_max = jnp.maximum(blk_max, s_sc[r:r + K_ROWS, :])
        m_prev = m_sc[...]
        m_new = jnp.maximum(m_prev, jnp.max(blk_max, axis=0, keepdims=True))
        alpha = jnp.exp2(m_prev - m_new)
        p_sum = jnp.zeros((K_ROWS, Q_CHUNK), F32)
        acc = alpha * acc_sc[...]
        for kt in range(0, nk, Q_CHUNK):
            parts = []
            for r in range(kt, kt + Q_CHUNK, K_ROWS):
                p = jnp.exp2(s_sc[r:r + K_ROWS, :] - m_new)
                p_sum = p_sum + p
                parts.append(p.astype(BF16))
            acc = acc + _dot(vt_ref[0, hh, jk, :, kt:kt + Q_CHUNK], jnp.concatenate(parts, axis=0))
        l_sc[...] = alpha * l_sc[...] + jnp.sum(p_sum, axis=0, keepdims=True)
        acc_sc[...] = acc
        m_sc[...] = m_new

    def stage(cur, cur_buf, diagonal, nxt=None):
        for chain in chains:
            if nxt is not None:
                scores(chain, nxt, 1 - cur_buf)
            softmax_values(chain, cur, cur_buf, diagonal)

    for chain in chains:
        scores(chain, 0, 0)

    def pair(jj, carry):
        stage(2 * jj, 0, False, nxt=2 * jj + 1)
        stage(2 * jj + 1, 1, False, nxt=2 * jj + 2)
        return carry

    lax.fori_loop(0, i // 2, pair, 0)

    @pl.when(lax.rem(i, 2) == 0)
    def _():
        stage(i, 0, True)

    @pl.when(lax.rem(i, 2) == 1)
    def _():
        stage(i - 1, 0, False, nxt=i)
        stage(i, 1, True)

    for hh, c, _, l_sc, acc_sc, _, _ in chains:
        o_ref[0, c * Q_CHUNK:(c + 1) * Q_CHUNK, hh * V_HEAD:(hh + 1) * V_HEAD] = (
            jnp.transpose(acc_sc[...] / l_sc[...]).astype(BF16))


def _attention(qt, k, vt, tq):
    b, nh, s, _ = k.shape
    return pl.pallas_call(
        functools.partial(_attn_kernel, tq=tq),
        grid=(b, s // tq),
        in_specs=[
            pl.BlockSpec((1, nh, QK_PAD, tq), lambda bi, i: (bi, 0, 0, i)),
            pl.BlockSpec((1, nh, s, QK_PAD), lambda bi, i: (bi, 0, 0, 0)),
            pl.BlockSpec((1, nh, s // tq, V_HEAD, tq), lambda bi, i: (bi, 0, 0, 0, 0)),
        ],
        out_specs=pl.BlockSpec((1, tq, D_ATTN), lambda bi, i: (bi, i, 0)),
        out_shape=jax.ShapeDtypeStruct((b, s, D_ATTN), BF16),
        scratch_shapes=[
            pltpu.VMEM((1, Q_CHUNK), F32), pltpu.VMEM((1, Q_CHUNK), F32), pltpu.VMEM((V_HEAD, Q_CHUNK), F32),
            pltpu.VMEM((tq, Q_CHUNK), F32), pltpu.VMEM((tq, Q_CHUNK), F32),
        ] * (nh * (tq // Q_CHUNK)),
        compiler_params=_cparams(2),
        name="attention",
    )(qt, k, vt)


OUT_ROWS = 256


def _row_chunks(ts):
    return [slice(r, r + OUT_ROWS) for r in range(0, ts, OUT_ROWS)]


def _outproj_matmuls(attn_ref, cv_ref, wo_ref, ga_ref, rows):
    an = _rms(attn_ref[0, rows, :].astype(F32), ga_ref[...]).astype(BF16)
    return _dot(an, wo_ref[0:D_ATTN, :]) + _dot(cv_ref[0, rows, :], wo_ref[D_ATTN:D_MODEL, :])


def _outproj_norm(y, x_ref, mod_ref, lng_ref, lnb_ref, rows):
    x1 = _layer_norm(DEEPNORM_ALPHA * x_ref[0, rows, :] + mod_ref[0, 2:3, :] * y, lng_ref[...], lnb_ref[...])
    h2 = x1 * (1.0 + mod_ref[0, 4:5, :]) + mod_ref[0, 3:4, :]
    return x1, h2


def _outproj_router_kernel(attn_ref, cv_ref, x_ref, mod_ref, wo_ref, ga_ref, lng_ref, lnb_ref, rw_ref, rb_ref,
                           x1_ref, h2_ref, rel_ref, gate_ref, relt_ref, base_ref, cnt_ref, run_sc, *, ts):
    first = jnp.logical_and(pl.program_id(0) == 0, pl.program_id(1) == 0)

    @pl.when(first)
    def _():
        run_sc[...] = jnp.zeros(run_sc.shape, F32)

    chunks = _row_chunks(ts)
    ys = [_outproj_matmuls(attn_ref, cv_ref, wo_ref, ga_ref, rows) for rows in chunks]
    lane = lax.broadcasted_iota(jnp.int32, (OUT_ROWS, LANES), 1)
    r_i = lax.broadcasted_iota(jnp.int32, (OUT_ROWS, OUT_ROWS), 0)
    c_i = lax.broadcasted_iota(jnp.int32, (OUT_ROWS, OUT_ROWS), 1)
    tri = jnp.where(c_i < r_i, 1.0, 0.0).astype(BF16)
    n_tile = jnp.zeros((1, LANES), F32)
    for rows, y in zip(chunks, ys):
        x1, h2 = _outproj_norm(y, x_ref, mod_ref, lng_ref, lnb_ref, rows)
        x1_ref[0, rows, :] = x1
        h2_hi = h2.astype(BF16)
        h2_ref[0, rows, :] = h2_hi

        h2_lo = (h2 - h2_hi.astype(F32)).astype(BF16)
        prod = _dot(h2_hi, rw_ref[...]) + _dot(h2_lo, rw_ref[...])
        logits = prod + pltpu.roll(prod, LANES - N_EXPERTS, 1) + rb_ref[...]
        logits = jnp.where(lane < N_EXPERTS, logits, NEG_BIG)
        v1 = jnp.max(logits, axis=-1, keepdims=True)
        i1 = jnp.min(jnp.where(logits == v1, lane, LANES), axis=-1, keepdims=True)
        rest = jnp.where(lane == i1, NEG_BIG, logits)
        v2 = jnp.max(rest, axis=-1, keepdims=True)
        i2 = jnp.min(jnp.where(rest == v2, lane, LANES), axis=-1, keepdims=True)
        e21 = jnp.exp(v2 - v1)
        g1 = 1.0 / (1.0 + e21)
        g2 = e21 / (1.0 + e21)
        is1 = lane == i1
        is2 = lane == i2
        sel = jnp.where(jnp.logical_or(is1, is2), 1.0, 0.0)
        gate_ref[rows, :] = jnp.where(is1, g1, 0.0) + jnp.where(is2, g2, 0.0)

        rank = _dot(tri, sel.astype(BF16)) + n_tile
        rel = jnp.where(sel > 0.0, rank, NOT_ROUTED)
        rel_ref[rows, :] = rel
        relt_ref[0, :, rows] = jnp.transpose(rel)[0:N_EXPERTS, :]
        n_tile = n_tile + jnp.sum(sel, axis=0, keepdims=True)

    base_ref[0] = run_sc[...].astype(jnp.int32)
    cnt_ref[0] = n_tile.astype(jnp.int32)
    run_sc[...] = run_sc[...] + n_tile


def _outproj_router(l, moe_i, attn, convn, x, mod, w_o, g_attn, ln_g, ln_b, rw, rb, ts):
    b, s, _ = x.shape
    nsb = s // ts
    const = lambda shape, idx=l: pl.BlockSpec((None,) + shape, lambda bi, j: (idx,) + (0,) * len(shape))
    tile3 = lambda w: pl.BlockSpec((1, ts, w), lambda bi, j: (bi, j, 0))
    in_specs = [
        tile3(D_ATTN), tile3(D_CONV), tile3(D_MODEL),
        pl.BlockSpec((None, 1, 8, D_MODEL), lambda bi, j: (l, bi, 0, 0)),
        const((D_MODEL, D_MODEL)), const((1, D_ATTN)), const((1, D_MODEL)), const((1, D_MODEL)),
    ]
    out_specs = [tile3(D_MODEL), tile3(D_MODEL)]
    out_shape = [jax.ShapeDtypeStruct((b, s, D_MODEL), F32), jax.ShapeDtypeStruct((b, s, D_MODEL), BF16)]
    args = [attn, convn, x, mod, w_o, g_attn, ln_g, ln_b]
    nt = b * nsb
    flat = lambda w: pl.BlockSpec((ts, w), lambda bi, j: (bi * nsb + j, 0))
    per_tile = lambda r, w: pl.BlockSpec((1, r, w), lambda bi, j: (bi * nsb + j, 0, 0))
    in_specs += [const((D_MODEL, LANES), moe_i), const((1, LANES), moe_i)]
    out_specs += [flat(LANES), flat(LANES), per_tile(N_EXPERTS, ts), per_tile(1, LANES), per_tile(1, LANES)]
    out_shape += [
        jax.ShapeDtypeStruct((nt * ts, LANES), F32), jax.ShapeDtypeStruct((nt * ts, LANES), F32),
        jax.ShapeDtypeStruct((nt, N_EXPERTS, ts), F32),
        jax.ShapeDtypeStruct((nt, 1, LANES), jnp.int32), jax.ShapeDtypeStruct((nt, 1, LANES), jnp.int32),
    ]
    return pl.pallas_call(
        functools.partial(_outproj_router_kernel, ts=ts), grid=(b, nsb), in_specs=in_specs, out_specs=out_specs,
        out_shape=out_shape, scratch_shapes=[pltpu.VMEM((1, LANES), F32)],
        compiler_params=_cparams(2), name="outproj_router",
    )(*args, rw, rb)


FF_CHUNK = 256


def _swiglu(x, wg, wu, wd, width):
    out = None
    pending = None
    for c in range(0, width, FF_CHUNK):
        w = min(FF_CHUNK, width - c)
        g = _dot(x, wg(c, w))
        u = _dot(x, wu(c, w))
        if pending is not None:
            d = _dot(pending[0], wd(*pending[1]))
            out = d if out is None else out + d
        pending = ((_silu(g) * u).astype(BF16), (c, w))
    d = _dot(pending[0], wd(*pending[1]))
    return d if out is None else out + d


def _outproj_ffn_kernel(attn_ref, cv_ref, x_ref, mod_ref, wo_ref, ga_ref, ln1g_ref, ln1b_ref,
                        wg_ref, wu_ref, wd_ref, ln2g_ref, ln2b_ref, o_ref, *, ts):
    chunks = _row_chunks(ts)
    ys = [_outproj_matmuls(attn_ref, cv_ref, wo_ref, ga_ref, rows) for rows in chunks]
    normed = [_outproj_norm(y, x_ref, mod_ref, ln1g_ref, ln1b_ref, rows) for rows, y in zip(chunks, ys)]
    x1 = jnp.concatenate([n[0] for n in normed], axis=0)
    h = jnp.concatenate([n[1].astype(BF16) for n in normed], axis=0)
    f = _swiglu(h, lambda c, w: wg_ref[:, c:c + w], lambda c, w: wu_ref[:, c:c + w],
                lambda c, w: wd_ref[c:c + w, :], D_FF)
    gate2 = mod_ref[0, 5:6, :]
    o_ref[0] = _layer_norm(DEEPNORM_ALPHA * x1 + gate2 * f, ln2g_ref[...], ln2b_ref[...])


def _outproj_dense_ffn(l, dense_i, attn, convn, x, mod, w_o, g_attn, ln1_g, ln1_b, wg, wu, wd, ln2_g, ln2_b, ts):
    b, s, _ = x.shape
    const = lambda shape, idx=l: pl.BlockSpec((None,) + shape, lambda bi, j: (idx,) + (0,) * len(shape),
                                              pipeline_mode=pl.Buffered(1))
    tile3 = lambda w: pl.BlockSpec((1, ts, w), lambda bi, j: (bi, j, 0))
    return pl.pallas_call(
        functools.partial(_outproj_ffn_kernel, ts=ts),
        grid=(b, s // ts),
        in_specs=[
            tile3(D_ATTN), tile3(D_CONV), tile3(D_MODEL),
            pl.BlockSpec((None, 1, 8, D_MODEL), lambda bi, j: (l, bi, 0, 0)),
            const((D_MODEL, D_MODEL)), const((1, D_ATTN)), const((1, D_MODEL)), const((1, D_MODEL)),
            const((D_MODEL, D_FF), dense_i), const((D_MODEL, D_FF), dense_i), const((D_FF, D_MODEL), dense_i),
            const((1, D_MODEL)), const((1, D_MODEL)),
        ],
        out_specs=tile3(D_MODEL),
        out_shape=jax.ShapeDtypeStruct((b, s, D_MODEL), F32),
        compiler_params=_cparams(2),
        name="outproj_dense_ffn",
    )(attn, convn, x, mod, w_o, g_attn, ln1_g, ln1_b, wg, wu, wd, ln2_g, ln2_b)


def _group_copy(src, dst, sem):
    return pltpu.make_async_copy(src, dst, sem)


DISPATCH_CHUNK = 256


def _dispatch_chunks(ts):
    return -(-(ts + ROW_ALIGN - 1) // DISPATCH_CHUNK)


def _floor_align(v):
    return pl.multiple_of(lax.shift_left(lax.shift_right_logical(v, 4), 4), ROW_ALIGN)


def _dispatch_kernel(base_ref, cnt_ref, h_ref, relt_ref, xs_ref, stage, carry, zbuf, sems, xsem, zsem, *,
                     ts, zero_rows):
    i = pl.program_id(0)
    last = pl.num_programs(0) - 1
    par = lax.rem(i, 2)
    ch = DISPATCH_CHUNK
    n_chunks = _dispatch_chunks(ts)
    h = h_ref[...]
    slot = lax.broadcasted_iota(jnp.int32, (ch, ts), 0).astype(F32)

    def group(step, e):
        a = base_ref[step * N_EXPERTS + e]
        a16 = _floor_align(a)
        return a, a16, a - a16

    def onehot(e, c):
        _, _, off = group(i, e)
        row = relt_ref[0, e:e + 1, :] + off.astype(F32)
        return jnp.where(row == slot + float(c * ch), 1.0, 0.0).astype(BF16)

    def first_copy(step, e):
        _, a16, _ = group(step, e)
        p = lax.rem(step, 2)
        return _group_copy(stage.at[p, e, 0:ch], xs_ref.at[e, pl.ds(a16, ch), :], sems.at[p, e])

    @pl.when(i == 0)
    def _():
        carry[...] = jnp.zeros(carry.shape, BF16)
        stage[:, :, n_chunks * ch:, :] = jnp.zeros((2, N_EXPERTS, ROW_ALIGN, D_MODEL), BF16)

    for e in range(N_EXPERTS):
        rows = _dot(onehot(e, 0), h)
        stage[par, e, 0:ROW_ALIGN] = (rows[0:ROW_ALIGN] + carry[e].astype(F32)).astype(BF16)
        stage[par, e, ROW_ALIGN:ch] = rows[ROW_ALIGN:ch].astype(BF16)
        stage[par, e, ch:ch + ROW_ALIGN] = jnp.zeros((ROW_ALIGN, D_MODEL), BF16)

    @pl.when(i > 0)
    def _():
        for e in range(N_EXPERTS):
            first_copy(i - 1, e).wait()

    for e in range(N_EXPERTS):
        first_copy(i, e).start()

    for e in range(N_EXPERTS):
        _, a16, off = group(i, e)
        n = cnt_ref[i * N_EXPERTS + e]
        for c in range(1, n_chunks):
            @pl.when(off + n > c * ch)
            def _(e=e, c=c, a16=a16):
                lo = c * ch
                stage[par, e, lo:lo + ch] = _dot(onehot(e, c), h).astype(BF16)
                if c + 1 < n_chunks:
                    stage[par, e, lo + ch:lo + ch + ROW_ALIGN] = jnp.zeros((ROW_ALIGN, D_MODEL), BF16)
                cp = _group_copy(stage.at[par, e, lo:lo + ch], xs_ref.at[e, pl.ds(a16 + lo, ch), :], xsem)
                cp.start()
                cp.wait()

    for e in range(N_EXPERTS):
        a, a16, _ = group(i, e)
        nxt16 = _floor_align(a + cnt_ref[i * N_EXPERTS + e])
        carry[e] = stage[par, e, pl.ds(pl.multiple_of(nxt16 - a16, ROW_ALIGN), ROW_ALIGN), :]

    @pl.when(i == last)
    def _():
        for e in range(N_EXPERTS):
            first_copy(i, e).wait()
        zbuf[...] = jnp.zeros(zbuf.shape, BF16)
        for e in range(N_EXPERTS):
            n = cnt_ref[i * N_EXPERTS + e]
            end = _floor_align(base_ref[i * N_EXPERTS + e] + n + (ROW_ALIGN - 1))
            copies = [
                _group_copy(zbuf, xs_ref.at[e, pl.ds(end + r * SLOT_CHUNK, SLOT_CHUNK), :], zsem.at[e])
                for r in range(zero_rows // SLOT_CHUNK)
            ]
            for cp in copies:
                cp.start()
            for cp in copies:
                cp.wait()


def _dispatch(base, cnt, h2, relt, ts, cap, zero_rows):
    t = h2.shape[0]
    nt = t // ts
    n_chunks = _dispatch_chunks(ts)
    grid_spec = pltpu.PrefetchScalarGridSpec(
        num_scalar_prefetch=2,
        grid=(nt,),
        in_specs=[
            pl.BlockSpec((ts, D_MODEL), lambda i, b_, c_: (i, 0)),
            pl.BlockSpec((1, N_EXPERTS, ts), lambda i, b_, c_: (i, 0, 0)),
        ],
        out_specs=pl.BlockSpec(memory_space=pl.ANY),
        scratch_shapes=[
            pltpu.VMEM((2, N_EXPERTS, n_chunks * DISPATCH_CHUNK + ROW_ALIGN, D_MODEL), BF16),
            pltpu.VMEM((N_EXPERTS, ROW_ALIGN, D_MODEL), BF16),
            pltpu.VMEM((SLOT_CHUNK, D_MODEL), BF16),
            pltpu.SemaphoreType.DMA((2, N_EXPERTS)),
            pltpu.SemaphoreType.DMA(()),
            pltpu.SemaphoreType.DMA((N_EXPERTS,)),
        ],
    )
    return pl.pallas_call(
        functools.partial(_dispatch_kernel, ts=ts, zero_rows=zero_rows),
        grid_spec=grid_spec,
        out_shape=jax.ShapeDtypeStruct((N_EXPERTS, cap, D_MODEL), BF16),
        compiler_params=_cparams(1),
        name="moe_dispatch",
    )(base, cnt, h2, relt)


def _expert_kernel(te_ref, tr_ref, rows_ref, nv_ref, x_ref, wg_ref, wu_ref, wd_ref, o_ref, acc_sc):
    g_i = pl.program_id(0)
    k = pl.program_id(1)

    valid = g_i < nv_ref[0]

    @pl.when(jnp.logical_and(valid, k == 0))
    def _():
        acc_sc[...] = jnp.zeros(acc_sc.shape, F32)

    for m in EXPERT_ROW_SIZES:
        @pl.when(jnp.logical_and(valid, rows_ref[g_i] == m))
        def _(m=m):
            part = _swiglu(x_ref[0, 0:m, :], lambda c, w: wg_ref[0, :, c:c + w].astype(BF16),
                           lambda c, w: wu_ref[0, :, c:c + w].astype(BF16),
                           lambda c, w: wd_ref[0, c:c + w, :].astype(BF16), wg_ref.shape[2])
            total = acc_sc[0:m, :] + part
            acc_sc[0:m, :] = total
            o_ref[0, 0:m, :] = total.astype(BF16)


def _expert_ffn(moe_i, tile_e, tile_r, tile_rows, n_valid, xs, wg, wu, wd, tf, n_steps):
    tm = EXPERT_ROWS
    nk = D_FF_EXPERT // tf
    cap = xs.shape[1]

    def k_eff(g, k, nv):
        return jnp.where(g < nv[0], k, nk - 1)

    grid_spec = pltpu.PrefetchScalarGridSpec(
        num_scalar_prefetch=4,
        grid=(n_steps, nk),
        in_specs=[
            pl.BlockSpec((1, tm, D_MODEL), lambda g, k, te, tr, rw, nv: (te[g], tr[g], 0)),
            pl.BlockSpec((None, 1, D_MODEL, tf), lambda g, k, te, tr, rw, nv: (moe_i, te[g], 0, k_eff(g, k, nv))),
            pl.BlockSpec((None, 1, D_MODEL, tf), lambda g, k, te, tr, rw, nv: (moe_i, te[g], 0, k_eff(g, k, nv))),
            pl.BlockSpec((None, 1, tf, D_MODEL), lambda g, k, te, tr, rw, nv: (moe_i, te[g], k_eff(g, k, nv), 0)),
        ],
        out_specs=pl.BlockSpec((1, tm, D_MODEL), lambda g, k, te, tr, rw, nv: (te[g], tr[g], 0)),
        scratch_shapes=[pltpu.VMEM((tm, D_MODEL), F32)],
    )
    return pl.pallas_call(
        _expert_kernel,
        grid_spec=grid_spec,
        out_shape=jax.ShapeDtypeStruct((N_EXPERTS, cap, D_MODEL), BF16),
        compiler_params=_cparams(2),
        name="moe_experts",
    )(tile_e, tile_r, tile_rows, n_valid, xs, wg, wu, wd)


def _combine_kernel(base_ref, cnt_ref, lim_ref, ys_ref, rel_ref, gate_ref, x1_ref, mod_ref, lng_ref, lnb_ref, o_ref,
                    slab, xslab, acc_sc, sems, xsem, *, ts):
    i = pl.program_id(0)
    last = pl.num_programs(0) - 1
    par = lax.rem(i, 2)
    n_chunks = -(-(ts + ROW_ALIGN - 1) // SLOT_CHUNK)
    slot = lax.broadcasted_iota(jnp.int32, (ts, SLOT_CHUNK), 1).astype(F32)

    def slab_start(step, e, c):
        want = _floor_align(base_ref[step * N_EXPERTS + e]) + c * SLOT_CHUNK
        start = jnp.maximum(jnp.minimum(want, lim_ref[e] - SLOT_CHUNK), 0)
        return pl.multiple_of(start, ROW_ALIGN), want - start

    def first_copy(step, e):
        start, _ = slab_start(step, e, 0)
        p = lax.rem(step, 2)
        return _group_copy(ys_ref.at[e, pl.ds(start, SLOT_CHUNK), :], slab.at[p, e], sems.at[p, e])

    def contribution(e, c, rows):
        a = base_ref[i * N_EXPERTS + e]
        _, moved = slab_start(i, e, c)
        pos = rel_ref[:, e:e + 1] + (a - _floor_align(a)).astype(F32)
        in_chunk = jnp.where(pos >= float(c * SLOT_CHUNK), 1.0, 0.0)
        onehot = jnp.where(pos + (moved - c * SLOT_CHUNK).astype(F32) == slot, in_chunk, 0.0).astype(BF16)
        return _dot(onehot, rows) * gate_ref[:, e:e + 1]

    @pl.when(i == 0)
    def _():
        for e in range(N_EXPERTS):
            first_copy(i, e).start()

    @pl.when(i < last)
    def _():
        for e in range(N_EXPERTS):
            first_copy(i + 1, e).start()

    for e in range(N_EXPERTS):
        first_copy(i, e).wait()
    f = contribution(0, 0, slab[par, 0])
    for e in range(1, N_EXPERTS):
        f = f + contribution(e, 0, slab[par, e])
    acc_sc[...] = f

    for e in range(N_EXPERTS):
        a = base_ref[i * N_EXPERTS + e]
        rows_in_block = a - _floor_align(a) + cnt_ref[i * N_EXPERTS + e]
        for c in range(1, n_chunks):
            @pl.when(rows_in_block > c * SLOT_CHUNK)
            def _(e=e, c=c):
                start, _ = slab_start(i, e, c)
                cp = _group_copy(ys_ref.at[e, pl.ds(start, SLOT_CHUNK), :], xslab, xsem)
                cp.start()
                cp.wait()
                acc_sc[...] += contribution(e, c, xslab[...])

    gate2 = mod_ref[0, 5:6, :]
    o_ref[...] = _layer_norm(DEEPNORM_ALPHA * x1_ref[...] + gate2 * acc_sc[...], lng_ref[...], lnb_ref[...])


def _combine(l, base, cnt, lim, ys, rel, gates, x1, mod, ln_g, ln_b, ts, tiles_per_batch):
    t = x1.shape[0]
    grid_spec = pltpu.PrefetchScalarGridSpec(
        num_scalar_prefetch=3,
        grid=(t // ts,),
        in_specs=[
            pl.BlockSpec(memory_space=pl.ANY),
            pl.BlockSpec((ts, LANES), lambda i, *_: (i, 0)),
            pl.BlockSpec((ts, LANES), lambda i, *_: (i, 0)),
            pl.BlockSpec((ts, D_MODEL), lambda i, *_: (i, 0)),
            pl.BlockSpec((None, 1, 8, D_MODEL), lambda i, *_: (l, i // tiles_per_batch, 0, 0)),
            pl.BlockSpec((None, 1, D_MODEL), lambda i, *_: (l, 0, 0)),
            pl.BlockSpec((None, 1, D_MODEL), lambda i, *_: (l, 0, 0)),
        ],
        out_specs=pl.BlockSpec((ts, D_MODEL), lambda i, *_: (i, 0)),
        scratch_shapes=[
            pltpu.VMEM((2, N_EXPERTS, SLOT_CHUNK, D_MODEL), BF16),
            pltpu.VMEM((SLOT_CHUNK, D_MODEL), BF16),
            pltpu.VMEM((ts, D_MODEL), F32),
            pltpu.SemaphoreType.DMA((2, N_EXPERTS)),
            pltpu.SemaphoreType.DMA(()),
        ],
    )
    return pl.pallas_call(
        functools.partial(_combine_kernel, ts=ts),
        grid_spec=grid_spec,
        out_shape=jax.ShapeDtypeStruct((t, D_MODEL), F32),
        compiler_params=_cparams(1),
        name="moe_combine",
    )(base, cnt, lim, ys, rel, gates, x1, mod, ln_g, ln_b)


def _expert_tile_plan(base, cnt, nt, n_steps):
    tm = EXPERT_ROWS
    i32 = lambda v: v.astype(jnp.int32)
    last_base = base.reshape(nt, N_EXPERTS)[-1]
    last_cnt = cnt.reshape(nt, N_EXPERTS)[-1]
    total = last_base + last_cnt
    tiles = jnp.maximum((total + tm - 1) // tm, 1)
    rest = total - (tiles - 1) * tm
    last_rows = jnp.full_like(rest, EXPERT_ROW_SIZES[0])
    for small, big in zip(EXPERT_ROW_SIZES[:-1], EXPERT_ROW_SIZES[1:]):
        last_rows = last_rows + jnp.where(rest > small, big - small, 0)
    ends = jnp.cumsum(tiles)
    n_valid = ends[-1]
    g = jnp.minimum(jnp.arange(n_steps, dtype=jnp.int32), n_valid - 1)
    tile_e = jnp.sum((g[:, None] >= ends[None, :]).astype(jnp.int32), axis=1)
    onehot = (tile_e[:, None] == jnp.arange(N_EXPERTS, dtype=jnp.int32)[None, :]).astype(jnp.int32)
    pick = lambda v: jnp.sum(onehot * v[None, :], axis=1)
    tile_r = g - pick(ends - tiles)
    tile_rows = jnp.where(tile_r == pick(tiles) - 1, pick(last_rows), tm)
    lim = (tiles - 1) * tm + last_rows
    return i32(tile_e), i32(tile_r), i32(tile_rows), i32(n_valid.reshape(1)), i32(lim)


def _top2_moe(l, moe_i, h2, x1, mod, routing, wg, wu, wd, ln_g, ln_b, ts, tiles_per_batch):
    rel, gates, relt, base3, cnt3 = routing
    t = h2.shape[0]
    nt = t // ts
    tm, tf = EXPERT_ROWS, EXPERT_FF_TILE
    base = base3[:, 0, :N_EXPERTS].reshape(-1)
    cnt = cnt3[:, 0, :N_EXPERTS].reshape(-1)
    zero_rows = -(-(EXPERT_ROW_STEP + ROW_ALIGN) // SLOT_CHUNK) * SLOT_CHUNK
    reach_rows = _dispatch_chunks(ts) * DISPATCH_CHUNK
    cap = -(-(t + ROW_ALIGN + reach_rows + zero_rows) // tm) * tm
    n_steps = -(-TOP_K * t // tm) + N_EXPERTS
    xs = _dispatch(base, cnt, h2, relt, ts, cap, zero_rows)
    tile_e, tile_r, tile_rows, n_valid, lim = _expert_tile_plan(base, cnt, nt, n_steps)
    ys = _expert_ffn(moe_i, tile_e, tile_r, tile_rows, n_valid, xs, wg, wu, wd, tf, n_steps)
    return _combine(l, base, cnt, lim, ys, rel, gates, x1, mod, ln_g, ln_b, ts, tiles_per_batch)


def kernel(x, c, positions, w_in, q_norm_g, kv_norm_g, w_uq, w_ukv, conv_w, mix_norm_g, w_o, w_ada, b_ada,
           ln1_g, ln1_b, ln2_g, ln2_b, ffn_w_gate, ffn_w_up, ffn_w_down, moe_router_w, moe_router_b,
           moe_w_gate, moe_w_up, moe_w_down):
    b, s, _ = x.shape
    n_layers = w_in.shape[0]
    ts = min(TOKEN_TILE, s)
    t = b * s
    tiles_per_batch = s // ts

    bp = -(-b // ROW_ALIGN) * ROW_ALIGN
    c_pad = jnp.pad(c, ((0, bp - b), (0, 0)))
    mod = _ada_mod(c_pad, w_ada, b_ada)
    mod = jnp.pad(jnp.transpose(mod[:, :, :b, :], (0, 2, 1, 3)), ((0, 0), (0, 0), (0, 2), (0, 0)))
    rope_tab = _rope_tables(positions, ts)

    rows = lambda v: v.reshape(v.shape[0], 1, -1)
    o3 = Q_LORA + KV_LORA + QK_ROPE
    w_lat = jnp.pad(w_in[:, :, :o3], ((0, 0), (0, 0), (0, LATENT_PAD - o3))).astype(BF16)
    w_gate = w_in[:, :, o3:o3 + 2 * D_CONV].astype(BF16)
    w_u = w_in[:, :, o3 + 2 * D_CONV:].astype(BF16)
    w_uq_t = jnp.transpose(jnp.pad(
        w_uq.reshape(n_layers, Q_LORA, N_HEADS, QK_NOPE + QK_ROPE),
        ((0, 0), (0, 0), (0, 0), (0, QK_PAD - QK_NOPE - QK_ROPE))
    ).reshape(n_layers, Q_LORA, N_HEADS * QK_PAD), (0, 2, 1)).astype(BF16)
    w_ukv_h = w_ukv.reshape(n_layers, KV_LORA, N_HEADS, QK_NOPE + V_HEAD)
    w_k = w_ukv_h[..., :QK_NOPE].reshape(n_layers, KV_LORA, N_HEADS * QK_NOPE).astype(BF16)
    w_v_t = jnp.transpose(w_ukv_h[..., QK_NOPE:].reshape(n_layers, KV_LORA, N_HEADS * V_HEAD), (0, 2, 1)).astype(BF16)
    w_o_b = w_o.astype(BF16)
    conv_w_p = jnp.pad(conv_w, ((0, 0), (0, 8 - CONV_WIDTH), (0, 0)))
    g_attn, g_conv = rows(mix_norm_g[:, :D_ATTN]), rows(mix_norm_g[:, D_ATTN:])
    ffn_wg, ffn_wu, ffn_wd = ffn_w_gate.astype(BF16), ffn_w_up.astype(BF16), ffn_w_down.astype(BF16)
    rw_hi = moe_router_w.astype(BF16)
    rw_lo = (moe_router_w - rw_hi.astype(F32)).astype(BF16)
    rw = jnp.pad(jnp.concatenate([rw_hi, rw_lo], axis=2), ((0, 0), (0, 0), (0, LANES - 2 * N_EXPERTS)))
    rb = rows(jnp.pad(moe_router_b, ((0, 0), (0, LANES - N_EXPERTS))))

    for l in range(n_layers):
        qt, k, vt, convn = _inproj(l, x, mod, w_lat, w_gate, w_u, w_uq_t, w_k, w_v_t, rows(q_norm_g), rows(kv_norm_g),
                                   conv_w_p, g_conv, rope_tab, ts)
        attn = _attention(qt, k, vt, ts)
        i = l // 2
        if l % 2 == 0:
            x = _outproj_dense_ffn(l, i, attn, convn, x, mod, w_o_b, g_attn, rows(ln1_g), rows(ln1_b),
                                   ffn_wg, ffn_wu, ffn_wd, rows(ln2_g), rows(ln2_b), ts)
        else:
            x1, h2, *routing = _outproj_router(l, i, attn, convn, x, mod, w_o_b, g_attn, rows(ln1_g), rows(ln1_b),
                                               rw, rb, ts)
            x = _top2_moe(l, i, h2.reshape(t, D_MODEL), x1.reshape(t, D_MODEL), mod, routing,
                          moe_w_gate, moe_w_up, moe_w_down, rows(ln2_g), rows(ln2_b), ts, tiles_per_batch)
            x = x.reshape(b, s, D_MODEL)
    return x
```

```python
import functools

import jax
import jax.numpy as jnp
from jax import lax
from jax.experimental import pallas as pl
from jax.experimental.pallas import tpu as pltpu

F32 = jnp.float32
BF16 = jnp.bfloat16

D_MODEL = 1024
N_HEADS = 4
QK_NOPE = 128
QK_ROPE = 64
V_HEAD = 128
Q_LORA = 256
KV_LORA = 128
D_ATTN = N_HEADS * V_HEAD
D_CONV = D_MODEL - D_ATTN
CONV_WIDTH = 3
ROPE_BASE = 10000.0
D_FF = 2816
N_EXPERTS = 8
TOP_K = 2
D_FF_EXPERT = 3584
RMS_EPS = 1e-6
LN_EPS = 1e-5
DEPTH = 2
DEEPNORM_ALPHA = (2 * DEPTH) ** 0.25
SM_SCALE = (QK_NOPE + QK_ROPE) ** -0.5
LOG2E = 1.4426950408889634
Q_SCALE = SM_SCALE * LOG2E

LANES = 128
QK_PAD = 2 * LANES
LATENT_PAD = 512
TOKEN_TILE = 512
EXPERT_FF_TILE = 512
ROW_ALIGN = 16
SLOT_CHUNK = 256
EXPERT_ROW_STEP = 384
EXPERT_ROW_SIZES = tuple(EXPERT_ROW_STEP * n for n in range(1, 7))
EXPERT_ROWS = EXPERT_ROW_SIZES[-1]
VMEM_LIMIT = 56 * 1024 * 1024
NEG_BIG = -1e30
NOT_ROUTED = -4096.0


def _cparams(n_axes):
    return pltpu.CompilerParams(dimension_semantics=("arbitrary",) * n_axes, vmem_limit_bytes=VMEM_LIMIT)


def _dot(a, b):
    return jnp.dot(a, b, preferred_element_type=F32)


def _rms(x, g):
    return x * lax.rsqrt(jnp.mean(x * x, axis=-1, keepdims=True) + RMS_EPS) * g


def _layer_norm(x, g, b):
    mu = jnp.mean(x, axis=-1, keepdims=True)
    xc = x - mu
    var = jnp.mean(xc * xc, axis=-1, keepdims=True)
    return xc * lax.rsqrt(var + LN_EPS) * g + b


def _silu(x):
    return x * jax.nn.sigmoid(x)


def _ada_kernel(c_ref, w_ref, b_ref, o_ref):
    ca = _silu(c_ref[...]).astype(BF16)
    o_ref[0, 0] = _dot(ca, w_ref[0].astype(BF16)) + b_ref[0]


def _ada_mod(c_pad, w_ada, b_ada):
    n_layers = w_ada.shape[0]
    bp = c_pad.shape[0]
    return pl.pallas_call(
        _ada_kernel,
        grid=(n_layers, 6),
        in_specs=[
            pl.BlockSpec((bp, D_MODEL), lambda l, j: (0, 0)),
            pl.BlockSpec((1, D_MODEL, D_MODEL), lambda l, j: (l, 0, j)),
            pl.BlockSpec((1, 1, D_MODEL), lambda l, j: (l, 0, j)),
        ],
        out_specs=pl.BlockSpec((1, 1, bp, D_MODEL), lambda l, j: (l, j, 0, 0)),
        out_shape=jax.ShapeDtypeStruct((n_layers, 6, bp, D_MODEL), F32),
        compiler_params=_cparams(2),
        name="ada_mod",
    )(c_pad, w_ada, b_ada.reshape(n_layers, 1, 6 * D_MODEL))


ROPE_HALF = QK_ROPE // 2


def _rope_kernel(pos_ref, f_ref, o_ref):
    ang = pos_ref[0].astype(F32) * f_ref[...]
    o_ref[0, 0:ROPE_HALF, :] = jnp.cos(ang)
    o_ref[0, ROPE_HALF:QK_ROPE, :] = jnp.sin(ang)


def _rope_tables(positions, ts):
    nt = positions.size // ts
    inv_freq = ROPE_BASE ** (-jnp.arange(0, QK_ROPE, 2, dtype=F32) / QK_ROPE)
    return pl.pallas_call(
        _rope_kernel,
        grid=(nt,),
        in_specs=[pl.BlockSpec((1, 1, ts), lambda i: (i, 0, 0)), pl.BlockSpec((ROPE_HALF, 1), lambda i: (0, 0))],
        out_specs=pl.BlockSpec((1, QK_ROPE, ts), lambda i: (i, 0, 0)),
        out_shape=jax.ShapeDtypeStruct((nt, QK_ROPE, ts), F32),
        compiler_params=_cparams(1),
        name="rope_tables",
    )(positions.reshape(nt, 1, ts), inv_freq.reshape(ROPE_HALF, 1))


def _inproj_kernel(x_ref, mod_ref, wlat_ref, wgate_ref, wu_ref, wuqt_ref, wk_ref, wvt_ref, qg_ref, kvg_ref, cw_ref,
                   cg_ref, rope_ref, qt_ref, k_ref, vt_ref, cv_ref, cu_ext, *, ts):
    j = pl.program_id(1)

    @pl.when(j == 0)
    def _():
        cu_ext[0:8, :] = jnp.zeros((8, D_CONV), F32)

    @pl.when(j > 0)
    def _():
        cu_ext[0:8, :] = cu_ext[ts:ts + 8, :]

    x = x_ref[0]
    shift = mod_ref[0, 0:1, :]
    scale = mod_ref[0, 1:2, :]
    h = (x * (1.0 + scale) + shift).astype(BF16)
    z_lat = _dot(h, wlat_ref[...])
    c_q = z_lat[:, 0:Q_LORA]
    c_kv = z_lat[:, Q_LORA:Q_LORA + KV_LORA]
    k_rope = z_lat[:, Q_LORA + KV_LORA:LATENT_PAD]
    z_gate = _dot(h, wgate_ref[...])
    gate_b = z_gate[:, 0:D_CONV]
    gate_c = z_gate[:, D_CONV:2 * D_CONV]
    u = _dot(h, wu_ref[...])
    cqn = _rms(c_q, qg_ref[...])
    ckvn = _rms(c_kv, kvg_ref[...])
    q_t = _dot(wuqt_ref[...], jnp.transpose(cqn).astype(BF16))
    k_nope = _dot(ckvn.astype(BF16), wk_ref[...])
    v_t = _dot(wvt_ref[...], jnp.transpose(ckvn).astype(BF16))

    cos = rope_ref[0, 0:ROPE_HALF, :]
    sin = rope_ref[0, ROPE_HALF:QK_ROPE, :]

    def rope_t(block):
        x1, x2 = block[0:ROPE_HALF, :], block[ROPE_HALF:QK_ROPE, :]
        return x1 * cos - x2 * sin, x2 * cos + x1 * sin

    k1, k2 = rope_t(jnp.transpose(k_rope)[0:QK_ROPE, :])
    k_rot = jnp.transpose(jnp.concatenate([k1, k2, jnp.zeros((LANES - QK_ROPE, ts), F32)], axis=0)).astype(BF16)
    for hh in range(N_HEADS):
        qo = hh * QK_PAD
        r0 = qo + QK_NOPE
        q1, q2 = rope_t(q_t[r0:r0 + QK_ROPE, :])
        qt_ref[0, hh, 0:QK_NOPE, :] = (q_t[qo:r0, :] * Q_SCALE).astype(BF16)
        qt_ref[0, hh, QK_NOPE:QK_NOPE + ROPE_HALF, :] = (q1 * Q_SCALE).astype(BF16)
        qt_ref[0, hh, QK_NOPE + ROPE_HALF:QK_NOPE + QK_ROPE, :] = (q2 * Q_SCALE).astype(BF16)
        qt_ref[0, hh, QK_NOPE + QK_ROPE:QK_PAD, :] = jnp.zeros((QK_PAD - QK_NOPE - QK_ROPE, ts), BF16)
        k_ref[0, hh, :, 0:LANES] = k_nope[:, hh * QK_NOPE:(hh + 1) * QK_NOPE].astype(BF16)
        k_ref[0, hh, :, LANES:QK_PAD] = k_rot
        vt_ref[0, hh, 0] = v_t[hh * V_HEAD:(hh + 1) * V_HEAD, :].astype(BF16)

    cu_ext[8:ts + 8, :] = gate_c * u
    conv = (cw_ref[2:3, :] * cu_ext[8:ts + 8, :]
            + cw_ref[1:2, :] * cu_ext[pl.ds(7, ts), :]
            + cw_ref[0:1, :] * cu_ext[pl.ds(6, ts), :])
    conv = gate_b * conv
    cv_ref[0] = _rms(conv, cg_ref[...]).astype(BF16)


def _inproj(l, x, mod, w_lat, w_gate, w_u, w_uq_t, w_k, w_v_t, qg, kvg, conv_w, conv_g, rope_tab, ts):
    b, s, _ = x.shape
    nsb = s // ts
    const = lambda shape: pl.BlockSpec((None,) + shape, lambda bi, j: (l,) + (0,) * len(shape))
    return pl.pallas_call(
        functools.partial(_inproj_kernel, ts=ts),
        grid=(b, nsb),
        in_specs=[
            pl.BlockSpec((1, ts, D_MODEL), lambda bi, j: (bi, j, 0)),
            pl.BlockSpec((None, 1, 8, D_MODEL), lambda bi, j: (l, bi, 0, 0)),
            const((D_MODEL, LATENT_PAD)),
            const((D_MODEL, 2 * D_CONV)),
            const((D_MODEL, D_CONV)),
            const((N_HEADS * QK_PAD, Q_LORA)),
            const((KV_LORA, N_HEADS * QK_NOPE)),
            const((N_HEADS * V_HEAD, KV_LORA)),
            const((1, Q_LORA)),
            const((1, KV_LORA)),
            const((8, D_CONV)),
            const((1, D_CONV)),
            pl.BlockSpec((1, QK_ROPE, ts), lambda bi, j: (bi * nsb + j, 0, 0)),
        ],
        out_specs=[
            pl.BlockSpec((1, N_HEADS, QK_PAD, ts), lambda bi, j: (bi, 0, 0, j)),
            pl.BlockSpec((1, N_HEADS, ts, QK_PAD), lambda bi, j: (bi, 0, j, 0)),
            pl.BlockSpec((1, N_HEADS, 1, V_HEAD, ts), lambda bi, j: (bi, 0, j, 0, 0)),
            pl.BlockSpec((1, ts, D_CONV), lambda bi, j: (bi, j, 0)),
        ],
        out_shape=[
            jax.ShapeDtypeStruct((b, N_HEADS, QK_PAD, s), BF16),
            jax.ShapeDtypeStruct((b, N_HEADS, s, QK_PAD), BF16),
            jax.ShapeDtypeStruct((b, N_HEADS, nsb, V_HEAD, ts), BF16),
            jax.ShapeDtypeStruct((b, s, D_CONV), BF16),
        ],
        scratch_shapes=[pltpu.VMEM((ts + 8, D_CONV), F32)],
        compiler_params=_cparams(2),
        name="inproj",
    )(x, mod, w_lat, w_gate, w_u, w_uq_t, w_k, w_v_t, qg, kvg, conv_w, conv_g, rope_tab)


Q_CHUNK = 256


K_ROWS = 16


def _attn_kernel(qt_ref, k_ref, vt_ref, o_ref, *scratch, tq):
    i = pl.program_id(1)
    n_chunks = tq // Q_CHUNK
    per_chain = 5
    chains = [(hh, c) + tuple(scratch[per_chain * (hh * n_chunks + c):per_chain * (hh * n_chunks + c + 1)])
              for hh in range(N_HEADS) for c in range(n_chunks)]
    for _, _, m_sc, l_sc, acc_sc, _, _ in chains:
        m_sc[...] = jnp.full(m_sc.shape, NEG_BIG, F32)
        l_sc[...] = jnp.zeros(l_sc.shape, F32)
        acc_sc[...] = jnp.zeros(acc_sc.shape, F32)

    def scores(chain, jk, buf):
        hh, c = chain[0], chain[1]
        kj = k_ref[0, hh, pl.ds(pl.multiple_of(jk * tq, tq), tq), :]
        chain[5 + buf][...] = _dot(kj, qt_ref[0, hh, :, c * Q_CHUNK:(c + 1) * Q_CHUNK])

    def softmax_values(chain, jk, buf, diagonal):
        hh, c, m_sc, l_sc, acc_sc = chain[:5]
        s_sc = chain[5 + buf]
        nk = (c + 1) * Q_CHUNK if diagonal else tq
        if diagonal:
            d0 = nk - Q_CHUNK
            key = lax.broadcasted_iota(jnp.int32, (Q_CHUNK, Q_CHUNK), 0)
            qry = lax.broadcasted_iota(jnp.int32, (Q_CHUNK, Q_CHUNK), 1)
            s_sc[d0:nk, :] = jnp.where(key <= qry, s_sc[d0:nk, :], NEG_BIG)
        blk_max = s_sc[0:K_ROWS, :]
        for r in range(K_ROWS, nk, K_ROWS):
            blk_max = jnp.maximum(blk_max, s_sc[r:r + K_ROWS, :])
        m_prev = m_sc[...]
        m_new = jnp.maximum(m_prev, jnp.max(blk_max, axis=0, keepdims=True))
        alpha = jnp.exp2(m_prev - m_new)
        p_sum = jnp.zeros((K_ROWS, Q_CHUNK), F32)
        acc = alpha * acc_sc[...]
        for kt in range(0, nk, Q_CHUNK):
            parts = []
            for r in range(kt, kt + Q_CHUNK, K_ROWS):
                p = jnp.exp2(s_sc[r:r + K_ROWS, :] - m_new)
                p_sum = p_sum + p
                parts.append(p.astype(BF16))
            acc = acc + _dot(vt_ref[0, hh, jk, :, kt:kt + Q_CHUNK], jnp.concatenate(parts, axis=0))
        l_sc[...] = alpha * l_sc[...] + jnp.sum(p_sum, axis=0, keepdims=True)
        acc_sc[...] = acc
        m_sc[...] = m_new

    def stage(cur, cur_buf, diagonal, nxt=None):
        for chain in chains:
            if nxt is not None:
                scores(chain, nxt, 1 - cur_buf)
            softmax_values(chain, cur, cur_buf, diagonal)

    for chain in chains:
        scores(chain, 0, 0)

    def pair(jj, carry):
        stage(2 * jj, 0, False, nxt=2 * jj + 1)
        stage(2 * jj + 1, 1, False, nxt=2 * jj + 2)
        return carry

    lax.fori_loop(0, i // 2, pair, 0)

    @pl.when(lax.rem(i, 2) == 0)
    def _():
        stage(i, 0, True)

    @pl.when(lax.rem(i, 2) == 1)
    def _():
        stage(i - 1, 0, False, nxt=i)
        stage(i, 1, True)

    for hh, c, _, l_sc, acc_sc, _, _ in chains:
        o_ref[0, c * Q_CHUNK:(c + 1) * Q_CHUNK, hh * V_HEAD:(hh + 1) * V_HEAD] = (
            jnp.transpose(acc_sc[...] / l_sc[...]).astype(BF16))


def _attention(qt, k, vt, tq):
    b, nh, s, _ = k.shape
    return pl.pallas_call(
        functools.partial(_attn_kernel, tq=tq),
        grid=(b, s // tq),
        in_specs=[
            pl.BlockSpec((1, nh, QK_PAD, tq), lambda bi, i: (bi, 0, 0, i)),
            pl.BlockSpec((1, nh, s, QK_PAD), lambda bi, i: (bi, 0, 0, 0)),
            pl.BlockSpec((1, nh, s // tq, V_HEAD, tq), lambda bi, i: (bi, 0, 0, 0, 0)),
        ],
        out_specs=pl.BlockSpec((1, tq, D_ATTN), lambda bi, i: (bi, i, 0)),
        out_shape=jax.ShapeDtypeStruct((b, s, D_ATTN), BF16),
        scratch_shapes=[
            pltpu.VMEM((1, Q_CHUNK), F32), pltpu.VMEM((1, Q_CHUNK), F32), pltpu.VMEM((V_HEAD, Q_CHUNK), F32),
            pltpu.VMEM((tq, Q_CHUNK), F32), pltpu.VMEM((tq, Q_CHUNK), F32),
        ] * (nh * (tq // Q_CHUNK)),
        compiler_params=_cparams(2),
        name="attention",
    )(qt, k, vt)


OUT_ROWS = 256


def _row_chunks(ts):
    return [slice(r, r + OUT_ROWS) for r in range(0, ts, OUT_ROWS)]


def _outproj_matmuls(attn_ref, cv_ref, wo_ref, ga_ref, rows):
    an = _rms(attn_ref[0, rows, :].astype(F32), ga_ref[...]).astype(BF16)
    return _dot(an, wo_ref[0:D_ATTN, :]) + _dot(cv_ref[0, rows, :], wo_ref[D_ATTN:D_MODEL, :])


def _outproj_norm(y, x_ref, mod_ref, lng_ref, lnb_ref, rows):
    x1 = _layer_norm(DEEPNORM_ALPHA * x_ref[0, rows, :] + mod_ref[0, 2:3, :] * y, lng_ref[...], lnb_ref[...])
    h2 = x1 * (1.0 + mod_ref[0, 4:5, :]) + mod_ref[0, 3:4, :]
    return x1, h2


def _outproj_router_kernel(attn_ref, cv_ref, x_ref, mod_ref, wo_ref, ga_ref, lng_ref, lnb_ref, rw_ref, rb_ref,
                           x1_ref, h2_ref, rel_ref, gate_ref, relt_ref, base_ref, cnt_ref, run_sc, *, ts):
    first = jnp.logical_and(pl.program_id(0) == 0, pl.program_id(1) == 0)

    @pl.when(first)
    def _():
        run_sc[...] = jnp.zeros(run_sc.shape, F32)

    chunks = _row_chunks(ts)
    ys = [_outproj_matmuls(attn_ref, cv_ref, wo_ref, ga_ref, rows) for rows in chunks]
    lane = lax.broadcasted_iota(jnp.int32, (OUT_ROWS, LANES), 1)
    r_i = lax.broadcasted_iota(jnp.int32, (OUT_ROWS, OUT_ROWS), 0)
    c_i = lax.broadcasted_iota(jnp.int32, (OUT_ROWS, OUT_ROWS), 1)
    tri = jnp.where(c_i < r_i, 1.0, 0.0).astype(BF16)
    n_tile = jnp.zeros((1, LANES), F32)
    for rows, y in zip(chunks, ys):
        x1, h2 = _outproj_norm(y, x_ref, mod_ref, lng_ref, lnb_ref, rows)
        x1_ref[0, rows, :] = x1
        h2_hi = h2.astype(BF16)
        h2_ref[0, rows, :] = h2_hi

        h2_lo = (h2 - h2_hi.astype(F32)).astype(BF16)
        prod = _dot(h2_hi, rw_ref[...]) + _dot(h2_lo, rw_ref[...])
        logits = prod + pltpu.roll(prod, LANES - N_EXPERTS, 1) + rb_ref[...]
        logits = jnp.where(lane < N_EXPERTS, logits, NEG_BIG)
        v1 = jnp.max(logits, axis=-1, keepdims=True)
        i1 = jnp.min(jnp.where(logits == v1, lane, LANES), axis=-1, keepdims=True)
        rest = jnp.where(lane == i1, NEG_BIG, logits)
        v2 = jnp.max(rest, axis=-1, keepdims=True)
        i2 = jnp.min(jnp.where(rest == v2, lane, LANES), axis=-1, keepdims=True)
        e21 = jnp.exp(v2 - v1)
        g1 = 1.0 / (1.0 + e21)
        g2 = e21 / (1.0 + e21)
        is1 = lane == i1
        is2 = lane == i2
        sel = jnp.where(jnp.logical_or(is1, is2), 1.0, 0.0)
        gate_ref[rows, :] = jnp.where(is1, g1, 0.0) + jnp.where(is2, g2, 0.0)

        rank = _dot(tri, sel.astype(BF16)) + n_tile
        rel = jnp.where(sel > 0.0, rank, NOT_ROUTED)
        rel_ref[rows, :] = rel
        relt_ref[0, :, rows] = jnp.transpose(rel)[0:N_EXPERTS, :]
        n_tile = n_tile + jnp.sum(sel, axis=0, keepdims=True)

    base_ref[0] = run_sc[...].astype(jnp.int32)
    cnt_ref[0] = n_tile.astype(jnp.int32)
    run_sc[...] = run_sc[...] + n_tile


def _outproj_router(l, moe_i, attn, convn, x, mod, w_o, g_attn, ln_g, ln_b, rw, rb, ts):
    b, s, _ = x.shape
    nsb = s // ts
    const = lambda shape, idx=l: pl.BlockSpec((None,) + shape, lambda bi, j: (idx,) + (0,) * len(shape))
    tile3 = lambda w: pl.BlockSpec((1, ts, w), lambda bi, j: (bi, j, 0))
    in_specs = [
        tile3(D_ATTN), tile3(D_CONV), tile3(D_MODEL),
        pl.BlockSpec((None, 1, 8, D_MODEL), lambda bi, j: (l, bi, 0, 0)),
        const((D_MODEL, D_MODEL)), const((1, D_ATTN)), const((1, D_MODEL)), const((1, D_MODEL)),
    ]
    out_specs = [tile3(D_MODEL), tile3(D_MODEL)]
    out_shape = [jax.ShapeDtypeStruct((b, s, D_MODEL), F32), jax.ShapeDtypeStruct((b, s, D_MODEL), BF16)]
    args = [attn, convn, x, mod, w_o, g_attn, ln_g, ln_b]
    nt = b * nsb
    flat = lambda w: pl.BlockSpec((ts, w), lambda bi, j: (bi * nsb + j, 0))
    per_tile = lambda r, w: pl.BlockSpec((1, r, w), lambda bi, j: (bi * nsb + j, 0, 0))
    in_specs += [const((D_MODEL, LANES), moe_i), const((1, LANES), moe_i)]
    out_specs += [flat(LANES), flat(LANES), per_tile(N_EXPERTS, ts), per_tile(1, LANES), per_tile(1, LANES)]
    out_shape += [
        jax.ShapeDtypeStruct((nt * ts, LANES), F32), jax.ShapeDtypeStruct((nt * ts, LANES), F32),
        jax.ShapeDtypeStruct((nt, N_EXPERTS, ts), F32),
        jax.ShapeDtypeStruct((nt, 1, LANES), jnp.int32), jax.ShapeDtypeStruct((nt, 1, LANES), jnp.int32),
    ]
    return pl.pallas_call(
        functools.partial(_outproj_router_kernel, ts=ts), grid=(b, nsb), in_specs=in_specs, out_specs=out_specs,
        out_shape=out_shape, scratch_shapes=[pltpu.VMEM((1, LANES), F32)],
        compiler_params=_cparams(2), name="outproj_router",
    )(*args, rw, rb)


FF_CHUNK = 256


def _swiglu(x, wg, wu, wd, width):
    out = None
    pending = None
    for c in range(0, width, FF_CHUNK):
        w = min(FF_CHUNK, width - c)
        g = _dot(x, wg(c, w))
        u = _dot(x, wu(c, w))
        if pending is not None:
            d = _dot(pending[0], wd(*pending[1]))
            out = d if out is None else out + d
        pending = ((_silu(g) * u).astype(BF16), (c, w))
    d = _dot(pending[0], wd(*pending[1]))
    return d if out is None else out + d


def _outproj_ffn_kernel(attn_ref, cv_ref, x_ref, mod_ref, wo_ref, ga_ref, ln1g_ref, ln1b_ref,
                        wg_ref, wu_ref, wd_ref, ln2g_ref, ln2b_ref, o_ref, *, ts):
    chunks = _row_chunks(ts)
    ys = [_outproj_matmuls(attn_ref, cv_ref, wo_ref, ga_ref, rows) for rows in chunks]
    normed = [_outproj_norm(y, x_ref, mod_ref, ln1g_ref, ln1b_ref, rows) for rows, y in zip(chunks, ys)]
    x1 = jnp.concatenate([n[0] for n in normed], axis=0)
    h = jnp.concatenate([n[1].astype(BF16) for n in normed], axis=0)
    f = _swiglu(h, lambda c, w: wg_ref[:, c:c + w], lambda c, w: wu_ref[:, c:c + w],
                lambda c, w: wd_ref[c:c + w, :], D_FF)
    gate2 = mod_ref[0, 5:6, :]
    o_ref[0] = _layer_norm(DEEPNORM_ALPHA * x1 + gate2 * f, ln2g_ref[...], ln2b_ref[...])


def _outproj_dense_ffn(l, dense_i, attn, convn, x, mod, w_o, g_attn, ln1_g, ln1_b, wg, wu, wd, ln2_g, ln2_b, ts):
    b, s, _ = x.shape
    const = lambda shape, idx=l: pl.BlockSpec((None,) + shape, lambda bi, j: (idx,) + (0,) * len(shape),
                                              pipeline_mode=pl.Buffered(1))
    tile3 = lambda w: pl.BlockSpec((1, ts, w), lambda bi, j: (bi, j, 0))
    return pl.pallas_call(
        functools.partial(_outproj_ffn_kernel, ts=ts),
        grid=(b, s // ts),
        in_specs=[
            tile3(D_ATTN), tile3(D_CONV), tile3(D_MODEL),
            pl.BlockSpec((None, 1, 8, D_MODEL), lambda bi, j: (l, bi, 0, 0)),
            const((D_MODEL, D_MODEL)), const((1, D_ATTN)), const((1, D_MODEL)), const((1, D_MODEL)),
            const((D_MODEL, D_FF), dense_i), const((D_MODEL, D_FF), dense_i), const((D_FF, D_MODEL), dense_i),
            const((1, D_MODEL)), const((1, D_MODEL)),
        ],
        out_specs=tile3(D_MODEL),
        out_shape=jax.ShapeDtypeStruct((b, s, D_MODEL), F32),
        compiler_params=_cparams(2),
        name="outproj_dense_ffn",
    )(attn, convn, x, mod, w_o, g_attn, ln1_g, ln1_b, wg, wu, wd, ln2_g, ln2_b)


def _group_copy(src, dst, sem):
    return pltpu.make_async_copy(src, dst, sem)


DISPATCH_CHUNK = 256


def _dispatch_chunks(ts):
    return -(-(ts + ROW_ALIGN - 1) // DISPATCH_CHUNK)


def _floor_align(v):
    return pl.multiple_of(lax.shift_left(lax.shift_right_logical(v, 4), 4), ROW_ALIGN)


def _dispatch_kernel(base_ref, cnt_ref, h_ref, relt_ref, xs_ref, stage, carry, zbuf, sems, xsem, zsem, *,
                     ts, zero_rows):
    i = pl.program_id(0)
    last = pl.num_programs(0) - 1
    par = lax.rem(i, 2)
    ch = DISPATCH_CHUNK
    n_chunks = _dispatch_chunks(ts)
    h = h_ref[...]
    slot = lax.broadcasted_iota(jnp.int32, (ch, ts), 0).astype(F32)

    def group(step, e):
        a = base_ref[step * N_EXPERTS + e]
        a16 = _floor_align(a)
        return a, a16, a - a16

    def onehot(e, c):
        _, _, off = group(i, e)
        row = relt_ref[0, e:e + 1, :] + off.astype(F32)
        return jnp.where(row == slot + float(c * ch), 1.0, 0.0).astype(BF16)

    def first_copy(step, e):
        _, a16, _ = group(step, e)
        p = lax.rem(step, 2)
        return _group_copy(stage.at[p, e, 0:ch], xs_ref.at[e, pl.ds(a16, ch), :], sems.at[p, e])

    @pl.when(i == 0)
    def _():
        carry[...] = jnp.zeros(carry.shape, BF16)
        stage[:, :, n_chunks * ch:, :] = jnp.zeros((2, N_EXPERTS, ROW_ALIGN, D_MODEL), BF16)

    for e in range(N_EXPERTS):
        rows = _dot(onehot(e, 0), h)
        stage[par, e, 0:ROW_ALIGN] = (rows[0:ROW_ALIGN] + carry[e].astype(F32)).astype(BF16)
        stage[par, e, ROW_ALIGN:ch] = rows[ROW_ALIGN:ch].astype(BF16)
        stage[par, e, ch:ch + ROW_ALIGN] = jnp.zeros((ROW_ALIGN, D_MODEL), BF16)

    @pl.when(i > 0)
    def _():
        for e in range(N_EXPERTS):
            first_copy(i - 1, e).wait()

    for e in range(N_EXPERTS):
        first_copy(i, e).start(priority=e % 2)

    for e in range(N_EXPERTS):
        _, a16, off = group(i, e)
        n = cnt_ref[i * N_EXPERTS + e]
        for c in range(1, n_chunks):
            @pl.when(off + n > c * ch)
            def _(e=e, c=c, a16=a16):
                lo = c * ch
                stage[par, e, lo:lo + ch] = _dot(onehot(e, c), h).astype(BF16)
                if c + 1 < n_chunks:
                    stage[par, e, lo + ch:lo + ch + ROW_ALIGN] = jnp.zeros((ROW_ALIGN, D_MODEL), BF16)
                cp = _group_copy(stage.at[par, e, lo:lo + ch], xs_ref.at[e, pl.ds(a16 + lo, ch), :], xsem)
                cp.start()
                cp.wait()

    for e in range(N_EXPERTS):
        a, a16, _ = group(i, e)
        nxt16 = _floor_align(a + cnt_ref[i * N_EXPERTS + e])
        carry[e] = stage[par, e, pl.ds(pl.multiple_of(nxt16 - a16, ROW_ALIGN), ROW_ALIGN), :]

    @pl.when(i == last)
    def _():
        for e in range(N_EXPERTS):
            first_copy(i, e).wait()
        zbuf[...] = jnp.zeros(zbuf.shape, BF16)
        for e in range(N_EXPERTS):
            n = cnt_ref[i * N_EXPERTS + e]
            end = _floor_align(base_ref[i * N_EXPERTS + e] + n + (ROW_ALIGN - 1))
            copies = [
                _group_copy(zbuf, xs_ref.at[e, pl.ds(end + r * SLOT_CHUNK, SLOT_CHUNK), :], zsem.at[e])
                for r in range(zero_rows // SLOT_CHUNK)
            ]
            for cp in copies:
                cp.start()
            for cp in copies:
                cp.wait()


def _dispatch(base, cnt, h2, relt, ts, cap, zero_rows):
    t = h2.shape[0]
    nt = t // ts
    n_chunks = _dispatch_chunks(ts)
    grid_spec = pltpu.PrefetchScalarGridSpec(
        num_scalar_prefetch=2,
        grid=(nt,),
        in_specs=[
            pl.BlockSpec((ts, D_MODEL), lambda i, b_, c_: (i, 0)),
            pl.BlockSpec((1, N_EXPERTS, ts), lambda i, b_, c_: (i, 0, 0)),
        ],
        out_specs=pl.BlockSpec(memory_space=pl.ANY),
        scratch_shapes=[
            pltpu.VMEM((2, N_EXPERTS, n_chunks * DISPATCH_CHUNK + ROW_ALIGN, D_MODEL), BF16),
            pltpu.VMEM((N_EXPERTS, ROW_ALIGN, D_MODEL), BF16),
            pltpu.VMEM((SLOT_CHUNK, D_MODEL), BF16),
            pltpu.SemaphoreType.DMA((2, N_EXPERTS)),
            pltpu.SemaphoreType.DMA(()),
            pltpu.SemaphoreType.DMA((N_EXPERTS,)),
        ],
    )
    return pl.pallas_call(
        functools.partial(_dispatch_kernel, ts=ts, zero_rows=zero_rows),
        grid_spec=grid_spec,
        out_shape=jax.ShapeDtypeStruct((N_EXPERTS, cap, D_MODEL), BF16),
        compiler_params=_cparams(1),
        name="moe_dispatch",
    )(base, cnt, h2, relt)


def _expert_kernel(te_ref, tr_ref, rows_ref, nv_ref, x_ref, wg_ref, wu_ref, wd_ref, o_ref, acc_sc):
    g_i = pl.program_id(0)
    k = pl.program_id(1)

    valid = g_i < nv_ref[0]

    @pl.when(jnp.logical_and(valid, k == 0))
    def _():
        acc_sc[...] = jnp.zeros(acc_sc.shape, F32)

    for m in EXPERT_ROW_SIZES:
        @pl.when(jnp.logical_and(valid, rows_ref[g_i] == m))
        def _(m=m):
            part = _swiglu(x_ref[0, 0:m, :], lambda c, w: wg_ref[0, :, c:c + w].astype(BF16),
                           lambda c, w: wu_ref[0, :, c:c + w].astype(BF16),
                           lambda c, w: wd_ref[0, c:c + w, :].astype(BF16), wg_ref.shape[2])
            total = acc_sc[0:m, :] + part
            acc_sc[0:m, :] = total
            o_ref[0, 0:m, :] = total.astype(BF16)


def _expert_ffn(moe_i, tile_e, tile_r, tile_rows, n_valid, xs, wg, wu, wd, tf, n_steps):
    tm = EXPERT_ROWS
    nk = D_FF_EXPERT // tf
    cap = xs.shape[1]

    def k_eff(g, k, nv):
        return jnp.where(g < nv[0], k, nk - 1)

    grid_spec = pltpu.PrefetchScalarGridSpec(
        num_scalar_prefetch=4,
        grid=(n_steps, nk),
        in_specs=[
            pl.BlockSpec((1, tm, D_MODEL), lambda g, k, te, tr, rw, nv: (te[g], tr[g], 0)),
            pl.BlockSpec((None, 1, D_MODEL, tf), lambda g, k, te, tr, rw, nv: (moe_i, te[g], 0, k_eff(g, k, nv))),
            pl.BlockSpec((None, 1, D_MODEL, tf), lambda g, k, te, tr, rw, nv: (moe_i, te[g], 0, k_eff(g, k, nv))),
            pl.BlockSpec((None, 1, tf, D_MODEL), lambda g, k, te, tr, rw, nv: (moe_i, te[g], k_eff(g, k, nv), 0)),
        ],
        out_specs=pl.BlockSpec((1, tm, D_MODEL), lambda g, k, te, tr, rw, nv: (te[g], tr[g], 0)),
        scratch_shapes=[pltpu.VMEM((tm, D_MODEL), F32)],
    )
    return pl.pallas_call(
        _expert_kernel,
        grid_spec=grid_spec,
        out_shape=jax.ShapeDtypeStruct((N_EXPERTS, cap, D_MODEL), BF16),
        compiler_params=_cparams(2),
        name="moe_experts",
    )(tile_e, tile_r, tile_rows, n_valid, xs, wg, wu, wd)


def _combine_kernel(base_ref, cnt_ref, lim_ref, ys_ref, rel_ref, gate_ref, x1_ref, mod_ref, lng_ref, lnb_ref, o_ref,
                    slab, xslab, acc_sc, sems, xsem, *, ts):
    i = pl.program_id(0)
    last = pl.num_programs(0) - 1
    par = lax.rem(i, 2)
    n_chunks = -(-(ts + ROW_ALIGN - 1) // SLOT_CHUNK)
    slot = lax.broadcasted_iota(jnp.int32, (ts, SLOT_CHUNK), 1).astype(F32)

    def slab_start(step, e, c):
        want = _floor_align(base_ref[step * N_EXPERTS + e]) + c * SLOT_CHUNK
        start = jnp.maximum(jnp.minimum(want, lim_ref[e] - SLOT_CHUNK), 0)
        return pl.multiple_of(start, ROW_ALIGN), want - start

    def first_copy(step, e):
        start, _ = slab_start(step, e, 0)
        p = lax.rem(step, 2)
        return _group_copy(ys_ref.at[e, pl.ds(start, SLOT_CHUNK), :], slab.at[p, e], sems.at[p, e])

    def contribution(e, c, rows):
        a = base_ref[i * N_EXPERTS + e]
        _, moved = slab_start(i, e, c)
        pos = rel_ref[:, e:e + 1] + (a - _floor_align(a)).astype(F32)
        in_chunk = jnp.where(pos >= float(c * SLOT_CHUNK), 1.0, 0.0)
        onehot = jnp.where(pos + (moved - c * SLOT_CHUNK).astype(F32) == slot, in_chunk, 0.0).astype(BF16)
        return _dot(onehot, rows) * gate_ref[:, e:e + 1]

    @pl.when(i == 0)
    def _():
        for e in range(N_EXPERTS):
            first_copy(i, e).start()

    @pl.when(i < last)
    def _():
        for e in range(N_EXPERTS):
            first_copy(i + 1, e).start(priority=e % 2)

    for e in range(N_EXPERTS):
        first_copy(i, e).wait()
    f = contribution(0, 0, slab[par, 0])
    for e in range(1, N_EXPERTS):
        f = f + contribution(e, 0, slab[par, e])
    acc_sc[...] = f

    for e in range(N_EXPERTS):
        a = base_ref[i * N_EXPERTS + e]
        rows_in_block = a - _floor_align(a) + cnt_ref[i * N_EXPERTS + e]
        for c in range(1, n_chunks):
            @pl.when(rows_in_block > c * SLOT_CHUNK)
            def _(e=e, c=c):
                start, _ = slab_start(i, e, c)
                cp = _group_copy(ys_ref.at[e, pl.ds(start, SLOT_CHUNK), :], xslab, xsem)
                cp.start()
                cp.wait()
                acc_sc[...] += contribution(e, c, xslab[...])

    gate2 = mod_ref[0, 5:6, :]
    o_ref[...] = _layer_norm(DEEPNORM_ALPHA * x1_ref[...] + gate2 * acc_sc[...], lng_ref[...], lnb_ref[...])


def _combine(l, base, cnt, lim, ys, rel, gates, x1, mod, ln_g, ln_b, ts, tiles_per_batch):
    t = x1.shape[0]
    grid_spec = pltpu.PrefetchScalarGridSpec(
        num_scalar_prefetch=3,
        grid=(t // ts,),
        in_specs=[
            pl.BlockSpec(memory_space=pl.ANY),
            pl.BlockSpec((ts, LANES), lambda i, *_: (i, 0)),
            pl.BlockSpec((ts, LANES), lambda i, *_: (i, 0)),
            pl.BlockSpec((ts, D_MODEL), lambda i, *_: (i, 0)),
            pl.BlockSpec((None, 1, 8, D_MODEL), lambda i, *_: (l, i // tiles_per_batch, 0, 0)),
            pl.BlockSpec((None, 1, D_MODEL), lambda i, *_: (l, 0, 0)),
            pl.BlockSpec((None, 1, D_MODEL), lambda i, *_: (l, 0, 0)),
        ],
        out_specs=pl.BlockSpec((ts, D_MODEL), lambda i, *_: (i, 0)),
        scratch_shapes=[
            pltpu.VMEM((2, N_EXPERTS, SLOT_CHUNK, D_MODEL), BF16),
            pltpu.VMEM((SLOT_CHUNK, D_MODEL), BF16),
            pltpu.VMEM((ts, D_MODEL), F32),
            pltpu.SemaphoreType.DMA((2, N_EXPERTS)),
            pltpu.SemaphoreType.DMA(()),
        ],
    )
    return pl.pallas_call(
        functools.partial(_combine_kernel, ts=ts),
        grid_spec=grid_spec,
        out_shape=jax.ShapeDtypeStruct((t, D_MODEL), F32),
        compiler_params=_cparams(1),
        name="moe_combine",
    )(base, cnt, lim, ys, rel, gates, x1, mod, ln_g, ln_b)


def _expert_tile_plan(base, cnt, nt, n_steps):
    tm = EXPERT_ROWS
    i32 = lambda v: v.astype(jnp.int32)
    last_base = base.reshape(nt, N_EXPERTS)[-1]
    last_cnt = cnt.reshape(nt, N_EXPERTS)[-1]
    total = last_base + last_cnt
    tiles = jnp.maximum((total + tm - 1) // tm, 1)
    rest = total - (tiles - 1) * tm
    last_rows = jnp.full_like(rest, EXPERT_ROW_SIZES[0])
    for small, big in zip(EXPERT_ROW_SIZES[:-1], EXPERT_ROW_SIZES[1:]):
        last_rows = last_rows + jnp.where(rest > small, big - small, 0)
    ends = jnp.cumsum(tiles)
    n_valid = ends[-1]
    g = jnp.minimum(jnp.arange(n_steps, dtype=jnp.int32), n_valid - 1)
    tile_e = jnp.sum((g[:, None] >= ends[None, :]).astype(jnp.int32), axis=1)
    onehot = (tile_e[:, None] == jnp.arange(N_EXPERTS, dtype=jnp.int32)[None, :]).astype(jnp.int32)
    pick = lambda v: jnp.sum(onehot * v[None, :], axis=1)
    tile_r = g - pick(ends - tiles)
    tile_rows = jnp.where(tile_r == pick(tiles) - 1, pick(last_rows), tm)
    lim = (tiles - 1) * tm + last_rows
    return i32(tile_e), i32(tile_r), i32(tile_rows), i32(n_valid.reshape(1)), i32(lim)


def _top2_moe(l, moe_i, h2, x1, mod, routing, wg, wu, wd, ln_g, ln_b, ts, tiles_per_batch):
    rel, gates, relt, base3, cnt3 = routing
    t = h2.shape[0]
    nt = t // ts
    tm, tf = EXPERT_ROWS, EXPERT_FF_TILE
    base = base3[:, 0, :N_EXPERTS].reshape(-1)
    cnt = cnt3[:, 0, :N_EXPERTS].reshape(-1)
    zero_rows = -(-(EXPERT_ROW_STEP + ROW_ALIGN) // SLOT_CHUNK) * SLOT_CHUNK
    reach_rows = _dispatch_chunks(ts) * DISPATCH_CHUNK
    cap = -(-(t + ROW_ALIGN + reach_rows + zero_rows) // tm) * tm
    n_steps = -(-TOP_K * t // tm) + N_EXPERTS
    xs = _dispatch(base, cnt, h2, relt, ts, cap, zero_rows)
    tile_e, tile_r, tile_rows, n_valid, lim = _expert_tile_plan(base, cnt, nt, n_steps)
    ys = _expert_ffn(moe_i, tile_e, tile_r, tile_rows, n_valid, xs, wg, wu, wd, tf, n_steps)
    return _combine(l, base, cnt, lim, ys, rel, gates, x1, mod, ln_g, ln_b, ts, tiles_per_batch)


def kernel(x, c, positions, w_in, q_norm_g, kv_norm_g, w_uq, w_ukv, conv_w, mix_norm_g, w_o, w_ada, b_ada,
           ln1_g, ln1_b, ln2_g, ln2_b, ffn_w_gate, ffn_w_up, ffn_w_down, moe_router_w, moe_router_b,
           moe_w_gate, moe_w_up, moe_w_down):
    b, s, _ = x.shape
    n_layers = w_in.shape[0]
    ts = min(TOKEN_TILE, s)
    t = b * s
    tiles_per_batch = s // ts

    bp = -(-b // ROW_ALIGN) * ROW_ALIGN
    c_pad = jnp.pad(c, ((0, bp - b), (0, 0)))
    mod = _ada_mod(c_pad, w_ada, b_ada)
    mod = jnp.pad(jnp.transpose(mod[:, :, :b, :], (0, 2, 1, 3)), ((0, 0), (0, 0), (0, 2), (0, 0)))
    rope_tab = _rope_tables(positions, ts)

    rows = lambda v: v.reshape(v.shape[0], 1, -1)
    o3 = Q_LORA + KV_LORA + QK_ROPE
    w_lat = jnp.pad(w_in[:, :, :o3], ((0, 0), (0, 0), (0, LATENT_PAD - o3))).astype(BF16)
    w_gate = w_in[:, :, o3:o3 + 2 * D_CONV].astype(BF16)
    w_u = w_in[:, :, o3 + 2 * D_CONV:].astype(BF16)
    w_uq_t = jnp.transpose(jnp.pad(
        w_uq.reshape(n_layers, Q_LORA, N_HEADS, QK_NOPE + QK_ROPE),
        ((0, 0), (0, 0), (0, 0), (0, QK_PAD - QK_NOPE - QK_ROPE))
    ).reshape(n_layers, Q_LORA, N_HEADS * QK_PAD), (0, 2, 1)).astype(BF16)
    w_ukv_h = w_ukv.reshape(n_layers, KV_LORA, N_HEADS, QK_NOPE + V_HEAD)
    w_k = w_ukv_h[..., :QK_NOPE].reshape(n_layers, KV_LORA, N_HEADS * QK_NOPE).astype(BF16)
    w_v_t = jnp.transpose(w_ukv_h[..., QK_NOPE:].reshape(n_layers, KV_LORA, N_HEADS * V_HEAD), (0, 2, 1)).astype(BF16)
    w_o_b = w_o.astype(BF16)
    conv_w_p = jnp.pad(conv_w, ((0, 0), (0, 8 - CONV_WIDTH), (0, 0)))
    g_attn, g_conv = rows(mix_norm_g[:, :D_ATTN]), rows(mix_norm_g[:, D_ATTN:])
    ffn_wg, ffn_wu, ffn_wd = ffn_w_gate.astype(BF16), ffn_w_up.astype(BF16), ffn_w_down.astype(BF16)
    rw_hi = moe_router_w.astype(BF16)
    rw_lo = (moe_router_w - rw_hi.astype(F32)).astype(BF16)
    rw = jnp.pad(jnp.concatenate([rw_hi, rw_lo], axis=2), ((0, 0), (0, 0), (0, LANES - 2 * N_EXPERTS)))
    rb = rows(jnp.pad(moe_router_b, ((0, 0), (0, LANES - N_EXPERTS))))

    for l in range(n_layers):
        qt, k, vt, convn = _inproj(l, x, mod, w_lat, w_gate, w_u, w_uq_t, w_k, w_v_t, rows(q_norm_g), rows(kv_norm_g),
                                   conv_w_p, g_conv, rope_tab, ts)
        attn = _attention(qt, k, vt, ts)
        i = l // 2
        if l % 2 == 0:
            x = _outproj_dense_ffn(l, i, attn, convn, x, mod, w_o_b, g_attn, rows(ln1_g), rows(ln1_b),
                                   ffn_wg, ffn_wu, ffn_wd, rows(ln2_g), rows(ln2_b), ts)
        else:
            x1, h2, *routing = _outproj_router(l, i, attn, convn, x, mod, w_o_b, g_attn, rows(ln1_g), rows(ln1_b),
                                               rw, rb, ts)
            x = _top2_moe(l, i, h2.reshape(t, D_MODEL), x1.reshape(t, D_MODEL), mod, routing,
                          moe_w_gate, moe_w_up, moe_w_down, rows(ln2_g), rows(ln2_b), ts, tiles_per_batch)
            x = x.reshape(b, s, D_MODEL)
    return x
```
